```python
import math, functools
import jax, jax.numpy as jnp
from jax import lax
import numpy as np

D_MODEL = 4096
BATCH = 1
SEQ = 16384
DEPTH = 4

HEAD_DIM = 128
Q_BLOCK = 128
SUPER_LEN = 1024
ROPE_THETA = 10000.0
NORM_EPS = 1e-6
SB_HEADS = 4
DSA_HEADS = 4
DSA_KV_HEADS = 1
DSA_IDX_HEADS = 4
DSA_IDX_DIM = 64
DSA_TOPK = 256
NSA_HEADS = 4
NSA_KV_HEADS = 1
NSA_CMP_LEN = 32
NSA_CMP_STRIDE = 16
NSA_SLC_LEN = 32
NSA_N_SEL = 8
NSA_WINDOW = 512
NSA_FORCED_SCORE = 1e4
DIL_PATTERNS = ((128, 1), (512, 4), (2048, 16))
DIL_GROUPS = len(DIL_PATTERNS)
DIL_HEADS_PER_GROUP = 2
DIL_MAX_WINDOW = max(w for w, _ in DIL_PATTERNS)
N_BRANCHES = 4
GATE_RANK = 256
PEER_HEADS = 8
PEER_N_KEYS = 64
PEER_EXPERTS = PEER_N_KEYS ** 2
PEER_KEY_DIM = 128
PEER_TOPK = 16

SB_W = SB_HEADS * HEAD_DIM
DSA_Q_W = DSA_HEADS * HEAD_DIM
DSA_KV_W = DSA_KV_HEADS * HEAD_DIM
DSA_IQ_W = DSA_IDX_HEADS * DSA_IDX_DIM
NSA_Q_W = NSA_HEADS * HEAD_DIM
NSA_KV_W = NSA_KV_HEADS * HEAD_DIM
DIL_Q_W = DIL_GROUPS * DIL_HEADS_PER_GROUP * HEAD_DIM
DIL_KV_W = DIL_HEADS_PER_GROUP * HEAD_DIM
IN_SPLITS = (SB_W, SB_W, SB_W,
             DSA_Q_W, DSA_KV_W, DSA_KV_W, DSA_IQ_W, DSA_IDX_DIM, DSA_IDX_HEADS,
             NSA_Q_W, NSA_KV_W, NSA_KV_W, NSA_KV_W, NSA_KV_W, NSA_KV_W, NSA_KV_W, NSA_HEADS * 3,
             DIL_Q_W, DIL_KV_W, DIL_KV_W)
IN_WIDTH = sum(IN_SPLITS)
SPLIT_OFFSETS = tuple(int(o) for o in np.cumsum(IN_SPLITS)[:-1])
BRANCH_WIDTHS = (SB_W, DSA_Q_W, NSA_Q_W, DIL_KV_W)
BRANCH_WIDTH = sum(BRANCH_WIDTHS)
BRANCH_OFFSETS = tuple(int(o) for o in np.cumsum(BRANCH_WIDTHS)[:-1])
N_QK_GAINS = 8

kernel_name = 'hybrid_sb_dsa_nsa_dilated_peer_trunk'


def rmsnorm(x, g):
    xf = x.astype(jnp.float32)
    y = xf * lax.rsqrt(jnp.mean(xf * xf, axis=-1, keepdims=True) + NORM_EPS)
    return (y * g.astype(jnp.float32)).astype(x.dtype)


def rope_tables(seq, dtype):
    inv = ROPE_THETA ** (-jnp.arange(0, HEAD_DIM, 2, dtype=jnp.float32) / HEAD_DIM)
    ang = jnp.arange(seq, dtype=jnp.float32)[:, None] * inv[None, :]
    return jnp.cos(ang)[None, :, None, :].astype(dtype), jnp.sin(ang)[None, :, None, :].astype(dtype)


def apply_rope(x, cos, sin):
    x1, x2 = jnp.split(x, 2, axis=-1)
    return jnp.concatenate([x1 * cos - x2 * sin, x2 * cos + x1 * sin], axis=-1)


def softmax_stats(s, mask):
    s = jnp.where(mask, s, -jnp.inf)
    m = jnp.max(s, axis=-1, keepdims=True)
    m = jnp.where(jnp.isfinite(m), m, 0.0)
    e = jnp.where(mask, jnp.exp(s - m), 0.0)
    return e, m, jnp.sum(e, axis=-1, keepdims=True)


def masked_softmax(s, mask):
    e, _, den = softmax_stats(s, mask)
    return e / jnp.maximum(den, jnp.finfo(jnp.float32).tiny)


def block_map(fn, n_blocks):
    out = lax.map(fn, jnp.arange(n_blocks))
    nb, b, t, f = out.shape
    return jnp.swapaxes(out, 0, 1).reshape(b, nb * t, f)


def causal_blocks(fn, seq):
    outs = []
    for start in range(0, seq, SUPER_LEN):
        stop = min(start + SUPER_LEN, seq)
        out = lax.map(lambda i: fn(start + i * Q_BLOCK, stop), jnp.arange((stop - start) // Q_BLOCK))
        nb, b, t, f = out.shape
        outs.append(jnp.swapaxes(out, 0, 1).reshape(b, nb * t, f))
    return jnp.concatenate(outs, axis=1)


def stick_breaking_attention(q, k, v):
    B, S, H, Dh = q.shape
    scale = Dh ** -0.5
    tri = jnp.tril(jnp.ones((Q_BLOCK, Q_BLOCK), jnp.float32), -1)

    def block(q0, n_keys):
        qb = lax.dynamic_slice_in_dim(q, q0, Q_BLOCK, axis=1)
        kb = k[:, :n_keys]
        vb = v[:, :n_keys]
        tpos = q0 + jnp.arange(Q_BLOCK)
        causal = (jnp.arange(n_keys)[None, :] < tpos[:, None])[None, None]
        z = jnp.einsum('bthd,bshd->bhts', qb, kb).astype(jnp.float32) * scale
        log_1m = jnp.where(causal, jax.nn.log_sigmoid(-z), 0.0)
        lm = log_1m.reshape(B, H, Q_BLOCK, n_keys // Q_BLOCK, Q_BLOCK)
        within = jnp.einsum('bhtnk,kj->bhtnj', lm, tri)
        bsum = jnp.sum(lm, axis=-1)
        later = lax.cumsum(bsum, axis=3, reverse=True) - bsum
        between = (within + later[..., None]).reshape(B, H, Q_BLOCK, n_keys)
        w = jnp.where(causal, jnp.exp(jax.nn.log_sigmoid(z) + between), 0.0)
        o = jnp.einsum('bhts,bshd->bthd', w.astype(v.dtype), vb)
        return o.reshape(B, Q_BLOCK, H * Dh)

    return causal_blocks(block, S)


def dsa_attention(q, k, v, iq, ik, iw):
    B, S, H, Dh = q.shape
    Hkv = k.shape[2]
    G = H // Hkv
    n_top = min(DSA_TOPK, S // 4)
    scale = Dh ** -0.5
    gather = jax.vmap(lambda t, idx: t[idx])

    def block(q0, n_keys):
        tpos = q0 + jnp.arange(Q_BLOCK)
        qb = lax.dynamic_slice_in_dim(q, q0, Q_BLOCK, axis=1).reshape(B, Q_BLOCK, Hkv, G, Dh)
        iqb = lax.dynamic_slice_in_dim(iq, q0, Q_BLOCK, axis=1)
        iwb = lax.dynamic_slice_in_dim(iw, q0, Q_BLOCK, axis=1).astype(jnp.float32)
        idx = jax.nn.relu(jnp.einsum('bthd,bsd->bths', iqb, ik[:, :n_keys]).astype(jnp.float32))
        idx = jnp.einsum('bths,bth->bts', idx, iwb)
        idx = jnp.where((jnp.arange(n_keys)[None, :] <= tpos[:, None])[None], idx, -jnp.inf)
        _, sel = lax.top_k(idx, n_top)
        valid = (sel <= tpos[None, :, None])[:, :, None, None, :]
        kg = gather(k, sel)
        vg = gather(v, sel)
        s = jnp.einsum('btgrd,btkgd->btgrk', qb, kg).astype(jnp.float32) * scale
        p = masked_softmax(s, valid)
        o = jnp.einsum('btgrk,btkgd->btgrd', p.astype(v.dtype), vg)
        return o.reshape(B, Q_BLOCK, H * Dh)

    return causal_blocks(block, S)


def nsa_compress(x, blk_idx, pe, w1, w2):
    blocks = x[:, blk_idx] + pe[None, None, :, None, :]
    hid = jax.nn.gelu(jnp.einsum('bnlgd,lde->bnge', blocks, w1))
    return jnp.einsum('bnge,ef->bngf', hid, w2)


def nsa_attention(q, kc, vc, ks, vs, kw, vw, gates, kc_gain, pe, w1, w2):
    B, S, H, Dh = q.shape
    Hkv = ks.shape[2]
    G = H // Hkv
    scale = Dh ** -0.5
    n_cmp = (S - NSA_CMP_LEN) // NSA_CMP_STRIDE + 1
    cmp_start = jnp.arange(n_cmp) * NSA_CMP_STRIDE
    cmp_end = cmp_start + NSA_CMP_LEN - 1
    blk_idx = cmp_start[:, None] + jnp.arange(NSA_CMP_LEN)[None, :]
    k_cmp = rmsnorm(nsa_compress(kc, blk_idx, pe[0], w1[0], w2[0]), kc_gain)
    v_cmp = nsa_compress(vc, blk_idx, pe[1], w1[1], w2[1])
    n_slc = S // NSA_SLC_LEN
    slc_start = jnp.arange(n_slc) * NSA_SLC_LEN
    overlap = ((cmp_start[:, None] < slc_start[None, :] + NSA_SLC_LEN)
               & (cmp_end[:, None] >= slc_start[None, :])).astype(jnp.float32)
    n_sel = min(NSA_N_SEL, n_slc)
    ks_t = jnp.swapaxes(ks, 1, 2)
    vs_t = jnp.swapaxes(vs, 1, 2)
    pad = ((0, 0), (NSA_WINDOW, 0), (0, 0), (0, 0))
    kw_pad = jnp.pad(kw, pad)
    vw_pad = jnp.pad(vw, pad)
    gather = jax.vmap(jax.vmap(lambda t, idx: t[idx]))

    def block(i):
        q0 = i * Q_BLOCK
        tpos = q0 + jnp.arange(Q_BLOCK)
        qb = lax.dynamic_slice_in_dim(q, q0, Q_BLOCK, axis=1).reshape(B, Q_BLOCK, Hkv, G, Dh)
        gb = lax.dynamic_slice_in_dim(gates, q0, Q_BLOCK, axis=1).reshape(B, Q_BLOCK, Hkv, G, 3)
        s_c = jnp.einsum('btgrd,bngd->btgrn', qb, k_cmp).astype(jnp.float32) * scale
        p_c = masked_softmax(s_c, (cmp_end[None, :] <= tpos[:, None])[None, :, None, None, :])
        o_c = jnp.einsum('btgrn,bngd->btgrd', p_c.astype(vc.dtype), v_cmp)
        imp = jnp.einsum('btgrn,nm->btgm', p_c, overlap)
        cur = (tpos // NSA_SLC_LEN)[:, None]
        blk = jnp.arange(n_slc)[None, :]
        forced = (blk == 0) | (blk == cur) | (blk == cur - 1)
        admissible = slc_start[None, :] <= tpos[:, None]
        imp = jnp.where(forced[None, :, None, :], NSA_FORCED_SCORE, imp)
        imp = jnp.where(admissible[None, :, None, :], imp, -jnp.inf)
        _, sel = lax.top_k(imp, n_sel)
        tok = (sel[..., None] * NSA_SLC_LEN + jnp.arange(NSA_SLC_LEN)).reshape(
            B, Q_BLOCK, Hkv, n_sel * NSA_SLC_LEN)
        tok_t = jnp.swapaxes(tok, 1, 2)
        kg = gather(ks_t, tok_t)
        vg = gather(vs_t, tok_t)
        s_s = jnp.einsum('btgrd,bgtkd->btgrk', qb, kg).astype(jnp.float32) * scale
        p_s = masked_softmax(s_s, (tok <= tpos[None, :, None, None])[:, :, :, None, :])
        o_s = jnp.einsum('btgrk,bgtkd->btgrd', p_s.astype(vs.dtype), vg)
        wpos = q0 - NSA_WINDOW + jnp.arange(NSA_WINDOW + Q_BLOCK)
        kwb = lax.dynamic_slice_in_dim(kw_pad, q0, NSA_WINDOW + Q_BLOCK, axis=1)
        vwb = lax.dynamic_slice_in_dim(vw_pad, q0, NSA_WINDOW + Q_BLOCK, axis=1)
        dist = tpos[:, None] - wpos[None, :]
        mask_w = (dist >= 0) & (dist < NSA_WINDOW) & (wpos[None, :] >= 0)
        s_w = jnp.einsum('btgrd,bjgd->btgrj', qb, kwb).astype(jnp.float32) * scale
        p_w = masked_softmax(s_w, mask_w[None, :, None, None, :])
        o_w = jnp.einsum('btgrj,bjgd->btgrd', p_w.astype(vw.dtype), vwb)
        o = gb[..., 0:1] * o_c + gb[..., 1:2] * o_s + gb[..., 2:3] * o_w
        return o.reshape(B, Q_BLOCK, H * Dh)

    return block_map(block, S // Q_BLOCK)


def dilated_attention(q, k, v):
    B, S, NG, Hs, Dh = q.shape
    scale = Dh ** -0.5
    pad = ((0, 0), (DIL_MAX_WINDOW, 0), (0, 0), (0, 0))
    k_pad = jnp.pad(k, pad)
    v_pad = jnp.pad(v, pad)

    def to_tokens(a):
        return jnp.transpose(a[..., 0], (0, 3, 2, 1)).reshape(B, Q_BLOCK, Hs)[..., None]

    def block(i):
        q0 = i * Q_BLOCK
        qb = lax.dynamic_slice_in_dim(q, q0, Q_BLOCK, axis=1)
        nums, maxes, dens = [], [], []
        for g, (window, dilation) in enumerate(DIL_PATTERNS):
            n_q = Q_BLOCK // dilation
            n_k = (window + Q_BLOCK) // dilation
            start = q0 + DIL_MAX_WINDOW - window
            kb = lax.dynamic_slice_in_dim(k_pad, start, window + Q_BLOCK, axis=1).reshape(B, n_k, dilation, Hs, Dh)
            vb = lax.dynamic_slice_in_dim(v_pad, start, window + Q_BLOCK, axis=1).reshape(B, n_k, dilation, Hs, Dh)
            qr = qb[:, :, g].reshape(B, n_q, dilation, Hs, Dh)
            ii = jnp.arange(n_q)[:, None]
            jj = jnp.arange(n_k)[None, :]
            kpos = q0 - window + jj[None] * dilation + jnp.arange(dilation)[:, None, None]
            mask = ((jj >= ii) & (jj <= ii + window // dilation))[None] & (kpos >= 0)
            s = jnp.einsum('bicsd,bjcsd->bscij', qr, kb).astype(jnp.float32) * scale
            e, m, den = softmax_stats(s, mask[None, None])
            num = jnp.einsum('bscij,bjcsd->bicsd', e, vb.astype(jnp.float32)).reshape(B, Q_BLOCK, Hs, Dh)
            nums.append(num)
            maxes.append(to_tokens(m))
            dens.append(to_tokens(den))
        m_all = functools.reduce(jnp.maximum, maxes)
        wts = [jnp.exp(m - m_all) for m in maxes]
        num = functools.reduce(jnp.add, [w * n for w, n in zip(wts, nums)])
        den = functools.reduce(jnp.add, [w * d for w, d in zip(wts, dens)])
        o = (num / den).astype(v.dtype)
        return o.reshape(B, Q_BLOCK, Hs * Dh)

    return block_map(block, S // Q_BLOCK)


def token_mixing(xn, cos, sin, w_in, qk_gain, nsa_pe, nsa_w1, nsa_w2, gate_down, gate_up, w_branch, w_out):
    B, S, _ = xn.shape
    (a_q, a_k, a_v, b_q, b_k, b_v, b_iq, b_ik, b_iw,
     c_q, c_kc, c_vc, c_ks, c_vs, c_kw, c_vw, c_g,
     d_q, d_k, d_v) = jnp.split(xn @ w_in, SPLIT_OFFSETS, axis=-1)

    def heads(t, n):
        return t.reshape(B, S, n, -1)

    def qk(t, n, gi):
        return apply_rope(rmsnorm(heads(t, n), qk_gain[gi]), cos, sin)

    o_a = stick_breaking_attention(heads(a_q, SB_HEADS), heads(a_k, SB_HEADS), heads(a_v, SB_HEADS))
    o_b = dsa_attention(qk(b_q, DSA_HEADS, 0), qk(b_k, DSA_KV_HEADS, 1), heads(b_v, DSA_KV_HEADS),
                        heads(b_iq, DSA_IDX_HEADS), b_ik, b_iw)
    o_c = nsa_attention(qk(c_q, NSA_HEADS, 2),
                        apply_rope(heads(c_kc, NSA_KV_HEADS), cos, sin), heads(c_vc, NSA_KV_HEADS),
                        qk(c_ks, NSA_KV_HEADS, 4), heads(c_vs, NSA_KV_HEADS),
                        qk(c_kw, NSA_KV_HEADS, 5), heads(c_vw, NSA_KV_HEADS),
                        jax.nn.sigmoid(c_g).reshape(B, S, NSA_HEADS, 3),
                        qk_gain[3], nsa_pe, nsa_w1, nsa_w2)
    dq = qk(d_q, DIL_GROUPS * DIL_HEADS_PER_GROUP, 6).reshape(B, S, DIL_GROUPS, DIL_HEADS_PER_GROUP, HEAD_DIM)
    o_d = dilated_attention(dq, qk(d_k, DIL_HEADS_PER_GROUP, 7), heads(d_v, DIL_HEADS_PER_GROUP))

    z = xn @ gate_down
    merged = None
    for i, (o, wb) in enumerate(zip((o_a, o_b, o_c, o_d), jnp.split(w_branch, BRANCH_OFFSETS, axis=0))):
        gate = jax.nn.sigmoid((z @ gate_up[i]).astype(jnp.float32)).astype(xn.dtype)
        term = gate * (o @ wb)
        merged = term if merged is None else merged + term
    return merged @ w_out


def peer_ffn(xn, wq, subkeys, u, v):
    B, S, _ = xn.shape
    q = (xn @ wq).reshape(B, S, PEER_HEADS, PEER_KEY_DIM)
    q1, q2 = jnp.split(q, 2, axis=-1)
    s1 = jnp.einsum('bshd,hnd->bshn', q1, subkeys[:, 0]).astype(jnp.float32)
    s2 = jnp.einsum('bshd,hnd->bshn', q2, subkeys[:, 1]).astype(jnp.float32)
    v1, i1 = lax.top_k(s1, PEER_TOPK)
    v2, i2 = lax.top_k(s2, PEER_TOPK)
    cand_s = (v1[..., :, None] + v2[..., None, :]).reshape(B, S, PEER_HEADS, PEER_TOPK * PEER_TOPK)
    cand_e = (i1[..., :, None] * PEER_N_KEYS + i2[..., None, :]).reshape(B, S, PEER_HEADS, PEER_TOPK * PEER_TOPK)
    top_s, top_pos = lax.top_k(cand_s, PEER_TOPK)
    experts = jnp.take_along_axis(cand_e, top_pos, axis=-1).reshape(B * S, PEER_HEADS * PEER_TOPK)
    gates = jax.nn.softmax(top_s, axis=-1).reshape(B * S, PEER_HEADS * PEER_TOPK)
    act = jax.nn.gelu(jnp.einsum('bsd,ed->bse', xn, u).astype(jnp.float32)).reshape(B * S, PEER_EXPERTS)
    coef = gates * jnp.take_along_axis(act, experts, axis=-1)
    wmat = jnp.zeros((B * S, PEER_EXPERTS), jnp.float32).at[jnp.arange(B * S)[:, None], experts].add(coef)
    return (wmat.astype(v.dtype) @ v).reshape(B, S, -1)


def setup_inputs(seed: int = 0) -> dict:
    key = jax.random.key(seed)
    ks = jax.random.split(key, 16)

    def nrm(k, shape, scale):
        return jax.random.normal(k, shape, jnp.float32) * scale

    return {
        'x': nrm(ks[0], (BATCH, SEQ, D_MODEL), 1.0),
        'norm_mix': 1.0 + nrm(ks[1], (DEPTH, D_MODEL), 0.02),
        'w_in': nrm(ks[2], (DEPTH, D_MODEL, IN_WIDTH), D_MODEL ** -0.5),
        'qk_gain': 1.0 + nrm(ks[3], (DEPTH, N_QK_GAINS, HEAD_DIM), 0.02),
        'nsa_pe': nrm(ks[4], (DEPTH, 2, NSA_CMP_LEN, HEAD_DIM), 0.1),
        'nsa_w1': nrm(ks[5], (DEPTH, 2, NSA_CMP_LEN, HEAD_DIM, HEAD_DIM), (NSA_CMP_LEN * HEAD_DIM) ** -0.5),
        'nsa_w2': nrm(ks[6], (DEPTH, 2, HEAD_DIM, HEAD_DIM), HEAD_DIM ** -0.5),
        'gate_down': nrm(ks[7], (DEPTH, D_MODEL, GATE_RANK), D_MODEL ** -0.5),
        'gate_up': nrm(ks[8], (DEPTH, N_BRANCHES, GATE_RANK, D_MODEL), GATE_RANK ** -0.5),
        'w_branch': nrm(ks[9], (DEPTH, BRANCH_WIDTH, D_MODEL), SB_W ** -0.5),
        'w_out': nrm(ks[10], (DEPTH, D_MODEL, D_MODEL), D_MODEL ** -0.5),
        'norm_ffn': 1.0 + nrm(ks[11], (DEPTH, D_MODEL), 0.02),
        'peer_wq': nrm(ks[12], (DEPTH, D_MODEL, PEER_HEADS * PEER_KEY_DIM), D_MODEL ** -0.5),
        'peer_subkeys': nrm(ks[13], (DEPTH, PEER_HEADS, 2, PEER_N_KEYS, PEER_KEY_DIM // 2), (PEER_KEY_DIM // 2) ** -0.5),
        'peer_u': nrm(ks[14], (DEPTH, PEER_EXPERTS, D_MODEL), D_MODEL ** -0.5),
        'peer_v': nrm(ks[15], (DEPTH, PEER_EXPERTS, D_MODEL), (PEER_HEADS * PEER_TOPK) ** -0.5),
    }


def reference(x, norm_mix, w_in, qk_gain, nsa_pe, nsa_w1, nsa_w2, gate_down, gate_up, w_branch, w_out,
              norm_ffn, peer_wq, peer_subkeys, peer_u, peer_v):
    cos, sin = rope_tables(x.shape[1], x.dtype)
    h = x
    for l in range(DEPTH):
        xn = rmsnorm(h, norm_mix[l])
        h = h + token_mixing(xn, cos, sin, w_in[l], qk_gain[l], nsa_pe[l], nsa_w1[l], nsa_w2[l],
                             gate_down[l], gate_up[l], w_branch[l], w_out[l])
        h = h + peer_ffn(rmsnorm(h, norm_ffn[l]), peer_wq[l], peer_subkeys[l], peer_u[l], peer_v[l])
    return h
```

```python
import math, functools
import jax, jax.numpy as jnp
from jax import lax
import numpy as np
from jax.experimental import pallas as pl
from jax.experimental.pallas import tpu as pltpu


def _mm_kernel(a_ref, b_ref, o_ref):
    o_ref[...] = jnp.dot(a_ref[...], b_ref[...], preferred_element_type=jnp.float32)


def _pick(n, prefs):
    for p in prefs:
        if n % p == 0:
            return p
    return n


def _mm(a, b):
    lead = a.shape[:-1]
    K = a.shape[-1]
    N = b.shape[-1]
    a2 = a.reshape(-1, K).astype(jnp.bfloat16)
    M = a2.shape[0]
    n_pad = (-N) % 128
    b2 = b.astype(jnp.bfloat16)
    if n_pad:
        b2 = jnp.pad(b2, ((0, 0), (0, n_pad)))
    Np = N + n_pad
    tm = _pick(M, (1024, 512, 256, 128))
    tn = _pick(Np, (512, 256, 128))
    out = pl.pallas_call(
        _mm_kernel,
        grid=(M // tm, Np // tn),
        in_specs=[pl.BlockSpec((tm, K), lambda i, j: (i, 0)),
                  pl.BlockSpec((K, tn), lambda i, j: (0, j))],
        out_specs=pl.BlockSpec((tm, tn), lambda i, j: (i, j)),
        out_shape=jax.ShapeDtypeStruct((M, Np), jnp.float32),
        compiler_params=pltpu.CompilerParams(
            dimension_semantics=("arbitrary", "arbitrary"),
            vmem_limit_bytes=48 * 1024 * 1024),
    )(a2, b2)
    if n_pad:
        out = out[:, :N]
    return out.reshape(*lead, N)


D_MODEL = 4096
BATCH = 1
SEQ = 16384
DEPTH = 4

HEAD_DIM = 128
Q_BLOCK = 128
SUPER_LEN = 1024
ROPE_THETA = 10000.0
NORM_EPS = 1e-6
SB_HEADS = 4
DSA_HEADS = 4
DSA_KV_HEADS = 1
DSA_IDX_HEADS = 4
DSA_IDX_DIM = 64
DSA_TOPK = 256
NSA_HEADS = 4
NSA_KV_HEADS = 1
NSA_CMP_LEN = 32
NSA_CMP_STRIDE = 16
NSA_SLC_LEN = 32
NSA_N_SEL = 8
NSA_WINDOW = 512
NSA_FORCED_SCORE = 1e4
DIL_PATTERNS = ((128, 1), (512, 4), (2048, 16))
DIL_GROUPS = len(DIL_PATTERNS)
DIL_HEADS_PER_GROUP = 2
DIL_MAX_WINDOW = max(w for w, _ in DIL_PATTERNS)
N_BRANCHES = 4
GATE_RANK = 256
PEER_HEADS = 8
PEER_N_KEYS = 64
PEER_EXPERTS = PEER_N_KEYS ** 2
PEER_KEY_DIM = 128
PEER_TOPK = 16

SB_W = SB_HEADS * HEAD_DIM
DSA_Q_W = DSA_HEADS * HEAD_DIM
DSA_KV_W = DSA_KV_HEADS * HEAD_DIM
DSA_IQ_W = DSA_IDX_HEADS * DSA_IDX_DIM
NSA_Q_W = NSA_HEADS * HEAD_DIM
NSA_KV_W = NSA_KV_HEADS * HEAD_DIM
DIL_Q_W = DIL_GROUPS * DIL_HEADS_PER_GROUP * HEAD_DIM
DIL_KV_W = DIL_HEADS_PER_GROUP * HEAD_DIM
IN_SPLITS = (SB_W, SB_W, SB_W,
             DSA_Q_W, DSA_KV_W, DSA_KV_W, DSA_IQ_W, DSA_IDX_DIM, DSA_IDX_HEADS,
             NSA_Q_W, NSA_KV_W, NSA_KV_W, NSA_KV_W, NSA_KV_W, NSA_KV_W, NSA_KV_W, NSA_HEADS * 3,
             DIL_Q_W, DIL_KV_W, DIL_KV_W)
IN_WIDTH = sum(IN_SPLITS)
SPLIT_OFFSETS = tuple(int(o) for o in np.cumsum(IN_SPLITS)[:-1])
BRANCH_WIDTHS = (SB_W, DSA_Q_W, NSA_Q_W, DIL_KV_W)
BRANCH_WIDTH = sum(BRANCH_WIDTHS)
BRANCH_OFFSETS = tuple(int(o) for o in np.cumsum(BRANCH_WIDTHS)[:-1])
N_QK_GAINS = 8


def rmsnorm(x, g):
    xf = x.astype(jnp.float32)
    y = xf * lax.rsqrt(jnp.mean(xf * xf, axis=-1, keepdims=True) + NORM_EPS)
    return (y * g.astype(jnp.float32)).astype(x.dtype)


def rope_tables(seq, dtype):
    inv = ROPE_THETA ** (-jnp.arange(0, HEAD_DIM, 2, dtype=jnp.float32) / HEAD_DIM)
    ang = jnp.arange(seq, dtype=jnp.float32)[:, None] * inv[None, :]
    return jnp.cos(ang)[None, :, None, :].astype(dtype), jnp.sin(ang)[None, :, None, :].astype(dtype)


def apply_rope(x, cos, sin):
    x1, x2 = jnp.split(x, 2, axis=-1)
    return jnp.concatenate([x1 * cos - x2 * sin, x2 * cos + x1 * sin], axis=-1)


def softmax_stats(s, mask):
    s = jnp.where(mask, s, -jnp.inf)
    m = jnp.max(s, axis=-1, keepdims=True)
    m = jnp.where(jnp.isfinite(m), m, 0.0)
    e = jnp.where(mask, jnp.exp(s - m), 0.0)
    return e, m, jnp.sum(e, axis=-1, keepdims=True)


def masked_softmax(s, mask):
    e, _, den = softmax_stats(s, mask)
    return e / jnp.maximum(den, jnp.finfo(jnp.float32).tiny)


def block_map(fn, n_blocks):
    out = lax.map(fn, jnp.arange(n_blocks))
    nb, b, t, f = out.shape
    return jnp.swapaxes(out, 0, 1).reshape(b, nb * t, f)


def causal_blocks(fn, seq):
    outs = []
    for start in range(0, seq, SUPER_LEN):
        stop = min(start + SUPER_LEN, seq)
        out = lax.map(lambda i: fn(start + i * Q_BLOCK, stop), jnp.arange((stop - start) // Q_BLOCK))
        nb, b, t, f = out.shape
        outs.append(jnp.swapaxes(out, 0, 1).reshape(b, nb * t, f))
    return jnp.concatenate(outs, axis=1)


def stick_breaking_attention(q, k, v):
    B, S, H, Dh = q.shape
    scale = Dh ** -0.5
    tri = jnp.tril(jnp.ones((Q_BLOCK, Q_BLOCK), jnp.float32), -1)

    def block(q0, n_keys):
        qb = lax.dynamic_slice_in_dim(q, q0, Q_BLOCK, axis=1)
        kb = k[:, :n_keys]
        vb = v[:, :n_keys]
        tpos = q0 + jnp.arange(Q_BLOCK)
        causal = (jnp.arange(n_keys)[None, :] < tpos[:, None])[None, None]
        z = jnp.einsum('bthd,bshd->bhts', qb, kb).astype(jnp.float32) * scale
        log_1m = jnp.where(causal, jax.nn.log_sigmoid(-z), 0.0)
        lm = log_1m.reshape(B, H, Q_BLOCK, n_keys // Q_BLOCK, Q_BLOCK)
        within = jnp.einsum('bhtnk,kj->bhtnj', lm, tri)
        bsum = jnp.sum(lm, axis=-1)
        later = lax.cumsum(bsum, axis=3, reverse=True) - bsum
        between = (within + later[..., None]).reshape(B, H, Q_BLOCK, n_keys)
        w = jnp.where(causal, jnp.exp(jax.nn.log_sigmoid(z) + between), 0.0)
        o = jnp.einsum('bhts,bshd->bthd', w.astype(v.dtype), vb)
        return o.reshape(B, Q_BLOCK, H * Dh)

    return causal_blocks(block, S)


def dsa_attention(q, k, v, iq, ik, iw):
    B, S, H, Dh = q.shape
    Hkv = k.shape[2]
    G = H // Hkv
    n_top = min(DSA_TOPK, S // 4)
    scale = Dh ** -0.5
    gather = jax.vmap(lambda t, idx: t[idx])

    def block(q0, n_keys):
        tpos = q0 + jnp.arange(Q_BLOCK)
        qb = lax.dynamic_slice_in_dim(q, q0, Q_BLOCK, axis=1).reshape(B, Q_BLOCK, Hkv, G, Dh)
        iqb = lax.dynamic_slice_in_dim(iq, q0, Q_BLOCK, axis=1)
        iwb = lax.dynamic_slice_in_dim(iw, q0, Q_BLOCK, axis=1).astype(jnp.float32)
        idx = jax.nn.relu(jnp.einsum('bthd,bsd->bths', iqb, ik[:, :n_keys]).astype(jnp.float32))
        idx = jnp.einsum('bths,bth->bts', idx, iwb)
        idx = jnp.where((jnp.arange(n_keys)[None, :] <= tpos[:, None])[None], idx, -jnp.inf)
        _, sel = lax.top_k(idx, n_top)
        valid = (sel <= tpos[None, :, None])[:, :, None, None, :]
        kg = gather(k, sel)
        vg = gather(v, sel)
        s = jnp.einsum('btgrd,btkgd->btgrk', qb, kg).astype(jnp.float32) * scale
        p = masked_softmax(s, valid)
        o = jnp.einsum('btgrk,btkgd->btgrd', p.astype(v.dtype), vg)
        return o.reshape(B, Q_BLOCK, H * Dh)

    return causal_blocks(block, S)


def nsa_compress(x, blk_idx, pe, w1, w2):
    blocks = x[:, blk_idx] + pe[None, None, :, None, :]
    hid = jax.nn.gelu(jnp.einsum('bnlgd,lde->bnge', blocks, w1))
    return jnp.einsum('bnge,ef->bngf', hid, w2)


def nsa_attention(q, kc, vc, ks, vs, kw, vw, gates, kc_gain, pe, w1, w2):
    B, S, H, Dh = q.shape
    Hkv = ks.shape[2]
    G = H // Hkv
    scale = Dh ** -0.5
    n_cmp = (S - NSA_CMP_LEN) // NSA_CMP_STRIDE + 1
    cmp_start = jnp.arange(n_cmp) * NSA_CMP_STRIDE
    cmp_end = cmp_start + NSA_CMP_LEN - 1
    blk_idx = cmp_start[:, None] + jnp.arange(NSA_CMP_LEN)[None, :]
    k_cmp = rmsnorm(nsa_compress(kc, blk_idx, pe[0], w1[0], w2[0]), kc_gain)
    v_cmp = nsa_compress(vc, blk_idx, pe[1], w1[1], w2[1])
    n_slc = S // NSA_SLC_LEN
    slc_start = jnp.arange(n_slc) * NSA_SLC_LEN
    overlap = ((cmp_start[:, None] < slc_start[None, :] + NSA_SLC_LEN)
               & (cmp_end[:, None] >= slc_start[None, :])).astype(jnp.float32)
    n_sel = min(NSA_N_SEL, n_slc)
    ks_t = jnp.swapaxes(ks, 1, 2)
    vs_t = jnp.swapaxes(vs, 1, 2)
    pad = ((0, 0), (NSA_WINDOW, 0), (0, 0), (0, 0))
    kw_pad = jnp.pad(kw, pad)
    vw_pad = jnp.pad(vw, pad)
    gather = jax.vmap(jax.vmap(lambda t, idx: t[idx]))

    def block(i):
        q0 = i * Q_BLOCK
        tpos = q0 + jnp.arange(Q_BLOCK)
        qb = lax.dynamic_slice_in_dim(q, q0, Q_BLOCK, axis=1).reshape(B, Q_BLOCK, Hkv, G, Dh)
        gb = lax.dynamic_slice_in_dim(gates, q0, Q_BLOCK, axis=1).reshape(B, Q_BLOCK, Hkv, G, 3)
        s_c = jnp.einsum('btgrd,bngd->btgrn', qb, k_cmp).astype(jnp.float32) * scale
        p_c = masked_softmax(s_c, (cmp_end[None, :] <= tpos[:, None])[None, :, None, None, :])
        o_c = jnp.einsum('btgrn,bngd->btgrd', p_c.astype(vc.dtype), v_cmp)
        imp = jnp.einsum('btgrn,nm->btgm', p_c, overlap)
        cur = (tpos // NSA_SLC_LEN)[:, None]
        blk = jnp.arange(n_slc)[None, :]
        forced = (blk == 0) | (blk == cur) | (blk == cur - 1)
        admissible = slc_start[None, :] <= tpos[:, None]
        imp = jnp.where(forced[None, :, None, :], NSA_FORCED_SCORE, imp)
        imp = jnp.where(admissible[None, :, None, :], imp, -jnp.inf)
        _, sel = lax.top_k(imp, n_sel)
        tok = (sel[..., None] * NSA_SLC_LEN + jnp.arange(NSA_SLC_LEN)).reshape(
            B, Q_BLOCK, Hkv, n_sel * NSA_SLC_LEN)
        tok_t = jnp.swapaxes(tok, 1, 2)
        kg = gather(ks_t, tok_t)
        vg = gather(vs_t, tok_t)
        s_s = jnp.einsum('btgrd,bgtkd->btgrk', qb, kg).astype(jnp.float32) * scale
        p_s = masked_softmax(s_s, (tok <= tpos[None, :, None, None])[:, :, :, None, :])
        o_s = jnp.einsum('btgrk,bgtkd->btgrd', p_s.astype(vs.dtype), vg)
        wpos = q0 - NSA_WINDOW + jnp.arange(NSA_WINDOW + Q_BLOCK)
        kwb = lax.dynamic_slice_in_dim(kw_pad, q0, NSA_WINDOW + Q_BLOCK, axis=1)
        vwb = lax.dynamic_slice_in_dim(vw_pad, q0, NSA_WINDOW + Q_BLOCK, axis=1)
        dist = tpos[:, None] - wpos[None, :]
        mask_w = (dist >= 0) & (dist < NSA_WINDOW) & (wpos[None, :] >= 0)
        s_w = jnp.einsum('btgrd,bjgd->btgrj', qb, kwb).astype(jnp.float32) * scale
        p_w = masked_softmax(s_w, mask_w[None, :, None, None, :])
        o_w = jnp.einsum('btgrj,bjgd->btgrd', p_w.astype(vw.dtype), vwb)
        o = gb[..., 0:1] * o_c + gb[..., 1:2] * o_s + gb[..., 2:3] * o_w
        return o.reshape(B, Q_BLOCK, H * Dh)

    return block_map(block, S // Q_BLOCK)


def dilated_attention(q, k, v):
    B, S, NG, Hs, Dh = q.shape
    scale = Dh ** -0.5
    pad = ((0, 0), (DIL_MAX_WINDOW, 0), (0, 0), (0, 0))
    k_pad = jnp.pad(k, pad)
    v_pad = jnp.pad(v, pad)

    def to_tokens(a):
        return jnp.transpose(a[..., 0], (0, 3, 2, 1)).reshape(B, Q_BLOCK, Hs)[..., None]

    def block(i):
        q0 = i * Q_BLOCK
        qb = lax.dynamic_slice_in_dim(q, q0, Q_BLOCK, axis=1)
        nums, maxes, dens = [], [], []
        for g, (window, dilation) in enumerate(DIL_PATTERNS):
            n_q = Q_BLOCK // dilation
            n_k = (window + Q_BLOCK) // dilation
            start = q0 + DIL_MAX_WINDOW - window
            kb = lax.dynamic_slice_in_dim(k_pad, start, window + Q_BLOCK, axis=1).reshape(B, n_k, dilation, Hs, Dh)
            vb = lax.dynamic_slice_in_dim(v_pad, start, window + Q_BLOCK, axis=1).reshape(B, n_k, dilation, Hs, Dh)
            qr = qb[:, :, g].reshape(B, n_q, dilation, Hs, Dh)
            ii = jnp.arange(n_q)[:, None]
            jj = jnp.arange(n_k)[None, :]
            kpos = q0 - window + jj[None] * dilation + jnp.arange(dilation)[:, None, None]
            mask = ((jj >= ii) & (jj <= ii + window // dilation))[None] & (kpos >= 0)
            s = jnp.einsum('bicsd,bjcsd->bscij', qr, kb).astype(jnp.float32) * scale
            e, m, den = softmax_stats(s, mask[None, None])
            num = jnp.einsum('bscij,bjcsd->bicsd', e, vb.astype(jnp.float32)).reshape(B, Q_BLOCK, Hs, Dh)
            nums.append(num)
            maxes.append(to_tokens(m))
            dens.append(to_tokens(den))
        m_all = functools.reduce(jnp.maximum, maxes)
        wts = [jnp.exp(m - m_all) for m in maxes]
        num = functools.reduce(jnp.add, [w * n for w, n in zip(wts, nums)])
        den = functools.reduce(jnp.add, [w * d for w, d in zip(wts, dens)])
        o = (num / den).astype(v.dtype)
        return o.reshape(B, Q_BLOCK, Hs * Dh)

    return block_map(block, S // Q_BLOCK)


def token_mixing(xn, cos, sin, w_in, qk_gain, nsa_pe, nsa_w1, nsa_w2, gate_down, gate_up, w_branch, w_out):
    B, S, _ = xn.shape
    (a_q, a_k, a_v, b_q, b_k, b_v, b_iq, b_ik, b_iw,
     c_q, c_kc, c_vc, c_ks, c_vs, c_kw, c_vw, c_g,
     d_q, d_k, d_v) = jnp.split(_mm(xn, w_in), SPLIT_OFFSETS, axis=-1)

    def heads(t, n):
        return t.reshape(B, S, n, -1)

    def qk(t, n, gi):
        return apply_rope(rmsnorm(heads(t, n), qk_gain[gi]), cos, sin)

    o_a = stick_breaking_attention(heads(a_q, SB_HEADS), heads(a_k, SB_HEADS), heads(a_v, SB_HEADS))
    o_b = dsa_attention(qk(b_q, DSA_HEADS, 0), qk(b_k, DSA_KV_HEADS, 1), heads(b_v, DSA_KV_HEADS),
                        heads(b_iq, DSA_IDX_HEADS), b_ik, b_iw)
    o_c = nsa_attention(qk(c_q, NSA_HEADS, 2),
                        apply_rope(heads(c_kc, NSA_KV_HEADS), cos, sin), heads(c_vc, NSA_KV_HEADS),
                        qk(c_ks, NSA_KV_HEADS, 4), heads(c_vs, NSA_KV_HEADS),
                        qk(c_kw, NSA_KV_HEADS, 5), heads(c_vw, NSA_KV_HEADS),
                        jax.nn.sigmoid(c_g).reshape(B, S, NSA_HEADS, 3),
                        qk_gain[3], nsa_pe, nsa_w1, nsa_w2)
    dq = qk(d_q, DIL_GROUPS * DIL_HEADS_PER_GROUP, 6).reshape(B, S, DIL_GROUPS, DIL_HEADS_PER_GROUP, HEAD_DIM)
    o_d = dilated_attention(dq, qk(d_k, DIL_HEADS_PER_GROUP, 7), heads(d_v, DIL_HEADS_PER_GROUP))

    z = _mm(xn, gate_down)
    merged = None
    for i, (o, wb) in enumerate(zip((o_a, o_b, o_c, o_d), jnp.split(w_branch, BRANCH_OFFSETS, axis=0))):
        gate = jax.nn.sigmoid(_mm(z, gate_up[i]).astype(jnp.float32)).astype(xn.dtype)
        term = gate * _mm(o, wb)
        merged = term if merged is None else merged + term
    return _mm(merged, w_out)


def peer_ffn(xn, wq, subkeys, u, v):
    B, S, _ = xn.shape
    q = _mm(xn, wq).reshape(B, S, PEER_HEADS, PEER_KEY_DIM)
    q1, q2 = jnp.split(q, 2, axis=-1)
    s1 = jnp.einsum('bshd,hnd->bshn', q1, subkeys[:, 0]).astype(jnp.float32)
    s2 = jnp.einsum('bshd,hnd->bshn', q2, subkeys[:, 1]).astype(jnp.float32)
    v1, i1 = lax.top_k(s1, PEER_TOPK)
    v2, i2 = lax.top_k(s2, PEER_TOPK)
    cand_s = (v1[..., :, None] + v2[..., None, :]).reshape(B, S, PEER_HEADS, PEER_TOPK * PEER_TOPK)
    cand_e = (i1[..., :, None] * PEER_N_KEYS + i2[..., None, :]).reshape(B, S, PEER_HEADS, PEER_TOPK * PEER_TOPK)
    top_s, top_pos = lax.top_k(cand_s, PEER_TOPK)
    experts = jnp.take_along_axis(cand_e, top_pos, axis=-1).reshape(B * S, PEER_HEADS * PEER_TOPK)
    gates = jax.nn.softmax(top_s, axis=-1).reshape(B * S, PEER_HEADS * PEER_TOPK)
    act = jax.nn.gelu(_mm(xn, u.T).astype(jnp.float32)).reshape(B * S, PEER_EXPERTS)
    coef = gates * jnp.take_along_axis(act, experts, axis=-1)
    wmat = jnp.zeros((B * S, PEER_EXPERTS), jnp.float32).at[jnp.arange(B * S)[:, None], experts].add(coef)
    return _mm(wmat.astype(v.dtype), v).reshape(B, S, -1)


def kernel(x, norm_mix, w_in, qk_gain, nsa_pe, nsa_w1, nsa_w2, gate_down, gate_up, w_branch, w_out,
           norm_ffn, peer_wq, peer_subkeys, peer_u, peer_v):
    cos, sin = rope_tables(x.shape[1], x.dtype)
    h = x
    for l in range(DEPTH):
        xn = rmsnorm(h, norm_mix[l])
        h = h + token_mixing(xn, cos, sin, w_in[l], qk_gain[l], nsa_pe[l], nsa_w1[l], nsa_w2[l],
                             gate_down[l], gate_up[l], w_branch[l], w_out[l])
        h = h + peer_ffn(rmsnorm(h, norm_ffn[l]), peer_wq[l], peer_subkeys[l], peer_u[l], peer_v[l])
    return h
```

```python
import math, functools
import jax, jax.numpy as jnp
from jax import lax
import numpy as np
from jax.experimental import pallas as pl
from jax.experimental.pallas import tpu as pltpu

D_MODEL = 4096
BATCH = 1
SEQ = 16384
DEPTH = 4

HEAD_DIM = 128
ROPE_THETA = 10000.0
NORM_EPS = 1e-6
SB_HEADS = 4
DSA_HEADS = 4
DSA_KV_HEADS = 1
DSA_IDX_HEADS = 4
DSA_IDX_DIM = 64
DSA_TOPK = 256
NSA_HEADS = 4
NSA_KV_HEADS = 1
NSA_CMP_LEN = 32
NSA_CMP_STRIDE = 16
NSA_SLC_LEN = 32
NSA_N_SEL = 8
NSA_WINDOW = 512
NSA_FORCED_SCORE = 1e4
DIL_PATTERNS = ((128, 1), (512, 4), (2048, 16))
DIL_GROUPS = len(DIL_PATTERNS)
DIL_HEADS_PER_GROUP = 2
DIL_MAX_WINDOW = max(w for w, _ in DIL_PATTERNS)
N_BRANCHES = 4
GATE_RANK = 256
PEER_HEADS = 8
PEER_N_KEYS = 64
PEER_EXPERTS = PEER_N_KEYS ** 2
PEER_KEY_DIM = 128
PEER_TOPK = 16

SB_W = SB_HEADS * HEAD_DIM
DSA_Q_W = DSA_HEADS * HEAD_DIM
DSA_KV_W = DSA_KV_HEADS * HEAD_DIM
DSA_IQ_W = DSA_IDX_HEADS * DSA_IDX_DIM
NSA_Q_W = NSA_HEADS * HEAD_DIM
NSA_KV_W = NSA_KV_HEADS * HEAD_DIM
DIL_Q_W = DIL_GROUPS * DIL_HEADS_PER_GROUP * HEAD_DIM
DIL_KV_W = DIL_HEADS_PER_GROUP * HEAD_DIM
IN_SPLITS = (SB_W, SB_W, SB_W,
             DSA_Q_W, DSA_KV_W, DSA_KV_W, DSA_IQ_W, DSA_IDX_DIM, DSA_IDX_HEADS,
             NSA_Q_W, NSA_KV_W, NSA_KV_W, NSA_KV_W, NSA_KV_W, NSA_KV_W, NSA_KV_W, NSA_HEADS * 3,
             DIL_Q_W, DIL_KV_W, DIL_KV_W)
IN_WIDTH = sum(IN_SPLITS)
SPLIT_OFFSETS = tuple(int(o) for o in np.cumsum(IN_SPLITS)[:-1])
BRANCH_WIDTHS = (SB_W, DSA_Q_W, NSA_Q_W, DIL_KV_W)
BRANCH_WIDTH = sum(BRANCH_WIDTHS)
BRANCH_OFFSETS = tuple(int(o) for o in np.cumsum(BRANCH_WIDTHS)[:-1])
N_QK_GAINS = 8

ATT_SCALE = HEAD_DIM ** -0.5
MASKED = -1e30
EXTRACTED = -3e38
INT32_MIN = -2 ** 31
F32_TINY = float(np.finfo(np.float32).tiny)
VMEM_LIMIT = 56 * 1024 * 1024

BF16 = jnp.bfloat16
F32 = jnp.float32


def _params(n_grid):
    return pltpu.CompilerParams(dimension_semantics=("arbitrary",) * n_grid, vmem_limit_bytes=VMEM_LIMIT)


def _resident(shape):
    nd = len(shape)
    return pl.BlockSpec(shape, lambda *_: (0,) * nd, pipeline_mode=pl.Buffered(1))


def _dot_nt(a, b):
    return lax.dot_general(a, b, (((1,), (1,)), ((), ())), preferred_element_type=F32)


def _dot(a, b):
    return jnp.dot(a, b, preferred_element_type=F32)


def _mm_kernel(a_ref, b_ref, o_ref):
    o_ref[...] = jnp.dot(a_ref[...], b_ref[...], preferred_element_type=jnp.float32)


def _pick(n, prefs):
    for p in prefs:
        if n % p == 0:
            return p
    return n


def _mm(a, b):
    lead = a.shape[:-1]
    K = a.shape[-1]
    N = b.shape[-1]
    a2 = a.reshape(-1, K).astype(jnp.bfloat16)
    M = a2.shape[0]
    n_pad = (-N) % 128
    b2 = b.astype(jnp.bfloat16)
    if n_pad:
        b2 = jnp.pad(b2, ((0, 0), (0, n_pad)))
    Np = N + n_pad
    tm = _pick(M, (1024, 512, 256, 128))
    tn = _pick(Np, (512, 256, 128))
    out = pl.pallas_call(
        _mm_kernel,
        grid=(M // tm, Np // tn),
        in_specs=[pl.BlockSpec((tm, K), lambda i, j: (i, 0)),
                  pl.BlockSpec((K, tn), lambda i, j: (0, j))],
        out_specs=pl.BlockSpec((tm, tn), lambda i, j: (i, j)),
        out_shape=jax.ShapeDtypeStruct((M, Np), jnp.float32),
        compiler_params=pltpu.CompilerParams(
            dimension_semantics=("arbitrary", "arbitrary"),
            vmem_limit_bytes=48 * 1024 * 1024),
        name="dense_mm",
    )(a2, b2)
    if n_pad:
        out = out[:, :N]
    return out.reshape(*lead, N)


def rmsnorm(x, g):
    xf = x.astype(jnp.float32)
    y = xf * lax.rsqrt(jnp.mean(xf * xf, axis=-1, keepdims=True) + NORM_EPS)
    return (y * g.astype(jnp.float32)).astype(x.dtype)


def rope_tables(seq, dtype):
    inv = ROPE_THETA ** (-jnp.arange(0, HEAD_DIM, 2, dtype=jnp.float32) / HEAD_DIM)
    ang = jnp.arange(seq, dtype=jnp.float32)[:, None] * inv[None, :]
    return jnp.cos(ang)[None, :, None, :].astype(dtype), jnp.sin(ang)[None, :, None, :].astype(dtype)


def apply_rope(x, cos, sin):
    x1, x2 = jnp.split(x, 2, axis=-1)
    return jnp.concatenate([x1 * cos - x2 * sin, x2 * cos + x1 * sin], axis=-1)


def _stack_heads(q_ref, n_heads):
    return jnp.concatenate([q_ref[:, r * HEAD_DIM:(r + 1) * HEAD_DIM] for r in range(n_heads)], axis=0)


def _tile_rows(x, n):
    return jnp.concatenate([x] * n, axis=0)


def _online_softmax_step(s, keep, v_chunk, m_scr, l_scr, acc_scr):
    sm = jnp.where(keep > 0.5, s, MASKED)
    m_old = m_scr[...]
    m_new = jnp.maximum(m_old, jnp.max(sm, axis=1, keepdims=True))
    p = jnp.exp(sm - m_new) * keep
    alpha = jnp.exp(m_old - m_new)
    l_scr[...] = alpha * l_scr[...] + jnp.sum(p, axis=1, keepdims=True)
    acc_scr[...] = alpha * acc_scr[...] + _dot(p.astype(BF16), v_chunk)
    m_scr[...] = m_new


def _masked_softmax(s, mask):
    sm = jnp.where(mask, s, MASKED)
    m = jnp.max(sm, axis=1, keepdims=True)
    e = jnp.where(mask, jnp.exp(sm - m), 0.0)
    den = jnp.sum(e, axis=1, keepdims=True)
    return e / jnp.maximum(den, F32_TINY)


SB_T = 256


def _sb_kernel(q_ref, k_ref, v_ref, tri_ref, o_ref):
    i = pl.program_id(1)
    T = SB_T
    q0 = i * T
    q = q_ref[...]
    tri = tri_ref[...]
    tpos = q0 + lax.broadcasted_iota(jnp.int32, (T, 1), 0)

    def body(n, carry):
        later, acc = carry
        s0 = pl.multiple_of((i - n) * T, T)
        kc = k_ref[pl.ds(s0, T), :]
        vc = v_ref[pl.ds(s0, T), :]
        z = _dot_nt(q, kc) * ATT_SCALE
        spos = s0 + lax.broadcasted_iota(jnp.int32, (T, T), 1)
        causal = spos < tpos
        lm = jnp.where(causal, -(jnp.maximum(z, 0.0) + jnp.log1p(jnp.exp(-jnp.abs(z)))), 0.0)
        hi = lm.astype(BF16)
        lo = (lm - hi.astype(F32)).astype(BF16)
        within = _dot(hi, tri) + _dot(lo, tri)
        w = jnp.where(causal, jnp.exp(z + lm + within + later), 0.0)
        acc = acc + _dot(w.astype(BF16), vc)
        later = later + jnp.sum(lm, axis=1, keepdims=True)
        return later, acc

    _, acc = lax.fori_loop(0, i + 1, body, (jnp.zeros((T, 1), F32), jnp.zeros((T, HEAD_DIM), F32)))
    o_ref[...] = acc.astype(o_ref.dtype)


def sb_attention(q, k, v):
    S = q.shape[0]
    T = SB_T
    tri = jnp.tril(jnp.ones((T, T), F32), -1).astype(BF16)
    return pl.pallas_call(
        _sb_kernel,
        grid=(SB_HEADS, S // T),
        in_specs=[pl.BlockSpec((T, HEAD_DIM), lambda h, i: (i, h)),
                  pl.BlockSpec((S, HEAD_DIM), lambda h, i: (0, h)),
                  pl.BlockSpec((S, HEAD_DIM), lambda h, i: (0, h)),
                  pl.BlockSpec((T, T), lambda h, i: (0, 0))],
        out_specs=pl.BlockSpec((T, HEAD_DIM), lambda h, i: (i, h)),
        out_shape=jax.ShapeDtypeStruct((S, SB_W), BF16),
        compiler_params=_params(2),
        name="sb_attention",
    )(q, k, v, tri)


DSA_T = 128
DSA_C = 512


def _dsa_kernel(n_top, q_ref, iq_ref, iw_ref, k_ref, v_ref, ik_ref, tri_ref, o_ref,
                key_scr, m_scr, l_scr, acc_scr):
    i = pl.program_id(0)
    T, C = DSA_T, DSA_C
    H = DSA_HEADS
    q0 = i * T
    n_chunks = (q0 + T + C - 1) // C
    tpos = q0 + lax.broadcasted_iota(jnp.int32, (T, 1), 0)
    iw = iw_ref[...]

    def idx_body(c, carry):
        s0 = pl.multiple_of(c * C, C)
        ikc = ik_ref[pl.ds(s0, C), :]
        acc = jnp.zeros((T, C), F32)
        for h in range(DSA_IDX_HEADS):
            d = _dot_nt(iq_ref[:, h * DSA_IDX_DIM:(h + 1) * DSA_IDX_DIM], ikc)
            acc = acc + jnp.maximum(d, 0.0) * iw[:, h:h + 1]
        bits = lax.bitcast_convert_type(acc, jnp.int32)
        key = jnp.where(bits < 0, bits ^ jnp.int32(0x7FFFFFFF), bits)
        key = jnp.where(acc == 0.0, 0, key)
        spos = s0 + lax.broadcasted_iota(jnp.int32, (T, C), 1)
        key_scr[c] = jnp.where(spos <= tpos, key, INT32_MIN)
        return carry

    lax.fori_loop(0, n_chunks, idx_body, 0)

    def count_ge(cand):
        def body(c, acc):
            kk = key_scr[c]
            for u in range(C // 128):
                acc = acc + jnp.where(kk[:, u * 128:(u + 1) * 128] >= cand, 1.0, 0.0)
            return acc
        acc = lax.fori_loop(0, n_chunks, body, jnp.zeros((T, 128), F32))
        return jnp.sum(acc, axis=1, keepdims=True)

    def bis_body(b, base):
        cand = base + (jnp.int32(1) << (jnp.int32(31) - b))
        return jnp.where(count_ge(cand) >= n_top, cand, base)

    thr = lax.fori_loop(0, 32, bis_body, jnp.full((T, 128), INT32_MIN, jnp.int32))
    need = n_top - count_ge(thr + 1)
    thr1 = thr[:, 0:1]

    m_scr[...] = jnp.full(m_scr.shape, MASKED, F32)
    l_scr[...] = jnp.zeros(l_scr.shape, F32)
    acc_scr[...] = jnp.zeros(acc_scr.shape, F32)
    qs = _stack_heads(q_ref, H)
    tri = tri_ref[...]

    def att_body(c, eq_seen):
        s0 = pl.multiple_of(c * C, C)
        kk = key_scr[c]
        eq = jnp.where(kk == thr1, 1.0, 0.0)
        pref = eq_seen + _dot(eq.astype(BF16), tri)
        spos = s0 + lax.broadcasted_iota(jnp.int32, (T, C), 1)
        take = (kk > thr1) | ((eq > 0.5) & (pref <= need))
        keep = jnp.where(take & (spos <= tpos), 1.0, 0.0)
        s = _dot_nt(qs, k_ref[pl.ds(s0, C), :]) * ATT_SCALE
        _online_softmax_step(s, _tile_rows(keep, H), v_ref[pl.ds(s0, C), :], m_scr, l_scr, acc_scr)
        return eq_seen + jnp.sum(eq, axis=1, keepdims=True)

    lax.fori_loop(0, n_chunks, att_body, jnp.zeros((T, 1), F32))
    out = acc_scr[...] / jnp.maximum(l_scr[...], F32_TINY)
    for r in range(H):
        o_ref[:, r * HEAD_DIM:(r + 1) * HEAD_DIM] = out[r * T:(r + 1) * T].astype(o_ref.dtype)


def dsa_attention(q, k, v, iq, ik, iw):
    S = q.shape[0]
    T, C = DSA_T, DSA_C
    n_top = min(DSA_TOPK, S // 4)
    tri = jnp.triu(jnp.ones((C, C), F32)).astype(BF16)
    return pl.pallas_call(
        functools.partial(_dsa_kernel, n_top),
        grid=(S // T,),
        in_specs=[pl.BlockSpec((T, DSA_Q_W), lambda i: (i, 0)),
                  pl.BlockSpec((T, DSA_IQ_W), lambda i: (i, 0)),
                  pl.BlockSpec((T, DSA_IDX_HEADS), lambda i: (i, 0)),
                  _resident((S, HEAD_DIM)), _resident((S, HEAD_DIM)), _resident((S, DSA_IDX_DIM)),
                  _resident((C, C))],
        out_specs=pl.BlockSpec((T, DSA_Q_W), lambda i: (i, 0)),
        out_shape=jax.ShapeDtypeStruct((S, DSA_Q_W), BF16),
        scratch_shapes=[pltpu.VMEM((S // C, T, C), jnp.int32),
                        pltpu.VMEM((DSA_HEADS * T, 1), F32),
                        pltpu.VMEM((DSA_HEADS * T, 1), F32),
                        pltpu.VMEM((DSA_HEADS * T, HEAD_DIM), F32)],
        compiler_params=_params(1),
        name="dsa_attention",
    )(q, iq, iw, k, v, ik, tri)


NSA_T = 128
NSA_C = 512
NSA_GROUP = 128


def _nsa_kernel(q_ref, g_ref, kc_ref, vc_ref, ov_ref, ks_ref, vs_ref, kw_ref, vw_ref, e_ref, o_ref,
                sel_scr, m_scr, l_scr, acc_scr):
    i = pl.program_id(0)
    T, C = NSA_T, NSA_C
    H = NSA_HEADS
    q0 = pl.multiple_of(i * T, T)
    ncp = kc_ref.shape[0]
    ns = ov_ref.shape[1]
    tpos = q0 + lax.broadcasted_iota(jnp.int32, (T, 1), 0)
    tpos4 = _tile_rows(tpos, H)
    qs = _stack_heads(q_ref, H)

    s_c = _dot_nt(qs, kc_ref[...]) * ATT_SCALE
    cmp_end = lax.broadcasted_iota(jnp.int32, (H * T, ncp), 1) * NSA_CMP_STRIDE + (NSA_CMP_LEN - 1)
    p_c = _masked_softmax(s_c, cmp_end <= tpos4)
    o_c = _dot(p_c.astype(BF16), vc_ref[...])

    psum = p_c[0:T]
    for r in range(1, H):
        psum = psum + p_c[r * T:(r + 1) * T]
    ov = ov_ref[...]
    hi = psum.astype(BF16)
    r1 = psum - hi.astype(F32)
    mid = r1.astype(BF16)
    lo = (r1 - mid.astype(F32)).astype(BF16)
    imp = _dot(hi, ov) + _dot(mid, ov) + _dot(lo, ov)

    blk = lax.broadcasted_iota(jnp.int32, (T, ns), 1)
    cur = tpos >> 5
    forced = (blk == 0) | (blk == cur) | (blk == cur - 1)
    imp = jnp.where(forced, NSA_FORCED_SCORE, imp)
    imp = jnp.where(blk * NSA_SLC_LEN <= tpos, imp, MASKED)
    blk_f = blk.astype(F32)
    sel = jnp.zeros((T, ns), F32)
    for _ in range(min(NSA_N_SEL, ns)):
        mx = jnp.max(imp, axis=1, keepdims=True)
        first = jnp.min(jnp.where(imp == mx, blk_f, float(ns)), axis=1, keepdims=True)
        hit = blk_f == first
        sel = jnp.where(hit, 1.0, sel)
        imp = jnp.where(hit, EXTRACTED, imp)
    for g in range(ns // NSA_GROUP):
        sel_scr[g] = sel[:, g * NSA_GROUP:(g + 1) * NSA_GROUP].astype(BF16)

    m_scr[...] = jnp.full(m_scr.shape, MASKED, F32)
    l_scr[...] = jnp.zeros(l_scr.shape, F32)
    acc_scr[...] = jnp.zeros(acc_scr.shape, F32)
    chunks_per_group = NSA_GROUP * NSA_SLC_LEN // C

    def sel_body(c, carry):
        s0 = pl.multiple_of(c * C, C)
        member = _dot(sel_scr[c // chunks_per_group], e_ref[c % chunks_per_group])
        spos = s0 + lax.broadcasted_iota(jnp.int32, (T, C), 1)
        keep = jnp.where((member > 0.5) & (spos <= tpos), 1.0, 0.0)
        s = _dot_nt(qs, ks_ref[pl.ds(s0, C), :]) * ATT_SCALE
        _online_softmax_step(s, _tile_rows(keep, H), vs_ref[pl.ds(s0, C), :], m_scr, l_scr, acc_scr)
        return carry

    lax.fori_loop(0, (q0 + T + C - 1) // C, sel_body, 0)
    o_s = acc_scr[...] / jnp.maximum(l_scr[...], F32_TINY)

    nw = NSA_WINDOW + T
    s_w = _dot_nt(qs, kw_ref[pl.ds(q0, nw), :]) * ATT_SCALE
    wpos = q0 - NSA_WINDOW + lax.broadcasted_iota(jnp.int32, (H * T, nw), 1)
    dist = tpos4 - wpos
    p_w = _masked_softmax(s_w, (dist >= 0) & (dist < NSA_WINDOW) & (wpos >= 0))
    o_w = _dot(p_w.astype(BF16), vw_ref[pl.ds(q0, nw), :])

    gate = jax.nn.sigmoid(g_ref[...])
    for r in range(H):
        rows = slice(r * T, (r + 1) * T)
        o = (gate[:, 3 * r:3 * r + 1] * o_c[rows] + gate[:, 3 * r + 1:3 * r + 2] * o_s[rows]
             + gate[:, 3 * r + 2:3 * r + 3] * o_w[rows])
        o_ref[:, r * HEAD_DIM:(r + 1) * HEAD_DIM] = o.astype(o_ref.dtype)


def nsa_attention(q, g, k_cmp, v_cmp, ks, vs, kw, vw):
    S = q.shape[0]
    T, C = NSA_T, NSA_C
    n_cmp = k_cmp.shape[0]
    ncp = -(-n_cmp // 128) * 128
    ns = S // NSA_SLC_LEN
    k_cmp = jnp.pad(k_cmp, ((0, ncp - n_cmp), (0, 0)))
    v_cmp = jnp.pad(v_cmp, ((0, ncp - n_cmp), (0, 0)))
    cmp_start = np.arange(ncp) * NSA_CMP_STRIDE
    slc_start = np.arange(ns) * NSA_SLC_LEN
    overlap = ((cmp_start[:, None] < slc_start[None, :] + NSA_SLC_LEN)
               & (cmp_start[:, None] + NSA_CMP_LEN - 1 >= slc_start[None, :]))
    overlap = jnp.asarray(overlap, BF16)
    cpg = NSA_GROUP * NSA_SLC_LEN // C
    tok_blk = (np.arange(cpg)[:, None] * C + np.arange(C)[None, :]) // NSA_SLC_LEN
    expand = jnp.asarray(np.arange(NSA_GROUP)[None, :, None] == tok_blk[:, None, :], BF16)
    kw = jnp.pad(kw, ((NSA_WINDOW, 0), (0, 0)))
    vw = jnp.pad(vw, ((NSA_WINDOW, 0), (0, 0)))
    return pl.pallas_call(
        _nsa_kernel,
        grid=(S // T,),
        in_specs=[pl.BlockSpec((T, NSA_Q_W), lambda i: (i, 0)),
                  pl.BlockSpec((T, NSA_HEADS * 3), lambda i: (i, 0)),
                  _resident((ncp, HEAD_DIM)), _resident((ncp, HEAD_DIM)), _resident((ncp, ns)),
                  _resident((S, HEAD_DIM)), _resident((S, HEAD_DIM)),
                  _resident((S + NSA_WINDOW, HEAD_DIM)), _resident((S + NSA_WINDOW, HEAD_DIM)),
                  _resident((cpg, NSA_GROUP, C))],
        out_specs=pl.BlockSpec((T, NSA_Q_W), lambda i: (i, 0)),
        out_shape=jax.ShapeDtypeStruct((S, NSA_Q_W), BF16),
        scratch_shapes=[pltpu.VMEM((ns // NSA_GROUP, T, NSA_GROUP), BF16),
                        pltpu.VMEM((NSA_HEADS * T, 1), F32),
                        pltpu.VMEM((NSA_HEADS * T, 1), F32),
                        pltpu.VMEM((NSA_HEADS * T, HEAD_DIM), F32)],
        compiler_params=_params(1),
        name="nsa_attention",
    )(q, g, k_cmp, v_cmp, overlap, ks, vs, kw, vw, expand)


def nsa_compress(x, pe, w1, w2):
    S = x.shape[0]
    half = NSA_CMP_STRIDE * HEAD_DIM
    x16 = x.reshape(S // NSA_CMP_STRIDE, half)
    pe_flat = pe.reshape(NSA_CMP_LEN * HEAD_DIM)
    w1_flat = w1.reshape(NSA_CMP_LEN * HEAD_DIM, HEAD_DIM)
    first = _mm(x16 + pe_flat[None, :half], w1_flat[:half])
    second = _mm(x16 + pe_flat[None, half:], w1_flat[half:])
    hid = jax.nn.gelu(first[:-1] + second[1:])
    hid = jnp.pad(hid, ((0, 1), (0, 0)))
    return _mm(hid, w2)[:-1]


DIL_T = 128


def _dil_kernel(q_ref, k_ref, v_ref, o_ref):
    i = pl.program_id(0)
    T = DIL_T
    q0 = pl.multiple_of(i * T, T)
    tl = lax.broadcasted_iota(jnp.int32, (T, 1), 0)
    for s in range(DIL_HEADS_PER_GROUP):
        cols = slice(s * HEAD_DIM, (s + 1) * HEAD_DIM)
        parts = []
        m = jnp.full((T, 1), MASKED, F32)
        for g, (w, r) in enumerate(DIL_PATTERNS):
            n = w + T
            start = pl.multiple_of(q0 + (DIL_MAX_WINDOW - w), T)
            head = g * DIL_HEADS_PER_GROUP + s
            sc = _dot_nt(q_ref[:, head * HEAD_DIM:(head + 1) * HEAD_DIM], k_ref[pl.ds(start, n), cols]) * ATT_SCALE
            b = lax.broadcasted_iota(jnp.int32, (T, n), 1)
            d = tl + w - b
            mask = (d >= 0) & (d <= w) & ((d & (r - 1)) == 0) & (q0 - w + b >= 0)
            sm = jnp.where(mask, sc, MASKED)
            m = jnp.maximum(m, jnp.max(sm, axis=1, keepdims=True))
            parts.append((sm, mask, start, n))
        num = jnp.zeros((T, HEAD_DIM), F32)
        den = jnp.zeros((T, 1), F32)
        for sm, mask, start, n in parts:
            e = jnp.where(mask, jnp.exp(sm - m), 0.0)
            den = den + jnp.sum(e, axis=1, keepdims=True)
            num = num + _dot(e.astype(BF16), v_ref[pl.ds(start, n), cols])
        o_ref[:, cols] = (num / den).astype(o_ref.dtype)


def dilated_attention(q, k, v):
    S = q.shape[0]
    T = DIL_T
    k = jnp.pad(k, ((DIL_MAX_WINDOW, 0), (0, 0)))
    v = jnp.pad(v, ((DIL_MAX_WINDOW, 0), (0, 0)))
    return pl.pallas_call(
        _dil_kernel,
        grid=(S // T,),
        in_specs=[pl.BlockSpec((T, DIL_Q_W), lambda i: (i, 0)),
                  _resident((S + DIL_MAX_WINDOW, DIL_KV_W)), _resident((S + DIL_MAX_WINDOW, DIL_KV_W))],
        out_specs=pl.BlockSpec((T, DIL_KV_W), lambda i: (i, 0)),
        out_shape=jax.ShapeDtypeStruct((S, DIL_KV_W), BF16),
        compiler_params=_params(1),
        name="dilated_attention",
    )(q, k, v)


def token_mixing(xn, cos, sin, w_in, qk_gain, nsa_pe, nsa_w1, nsa_w2, gate_down, gate_up, w_branch, w_out):
    B, S, _ = xn.shape
    (a_q, a_k, a_v, b_q, b_k, b_v, b_iq, b_ik, b_iw,
     c_q, c_kc, c_vc, c_ks, c_vs, c_kw, c_vw, c_g,
     d_q, d_k, d_v) = jnp.split(_mm(xn, w_in)[0], SPLIT_OFFSETS, axis=-1)

    def rope(t, n):
        return apply_rope(t.reshape(1, S, n, HEAD_DIM), cos, sin).reshape(S, n * HEAD_DIM)

    def qk(t, n, gi):
        return rope(rmsnorm(t.reshape(S, n, HEAD_DIM), qk_gain[gi]), n)

    bf = lambda t: t.astype(BF16)
    o_a = sb_attention(bf(a_q), bf(a_k), bf(a_v))
    o_b = dsa_attention(bf(qk(b_q, DSA_HEADS, 0)), bf(qk(b_k, DSA_KV_HEADS, 1)), bf(b_v),
                        bf(b_iq), bf(b_ik), b_iw)
    k_cmp = rmsnorm(nsa_compress(rope(c_kc, NSA_KV_HEADS), nsa_pe[0], nsa_w1[0], nsa_w2[0]), qk_gain[3])
    v_cmp = nsa_compress(c_vc, nsa_pe[1], nsa_w1[1], nsa_w2[1])
    o_c = nsa_attention(bf(qk(c_q, NSA_HEADS, 2)), c_g, bf(k_cmp), bf(v_cmp),
                        bf(qk(c_ks, NSA_KV_HEADS, 4)), bf(c_vs), bf(qk(c_kw, NSA_KV_HEADS, 5)), bf(c_vw))
    o_d = dilated_attention(bf(qk(d_q, DIL_GROUPS * DIL_HEADS_PER_GROUP, 6)),
                            bf(qk(d_k, DIL_HEADS_PER_GROUP, 7)), bf(d_v))

    z = _mm(xn, gate_down)
    merged = None
    for i, (o, wb) in enumerate(zip((o_a, o_b, o_c, o_d), jnp.split(w_branch, BRANCH_OFFSETS, axis=0))):
        gate = jax.nn.sigmoid(_mm(z, gate_up[i]).astype(jnp.float32)).astype(xn.dtype)
        term = gate * _mm(o[None], wb)
        merged = term if merged is None else merged + term
    return _mm(merged, w_out)


def peer_ffn(xn, wq, subkeys, u, v):
    B, S, _ = xn.shape
    q = _mm(xn, wq).reshape(B, S, PEER_HEADS, PEER_KEY_DIM)
    q1, q2 = jnp.split(q, 2, axis=-1)
    s1 = jnp.einsum('bshd,hnd->bshn', q1, subkeys[:, 0]).astype(jnp.float32)
    s2 = jnp.einsum('bshd,hnd->bshn', q2, subkeys[:, 1]).astype(jnp.float32)
    v1, i1 = lax.top_k(s1, PEER_TOPK)
    v2, i2 = lax.top_k(s2, PEER_TOPK)
    cand_s = (v1[..., :, None] + v2[..., None, :]).reshape(B, S, PEER_HEADS, PEER_TOPK * PEER_TOPK)
    cand_e = (i1[..., :, None] * PEER_N_KEYS + i2[..., None, :]).reshape(B, S, PEER_HEADS, PEER_TOPK * PEER_TOPK)
    top_s, top_pos = lax.top_k(cand_s, PEER_TOPK)
    experts = jnp.take_along_axis(cand_e, top_pos, axis=-1).reshape(B * S, PEER_HEADS * PEER_TOPK)
    gates = jax.nn.softmax(top_s, axis=-1).reshape(B * S, PEER_HEADS * PEER_TOPK)
    act = jax.nn.gelu(_mm(xn, u.T).astype(jnp.float32)).reshape(B * S, PEER_EXPERTS)
    coef = gates * jnp.take_along_axis(act, experts, axis=-1)
    wmat = jnp.zeros((B * S, PEER_EXPERTS), jnp.float32).at[jnp.arange(B * S)[:, None], experts].add(coef)
    return _mm(wmat.astype(v.dtype), v).reshape(B, S, -1)


def kernel(x, norm_mix, w_in, qk_gain, nsa_pe, nsa_w1, nsa_w2, gate_down, gate_up, w_branch, w_out,
           norm_ffn, peer_wq, peer_subkeys, peer_u, peer_v):
    cos, sin = rope_tables(x.shape[1], x.dtype)
    h = x
    for l in range(DEPTH):
        xn = rmsnorm(h, norm_mix[l])
        h = h + token_mixing(xn, cos, sin, w_in[l], qk_gain[l], nsa_pe[l], nsa_w1[l], nsa_w2[l],
                             gate_down[l], gate_up[l], w_branch[l], w_out[l])
        h = h + peer_ffn(rmsnorm(h, norm_ffn[l]), peer_wq[l], peer_subkeys[l], peer_u[l], peer_v[l])
    return h
```

```python
import math, functools
import jax, jax.numpy as jnp
from jax import lax
import numpy as np
from jax.experimental import pallas as pl
from jax.experimental.pallas import tpu as pltpu

D_MODEL = 4096
BATCH = 1
SEQ = 16384
DEPTH = 4

HEAD_DIM = 128
ROPE_THETA = 10000.0
NORM_EPS = 1e-6
SB_HEADS = 4
DSA_HEADS = 4
DSA_KV_HEADS = 1
DSA_IDX_HEADS = 4
DSA_IDX_DIM = 64
DSA_TOPK = 256
NSA_HEADS = 4
NSA_KV_HEADS = 1
NSA_CMP_LEN = 32
NSA_CMP_STRIDE = 16
NSA_SLC_LEN = 32
NSA_N_SEL = 8
NSA_WINDOW = 512
NSA_FORCED_SCORE = 1e4
DIL_PATTERNS = ((128, 1), (512, 4), (2048, 16))
DIL_GROUPS = len(DIL_PATTERNS)
DIL_HEADS_PER_GROUP = 2
DIL_MAX_WINDOW = max(w for w, _ in DIL_PATTERNS)
N_BRANCHES = 4
GATE_RANK = 256
PEER_HEADS = 8
PEER_N_KEYS = 64
PEER_EXPERTS = PEER_N_KEYS ** 2
PEER_KEY_DIM = 128
PEER_TOPK = 16

SB_W = SB_HEADS * HEAD_DIM
DSA_Q_W = DSA_HEADS * HEAD_DIM
DSA_KV_W = DSA_KV_HEADS * HEAD_DIM
DSA_IQ_W = DSA_IDX_HEADS * DSA_IDX_DIM
NSA_Q_W = NSA_HEADS * HEAD_DIM
NSA_KV_W = NSA_KV_HEADS * HEAD_DIM
DIL_Q_W = DIL_GROUPS * DIL_HEADS_PER_GROUP * HEAD_DIM
DIL_KV_W = DIL_HEADS_PER_GROUP * HEAD_DIM
IN_SPLITS = (SB_W, SB_W, SB_W,
             DSA_Q_W, DSA_KV_W, DSA_KV_W, DSA_IQ_W, DSA_IDX_DIM, DSA_IDX_HEADS,
             NSA_Q_W, NSA_KV_W, NSA_KV_W, NSA_KV_W, NSA_KV_W, NSA_KV_W, NSA_KV_W, NSA_HEADS * 3,
             DIL_Q_W, DIL_KV_W, DIL_KV_W)
IN_WIDTH = sum(IN_SPLITS)
SPLIT_OFFSETS = tuple(int(o) for o in np.cumsum(IN_SPLITS)[:-1])
BRANCH_WIDTHS = (SB_W, DSA_Q_W, NSA_Q_W, DIL_KV_W)
BRANCH_WIDTH = sum(BRANCH_WIDTHS)
BRANCH_OFFSETS = tuple(int(o) for o in np.cumsum(BRANCH_WIDTHS)[:-1])
N_QK_GAINS = 8

ATT_SCALE = HEAD_DIM ** -0.5
LOG2_SCALE = ATT_SCALE * math.log2(math.e)
MASKED = -1e30
EXTRACTED = -3e38
INT32_MIN = -2 ** 31
F32_TINY = float(np.finfo(np.float32).tiny)
VMEM_LIMIT = 56 * 1024 * 1024

BF16 = jnp.bfloat16
F32 = jnp.float32


def _params(n_grid):
    return pltpu.CompilerParams(dimension_semantics=("arbitrary",) * n_grid, vmem_limit_bytes=VMEM_LIMIT)


def _resident(shape):
    nd = len(shape)
    return pl.BlockSpec(shape, lambda *_: (0,) * nd, pipeline_mode=pl.Buffered(1))


def _dot_nt(a, b):
    return lax.dot_general(a, b, (((1,), (1,)), ((), ())), preferred_element_type=F32)


def _dot(a, b):
    return jnp.dot(a, b, preferred_element_type=F32)


def _mm_kernel(a_ref, b_ref, o_ref):
    o_ref[...] = _dot(a_ref[...], b_ref[...]).astype(o_ref.dtype)


def _mm_res_kernel(a_ref, b_ref, r_ref, o_ref):
    o_ref[...] = (r_ref[...] + _dot(a_ref[...], b_ref[...])).astype(o_ref.dtype)


def _pick(n, prefs):
    for p in prefs:
        if n % p == 0:
            return p
    return n


def _mm(a, b, out_dtype=F32, residual=None):
    lead = a.shape[:-1]
    K = a.shape[-1]
    N = b.shape[-1]
    a2 = a.reshape(-1, K).astype(jnp.bfloat16)
    M = a2.shape[0]
    n_pad = (-N) % 128
    b2 = b.astype(jnp.bfloat16)
    if n_pad:
        b2 = jnp.pad(b2, ((0, 0), (0, n_pad)))
    Np = N + n_pad
    tm = _pick(M, (1024, 512, 256, 128))
    tn = _pick(Np, (512, 256, 128))
    in_specs = [pl.BlockSpec((tm, K), lambda i, j: (i, 0)),
                pl.BlockSpec((K, tn), lambda i, j: (0, j))]
    args = [a2, b2]
    body = _mm_kernel
    if residual is not None:
        assert n_pad == 0
        in_specs.append(pl.BlockSpec((tm, tn), lambda i, j: (i, j)))
        args.append(residual.reshape(M, N))
        body = _mm_res_kernel
    out = pl.pallas_call(
        body,
        grid=(M // tm, Np // tn),
        in_specs=in_specs,
        out_specs=pl.BlockSpec((tm, tn), lambda i, j: (i, j)),
        out_shape=jax.ShapeDtypeStruct((M, Np), out_dtype),
        compiler_params=pltpu.CompilerParams(
            dimension_semantics=("arbitrary", "arbitrary"),
            vmem_limit_bytes=VMEM_LIMIT),
        name="dense_mm",
    )(*args)
    if n_pad:
        out = out[:, :N]
    return out.reshape(*lead, N)


def rmsnorm(x, g):
    xf = x.astype(jnp.float32)
    y = xf * lax.rsqrt(jnp.mean(xf * xf, axis=-1, keepdims=True) + NORM_EPS)
    return (y * g.astype(jnp.float32)).astype(x.dtype)


def rope_tables(seq, dtype):
    inv = ROPE_THETA ** (-jnp.arange(0, HEAD_DIM, 2, dtype=jnp.float32) / HEAD_DIM)
    ang = jnp.arange(seq, dtype=jnp.float32)[:, None] * inv[None, :]
    return jnp.cos(ang)[None, :, None, :].astype(dtype), jnp.sin(ang)[None, :, None, :].astype(dtype)


def apply_rope(x, cos, sin):
    x1, x2 = jnp.split(x, 2, axis=-1)
    return jnp.concatenate([x1 * cos - x2 * sin, x2 * cos + x1 * sin], axis=-1)


def _stack_heads(q_ref, n_heads):
    return jnp.concatenate([q_ref[:, r * HEAD_DIM:(r + 1) * HEAD_DIM] for r in range(n_heads)], axis=0)


def _tile_rows(x, n):
    return jnp.concatenate([x] * n, axis=0)


def _flash_init(m_scr, acc_scr):
    m_scr[...] = jnp.full(m_scr.shape, MASKED, F32)
    acc_scr[...] = jnp.zeros(acc_scr.shape, F32)


def _flash_step(s, bias, v_aug, m_scr, acc_scr, n_heads):
    T, C = bias.shape
    ps, alphas = [], []
    for r in range(n_heads):
        rows = slice(r * T, (r + 1) * T)
        sm = s[rows] + bias
        m_old = m_scr[rows]
        m_new = jnp.maximum(m_old, jnp.max(sm, axis=1, keepdims=True))
        m_scr[rows] = m_new
        p = [jnp.exp2(sm[:, u * 128:(u + 1) * 128] - m_new) for u in range(C // 128)]
        ps.append(jnp.concatenate(p, axis=1).astype(BF16))
        alpha = jnp.exp2(m_old - m_new)
        alphas.append(jnp.concatenate([alpha, alpha], axis=1))
    pv = _dot(jnp.concatenate(ps, axis=0), v_aug)
    acc_scr[...] = jnp.concatenate(alphas, axis=0) * acc_scr[...] + pv


def _flash_result(acc_scr):
    acc = acc_scr[...]
    return acc[:, :HEAD_DIM] / acc[:, HEAD_DIM:]


def _masked_softmax(s, mask):
    sm = jnp.where(mask, s, MASKED)
    m = jnp.max(sm, axis=1, keepdims=True)
    e = jnp.where(mask, jnp.exp2(sm - m), 0.0)
    den = jnp.sum(e, axis=1, keepdims=True)
    return e / jnp.maximum(den, F32_TINY)


SB_T = 256


def _sb_kernel(q_ref, k_ref, v_ref, tri_ref, o_ref, later_scr, acc_scr):
    i = pl.program_id(0)
    T = SB_T
    q0 = pl.multiple_of(i * T, T)
    tri = tri_ref[...]
    later_scr[...] = jnp.zeros(later_scr.shape, F32)
    acc_scr[...] = jnp.zeros(acc_scr.shape, F32)

    def chunk(s0, causal):
        for h in range(SB_HEADS):
            cols = slice(h * HEAD_DIM, (h + 1) * HEAD_DIM)
            z = _dot_nt(q_ref[:, cols], k_ref[pl.ds(s0, T), cols]) * ATT_SCALE
            sp = jnp.log(1.0 + jnp.exp(-jnp.abs(z)))
            lm = -(jnp.maximum(z, 0.0) + sp)
            ls = jnp.minimum(z, 0.0) - sp
            if causal is not None:
                lm = jnp.where(causal, lm, 0.0)
            within = _dot(lm.astype(BF16), tri)
            w = jnp.exp(ls + within + later_scr[h])
            if causal is not None:
                w = jnp.where(causal, w, 0.0)
            acc_scr[h] += _dot(w.astype(BF16), v_ref[pl.ds(s0, T), cols])
            later_scr[h] += jnp.sum(lm, axis=1, keepdims=True)

    row = lax.broadcasted_iota(jnp.int32, (T, T), 0)
    col = lax.broadcasted_iota(jnp.int32, (T, T), 1)
    chunk(q0, col < row)

    def body(n, carry):
        chunk(pl.multiple_of((i - n) * T, T), None)
        return carry

    lax.fori_loop(1, i + 1, body, 0)
    for h in range(SB_HEADS):
        o_ref[:, h * HEAD_DIM:(h + 1) * HEAD_DIM] = acc_scr[h].astype(o_ref.dtype)


def sb_attention(q, k, v):
    S = q.shape[0]
    T = SB_T
    tri = jnp.tril(jnp.ones((T, T), F32), -1).astype(BF16)
    return pl.pallas_call(
        _sb_kernel,
        grid=(S // T,),
        in_specs=[pl.BlockSpec((T, SB_W), lambda i: (i, 0)),
                  _resident((S, SB_W)), _resident((S, SB_W)), _resident((T, T))],
        out_specs=pl.BlockSpec((T, SB_W), lambda i: (i, 0)),
        out_shape=jax.ShapeDtypeStruct((S, SB_W), BF16),
        scratch_shapes=[pltpu.VMEM((SB_HEADS, T, 1), F32),
                        pltpu.VMEM((SB_HEADS, T, HEAD_DIM), F32)],
        compiler_params=_params(1),
        name="sb_attention",
    )(q, k, v, tri)


DSA_T = 128
DSA_C = 512
DSA_CH = 256


def _dsa_kernel(n_top, q_ref, iq_ref, iw_ref, k_ref, v_ref, ik_ref, tri_ref, o_ref,
                key_scr, m_scr, acc_scr):
    i = pl.program_id(0)
    T, C, CH = DSA_T, DSA_C, DSA_CH
    H = DSA_HEADS
    q0 = i * T
    n_chunks = (q0 + T + C - 1) // C
    tpos = q0 + lax.broadcasted_iota(jnp.int32, (T, 1), 0)
    iw = iw_ref[...]

    def index_keys(c, causal):
        s0 = pl.multiple_of(c * C, C)
        ikc = ik_ref[pl.ds(s0, C), :]
        acc = jnp.zeros((T, C), F32)
        for h in range(DSA_IDX_HEADS):
            d = _dot_nt(iq_ref[:, h * DSA_IDX_DIM:(h + 1) * DSA_IDX_DIM], ikc)
            acc = acc + jnp.maximum(d, 0.0) * iw[:, h:h + 1]
        bits = lax.bitcast_convert_type(acc, jnp.int32)
        key = jnp.where(bits < 0, bits ^ jnp.int32(0x7FFFFFFF), bits)
        key = jnp.where(acc == 0.0, 0, key)
        if causal:
            spos = s0 + lax.broadcasted_iota(jnp.int32, (T, C), 1)
            key = jnp.where(spos <= tpos, key, INT32_MIN)
        key_scr[c] = key

    def idx_body(c, carry):
        index_keys(c, False)
        return carry

    lax.fori_loop(0, n_chunks - 1, idx_body, 0)
    index_keys(n_chunks - 1, True)

    def count_ge(cand):
        def body(c, acc):
            kk = key_scr[c]
            for u in range(C // 128):
                acc = acc + jnp.where(kk[:, u * 128:(u + 1) * 128] >= cand, 1.0, 0.0)
            return acc
        acc = lax.fori_loop(0, n_chunks, body, jnp.zeros((T, 128), F32))
        return jnp.sum(acc, axis=1, keepdims=True)

    def bis_body(b, base):
        cand = base + (jnp.int32(1) << (jnp.int32(31) - b))
        return jnp.where(count_ge(cand) >= n_top, cand, base)

    thr = lax.fori_loop(0, 32, bis_body, jnp.full((T, 128), INT32_MIN, jnp.int32))
    thr = jnp.maximum(thr, INT32_MIN + 1)
    need = jnp.broadcast_to(n_top - count_ge(thr + 1), (T, 128))
    tri = tri_ref[...]

    _flash_init(m_scr, acc_scr)
    qs = _stack_heads(q_ref, H)

    def att_body(c, eq_seen):
        kk = key_scr[c]
        for half in range(C // CH):
            s0 = pl.multiple_of(c * C + half * CH, CH)
            eqs, gts = [], []
            for u in range(CH // 128):
                ku = kk[:, half * CH + u * 128:half * CH + (u + 1) * 128]
                eqs.append(ku == thr)
                gts.append(ku > thr)
            eq = jnp.concatenate([jnp.where(e, 1.0, 0.0) for e in eqs], axis=1).astype(BF16)
            pref = _dot(eq, tri)
            bias = [jnp.where(gts[u] | (eqs[u] & (eq_seen + pref[:, u * 128:(u + 1) * 128] <= need)),
                              0.0, MASKED) for u in range(CH // 128)]
            eq_seen = eq_seen + pref[:, CH:]
            s = _dot_nt(qs, k_ref[pl.ds(s0, CH), :])
            _flash_step(s, jnp.concatenate(bias, axis=1), v_ref[pl.ds(s0, CH), :], m_scr, acc_scr, H)
        return eq_seen

    lax.fori_loop(0, n_chunks, att_body, jnp.zeros((T, 128), F32))
    out = _flash_result(acc_scr)
    for r in range(H):
        o_ref[:, r * HEAD_DIM:(r + 1) * HEAD_DIM] = out[r * T:(r + 1) * T].astype(o_ref.dtype)


def _with_ones(v):
    return jnp.concatenate([v, jnp.ones_like(v)], axis=1)


def dsa_attention(q, k, v, iq, ik, iw):
    S = q.shape[0]
    T, C, CH = DSA_T, DSA_C, DSA_CH
    n_top = min(DSA_TOPK, S // 4)
    tri = jnp.concatenate([jnp.triu(jnp.ones((CH, CH), F32)), jnp.ones((CH, 128), F32)], axis=1).astype(BF16)
    return pl.pallas_call(
        functools.partial(_dsa_kernel, n_top),
        grid=(S // T,),
        in_specs=[pl.BlockSpec((T, DSA_Q_W), lambda i: (i, 0)),
                  pl.BlockSpec((T, DSA_IQ_W), lambda i: (i, 0)),
                  pl.BlockSpec((T, DSA_IDX_HEADS), lambda i: (i, 0)),
                  _resident((S, HEAD_DIM)), _resident((S, 2 * HEAD_DIM)), _resident((S, DSA_IDX_DIM)),
                  _resident((CH, CH + 128))],
        out_specs=pl.BlockSpec((T, DSA_Q_W), lambda i: (i, 0)),
        out_shape=jax.ShapeDtypeStruct((S, DSA_Q_W), BF16),
        scratch_shapes=[pltpu.VMEM((S // C, T, C), jnp.int32),
                        pltpu.VMEM((DSA_HEADS * T, HEAD_DIM), F32),
                        pltpu.VMEM((DSA_HEADS * T, 2 * HEAD_DIM), F32)],
        compiler_params=_params(1),
        name="dsa_attention",
    )(q, iq, iw, k, _with_ones(v), ik, tri)


NSA_T = 128
NSA_C = 512
NSA_CH = 256
NSA_GROUP = 128


def _nsa_kernel(q_ref, g_ref, kc_ref, vc_ref, ov_ref, ks_ref, vs_ref, kw_ref, vw_ref, e_ref, o_ref,
                sel_scr, m_scr, acc_scr):
    i = pl.program_id(0)
    T, C, CH = NSA_T, NSA_C, NSA_CH
    H = NSA_HEADS
    q0 = pl.multiple_of(i * T, T)
    ncp = kc_ref.shape[0]
    ns = ov_ref.shape[1]
    tpos = q0 + lax.broadcasted_iota(jnp.int32, (T, 1), 0)
    tpos4 = _tile_rows(tpos, H)
    qs = _stack_heads(q_ref, H)

    s_c = _dot_nt(qs, kc_ref[...])
    cmp_end = lax.broadcasted_iota(jnp.int32, (H * T, ncp), 1) * NSA_CMP_STRIDE + (NSA_CMP_LEN - 1)
    p_c = _masked_softmax(s_c, cmp_end <= tpos4)
    o_c = _dot(p_c.astype(BF16), vc_ref[...])

    psum = p_c[0:T]
    for r in range(1, H):
        psum = psum + p_c[r * T:(r + 1) * T]
    ov = ov_ref[...]
    hi = psum.astype(BF16)
    r1 = psum - hi.astype(F32)
    mid = r1.astype(BF16)
    lo = (r1 - mid.astype(F32)).astype(BF16)
    imp = _dot(hi, ov) + _dot(mid, ov) + _dot(lo, ov)

    blk = lax.broadcasted_iota(jnp.int32, (T, ns), 1)
    cur = tpos >> 5
    forced = (blk == 0) | (blk == cur) | (blk == cur - 1)
    imp = jnp.where(forced, NSA_FORCED_SCORE, imp)
    imp = jnp.where(blk * NSA_SLC_LEN <= tpos, imp, MASKED)
    blk_f = blk.astype(F32)
    sel = jnp.zeros((T, ns), F32)
    for _ in range(min(NSA_N_SEL, ns)):
        mx = jnp.max(imp, axis=1, keepdims=True)
        first = jnp.min(jnp.where(imp == mx, blk_f, float(ns)), axis=1, keepdims=True)
        hit = blk_f == first
        sel = jnp.where(hit, 1.0, sel)
        imp = jnp.where(hit, EXTRACTED, imp)
    for g in range(ns // NSA_GROUP):
        sel_scr[g] = sel[:, g * NSA_GROUP:(g + 1) * NSA_GROUP].astype(BF16)

    _flash_init(m_scr, acc_scr)
    chunks_per_group = NSA_GROUP * NSA_SLC_LEN // C

    def sel_body(c, carry):
        s0 = pl.multiple_of(c * C, C)
        member = _dot(sel_scr[c // chunks_per_group], e_ref[c % chunks_per_group])
        spos = s0 + lax.broadcasted_iota(jnp.int32, (T, C), 1)
        bias = jnp.where((member > 0.5) & (spos <= tpos), 0.0, MASKED)
        for half in range(C // CH):
            h0 = pl.multiple_of(s0 + half * CH, CH)
            s = _dot_nt(qs, ks_ref[pl.ds(h0, CH), :])
            _flash_step(s, bias[:, half * CH:(half + 1) * CH], vs_ref[pl.ds(h0, CH), :], m_scr, acc_scr, H)
        return carry

    lax.fori_loop(0, (q0 + T + C - 1) // C, sel_body, 0)
    o_s = _flash_result(acc_scr)

    nw = NSA_WINDOW + T
    s_w = _dot_nt(qs, kw_ref[pl.ds(q0, nw), :])
    wpos = q0 - NSA_WINDOW + lax.broadcasted_iota(jnp.int32, (H * T, nw), 1)
    dist = tpos4 - wpos
    p_w = _masked_softmax(s_w, (dist >= 0) & (dist < NSA_WINDOW) & (wpos >= 0))
    o_w = _dot(p_w.astype(BF16), vw_ref[pl.ds(q0, nw), :])

    gate = jax.nn.sigmoid(g_ref[...])
    for r in range(H):
        rows = slice(r * T, (r + 1) * T)
        o = (gate[:, 3 * r:3 * r + 1] * o_c[rows] + gate[:, 3 * r + 1:3 * r + 2] * o_s[rows]
             + gate[:, 3 * r + 2:3 * r + 3] * o_w[rows])
        o_ref[:, r * HEAD_DIM:(r + 1) * HEAD_DIM] = o.astype(o_ref.dtype)


def nsa_attention(q, g, k_cmp, v_cmp, ks, vs, kw, vw):
    S = q.shape[0]
    T, C = NSA_T, NSA_C
    n_cmp = k_cmp.shape[0]
    ncp = -(-n_cmp // 128) * 128
    ns = S // NSA_SLC_LEN
    k_cmp = jnp.pad(k_cmp, ((0, ncp - n_cmp), (0, 0)))
    v_cmp = jnp.pad(v_cmp, ((0, ncp - n_cmp), (0, 0)))
    cmp_start = np.arange(ncp) * NSA_CMP_STRIDE
    slc_start = np.arange(ns) * NSA_SLC_LEN
    overlap = ((cmp_start[:, None] < slc_start[None, :] + NSA_SLC_LEN)
               & (cmp_start[:, None] + NSA_CMP_LEN - 1 >= slc_start[None, :]))
    overlap = jnp.asarray(overlap, BF16)
    cpg = NSA_GROUP * NSA_SLC_LEN // C
    tok_blk = (np.arange(cpg)[:, None] * C + np.arange(C)[None, :]) // NSA_SLC_LEN
    expand = jnp.asarray(np.arange(NSA_GROUP)[None, :, None] == tok_blk[:, None, :], BF16)
    kw = jnp.pad(kw, ((NSA_WINDOW, 0), (0, 0)))
    vw = jnp.pad(vw, ((NSA_WINDOW, 0), (0, 0)))
    return pl.pallas_call(
        _nsa_kernel,
        grid=(S // T,),
        in_specs=[pl.BlockSpec((T, NSA_Q_W), lambda i: (i, 0)),
                  pl.BlockSpec((T, NSA_HEADS * 3), lambda i: (i, 0)),
                  _resident((ncp, HEAD_DIM)), _resident((ncp, HEAD_DIM)), _resident((ncp, ns)),
                  _resident((S, HEAD_DIM)), _resident((S, 2 * HEAD_DIM)),
                  _resident((S + NSA_WINDOW, HEAD_DIM)), _resident((S + NSA_WINDOW, HEAD_DIM)),
                  _resident((cpg, NSA_GROUP, C))],
        out_specs=pl.BlockSpec((T, NSA_Q_W), lambda i: (i, 0)),
        out_shape=jax.ShapeDtypeStruct((S, NSA_Q_W), BF16),
        scratch_shapes=[pltpu.VMEM((ns // NSA_GROUP, T, NSA_GROUP), BF16),
                        pltpu.VMEM((NSA_HEADS * T, HEAD_DIM), F32),
                        pltpu.VMEM((NSA_HEADS * T, 2 * HEAD_DIM), F32)],
        compiler_params=_params(1),
        name="nsa_attention",
    )(q, g, k_cmp, v_cmp, overlap, ks, _with_ones(vs), kw, vw, expand)


def nsa_compress(x, pe, w1, w2):
    S = x.shape[0]
    half = NSA_CMP_STRIDE * HEAD_DIM
    x16 = x.reshape(S // NSA_CMP_STRIDE, half)
    pe_flat = pe.reshape(NSA_CMP_LEN * HEAD_DIM)
    w1_flat = w1.reshape(NSA_CMP_LEN * HEAD_DIM, HEAD_DIM)
    first = _mm(x16 + pe_flat[None, :half], w1_flat[:half])
    second = _mm(x16 + pe_flat[None, half:], w1_flat[half:])
    hid = jax.nn.gelu(first[:-1] + second[1:])
    hid = jnp.pad(hid, ((0, 1), (0, 0)))
    return _mm(hid, w2)[:-1]


DIL_T = 128


def _dil_kernel(q_ref, k_ref, v_ref, o_ref):
    i = pl.program_id(0)
    T = DIL_T
    q0 = pl.multiple_of(i * T, T)
    tl = lax.broadcasted_iota(jnp.int32, (T, 1), 0)
    for s in range(DIL_HEADS_PER_GROUP):
        cols = slice(s * HEAD_DIM, (s + 1) * HEAD_DIM)
        parts = []
        m = jnp.full((T, 1), MASKED, F32)
        for g, (w, r) in enumerate(DIL_PATTERNS):
            n = w + T
            start = pl.multiple_of(q0 + (DIL_MAX_WINDOW - w), T)
            head = g * DIL_HEADS_PER_GROUP + s
            sc = _dot_nt(q_ref[:, head * HEAD_DIM:(head + 1) * HEAD_DIM], k_ref[pl.ds(start, n), cols]) * ATT_SCALE
            b = lax.broadcasted_iota(jnp.int32, (T, n), 1)
            d = tl + w - b
            mask = (d >= 0) & (d <= w) & ((d & (r - 1)) == 0) & (q0 - w + b >= 0)
            sm = jnp.where(mask, sc, MASKED)
            m = jnp.maximum(m, jnp.max(sm, axis=1, keepdims=True))
            parts.append((sm, mask, start, n))
        num = jnp.zeros((T, HEAD_DIM), F32)
        den = jnp.zeros((T, 1), F32)
        for sm, mask, start, n in parts:
            e = jnp.where(mask, jnp.exp(sm - m), 0.0)
            den = den + jnp.sum(e, axis=1, keepdims=True)
            num = num + _dot(e.astype(BF16), v_ref[pl.ds(start, n), cols])
        o_ref[:, cols] = (num / den).astype(o_ref.dtype)


def dilated_attention(q, k, v):
    S = q.shape[0]
    T = DIL_T
    k = jnp.pad(k, ((DIL_MAX_WINDOW, 0), (0, 0)))
    v = jnp.pad(v, ((DIL_MAX_WINDOW, 0), (0, 0)))
    return pl.pallas_call(
        _dil_kernel,
        grid=(S // T,),
        in_specs=[pl.BlockSpec((T, DIL_Q_W), lambda i: (i, 0)),
                  _resident((S + DIL_MAX_WINDOW, DIL_KV_W)), _resident((S + DIL_MAX_WINDOW, DIL_KV_W))],
        out_specs=pl.BlockSpec((T, DIL_KV_W), lambda i: (i, 0)),
        out_shape=jax.ShapeDtypeStruct((S, DIL_KV_W), BF16),
        compiler_params=_params(1),
        name="dilated_attention",
    )(q, k, v)


def token_mixing(h, xn, cos, sin, w_in, qk_gain, nsa_pe, nsa_w1, nsa_w2, gate_down, gate_up, w_branch, w_out):
    B, S, _ = xn.shape
    (a_q, a_k, a_v, b_q, b_k, b_v, b_iq, b_ik, b_iw,
     c_q, c_kc, c_vc, c_ks, c_vs, c_kw, c_vw, c_g,
     d_q, d_k, d_v) = jnp.split(_mm(xn, w_in)[0], SPLIT_OFFSETS, axis=-1)

    def rope(t, n):
        return apply_rope(t.reshape(1, S, n, HEAD_DIM), cos, sin).reshape(S, n * HEAD_DIM)

    def qk(t, n, gi):
        return rope(rmsnorm(t.reshape(S, n, HEAD_DIM), qk_gain[gi]), n)

    bf = lambda t: t.astype(BF16)
    o_a = sb_attention(bf(a_q), bf(a_k), bf(a_v))
    o_b = dsa_attention(bf(qk(b_q, DSA_HEADS, 0) * LOG2_SCALE), bf(qk(b_k, DSA_KV_HEADS, 1)), bf(b_v),
                        bf(b_iq), bf(b_ik), b_iw)
    k_cmp = rmsnorm(nsa_compress(rope(c_kc, NSA_KV_HEADS), nsa_pe[0], nsa_w1[0], nsa_w2[0]), qk_gain[3])
    v_cmp = nsa_compress(c_vc, nsa_pe[1], nsa_w1[1], nsa_w2[1])
    o_c = nsa_attention(bf(qk(c_q, NSA_HEADS, 2) * LOG2_SCALE), c_g, bf(k_cmp), bf(v_cmp),
                        bf(qk(c_ks, NSA_KV_HEADS, 4)), bf(c_vs), bf(qk(c_kw, NSA_KV_HEADS, 5)), bf(c_vw))
    o_d = dilated_attention(bf(qk(d_q, DIL_GROUPS * DIL_HEADS_PER_GROUP, 6)),
                            bf(qk(d_k, DIL_HEADS_PER_GROUP, 7)), bf(d_v))

    z = _mm(xn, gate_down, out_dtype=BF16)[0]
    merged = gated_merge(z, (o_a, o_b, o_c, o_d), gate_up, w_branch)
    return _mm(merged[None], w_out, residual=h)


MERGE_TM = 512
MERGE_TN = 512


def _merge_kernel(z_ref, oa_ref, ob_ref, oc_ref, od_ref, gu_ref, wb_ref, o_ref):
    z = z_ref[...]
    acc = None
    off = 0
    for i, br_ref in enumerate((oa_ref, ob_ref, oc_ref, od_ref)):
        width = BRANCH_WIDTHS[i]
        gate = jax.nn.sigmoid(_dot(z, gu_ref[i]))
        term = gate * _dot(br_ref[...], wb_ref[off:off + width, :])
        acc = term if acc is None else acc + term
        off += width
    o_ref[...] = acc.astype(o_ref.dtype)


def gated_merge(z, branches, gate_up, w_branch):
    S = z.shape[0]
    D = w_branch.shape[1]
    tm, tn = min(MERGE_TM, S), min(MERGE_TN, D)
    row = lambda i, j: (i, 0)
    return pl.pallas_call(
        _merge_kernel,
        grid=(S // tm, D // tn),
        in_specs=[pl.BlockSpec((tm, GATE_RANK), row)]
                 + [pl.BlockSpec((tm, w), row) for w in BRANCH_WIDTHS]
                 + [pl.BlockSpec((N_BRANCHES, GATE_RANK, tn), lambda i, j: (0, 0, j)),
                    pl.BlockSpec((BRANCH_WIDTH, tn), lambda i, j: (0, j))],
        out_specs=pl.BlockSpec((tm, tn), lambda i, j: (i, j)),
        out_shape=jax.ShapeDtypeStruct((S, D), BF16),
        compiler_params=_params(2),
        name="gated_merge",
    )(z, *branches, gate_up.astype(BF16), w_branch.astype(BF16))


PEER_ST = 256
PEER_TM = 512
PEER_EC = 512
PEER_HALF = PEER_KEY_DIM // 2
PEER_STAT_ROWS = 32


def _top_rows(scores, n, n_rows):
    st = scores.shape[1]
    rid = lax.broadcasted_iota(jnp.int32, (n_rows, st), 0)
    out = jnp.full((n_rows, st), EXTRACTED, F32)
    for r in range(n):
        m = jnp.max(scores, axis=0, keepdims=True)
        out = jnp.where(rid == r, m, out)
        scores = jnp.where(scores == m, EXTRACTED, scores)
    return out


def _peer_stats_kernel(q_ref, w1_ref, w2_ref, o_ref):
    st = q_ref.shape[0]
    k = PEER_TOPK
    rid = lax.broadcasted_iota(jnp.int32, (PEER_STAT_ROWS, st), 0)
    rid8 = lax.broadcasted_iota(jnp.int32, (8, st), 0)
    stats = jnp.zeros((PEER_STAT_ROWS, st), F32)
    for h in range(PEER_HEADS):
        qh = q_ref[:, h * PEER_KEY_DIM:(h + 1) * PEER_KEY_DIM]
        v1 = _top_rows(_dot_nt(w1_ref[h], qh), k + 1, 24)
        v2 = _top_rows(_dot_nt(w2_ref[h], qh), k + 1, 24)
        v2_top = v2[0:k]
        pieces = [v1[a:a + 1] + v2_top for a in range(k)]
        extra = jnp.where(rid8 == 0, v1[k:k + 1] + v2[0:1],
                          jnp.where(rid8 == 1, v1[0:1] + v2[k:k + 1], EXTRACTED))
        tops = _top_rows(jnp.concatenate(pieces + [extra], axis=0), k + 1, 24)
        c1 = tops[0:1]
        den = jnp.sum(jnp.exp(tops[0:k] - c1), axis=0, keepdims=True)
        thr = 0.5 * (tops[k - 1:k] + tops[k:k + 1])
        stats = jnp.where(rid == h, thr, stats)
        stats = jnp.where(rid == PEER_HEADS + h, c1, stats)
        stats = jnp.where(rid == 2 * PEER_HEADS + h, 1.0 / den, stats)
    o_ref[...] = stats


def _peer_weight_kernel(xn_ref, ut_ref, q_ref, kf_ref, st_ref, o_ref):
    act = jax.nn.gelu(_dot(xn_ref[...], ut_ref[...]))
    st = st_ref[...]
    w = jnp.zeros(act.shape, F32)
    for h in range(PEER_HEADS):
        sf = _dot(q_ref[:, h * PEER_KEY_DIM:(h + 1) * PEER_KEY_DIM], kf_ref[h])
        gate = jnp.exp(sf - st[:, PEER_HEADS + h:PEER_HEADS + h + 1]) * st[:, 2 * PEER_HEADS + h:2 * PEER_HEADS + h + 1]
        w = w + jnp.where(sf >= st[:, h:h + 1], gate, 0.0)
    o_ref[...] = (w * act).astype(o_ref.dtype)


def peer_ffn(h, xn, wq, subkeys, u, v):
    S, D = xn.shape
    nk, half = PEER_N_KEYS, PEER_HALF
    q = _mm(xn, wq, out_dtype=BF16)
    zeros = jnp.zeros((PEER_HEADS, nk, half), F32)
    w1 = jnp.concatenate([subkeys[:, 0], zeros], axis=-1).astype(BF16)
    w2 = jnp.concatenate([zeros, subkeys[:, 1]], axis=-1).astype(BF16)
    st_t = pl.pallas_call(
        _peer_stats_kernel,
        grid=(S // PEER_ST,),
        in_specs=[pl.BlockSpec((PEER_ST, PEER_HEADS * PEER_KEY_DIM), lambda i: (i, 0)),
                  pl.BlockSpec((PEER_HEADS, nk, PEER_KEY_DIM), lambda i: (0, 0, 0)),
                  pl.BlockSpec((PEER_HEADS, nk, PEER_KEY_DIM), lambda i: (0, 0, 0))],
        out_specs=pl.BlockSpec((PEER_STAT_ROWS, PEER_ST), lambda i: (0, i)),
        out_shape=jax.ShapeDtypeStruct((PEER_STAT_ROWS, S), F32),
        compiler_params=_params(1),
        name="peer_stats",
    )(q, w1, w2)
    stats = st_t.T
    top = jnp.repeat(jnp.swapaxes(subkeys[:, 0], 1, 2), nk, axis=2)
    bot = jnp.tile(jnp.swapaxes(subkeys[:, 1], 1, 2), (1, 1, nk))
    kfull = jnp.concatenate([top, bot], axis=1).astype(BF16)
    tm, ec = min(PEER_TM, S), PEER_EC
    wmat = pl.pallas_call(
        _peer_weight_kernel,
        grid=(S // tm, PEER_EXPERTS // ec),
        in_specs=[pl.BlockSpec((tm, D), lambda i, j: (i, 0)),
                  pl.BlockSpec((D, ec), lambda i, j: (0, j)),
                  pl.BlockSpec((tm, PEER_HEADS * PEER_KEY_DIM), lambda i, j: (i, 0)),
                  pl.BlockSpec((PEER_HEADS, PEER_KEY_DIM, ec), lambda i, j: (0, 0, j)),
                  pl.BlockSpec((tm, PEER_STAT_ROWS), lambda i, j: (i, 0))],
        out_specs=pl.BlockSpec((tm, ec), lambda i, j: (i, j)),
        out_shape=jax.ShapeDtypeStruct((S, PEER_EXPERTS), BF16),
        compiler_params=_params(2),
        name="peer_weights",
    )(xn, u.T.astype(BF16), q, kfull, stats)
    return _mm(wmat, v, residual=h)


def kernel(x, norm_mix, w_in, qk_gain, nsa_pe, nsa_w1, nsa_w2, gate_down, gate_up, w_branch, w_out,
           norm_ffn, peer_wq, peer_subkeys, peer_u, peer_v):
    cos, sin = rope_tables(x.shape[1], x.dtype)
    h = x
    for l in range(DEPTH):
        xn = rmsnorm(h, norm_mix[l]).astype(BF16)
        h = token_mixing(h, xn, cos, sin, w_in[l], qk_gain[l], nsa_pe[l], nsa_w1[l], nsa_w2[l],
                         gate_down[l], gate_up[l], w_branch[l], w_out[l])
        xn = rmsnorm(h, norm_ffn[l]).astype(BF16)
        h = peer_ffn(h[0], xn[0], peer_wq[l], peer_subkeys[l], peer_u[l], peer_v[l])[None]
    return h
```

```python
import math, functools
import jax, jax.numpy as jnp
from jax import lax
import numpy as np
from jax.experimental import pallas as pl
from jax.experimental.pallas import tpu as pltpu

D_MODEL = 4096
BATCH = 1
SEQ = 16384
DEPTH = 4

HEAD_DIM = 128
ROPE_THETA = 10000.0
NORM_EPS = 1e-6
SB_HEADS = 4
DSA_HEADS = 4
DSA_KV_HEADS = 1
DSA_IDX_HEADS = 4
DSA_IDX_DIM = 64
DSA_TOPK = 256
NSA_HEADS = 4
NSA_KV_HEADS = 1
NSA_CMP_LEN = 32
NSA_CMP_STRIDE = 16
NSA_SLC_LEN = 32
NSA_N_SEL = 8
NSA_WINDOW = 512
NSA_FORCED_SCORE = 1e4
DIL_PATTERNS = ((128, 1), (512, 4), (2048, 16))
DIL_GROUPS = len(DIL_PATTERNS)
DIL_HEADS_PER_GROUP = 2
DIL_MAX_WINDOW = max(w for w, _ in DIL_PATTERNS)
N_BRANCHES = 4
GATE_RANK = 256
PEER_HEADS = 8
PEER_N_KEYS = 64
PEER_EXPERTS = PEER_N_KEYS ** 2
PEER_KEY_DIM = 128
PEER_TOPK = 16

SB_W = SB_HEADS * HEAD_DIM
DSA_Q_W = DSA_HEADS * HEAD_DIM
DSA_KV_W = DSA_KV_HEADS * HEAD_DIM
DSA_IQ_W = DSA_IDX_HEADS * DSA_IDX_DIM
NSA_Q_W = NSA_HEADS * HEAD_DIM
NSA_KV_W = NSA_KV_HEADS * HEAD_DIM
DIL_Q_W = DIL_GROUPS * DIL_HEADS_PER_GROUP * HEAD_DIM
DIL_KV_W = DIL_HEADS_PER_GROUP * HEAD_DIM
IN_SPLITS = (SB_W, SB_W, SB_W,
             DSA_Q_W, DSA_KV_W, DSA_KV_W, DSA_IQ_W, DSA_IDX_DIM, DSA_IDX_HEADS,
             NSA_Q_W, NSA_KV_W, NSA_KV_W, NSA_KV_W, NSA_KV_W, NSA_KV_W, NSA_KV_W, NSA_HEADS * 3,
             DIL_Q_W, DIL_KV_W, DIL_KV_W)
IN_WIDTH = sum(IN_SPLITS)
SPLIT_OFFSETS = tuple(int(o) for o in np.cumsum(IN_SPLITS)[:-1])
BRANCH_WIDTHS = (SB_W, DSA_Q_W, NSA_Q_W, DIL_KV_W)
BRANCH_WIDTH = sum(BRANCH_WIDTHS)
BRANCH_OFFSETS = tuple(int(o) for o in np.cumsum(BRANCH_WIDTHS)[:-1])
N_QK_GAINS = 8

ATT_SCALE = HEAD_DIM ** -0.5
LOG2_SCALE = ATT_SCALE * math.log2(math.e)
MASKED = -1e30
EXTRACTED = -3e38
INT32_MIN = -2 ** 31
F32_TINY = float(np.finfo(np.float32).tiny)
VMEM_LIMIT = 56 * 1024 * 1024

BF16 = jnp.bfloat16
F32 = jnp.float32


def _params(n_grid):
    return pltpu.CompilerParams(dimension_semantics=("arbitrary",) * n_grid, vmem_limit_bytes=VMEM_LIMIT)


def _resident(shape):
    nd = len(shape)
    return pl.BlockSpec(shape, lambda *_: (0,) * nd, pipeline_mode=pl.Buffered(1))


def _dot_nt(a, b):
    return lax.dot_general(a, b, (((1,), (1,)), ((), ())), preferred_element_type=F32)


def _dot(a, b):
    return jnp.dot(a, b, preferred_element_type=F32)


def _mm_kernel(a_ref, b_ref, o_ref):
    o_ref[...] = _dot(a_ref[...], b_ref[...]).astype(o_ref.dtype)


def _mm_res_kernel(a_ref, b_ref, r_ref, o_ref):
    o_ref[...] = (r_ref[...] + _dot(a_ref[...], b_ref[...])).astype(o_ref.dtype)


def _pick(n, prefs):
    for p in prefs:
        if n % p == 0:
            return p
    return n


def _mm(a, b, out_dtype=F32, residual=None):
    lead = a.shape[:-1]
    K = a.shape[-1]
    N = b.shape[-1]
    a2 = a.reshape(-1, K).astype(jnp.bfloat16)
    M = a2.shape[0]
    n_pad = (-N) % 128
    b2 = b.astype(jnp.bfloat16)
    if n_pad:
        b2 = jnp.pad(b2, ((0, 0), (0, n_pad)))
    Np = N + n_pad
    tm = _pick(M, (1024, 512, 256, 128))
    tn = _pick(Np, (512, 256, 128))
    in_specs = [pl.BlockSpec((tm, K), lambda i, j: (i, 0)),
                pl.BlockSpec((K, tn), lambda i, j: (0, j))]
    args = [a2, b2]
    body = _mm_kernel
    if residual is not None:
        assert n_pad == 0
        in_specs.append(pl.BlockSpec((tm, tn), lambda i, j: (i, j)))
        args.append(residual.reshape(M, N))
        body = _mm_res_kernel
    out = pl.pallas_call(
        body,
        grid=(M // tm, Np // tn),
        in_specs=in_specs,
        out_specs=pl.BlockSpec((tm, tn), lambda i, j: (i, j)),
        out_shape=jax.ShapeDtypeStruct((M, Np), out_dtype),
        compiler_params=pltpu.CompilerParams(
            dimension_semantics=("arbitrary", "arbitrary"),
            vmem_limit_bytes=VMEM_LIMIT),
        name="dense_mm",
    )(*args)
    if n_pad:
        out = out[:, :N]
    return out.reshape(*lead, N)


def rmsnorm(x, g):
    xf = x.astype(jnp.float32)
    y = xf * lax.rsqrt(jnp.mean(xf * xf, axis=-1, keepdims=True) + NORM_EPS)
    return (y * g.astype(jnp.float32)).astype(x.dtype)


def rope_tables(seq, dtype):
    inv = ROPE_THETA ** (-jnp.arange(0, HEAD_DIM, 2, dtype=jnp.float32) / HEAD_DIM)
    ang = jnp.arange(seq, dtype=jnp.float32)[:, None] * inv[None, :]
    return jnp.cos(ang)[None, :, None, :].astype(dtype), jnp.sin(ang)[None, :, None, :].astype(dtype)


def _stack_heads(q_ref, n_heads):
    return jnp.concatenate([q_ref[:, r * HEAD_DIM:(r + 1) * HEAD_DIM] for r in range(n_heads)], axis=0)


def _tile_rows(x, n):
    return jnp.concatenate([x] * n, axis=0)


def _flash_init(m_scr, acc_scr):
    m_scr[...] = jnp.full(m_scr.shape, MASKED, F32)
    acc_scr[...] = jnp.zeros(acc_scr.shape, F32)


def _flash_step(s, bias, v_aug, m_scr, acc_scr, n_heads):
    T, C = bias.shape
    ps, alphas = [], []
    for r in range(n_heads):
        rows = slice(r * T, (r + 1) * T)
        sm = s[rows] + bias
        m_old = m_scr[rows]
        m_new = jnp.maximum(m_old, jnp.max(sm, axis=1, keepdims=True))
        m_scr[rows] = m_new
        p = [jnp.exp2(sm[:, u * 128:(u + 1) * 128] - m_new) for u in range(C // 128)]
        ps.append(jnp.concatenate(p, axis=1).astype(BF16))
        alpha = jnp.exp2(m_old - m_new)
        alphas.append(jnp.concatenate([alpha, alpha], axis=1))
    pv = _dot(jnp.concatenate(ps, axis=0), v_aug)
    acc_scr[...] = jnp.concatenate(alphas, axis=0) * acc_scr[...] + pv


def _flash_result(acc_scr):
    acc = acc_scr[...]
    return acc[:, :HEAD_DIM] / acc[:, HEAD_DIM:]


def _masked_softmax(s, mask):
    sm = jnp.where(mask, s, MASKED)
    m = jnp.max(sm, axis=1, keepdims=True)
    e = jnp.where(mask, jnp.exp2(sm - m), 0.0)
    den = jnp.sum(e, axis=1, keepdims=True)
    return e / jnp.maximum(den, F32_TINY)


SB_T = 256


def _sb_kernel(q_ref, k_ref, v_ref, tri_ref, o_ref, later_scr, acc_scr):
    i = pl.program_id(0)
    T = SB_T
    q0 = pl.multiple_of(i * T, T)
    tri = tri_ref[...]
    later_scr[...] = jnp.zeros(later_scr.shape, F32)
    acc_scr[...] = jnp.zeros(acc_scr.shape, F32)

    def chunk(s0, causal):
        heads = [slice(h * HEAD_DIM, (h + 1) * HEAD_DIM) for h in range(SB_HEADS)]
        zs = [_dot_nt(q_ref[:, cols], k_ref[pl.ds(s0, T), cols]) for cols in heads]
        lms = []
        for z in zs:
            sp = jnp.log2(1.0 + jnp.exp2(-jnp.abs(z)))
            lm = -(jnp.maximum(z, 0.0) + sp)
            if causal is not None:
                lm = jnp.where(causal, lm, 0.0)
            lms.append(lm)
        withins = [_dot(lm.astype(BF16), tri) for lm in lms]
        ws = []
        for h in range(SB_HEADS):
            e = zs[h] + lms[h] + withins[h]
            later = later_scr[h]
            w = jnp.concatenate([jnp.exp2(e[:, u * 128:(u + 1) * 128] + later) for u in range(T // 128)], axis=1)
            if causal is not None:
                w = jnp.where(causal, w, 0.0)
            ws.append(w.astype(BF16))
            later_scr[h] = later + jnp.sum(lms[h], axis=1, keepdims=True)
        for h, cols in enumerate(heads):
            acc_scr[h] += _dot(ws[h], v_ref[pl.ds(s0, T), cols])

    row = lax.broadcasted_iota(jnp.int32, (T, T), 0)
    col = lax.broadcasted_iota(jnp.int32, (T, T), 1)
    chunk(q0, col < row)

    def body(n, carry):
        chunk(pl.multiple_of((i - n) * T, T), None)
        return carry

    lax.fori_loop(1, i + 1, body, 0)
    for h in range(SB_HEADS):
        o_ref[:, h * HEAD_DIM:(h + 1) * HEAD_DIM] = acc_scr[h].astype(o_ref.dtype)


def sb_attention(q, k, v):
    S = q.shape[0]
    T = SB_T
    tri = jnp.tril(jnp.ones((T, T), F32), -1).astype(BF16)
    return pl.pallas_call(
        _sb_kernel,
        grid=(S // T,),
        in_specs=[pl.BlockSpec((T, SB_W), lambda i: (i, 0)),
                  _resident((S, SB_W)), _resident((S, SB_W)), _resident((T, T))],
        out_specs=pl.BlockSpec((T, SB_W), lambda i: (i, 0)),
        out_shape=jax.ShapeDtypeStruct((S, SB_W), BF16),
        scratch_shapes=[pltpu.VMEM((SB_HEADS, T, 128), F32),
                        pltpu.VMEM((SB_HEADS, T, HEAD_DIM), F32)],
        compiler_params=_params(1),
        name="sb_attention",
    )(q, k, v, tri)


DSA_T = 128
DSA_C = 512
DSA_CH = 256


def _dsa_kernel(n_top, q_ref, iq_ref, iw_ref, k_ref, v_ref, ik_ref, tri_ref, o_ref,
                key_scr, m_scr, acc_scr):
    i = pl.program_id(0)
    T, C, CH = DSA_T, DSA_C, DSA_CH
    H = DSA_HEADS
    q0 = i * T
    n_chunks = (q0 + T + C - 1) // C
    tpos = q0 + lax.broadcasted_iota(jnp.int32, (T, 1), 0)
    iw = iw_ref[...]

    def index_keys(c, causal):
        s0 = pl.multiple_of(c * C, C)
        ikc = ik_ref[pl.ds(s0, C), :]
        acc = jnp.zeros((T, C), F32)
        for h in range(DSA_IDX_HEADS):
            d = _dot_nt(iq_ref[:, h * DSA_IDX_DIM:(h + 1) * DSA_IDX_DIM], ikc)
            acc = acc + jnp.maximum(d, 0.0) * iw[:, h:h + 1]
        bits = lax.bitcast_convert_type(acc, jnp.int32)
        key = jnp.where(bits < 0, bits ^ jnp.int32(0x7FFFFFFF), bits)
        key = jnp.where(acc == 0.0, 0, key)
        if causal:
            spos = s0 + lax.broadcasted_iota(jnp.int32, (T, C), 1)
            key = jnp.where(spos <= tpos, key, INT32_MIN)
        key_scr[c] = key

    def idx_body(c, carry):
        index_keys(c, False)
        return carry

    lax.fori_loop(0, n_chunks - 1, idx_body, 0)
    index_keys(n_chunks - 1, True)

    def count_ge(cand):
        def body(c, acc):
            kk = key_scr[c]
            for u in range(C // 128):
                acc = acc + jnp.where(kk[:, u * 128:(u + 1) * 128] >= cand, 1.0, 0.0)
            return acc
        acc = lax.fori_loop(0, n_chunks, body, jnp.zeros((T, 128), F32))
        return jnp.sum(acc, axis=1, keepdims=True)

    def bis_body(b, base):
        cand = base + (jnp.int32(1) << (jnp.int32(31) - b))
        return jnp.where(count_ge(cand) >= n_top, cand, base)

    thr = lax.fori_loop(0, 32, bis_body, jnp.full((T, 128), INT32_MIN, jnp.int32))
    thr = jnp.maximum(thr, INT32_MIN + 1)
    need = jnp.broadcast_to(n_top - count_ge(thr + 1), (T, 128))
    tri = tri_ref[...]

    _flash_init(m_scr, acc_scr)
    qs = _stack_heads(q_ref, H)

    def att_body(c, eq_seen):
        kk = key_scr[c]
        for half in range(C // CH):
            s0 = pl.multiple_of(c * C + half * CH, CH)
            eqs, gts = [], []
            for u in range(CH // 128):
                ku = kk[:, half * CH + u * 128:half * CH + (u + 1) * 128]
                eqs.append(ku == thr)
                gts.append(ku > thr)
            eq = jnp.concatenate([jnp.where(e, 1.0, 0.0) for e in eqs], axis=1).astype(BF16)
            pref = _dot(eq, tri)
            bias = [jnp.where(gts[u] | (eqs[u] & (eq_seen + pref[:, u * 128:(u + 1) * 128] <= need)),
                              0.0, MASKED) for u in range(CH // 128)]
            eq_seen = eq_seen + pref[:, CH:]
            s = _dot_nt(qs, k_ref[pl.ds(s0, CH), :])
            _flash_step(s, jnp.concatenate(bias, axis=1), v_ref[pl.ds(s0, CH), :], m_scr, acc_scr, H)
        return eq_seen

    lax.fori_loop(0, n_chunks, att_body, jnp.zeros((T, 128), F32))
    out = _flash_result(acc_scr)
    for r in range(H):
        o_ref[:, r * HEAD_DIM:(r + 1) * HEAD_DIM] = out[r * T:(r + 1) * T].astype(o_ref.dtype)


def dsa_attention(q, k, v, iq, ik, iw):
    S = q.shape[0]
    T, C, CH = DSA_T, DSA_C, DSA_CH
    n_top = min(DSA_TOPK, S // 4)
    tri = jnp.concatenate([jnp.triu(jnp.ones((CH, CH), F32)), jnp.ones((CH, 128), F32)], axis=1).astype(BF16)
    return pl.pallas_call(
        functools.partial(_dsa_kernel, n_top),
        grid=(S // T,),
        in_specs=[pl.BlockSpec((T, DSA_Q_W), lambda i: (i, 0)),
                  pl.BlockSpec((T, DSA_IQ_W), lambda i: (i, 0)),
                  pl.BlockSpec((T, DSA_IDX_HEADS), lambda i: (i, 0)),
                  _resident((S, HEAD_DIM)), _resident((S, 2 * HEAD_DIM)), _resident((S, DSA_IDX_DIM)),
                  _resident((CH, CH + 128))],
        out_specs=pl.BlockSpec((T, DSA_Q_W), lambda i: (i, 0)),
        out_shape=jax.ShapeDtypeStruct((S, DSA_Q_W), BF16),
        scratch_shapes=[pltpu.VMEM((S // C, T, C), jnp.int32),
                        pltpu.VMEM((DSA_HEADS * T, HEAD_DIM), F32),
                        pltpu.VMEM((DSA_HEADS * T, 2 * HEAD_DIM), F32)],
        compiler_params=_params(1),
        name="dsa_attention",
    )(q, iq, iw, k, v, ik, tri)


NSA_T = 128
NSA_C = 512
NSA_CH = 256
NSA_GROUP = 128


def _nsa_kernel(q_ref, g_ref, kc_ref, vc_ref, ov_ref, ks_ref, vs_ref, kw_ref, vw_ref, e_ref, o_ref,
                sel_scr, m_scr, acc_scr):
    i = pl.program_id(0)
    T, C, CH = NSA_T, NSA_C, NSA_CH
    H = NSA_HEADS
    q0 = pl.multiple_of(i * T, T)
    ncp = kc_ref.shape[0]
    ns = ov_ref.shape[1]
    tpos = q0 + lax.broadcasted_iota(jnp.int32, (T, 1), 0)
    tpos4 = _tile_rows(tpos, H)
    qs = _stack_heads(q_ref, H)

    s_c = _dot_nt(qs, kc_ref[...])
    cmp_end = lax.broadcasted_iota(jnp.int32, (H * T, ncp), 1) * NSA_CMP_STRIDE + (NSA_CMP_LEN - 1)
    p_c = _masked_softmax(s_c, cmp_end <= tpos4)
    o_c = _dot(p_c.astype(BF16), vc_ref[...])

    psum = p_c[0:T]
    for r in range(1, H):
        psum = psum + p_c[r * T:(r + 1) * T]
    ov = ov_ref[...]
    hi = psum.astype(BF16)
    r1 = psum - hi.astype(F32)
    mid = r1.astype(BF16)
    lo = (r1 - mid.astype(F32)).astype(BF16)
    imp = _dot(hi, ov) + _dot(mid, ov) + _dot(lo, ov)

    blk = lax.broadcasted_iota(jnp.int32, (T, ns), 1)
    cur = tpos >> (NSA_SLC_LEN.bit_length() - 1)
    forced = (blk == 0) | (blk == cur) | (blk == cur - 1)
    imp = jnp.where(forced, NSA_FORCED_SCORE, imp)
    imp = jnp.where(blk * NSA_SLC_LEN <= tpos, imp, MASKED)
    blk_f = blk.astype(F32)
    sel = jnp.zeros((T, ns), F32)
    for _ in range(min(NSA_N_SEL, ns)):
        mx = jnp.max(imp, axis=1, keepdims=True)
        first = jnp.min(jnp.where(imp == mx, blk_f, float(ns)), axis=1, keepdims=True)
        hit = blk_f == first
        sel = jnp.where(hit, 1.0, sel)
        imp = jnp.where(hit, EXTRACTED, imp)
    for g in range(ns // NSA_GROUP):
        sel_scr[g] = sel[:, g * NSA_GROUP:(g + 1) * NSA_GROUP].astype(BF16)

    _flash_init(m_scr, acc_scr)
    chunks_per_group = NSA_GROUP * NSA_SLC_LEN // C

    def sel_body(c, carry):
        s0 = pl.multiple_of(c * C, C)
        member = _dot(sel_scr[c // chunks_per_group], e_ref[c % chunks_per_group])
        spos = s0 + lax.broadcasted_iota(jnp.int32, (T, C), 1)
        bias = jnp.where((member > 0.5) & (spos <= tpos), 0.0, MASKED)
        for half in range(C // CH):
            h0 = pl.multiple_of(s0 + half * CH, CH)
            s = _dot_nt(qs, ks_ref[pl.ds(h0, CH), :])
            _flash_step(s, bias[:, half * CH:(half + 1) * CH], vs_ref[pl.ds(h0, CH), :], m_scr, acc_scr, H)
        return carry

    lax.fori_loop(0, (q0 + T + C - 1) // C, sel_body, 0)
    o_s = _flash_result(acc_scr)

    nw = NSA_WINDOW + T
    s_w = _dot_nt(qs, kw_ref[pl.ds(q0, nw), :])
    wpos = q0 - NSA_WINDOW + lax.broadcasted_iota(jnp.int32, (H * T, nw), 1)
    dist = tpos4 - wpos
    p_w = _masked_softmax(s_w, (dist >= 0) & (dist < NSA_WINDOW) & (wpos >= 0))
    o_w = _dot(p_w.astype(BF16), vw_ref[pl.ds(q0, nw), :])

    gate = jax.nn.sigmoid(g_ref[...])
    for r in range(H):
        rows = slice(r * T, (r + 1) * T)
        o = (gate[:, 3 * r:3 * r + 1] * o_c[rows] + gate[:, 3 * r + 1:3 * r + 2] * o_s[rows]
             + gate[:, 3 * r + 2:3 * r + 3] * o_w[rows])
        o_ref[:, r * HEAD_DIM:(r + 1) * HEAD_DIM] = o.astype(o_ref.dtype)


def nsa_attention(q, g, k_cmp, v_cmp, ks, vs, kw, vw):
    S = q.shape[0]
    T, C = NSA_T, NSA_C
    n_cmp = k_cmp.shape[0]
    ncp = -(-n_cmp // 128) * 128
    ns = S // NSA_SLC_LEN
    k_cmp = jnp.pad(k_cmp, ((0, ncp - n_cmp), (0, 0)))
    v_cmp = jnp.pad(v_cmp, ((0, ncp - n_cmp), (0, 0)))
    cmp_start = np.arange(ncp) * NSA_CMP_STRIDE
    slc_start = np.arange(ns) * NSA_SLC_LEN
    overlap = ((cmp_start[:, None] < slc_start[None, :] + NSA_SLC_LEN)
               & (cmp_start[:, None] + NSA_CMP_LEN - 1 >= slc_start[None, :]))
    overlap = jnp.asarray(overlap, BF16)
    cpg = NSA_GROUP * NSA_SLC_LEN // C
    tok_blk = (np.arange(cpg)[:, None] * C + np.arange(C)[None, :]) // NSA_SLC_LEN
    expand = jnp.asarray(np.arange(NSA_GROUP)[None, :, None] == tok_blk[:, None, :], BF16)
    kw = jnp.pad(kw, ((NSA_WINDOW, 0), (0, 0)))
    vw = jnp.pad(vw, ((NSA_WINDOW, 0), (0, 0)))
    return pl.pallas_call(
        _nsa_kernel,
        grid=(S // T,),
        in_specs=[pl.BlockSpec((T, NSA_Q_W), lambda i: (i, 0)),
                  pl.BlockSpec((T, NSA_HEADS * 3), lambda i: (i, 0)),
                  _resident((ncp, HEAD_DIM)), _resident((ncp, HEAD_DIM)), _resident((ncp, ns)),
                  _resident((S, HEAD_DIM)), _resident((S, 2 * HEAD_DIM)),
                  _resident((S + NSA_WINDOW, HEAD_DIM)), _resident((S + NSA_WINDOW, HEAD_DIM)),
                  _resident((cpg, NSA_GROUP, C))],
        out_specs=pl.BlockSpec((T, NSA_Q_W), lambda i: (i, 0)),
        out_shape=jax.ShapeDtypeStruct((S, NSA_Q_W), BF16),
        scratch_shapes=[pltpu.VMEM((ns // NSA_GROUP, T, NSA_GROUP), BF16),
                        pltpu.VMEM((NSA_HEADS * T, HEAD_DIM), F32),
                        pltpu.VMEM((NSA_HEADS * T, 2 * HEAD_DIM), F32)],
        compiler_params=_params(1),
        name="nsa_attention",
    )(q, g, k_cmp, v_cmp, overlap, ks, vs, kw, vw, expand)


def nsa_compress(x, pe, w1, w2):
    S = x.shape[0]
    half = NSA_CMP_STRIDE * HEAD_DIM
    x16 = x.reshape(S // NSA_CMP_STRIDE, half)
    pe_flat = pe.reshape(NSA_CMP_LEN * HEAD_DIM)
    w1_flat = w1.reshape(NSA_CMP_LEN * HEAD_DIM, HEAD_DIM)
    first = _mm(x16 + pe_flat[None, :half], w1_flat[:half])
    second = _mm(x16 + pe_flat[None, half:], w1_flat[half:])
    hid = jax.nn.gelu(first[:-1] + second[1:])
    hid = jnp.pad(hid, ((0, 1), (0, 0)))
    return _mm(hid, w2)[:-1]


DIL_T = 128


def _dil_kernel(q_ref, k_ref, v_ref, o_ref):
    i = pl.program_id(0)
    T = DIL_T
    q0 = pl.multiple_of(i * T, T)
    tl = lax.broadcasted_iota(jnp.int32, (T, 1), 0)
    for s in range(DIL_HEADS_PER_GROUP):
        cols = slice(s * HEAD_DIM, (s + 1) * HEAD_DIM)
        parts = []
        m = jnp.full((T, 1), MASKED, F32)
        for g, (w, r) in enumerate(DIL_PATTERNS):
            n = w + T
            start = pl.multiple_of(q0 + (DIL_MAX_WINDOW - w), T)
            head = g * DIL_HEADS_PER_GROUP + s
            sc = _dot_nt(q_ref[:, head * HEAD_DIM:(head + 1) * HEAD_DIM], k_ref[pl.ds(start, n), cols]) * ATT_SCALE
            b = lax.broadcasted_iota(jnp.int32, (T, n), 1)
            d = tl + w - b
            mask = (d >= 0) & (d <= w) & ((d & (r - 1)) == 0) & (q0 - w + b >= 0)
            sm = jnp.where(mask, sc, MASKED)
            m = jnp.maximum(m, jnp.max(sm, axis=1, keepdims=True))
            parts.append((sm, mask, start, n))
        num = jnp.zeros((T, HEAD_DIM), F32)
        den = jnp.zeros((T, 1), F32)
        for sm, mask, start, n in parts:
            e = jnp.where(mask, jnp.exp(sm - m), 0.0)
            den = den + jnp.sum(e, axis=1, keepdims=True)
            num = num + _dot(e.astype(BF16), v_ref[pl.ds(start, n), cols])
        o_ref[:, cols] = (num / den).astype(o_ref.dtype)


def dilated_attention(q, k, v):
    S = q.shape[0]
    T = DIL_T
    k = jnp.pad(k, ((DIL_MAX_WINDOW, 0), (0, 0)))
    v = jnp.pad(v, ((DIL_MAX_WINDOW, 0), (0, 0)))
    return pl.pallas_call(
        _dil_kernel,
        grid=(S // T,),
        in_specs=[pl.BlockSpec((T, DIL_Q_W), lambda i: (i, 0)),
                  _resident((S + DIL_MAX_WINDOW, DIL_KV_W)), _resident((S + DIL_MAX_WINDOW, DIL_KV_W))],
        out_specs=pl.BlockSpec((T, DIL_KV_W), lambda i: (i, 0)),
        out_shape=jax.ShapeDtypeStruct((S, DIL_KV_W), BF16),
        compiler_params=_params(1),
        name="dilated_attention",
    )(q, k, v)


NORM_TM = 512
NORM_TN = 512


def _norm_mm_kernel(x_ref, g_ref, b_ref, o_ref, xn_ref):
    @pl.when(pl.program_id(1) == 0)
    def _():
        x = x_ref[...]
        y = x * lax.rsqrt(jnp.mean(x * x, axis=-1, keepdims=True) + NORM_EPS)
        xn_ref[...] = (y * g_ref[...]).astype(xn_ref.dtype)
    o_ref[...] = _dot(xn_ref[...], b_ref[...]).astype(o_ref.dtype)


def norm_mm(x, gain, w, out_dtype, return_xn=False):
    M, K = x.shape
    N = w.shape[1]
    tm, tn = min(NORM_TM, M), min(NORM_TN, N)
    main_spec = pl.BlockSpec((tm, tn), lambda i, j: (i, j))
    xn_spec = pl.BlockSpec((tm, K), lambda i, j: (i, 0))
    main_shape = jax.ShapeDtypeStruct((M, N), out_dtype)
    return pl.pallas_call(
        _norm_mm_kernel,
        grid=(M // tm, N // tn),
        in_specs=[xn_spec, pl.BlockSpec((1, K), lambda i, j: (0, 0)), pl.BlockSpec((K, tn), lambda i, j: (0, j))],
        out_specs=(main_spec, xn_spec) if return_xn else main_spec,
        out_shape=(main_shape, jax.ShapeDtypeStruct((M, K), BF16)) if return_xn else main_shape,
        scratch_shapes=[] if return_xn else [pltpu.VMEM((tm, K), BF16)],
        compiler_params=_params(2),
        name="norm_mm",
    )(x, gain.reshape(1, K).astype(F32), w)


PREP_T = 256
PREP_SEGMENTS = (
    ("z", GATE_RANK, None, False, 1.0, "bf16"),
    ("a_q", SB_W, None, False, LOG2_SCALE, "bf16"),
    ("a_k", SB_W, None, False, 1.0, "bf16"),
    ("a_v", SB_W, None, False, 1.0, "bf16"),
    ("b_q", DSA_Q_W, 0, True, LOG2_SCALE, "bf16"),
    ("b_k", DSA_KV_W, 1, True, 1.0, "bf16"),
    ("b_v", DSA_KV_W, None, False, 1.0, "ones"),
    ("b_iq", DSA_IQ_W, None, False, 1.0, "bf16"),
    ("b_ik", HEAD_DIM, None, False, 1.0, "ik"),
    ("c_q", NSA_Q_W, 2, True, LOG2_SCALE, "bf16"),
    ("c_kc", NSA_KV_W, None, True, 1.0, "f32"),
    ("c_vc", NSA_KV_W, None, False, 1.0, "f32"),
    ("c_ks", NSA_KV_W, 4, True, 1.0, "bf16"),
    ("c_vs", NSA_KV_W, None, False, 1.0, "ones"),
    ("c_kw", NSA_KV_W, 5, True, 1.0, "bf16"),
    ("c_vw", NSA_KV_W, None, False, 1.0, "bf16"),
    ("d_q", DIL_Q_W, 6, True, 1.0, "bf16"),
    ("d_k", DIL_KV_W, 7, True, 1.0, "bf16"),
    ("d_v", DIL_KV_W, None, False, 1.0, "bf16"),
    ("misc", HEAD_DIM, None, False, 1.0, "misc"),
)
PREP_WIDTH = sum(seg[1] for seg in PREP_SEGMENTS)
N_MISC = DSA_IDX_HEADS + NSA_HEADS * 3


def _prep_outputs(S):
    shapes = []
    for name, width, _, _, _, kind in PREP_SEGMENTS:
        if kind == "bf16":
            shapes.append((width, BF16))
        elif kind == "ones":
            shapes.append((width + HEAD_DIM, BF16))
        elif kind == "f32":
            shapes.append((width, F32))
        elif kind == "ik":
            shapes.append((DSA_IDX_DIM, BF16))
        else:
            shapes.append((N_MISC, F32))
    return shapes


def _prep_kernel(p_ref, g_ref, cs_ref, sn_ref, *o_refs):
    cs = cs_ref[...]
    sn = sn_ref[...]
    off = 0
    for (name, width, gi, rope, scale, kind), o_ref in zip(PREP_SEGMENTS, o_refs):
        if kind == "ik":
            o_ref[...] = p_ref[:, off:off + DSA_IDX_DIM].astype(o_ref.dtype)
        elif kind == "misc":
            o_ref[...] = p_ref[:, off:off + N_MISC]
        else:
            for hd in range(width // HEAD_DIM):
                lanes = slice(hd * HEAD_DIM, (hd + 1) * HEAD_DIM)
                x = p_ref[:, off + hd * HEAD_DIM:off + (hd + 1) * HEAD_DIM]
                if gi is not None:
                    x = x * lax.rsqrt(jnp.mean(x * x, axis=-1, keepdims=True) + NORM_EPS) * g_ref[gi:gi + 1, :]
                if rope:
                    x = x * cs + pltpu.roll(x, HEAD_DIM // 2, 1) * sn
                if scale != 1.0:
                    x = x * scale
                o_ref[:, lanes] = x.astype(o_ref.dtype)
            if kind == "ones":
                o_ref[:, width:] = jnp.ones((o_ref.shape[0], HEAD_DIM), o_ref.dtype)
        off += width


def prepare_operands(proj, qk_gain, cs, sn):
    S = proj.shape[0]
    T = min(PREP_T, S)
    row = lambda i: (i, 0)
    shapes = _prep_outputs(S)
    outs = pl.pallas_call(
        _prep_kernel,
        grid=(S // T,),
        in_specs=[pl.BlockSpec((T, PREP_WIDTH), row),
                  pl.BlockSpec((N_QK_GAINS, HEAD_DIM), lambda i: (0, 0)),
                  pl.BlockSpec((T, HEAD_DIM), row), pl.BlockSpec((T, HEAD_DIM), row)],
        out_specs=tuple(pl.BlockSpec((T, w), row) for w, _ in shapes),
        out_shape=tuple(jax.ShapeDtypeStruct((S, w), dt) for w, dt in shapes),
        compiler_params=_params(1),
        name="prepare_operands",
    )(proj, qk_gain, cs, sn)
    return {seg[0]: o for seg, o in zip(PREP_SEGMENTS, outs)}


def fused_in_weights(w_in, gate_down):
    names = ("a_q", "a_k", "a_v", "b_q", "b_k", "b_v", "b_iq", "b_ik", "b_iw",
             "c_q", "c_kc", "c_vc", "c_ks", "c_vs", "c_kw", "c_vw", "c_g", "d_q", "d_k", "d_v")
    parts = dict(zip(names, jnp.split(w_in, SPLIT_OFFSETS, axis=1)))
    parts["z"] = gate_down
    parts["misc"] = jnp.concatenate([parts["b_iw"], parts["c_g"]], axis=1)
    cols = []
    for name, width, *_ in PREP_SEGMENTS:
        p = parts[name]
        cols.append(jnp.pad(p, ((0, 0), (0, width - p.shape[1]))))
    return jnp.concatenate(cols, axis=1).astype(BF16)


def token_mixing(h, cs, sn, norm_gain, w_in, qk_gain, nsa_pe, nsa_w1, nsa_w2, gate_down, gate_up, w_branch, w_out):
    proj = norm_mm(h, norm_gain, fused_in_weights(w_in, gate_down), F32)
    t = prepare_operands(proj, qk_gain, cs, sn)
    o_a = sb_attention(t["a_q"], t["a_k"], t["a_v"])
    o_b = dsa_attention(t["b_q"], t["b_k"], t["b_v"], t["b_iq"], t["b_ik"], t["misc"][:, :DSA_IDX_HEADS])
    k_cmp = rmsnorm(nsa_compress(t["c_kc"], nsa_pe[0], nsa_w1[0], nsa_w2[0]), qk_gain[3])
    v_cmp = nsa_compress(t["c_vc"], nsa_pe[1], nsa_w1[1], nsa_w2[1])
    o_c = nsa_attention(t["c_q"], t["misc"][:, DSA_IDX_HEADS:], k_cmp.astype(BF16), v_cmp.astype(BF16),
                        t["c_ks"], t["c_vs"], t["c_kw"], t["c_vw"])
    o_d = dilated_attention(t["d_q"], t["d_k"], t["d_v"])
    merged = gated_merge(t["z"], (o_a, o_b, o_c, o_d), gate_up, w_branch)
    return _mm(merged, w_out, residual=h)


MERGE_TM = 512
MERGE_TN = 512


def _merge_kernel(z_ref, oa_ref, ob_ref, oc_ref, od_ref, gu_ref, wb_ref, o_ref):
    z = z_ref[...]
    acc = None
    off = 0
    for i, br_ref in enumerate((oa_ref, ob_ref, oc_ref, od_ref)):
        width = BRANCH_WIDTHS[i]
        gate = jax.nn.sigmoid(_dot(z, gu_ref[i]))
        term = gate * _dot(br_ref[...], wb_ref[off:off + width, :])
        acc = term if acc is None else acc + term
        off += width
    o_ref[...] = acc.astype(o_ref.dtype)


def gated_merge(z, branches, gate_up, w_branch):
    S = z.shape[0]
    D = w_branch.shape[1]
    tm, tn = min(MERGE_TM, S), min(MERGE_TN, D)
    row = lambda i, j: (i, 0)
    return pl.pallas_call(
        _merge_kernel,
        grid=(S // tm, D // tn),
        in_specs=[pl.BlockSpec((tm, GATE_RANK), row)]
                 + [pl.BlockSpec((tm, w), row) for w in BRANCH_WIDTHS]
                 + [pl.BlockSpec((N_BRANCHES, GATE_RANK, tn), lambda i, j: (0, 0, j)),
                    pl.BlockSpec((BRANCH_WIDTH, tn), lambda i, j: (0, j))],
        out_specs=pl.BlockSpec((tm, tn), lambda i, j: (i, j)),
        out_shape=jax.ShapeDtypeStruct((S, D), BF16),
        compiler_params=_params(2),
        name="gated_merge",
    )(z, *branches, gate_up.astype(BF16), w_branch.astype(BF16))


PEER_ST = 256
PEER_TM = 512
PEER_EC = 512
PEER_HALF = PEER_KEY_DIM // 2
PEER_STAT_ROWS = 32


def _top_rows(scores, n, n_rows):
    st = scores.shape[1]
    rid = lax.broadcasted_iota(jnp.int32, (n_rows, st), 0)
    out = jnp.full((n_rows, st), EXTRACTED, F32)
    for r in range(n):
        m = jnp.max(scores, axis=0, keepdims=True)
        out = jnp.where(rid == r, m, out)
        scores = jnp.where(scores == m, EXTRACTED, scores)
    return out


def _peer_stats_kernel(q_ref, w1_ref, w2_ref, o_ref):
    st = q_ref.shape[0]
    k = PEER_TOPK
    rid = lax.broadcasted_iota(jnp.int32, (PEER_STAT_ROWS, st), 0)
    rid8 = lax.broadcasted_iota(jnp.int32, (8, st), 0)
    stats = jnp.zeros((PEER_STAT_ROWS, st), F32)
    for h in range(PEER_HEADS):
        qh = q_ref[:, h * PEER_KEY_DIM:(h + 1) * PEER_KEY_DIM]
        v1 = _top_rows(_dot_nt(w1_ref[h], qh), k + 1, 24)
        v2 = _top_rows(_dot_nt(w2_ref[h], qh), k + 1, 24)
        v2_top = v2[0:k]
        pieces = [v1[a:a + 1] + v2_top for a in range(k)]
        extra = jnp.where(rid8 == 0, v1[k:k + 1] + v2[0:1],
                          jnp.where(rid8 == 1, v1[0:1] + v2[k:k + 1], EXTRACTED))
        tops = _top_rows(jnp.concatenate(pieces + [extra], axis=0), k + 1, 24)
        c1 = tops[0:1]
        den = jnp.sum(jnp.exp(tops[0:k] - c1), axis=0, keepdims=True)
        thr = 0.5 * (tops[k - 1:k] + tops[k:k + 1])
        stats = jnp.where(rid == h, thr, stats)
        stats = jnp.where(rid == PEER_HEADS + h, c1, stats)
        stats = jnp.where(rid == 2 * PEER_HEADS + h, 1.0 / den, stats)
    o_ref[...] = stats


def _peer_weight_kernel(xn_ref, ut_ref, q_ref, kf_ref, st_ref, o_ref):
    act = jax.nn.gelu(_dot(xn_ref[...], ut_ref[...]))
    st = st_ref[...]
    w = jnp.zeros(act.shape, F32)
    for h in range(PEER_HEADS):
        sf = _dot(q_ref[:, h * PEER_KEY_DIM:(h + 1) * PEER_KEY_DIM], kf_ref[h])
        gate = jnp.exp(sf - st[:, PEER_HEADS + h:PEER_HEADS + h + 1]) * st[:, 2 * PEER_HEADS + h:2 * PEER_HEADS + h + 1]
        w = w + jnp.where(sf >= st[:, h:h + 1], gate, 0.0)
    o_ref[...] = (w * act).astype(o_ref.dtype)


def peer_ffn(h, norm_gain, wq, subkeys, u, v):
    S, D = h.shape
    nk, half = PEER_N_KEYS, PEER_HALF
    q, xn = norm_mm(h, norm_gain, wq.astype(BF16), BF16, return_xn=True)
    zeros = jnp.zeros((PEER_HEADS, nk, half), F32)
    w1 = jnp.concatenate([subkeys[:, 0], zeros], axis=-1).astype(BF16)
    w2 = jnp.concatenate([zeros, subkeys[:, 1]], axis=-1).astype(BF16)
    st_t = pl.pallas_call(
        _peer_stats_kernel,
        grid=(S // PEER_ST,),
        in_specs=[pl.BlockSpec((PEER_ST, PEER_HEADS * PEER_KEY_DIM), lambda i: (i, 0)),
                  pl.BlockSpec((PEER_HEADS, nk, PEER_KEY_DIM), lambda i: (0, 0, 0)),
                  pl.BlockSpec((PEER_HEADS, nk, PEER_KEY_DIM), lambda i: (0, 0, 0))],
        out_specs=pl.BlockSpec((PEER_STAT_ROWS, PEER_ST), lambda i: (0, i)),
        out_shape=jax.ShapeDtypeStruct((PEER_STAT_ROWS, S), F32),
        compiler_params=_params(1),
        name="peer_stats",
    )(q, w1, w2)
    stats = st_t.T
    top = jnp.repeat(jnp.swapaxes(subkeys[:, 0], 1, 2), nk, axis=2)
    bot = jnp.tile(jnp.swapaxes(subkeys[:, 1], 1, 2), (1, 1, nk))
    kfull = jnp.concatenate([top, bot], axis=1).astype(BF16)
    tm, ec = min(PEER_TM, S), PEER_EC
    wmat = pl.pallas_call(
        _peer_weight_kernel,
        grid=(S // tm, PEER_EXPERTS // ec),
        in_specs=[pl.BlockSpec((tm, D), lambda i, j: (i, 0)),
                  pl.BlockSpec((D, ec), lambda i, j: (0, j)),
                  pl.BlockSpec((tm, PEER_HEADS * PEER_KEY_DIM), lambda i, j: (i, 0)),
                  pl.BlockSpec((PEER_HEADS, PEER_KEY_DIM, ec), lambda i, j: (0, 0, j)),
                  pl.BlockSpec((tm, PEER_STAT_ROWS), lambda i, j: (i, 0))],
        out_specs=pl.BlockSpec((tm, ec), lambda i, j: (i, j)),
        out_shape=jax.ShapeDtypeStruct((S, PEER_EXPERTS), BF16),
        compiler_params=_params(2),
        name="peer_weights",
    )(xn, u.T.astype(BF16), q, kfull, stats)
    return _mm(wmat, v, residual=h)


def kernel(x, norm_mix, w_in, qk_gain, nsa_pe, nsa_w1, nsa_w2, gate_down, gate_up, w_branch, w_out,
           norm_ffn, peer_wq, peer_subkeys, peer_u, peer_v):
    S = x.shape[1]
    cos, sin = rope_tables(S, x.dtype)
    cos, sin = cos.reshape(S, HEAD_DIM // 2), sin.reshape(S, HEAD_DIM // 2)
    cs = jnp.concatenate([cos, cos], axis=1)
    sn = jnp.concatenate([-sin, sin], axis=1)
    h = x[0]
    for l in range(DEPTH):
        h = token_mixing(h, cs, sn, norm_mix[l], w_in[l], qk_gain[l], nsa_pe[l], nsa_w1[l], nsa_w2[l],
                         gate_down[l], gate_up[l], w_branch[l], w_out[l])
        h = peer_ffn(h, norm_ffn[l], peer_wq[l], peer_subkeys[l], peer_u[l], peer_v[l])
    return h[None]
```

```python
import math, functools
import jax, jax.numpy as jnp
from jax import lax
import numpy as np
from jax.experimental import pallas as pl
from jax.experimental.pallas import tpu as pltpu

D_MODEL = 4096
BATCH = 1
SEQ = 16384
DEPTH = 4

HEAD_DIM = 128
ROPE_THETA = 10000.0
NORM_EPS = 1e-6
SB_HEADS = 4
DSA_HEADS = 4
DSA_KV_HEADS = 1
DSA_IDX_HEADS = 4
DSA_IDX_DIM = 64
DSA_TOPK = 256
NSA_HEADS = 4
NSA_KV_HEADS = 1
NSA_CMP_LEN = 32
NSA_CMP_STRIDE = 16
NSA_SLC_LEN = 32
NSA_N_SEL = 8
NSA_WINDOW = 512
NSA_FORCED_SCORE = 1e4
DIL_PATTERNS = ((128, 1), (512, 4), (2048, 16))
DIL_GROUPS = len(DIL_PATTERNS)
DIL_HEADS_PER_GROUP = 2
DIL_MAX_WINDOW = max(w for w, _ in DIL_PATTERNS)
N_BRANCHES = 4
GATE_RANK = 256
PEER_HEADS = 8
PEER_N_KEYS = 64
PEER_EXPERTS = PEER_N_KEYS ** 2
PEER_KEY_DIM = 128
PEER_TOPK = 16

SB_W = SB_HEADS * HEAD_DIM
DSA_Q_W = DSA_HEADS * HEAD_DIM
DSA_KV_W = DSA_KV_HEADS * HEAD_DIM
DSA_IQ_W = DSA_IDX_HEADS * DSA_IDX_DIM
NSA_Q_W = NSA_HEADS * HEAD_DIM
NSA_KV_W = NSA_KV_HEADS * HEAD_DIM
DIL_Q_W = DIL_GROUPS * DIL_HEADS_PER_GROUP * HEAD_DIM
DIL_KV_W = DIL_HEADS_PER_GROUP * HEAD_DIM
IN_SPLITS = (SB_W, SB_W, SB_W,
             DSA_Q_W, DSA_KV_W, DSA_KV_W, DSA_IQ_W, DSA_IDX_DIM, DSA_IDX_HEADS,
             NSA_Q_W, NSA_KV_W, NSA_KV_W, NSA_KV_W, NSA_KV_W, NSA_KV_W, NSA_KV_W, NSA_HEADS * 3,
             DIL_Q_W, DIL_KV_W, DIL_KV_W)
IN_WIDTH = sum(IN_SPLITS)
SPLIT_OFFSETS = tuple(int(o) for o in np.cumsum(IN_SPLITS)[:-1])
BRANCH_WIDTHS = (SB_W, DSA_Q_W, NSA_Q_W, DIL_KV_W)
BRANCH_WIDTH = sum(BRANCH_WIDTHS)
BRANCH_OFFSETS = tuple(int(o) for o in np.cumsum(BRANCH_WIDTHS)[:-1])
N_QK_GAINS = 8

ATT_SCALE = HEAD_DIM ** -0.5
LOG2_SCALE = ATT_SCALE * math.log2(math.e)
MASKED = -1e30
EXTRACTED = -3e38
INT32_MIN = -2 ** 31
F32_TINY = float(np.finfo(np.float32).tiny)
VMEM_LIMIT = 56 * 1024 * 1024

BF16 = jnp.bfloat16
F32 = jnp.float32


def _params(n_grid):
    return pltpu.CompilerParams(dimension_semantics=("arbitrary",) * n_grid, vmem_limit_bytes=VMEM_LIMIT)


def _resident(shape):
    nd = len(shape)
    return pl.BlockSpec(shape, lambda *_: (0,) * nd, pipeline_mode=pl.Buffered(1))


def _dot_nt(a, b):
    return lax.dot_general(a, b, (((1,), (1,)), ((), ())), preferred_element_type=F32)


def _dot(a, b):
    return jnp.dot(a, b, preferred_element_type=F32)


def _mm_kernel(a_ref, b_ref, o_ref):
    o_ref[...] = _dot(a_ref[...], b_ref[...]).astype(o_ref.dtype)


def _mm_res_kernel(a_ref, b_ref, r_ref, o_ref):
    o_ref[...] = (r_ref[...] + _dot(a_ref[...], b_ref[...])).astype(o_ref.dtype)


def _pick(n, prefs):
    for p in prefs:
        if n % p == 0:
            return p
    return n


def _mm(a, b, out_dtype=F32, residual=None):
    lead = a.shape[:-1]
    K = a.shape[-1]
    N = b.shape[-1]
    a2 = a.reshape(-1, K).astype(jnp.bfloat16)
    M = a2.shape[0]
    n_pad = (-N) % 128
    b2 = b.astype(jnp.bfloat16)
    if n_pad:
        b2 = jnp.pad(b2, ((0, 0), (0, n_pad)))
    Np = N + n_pad
    tm = _pick(M, (1024, 512, 256, 128))
    tn = _pick(Np, (512, 256, 128))
    in_specs = [pl.BlockSpec((tm, K), lambda i, j: (i, 0)),
                pl.BlockSpec((K, tn), lambda i, j: (0, j))]
    args = [a2, b2]
    body = _mm_kernel
    if residual is not None:
        assert n_pad == 0
        in_specs.append(pl.BlockSpec((tm, tn), lambda i, j: (i, j)))
        args.append(residual.reshape(M, N))
        body = _mm_res_kernel
    out = pl.pallas_call(
        body,
        grid=(M // tm, Np // tn),
        in_specs=in_specs,
        out_specs=pl.BlockSpec((tm, tn), lambda i, j: (i, j)),
        out_shape=jax.ShapeDtypeStruct((M, Np), out_dtype),
        compiler_params=pltpu.CompilerParams(
            dimension_semantics=("arbitrary", "arbitrary"),
            vmem_limit_bytes=VMEM_LIMIT),
        name="dense_mm",
    )(*args)
    if n_pad:
        out = out[:, :N]
    return out.reshape(*lead, N)


def rmsnorm(x, g):
    xf = x.astype(jnp.float32)
    y = xf * lax.rsqrt(jnp.mean(xf * xf, axis=-1, keepdims=True) + NORM_EPS)
    return (y * g.astype(jnp.float32)).astype(x.dtype)


def rope_tables(seq, dtype):
    inv = ROPE_THETA ** (-jnp.arange(0, HEAD_DIM, 2, dtype=jnp.float32) / HEAD_DIM)
    ang = jnp.arange(seq, dtype=jnp.float32)[:, None] * inv[None, :]
    return jnp.cos(ang)[None, :, None, :].astype(dtype), jnp.sin(ang)[None, :, None, :].astype(dtype)


def _stack_heads(q_ref, n_heads):
    return jnp.concatenate([q_ref[:, r * HEAD_DIM:(r + 1) * HEAD_DIM] for r in range(n_heads)], axis=0)


def _tile_rows(x, n):
    return jnp.concatenate([x] * n, axis=0)


def _flash_init(m_scr, acc_scr):
    m_scr[...] = jnp.full(m_scr.shape, MASKED, F32)
    acc_scr[...] = jnp.zeros(acc_scr.shape, F32)


def _flash_step(s, bias, v_aug, m_scr, acc_scr, n_heads):
    T, C = bias.shape
    ps, alphas = [], []
    for r in range(n_heads):
        rows = slice(r * T, (r + 1) * T)
        sm = s[rows] + bias
        m_old = m_scr[rows]
        m_new = jnp.maximum(m_old, jnp.max(sm, axis=1, keepdims=True))
        m_scr[rows] = m_new
        p = [jnp.exp2(sm[:, u * 128:(u + 1) * 128] - m_new) for u in range(C // 128)]
        ps.append(jnp.concatenate(p, axis=1).astype(BF16))
        alpha = jnp.exp2(m_old - m_new)
        alphas.append(jnp.concatenate([alpha, alpha], axis=1))
    pv = _dot(jnp.concatenate(ps, axis=0), v_aug)
    acc_scr[...] = jnp.concatenate(alphas, axis=0) * acc_scr[...] + pv


def _flash_result(acc_scr):
    acc = acc_scr[...]
    return acc[:, :HEAD_DIM] / acc[:, HEAD_DIM:]


def _masked_softmax(s, mask):
    sm = jnp.where(mask, s, MASKED)
    m = jnp.max(sm, axis=1, keepdims=True)
    e = jnp.where(mask, jnp.exp2(sm - m), 0.0)
    den = jnp.sum(e, axis=1, keepdims=True)
    return e / jnp.maximum(den, F32_TINY)


SB_T = 256
SB_DEAD_LOG2 = -200.0


def _sb_kernel(q_ref, k_ref, v_ref, tri_ref, o_ref, later_scr, acc_scr):
    i = pl.program_id(0)
    T = SB_T
    q0 = pl.multiple_of(i * T, T)
    tri = tri_ref[...]
    later_scr[...] = jnp.zeros(later_scr.shape, F32)
    acc_scr[...] = jnp.zeros(acc_scr.shape, F32)

    def chunk(s0, causal):
        heads = [slice(h * HEAD_DIM, (h + 1) * HEAD_DIM) for h in range(SB_HEADS)]
        zs = [_dot_nt(q_ref[:, cols], k_ref[pl.ds(s0, T), cols]) for cols in heads]
        lms = []
        for z in zs:
            sp = jnp.log2(1.0 + jnp.exp2(-jnp.abs(z)))
            lm = -(jnp.maximum(z, 0.0) + sp)
            if causal is not None:
                lm = jnp.where(causal, lm, 0.0)
            lms.append(lm)
        withins = [_dot(lm.astype(BF16), tri) for lm in lms]
        ws = []
        for h in range(SB_HEADS):
            e = zs[h] + lms[h] + withins[h]
            later = later_scr[h]
            w = jnp.concatenate([jnp.exp2(e[:, u * 128:(u + 1) * 128] + later) for u in range(T // 128)], axis=1)
            if causal is not None:
                w = jnp.where(causal, w, 0.0)
            ws.append(w.astype(BF16))
            later_scr[h] = later + jnp.sum(lms[h], axis=1, keepdims=True)
        for h, cols in enumerate(heads):
            acc_scr[h] += _dot(ws[h], v_ref[pl.ds(s0, T), cols])

    row = lax.broadcasted_iota(jnp.int32, (T, T), 0)
    col = lax.broadcasted_iota(jnp.int32, (T, T), 1)
    chunk(q0, col < row)

    def cond(state):
        n, worst = state
        return (n <= i) & (worst > SB_DEAD_LOG2)

    def body(state):
        n, _ = state
        chunk(pl.multiple_of((i - n) * T, T), None)
        return n + 1, jnp.max(later_scr[...])

    lax.while_loop(cond, body, (jnp.int32(1), jnp.max(later_scr[...])))
    for h in range(SB_HEADS):
        o_ref[:, h * HEAD_DIM:(h + 1) * HEAD_DIM] = acc_scr[h].astype(o_ref.dtype)


def sb_attention(q, k, v):
    S = q.shape[0]
    T = SB_T
    tri = jnp.tril(jnp.ones((T, T), F32), -1).astype(BF16)
    return pl.pallas_call(
        _sb_kernel,
        grid=(S // T,),
        in_specs=[pl.BlockSpec((T, SB_W), lambda i: (i, 0)),
                  _resident((S, SB_W)), _resident((S, SB_W)), _resident((T, T))],
        out_specs=pl.BlockSpec((T, SB_W), lambda i: (i, 0)),
        out_shape=jax.ShapeDtypeStruct((S, SB_W), BF16),
        scratch_shapes=[pltpu.VMEM((SB_HEADS, T, 128), F32),
                        pltpu.VMEM((SB_HEADS, T, HEAD_DIM), F32)],
        compiler_params=_params(1),
        name="sb_attention",
    )(q, k, v, tri)


DSA_T = 128
DSA_C = 512
DSA_CH = 256
DSA_BITS_PER_CHECK = 4


def _dsa_kernel(n_top, q_ref, iq_ref, iw_ref, k_ref, v_ref, ik_ref, tri_ref, o_ref,
                key_scr, m_scr, acc_scr):
    i = pl.program_id(0)
    T, C, CH = DSA_T, DSA_C, DSA_CH
    H = DSA_HEADS
    q0 = i * T
    n_chunks = (q0 + T + C - 1) // C
    tpos = q0 + lax.broadcasted_iota(jnp.int32, (T, 1), 0)
    iw = iw_ref[...]
    iq4 = jnp.concatenate([iq_ref[:, h * DSA_IDX_DIM:(h + 1) * DSA_IDX_DIM] for h in range(DSA_IDX_HEADS)], axis=0)

    def index_keys(c, causal):
        s0 = pl.multiple_of(c * C, C)
        d = _dot_nt(iq4, ik_ref[pl.ds(s0, C), :])
        acc = jnp.zeros((T, C), F32)
        for h in range(DSA_IDX_HEADS):
            acc = acc + jnp.maximum(d[h * T:(h + 1) * T], 0.0) * iw[:, h:h + 1]
        bits = lax.bitcast_convert_type(acc, jnp.int32)
        key = jnp.where(bits < 0, bits ^ jnp.int32(0x7FFFFFFF), bits)
        key = jnp.where(acc == 0.0, 0, key)
        if causal:
            spos = s0 + lax.broadcasted_iota(jnp.int32, (T, C), 1)
            key = jnp.where(spos <= tpos, key, INT32_MIN)
        key_scr[c] = key

    def idx_body(c, carry):
        index_keys(c, False)
        return carry

    lax.fori_loop(0, n_chunks - 1, idx_body, 0)
    index_keys(n_chunks - 1, True)

    def count_ge(cand):
        def body(c, acc):
            kk = key_scr[c]
            for u in range(C // 128):
                acc = acc + jnp.where(kk[:, u * 128:(u + 1) * 128] >= cand, 1.0, 0.0)
            return acc
        acc = lax.fori_loop(0, n_chunks, body, jnp.zeros((T, 128), F32))
        return jnp.sum(acc, axis=1, keepdims=True)

    ge0 = count_ge(jnp.zeros((T, 128), jnp.int32))
    gt0 = count_ge(jnp.ones((T, 128), jnp.int32))
    base0 = jnp.where(ge0 >= n_top, 0, jnp.full((T, 128), INT32_MIN, jnp.int32))
    done0 = ((gt0 < n_top) & (ge0 >= n_top)) | (ge0 == n_top) | (tpos + 1 < n_top)
    done0 = jnp.where(jnp.broadcast_to(done0, (T, 128)), 1.0, 0.0)

    def bis_cond(state):
        b, _, _, n_open = state
        return (b >= 0) & (n_open > 0.0)

    def bis_body(state):
        b, base, done, _ = state
        for step in range(DSA_BITS_PER_CHECK):
            bit = b - step
            cand = base + jnp.where(bit >= 0, jnp.int32(1) << jnp.maximum(bit, 0), 0)
            cnt = count_ge(cand)
            base = jnp.where((cnt >= n_top) & (done < 0.5), cand, base)
            done = jnp.where(cnt == n_top, 1.0, done)
        return b - DSA_BITS_PER_CHECK, base, done, jnp.sum(1.0 - done)

    _, thr, _, _ = lax.while_loop(bis_cond, bis_body, (jnp.int32(30), base0, done0, jnp.sum(1.0 - done0)))
    thr = jnp.maximum(thr, INT32_MIN + 1)
    need = jnp.broadcast_to(n_top - count_ge(thr + 1), (T, 128))
    tri = tri_ref[...]

    _flash_init(m_scr, acc_scr)
    qs = _stack_heads(q_ref, H)

    def att_body(c, eq_seen):
        kk = key_scr[c]
        for half in range(C // CH):
            s0 = pl.multiple_of(c * C + half * CH, CH)
            eqs, gts = [], []
            for u in range(CH // 128):
                ku = kk[:, half * CH + u * 128:half * CH + (u + 1) * 128]
                eqs.append(ku == thr)
                gts.append(ku > thr)
            eq = jnp.concatenate([jnp.where(e, 1.0, 0.0) for e in eqs], axis=1).astype(BF16)
            pref = _dot(eq, tri)
            bias = [jnp.where(gts[u] | (eqs[u] & (eq_seen + pref[:, u * 128:(u + 1) * 128] <= need)),
                              0.0, MASKED) for u in range(CH // 128)]
            eq_seen = eq_seen + pref[:, CH:]
            s = _dot_nt(qs, k_ref[pl.ds(s0, CH), :])
            _flash_step(s, jnp.concatenate(bias, axis=1), v_ref[pl.ds(s0, CH), :], m_scr, acc_scr, H)
        return eq_seen

    lax.fori_loop(0, n_chunks, att_body, jnp.zeros((T, 128), F32))
    out = _flash_result(acc_scr)
    for r in range(H):
        o_ref[:, r * HEAD_DIM:(r + 1) * HEAD_DIM] = out[r * T:(r + 1) * T].astype(o_ref.dtype)


def dsa_attention(q, k, v, iq, ik, iw):
    S = q.shape[0]
    T, C, CH = DSA_T, DSA_C, DSA_CH
    n_top = min(DSA_TOPK, S // 4)
    tri = jnp.concatenate([jnp.triu(jnp.ones((CH, CH), F32)), jnp.ones((CH, 128), F32)], axis=1).astype(BF16)
    return pl.pallas_call(
        functools.partial(_dsa_kernel, n_top),
        grid=(S // T,),
        in_specs=[pl.BlockSpec((T, DSA_Q_W), lambda i: (i, 0)),
                  pl.BlockSpec((T, DSA_IQ_W), lambda i: (i, 0)),
                  pl.BlockSpec((T, DSA_IDX_HEADS), lambda i: (i, 0)),
                  _resident((S, HEAD_DIM)), _resident((S, 2 * HEAD_DIM)), _resident((S, DSA_IDX_DIM)),
                  _resident((CH, CH + 128))],
        out_specs=pl.BlockSpec((T, DSA_Q_W), lambda i: (i, 0)),
        out_shape=jax.ShapeDtypeStruct((S, DSA_Q_W), BF16),
        scratch_shapes=[pltpu.VMEM((S // C, T, C), jnp.int32),
                        pltpu.VMEM((DSA_HEADS * T, HEAD_DIM), F32),
                        pltpu.VMEM((DSA_HEADS * T, 2 * HEAD_DIM), F32)],
        compiler_params=_params(1),
        name="dsa_attention",
    )(q, iq, iw, k, v, ik, tri)


NSA_T = 128
NSA_C = 512
NSA_CH = 256
NSA_GROUP = 128


def _nsa_kernel(q_ref, g_ref, kc_ref, vc_ref, ov_ref, ks_ref, vs_ref, kw_ref, vw_ref, e_ref, o_ref,
                sel_scr, m_scr, acc_scr):
    i = pl.program_id(0)
    T, C, CH = NSA_T, NSA_C, NSA_CH
    H = NSA_HEADS
    q0 = pl.multiple_of(i * T, T)
    ncp = kc_ref.shape[0]
    ns = ov_ref.shape[1]
    tpos = q0 + lax.broadcasted_iota(jnp.int32, (T, 1), 0)
    tpos4 = _tile_rows(tpos, H)
    qs = _stack_heads(q_ref, H)

    s_c = _dot_nt(qs, kc_ref[...])
    cmp_end = lax.broadcasted_iota(jnp.int32, (H * T, ncp), 1) * NSA_CMP_STRIDE + (NSA_CMP_LEN - 1)
    p_c = _masked_softmax(s_c, cmp_end <= tpos4)
    o_c = _dot(p_c.astype(BF16), vc_ref[...])

    psum = p_c[0:T]
    for r in range(1, H):
        psum = psum + p_c[r * T:(r + 1) * T]
    ov = ov_ref[...]
    hi = psum.astype(BF16)
    r1 = psum - hi.astype(F32)
    mid = r1.astype(BF16)
    lo = (r1 - mid.astype(F32)).astype(BF16)
    imp = _dot(hi, ov) + _dot(mid, ov) + _dot(lo, ov)

    blk = lax.broadcasted_iota(jnp.int32, (T, ns), 1)
    cur = tpos >> (NSA_SLC_LEN.bit_length() - 1)
    forced = (blk == 0) | (blk == cur) | (blk == cur - 1)
    imp = jnp.where(forced, NSA_FORCED_SCORE, imp)
    imp = jnp.where(blk * NSA_SLC_LEN <= tpos, imp, MASKED)
    blk_f = blk.astype(F32)
    sel = jnp.zeros((T, ns), F32)
    for _ in range(min(NSA_N_SEL, ns)):
        mx = jnp.max(imp, axis=1, keepdims=True)
        first = jnp.min(jnp.where(imp == mx, blk_f, float(ns)), axis=1, keepdims=True)
        hit = blk_f == first
        sel = jnp.where(hit, 1.0, sel)
        imp = jnp.where(hit, EXTRACTED, imp)
    for g in range(ns // NSA_GROUP):
        sel_scr[g] = sel[:, g * NSA_GROUP:(g + 1) * NSA_GROUP].astype(BF16)

    _flash_init(m_scr, acc_scr)
    chunks_per_group = NSA_GROUP * NSA_SLC_LEN // C

    def sel_body(c, carry):
        s0 = pl.multiple_of(c * C, C)
        member = _dot(sel_scr[c // chunks_per_group], e_ref[c % chunks_per_group])
        spos = s0 + lax.broadcasted_iota(jnp.int32, (T, C), 1)
        bias = jnp.where((member > 0.5) & (spos <= tpos), 0.0, MASKED)
        for half in range(C // CH):
            h0 = pl.multiple_of(s0 + half * CH, CH)
            s = _dot_nt(qs, ks_ref[pl.ds(h0, CH), :])
            _flash_step(s, bias[:, half * CH:(half + 1) * CH], vs_ref[pl.ds(h0, CH), :], m_scr, acc_scr, H)
        return carry

    lax.fori_loop(0, (q0 + T + C - 1) // C, sel_body, 0)
    o_s = _flash_result(acc_scr)

    nw = NSA_WINDOW + T
    s_w = _dot_nt(qs, kw_ref[pl.ds(q0, nw), :])
    wpos = q0 - NSA_WINDOW + lax.broadcasted_iota(jnp.int32, (H * T, nw), 1)
    dist = tpos4 - wpos
    p_w = _masked_softmax(s_w, (dist >= 0) & (dist < NSA_WINDOW) & (wpos >= 0))
    o_w = _dot(p_w.astype(BF16), vw_ref[pl.ds(q0, nw), :])

    gate = jax.nn.sigmoid(g_ref[...])
    for r in range(H):
        rows = slice(r * T, (r + 1) * T)
        o = (gate[:, 3 * r:3 * r + 1] * o_c[rows] + gate[:, 3 * r + 1:3 * r + 2] * o_s[rows]
             + gate[:, 3 * r + 2:3 * r + 3] * o_w[rows])
        o_ref[:, r * HEAD_DIM:(r + 1) * HEAD_DIM] = o.astype(o_ref.dtype)


def nsa_attention(q, g, k_cmp, v_cmp, ks, vs, kw, vw):
    S = q.shape[0]
    T, C = NSA_T, NSA_C
    n_cmp = k_cmp.shape[0]
    ncp = -(-n_cmp // 128) * 128
    ns = S // NSA_SLC_LEN
    k_cmp = jnp.pad(k_cmp, ((0, ncp - n_cmp), (0, 0)))
    v_cmp = jnp.pad(v_cmp, ((0, ncp - n_cmp), (0, 0)))
    cmp_start = np.arange(ncp) * NSA_CMP_STRIDE
    slc_start = np.arange(ns) * NSA_SLC_LEN
    overlap = ((cmp_start[:, None] < slc_start[None, :] + NSA_SLC_LEN)
               & (cmp_start[:, None] + NSA_CMP_LEN - 1 >= slc_start[None, :]))
    overlap = jnp.asarray(overlap, BF16)
    cpg = NSA_GROUP * NSA_SLC_LEN // C
    tok_blk = (np.arange(cpg)[:, None] * C + np.arange(C)[None, :]) // NSA_SLC_LEN
    expand = jnp.asarray(np.arange(NSA_GROUP)[None, :, None] == tok_blk[:, None, :], BF16)
    kw = jnp.pad(kw, ((NSA_WINDOW, 0), (0, 0)))
    vw = jnp.pad(vw, ((NSA_WINDOW, 0), (0, 0)))
    return pl.pallas_call(
        _nsa_kernel,
        grid=(S // T,),
        in_specs=[pl.BlockSpec((T, NSA_Q_W), lambda i: (i, 0)),
                  pl.BlockSpec((T, NSA_HEADS * 3), lambda i: (i, 0)),
                  _resident((ncp, HEAD_DIM)), _resident((ncp, HEAD_DIM)), _resident((ncp, ns)),
                  _resident((S, HEAD_DIM)), _resident((S, 2 * HEAD_DIM)),
                  _resident((S + NSA_WINDOW, HEAD_DIM)), _resident((S + NSA_WINDOW, HEAD_DIM)),
                  _resident((cpg, NSA_GROUP, C))],
        out_specs=pl.BlockSpec((T, NSA_Q_W), lambda i: (i, 0)),
        out_shape=jax.ShapeDtypeStruct((S, NSA_Q_W), BF16),
        scratch_shapes=[pltpu.VMEM((ns // NSA_GROUP, T, NSA_GROUP), BF16),
                        pltpu.VMEM((NSA_HEADS * T, HEAD_DIM), F32),
                        pltpu.VMEM((NSA_HEADS * T, 2 * HEAD_DIM), F32)],
        compiler_params=_params(1),
        name="nsa_attention",
    )(q, g, k_cmp, v_cmp, overlap, ks, vs, kw, vw, expand)


def nsa_compress(x, pe, w1, w2):
    S = x.shape[0]
    half = NSA_CMP_STRIDE * HEAD_DIM
    x16 = x.reshape(S // NSA_CMP_STRIDE, half)
    pe_flat = pe.reshape(NSA_CMP_LEN * HEAD_DIM)
    w1_flat = w1.reshape(NSA_CMP_LEN * HEAD_DIM, HEAD_DIM)
    first = _mm(x16 + pe_flat[None, :half], w1_flat[:half])
    second = _mm(x16 + pe_flat[None, half:], w1_flat[half:])
    hid = jax.nn.gelu(first[:-1] + second[1:])
    hid = jnp.pad(hid, ((0, 1), (0, 0)))
    return _mm(hid, w2)[:-1]


DIL_T = 128


def _dil_kernel(q_ref, k_ref, v_ref, o_ref):
    i = pl.program_id(0)
    T = DIL_T
    q0 = pl.multiple_of(i * T, T)
    tl = lax.broadcasted_iota(jnp.int32, (T, 1), 0)
    for s in range(DIL_HEADS_PER_GROUP):
        cols = slice(s * HEAD_DIM, (s + 1) * HEAD_DIM)
        parts = []
        m = jnp.full((T, 1), MASKED, F32)
        for g, (w, r) in enumerate(DIL_PATTERNS):
            n = w + T
            start = pl.multiple_of(q0 + (DIL_MAX_WINDOW - w), T)
            head = g * DIL_HEADS_PER_GROUP + s
            sc = _dot_nt(q_ref[:, head * HEAD_DIM:(head + 1) * HEAD_DIM], k_ref[pl.ds(start, n), cols]) * ATT_SCALE
            b = lax.broadcasted_iota(jnp.int32, (T, n), 1)
            d = tl + w - b
            mask = (d >= 0) & (d <= w) & ((d & (r - 1)) == 0) & (q0 - w + b >= 0)
            sm = jnp.where(mask, sc, MASKED)
            m = jnp.maximum(m, jnp.max(sm, axis=1, keepdims=True))
            parts.append((sm, mask, start, n))
        num = jnp.zeros((T, HEAD_DIM), F32)
        den = jnp.zeros((T, 1), F32)
        for sm, mask, start, n in parts:
            e = jnp.where(mask, jnp.exp(sm - m), 0.0)
            den = den + jnp.sum(e, axis=1, keepdims=True)
            num = num + _dot(e.astype(BF16), v_ref[pl.ds(start, n), cols])
        o_ref[:, cols] = (num / den).astype(o_ref.dtype)


def dilated_attention(q, k, v):
    S = q.shape[0]
    T = DIL_T
    k = jnp.pad(k, ((DIL_MAX_WINDOW, 0), (0, 0)))
    v = jnp.pad(v, ((DIL_MAX_WINDOW, 0), (0, 0)))
    return pl.pallas_call(
        _dil_kernel,
        grid=(S // T,),
        in_specs=[pl.BlockSpec((T, DIL_Q_W), lambda i: (i, 0)),
                  _resident((S + DIL_MAX_WINDOW, DIL_KV_W)), _resident((S + DIL_MAX_WINDOW, DIL_KV_W))],
        out_specs=pl.BlockSpec((T, DIL_KV_W), lambda i: (i, 0)),
        out_shape=jax.ShapeDtypeStruct((S, DIL_KV_W), BF16),
        compiler_params=_params(1),
        name="dilated_attention",
    )(q, k, v)


NORM_TM = 512
NORM_TN = 512


def _norm_mm_kernel(x_ref, g_ref, b_ref, o_ref, xn_ref):
    @pl.when(pl.program_id(1) == 0)
    def _():
        x = x_ref[...]
        y = x * lax.rsqrt(jnp.mean(x * x, axis=-1, keepdims=True) + NORM_EPS)
        xn_ref[...] = (y * g_ref[...]).astype(xn_ref.dtype)
    o_ref[...] = _dot(xn_ref[...], b_ref[...]).astype(o_ref.dtype)


def norm_mm(x, gain, w, out_dtype, return_xn=False):
    M, K = x.shape
    N = w.shape[1]
    tm, tn = min(NORM_TM, M), min(NORM_TN, N)
    main_spec = pl.BlockSpec((tm, tn), lambda i, j: (i, j))
    xn_spec = pl.BlockSpec((tm, K), lambda i, j: (i, 0))
    main_shape = jax.ShapeDtypeStruct((M, N), out_dtype)
    return pl.pallas_call(
        _norm_mm_kernel,
        grid=(M // tm, N // tn),
        in_specs=[xn_spec, pl.BlockSpec((1, K), lambda i, j: (0, 0)), pl.BlockSpec((K, tn), lambda i, j: (0, j))],
        out_specs=(main_spec, xn_spec) if return_xn else main_spec,
        out_shape=(main_shape, jax.ShapeDtypeStruct((M, K), BF16)) if return_xn else main_shape,
        scratch_shapes=[] if return_xn else [pltpu.VMEM((tm, K), BF16)],
        compiler_params=_params(2),
        name="norm_mm",
    )(x, gain.reshape(1, K).astype(F32), w)


PREP_T = 256
PREP_SEGMENTS = (
    ("z", GATE_RANK, None, False, 1.0, "bf16"),
    ("a_q", SB_W, None, False, LOG2_SCALE, "bf16"),
    ("a_k", SB_W, None, False, 1.0, "bf16"),
    ("a_v", SB_W, None, False, 1.0, "bf16"),
    ("b_q", DSA_Q_W, 0, True, LOG2_SCALE, "bf16"),
    ("b_k", DSA_KV_W, 1, True, 1.0, "bf16"),
    ("b_v", DSA_KV_W, None, False, 1.0, "ones"),
    ("b_iq", DSA_IQ_W, None, False, 1.0, "bf16"),
    ("b_ik", HEAD_DIM, None, False, 1.0, "ik"),
    ("c_q", NSA_Q_W, 2, True, LOG2_SCALE, "bf16"),
    ("c_kc", NSA_KV_W, None, True, 1.0, "f32"),
    ("c_vc", NSA_KV_W, None, False, 1.0, "f32"),
    ("c_ks", NSA_KV_W, 4, True, 1.0, "bf16"),
    ("c_vs", NSA_KV_W, None, False, 1.0, "ones"),
    ("c_kw", NSA_KV_W, 5, True, 1.0, "bf16"),
    ("c_vw", NSA_KV_W, None, False, 1.0, "bf16"),
    ("d_q", DIL_Q_W, 6, True, 1.0, "bf16"),
    ("d_k", DIL_KV_W, 7, True, 1.0, "bf16"),
    ("d_v", DIL_KV_W, None, False, 1.0, "bf16"),
    ("misc", HEAD_DIM, None, False, 1.0, "misc"),
)
PREP_WIDTH = sum(seg[1] for seg in PREP_SEGMENTS)
N_MISC = DSA_IDX_HEADS + NSA_HEADS * 3


def _prep_outputs(S):
    shapes = []
    for name, width, _, _, _, kind in PREP_SEGMENTS:
        if kind == "bf16":
            shapes.append((width, BF16))
        elif kind == "ones":
            shapes.append((width + HEAD_DIM, BF16))
        elif kind == "f32":
            shapes.append((width, F32))
        elif kind == "ik":
            shapes.append((DSA_IDX_DIM, BF16))
        else:
            shapes.append((N_MISC, F32))
    return shapes


def _prep_kernel(p_ref, g_ref, cs_ref, sn_ref, *o_refs):
    cs = cs_ref[...]
    sn = sn_ref[...]
    off = 0
    for (name, width, gi, rope, scale, kind), o_ref in zip(PREP_SEGMENTS, o_refs):
        if kind == "ik":
            o_ref[...] = p_ref[:, off:off + DSA_IDX_DIM].astype(o_ref.dtype)
        elif kind == "misc":
            o_ref[...] = p_ref[:, off:off + N_MISC]
        else:
            for hd in range(width // HEAD_DIM):
                lanes = slice(hd * HEAD_DIM, (hd + 1) * HEAD_DIM)
                x = p_ref[:, off + hd * HEAD_DIM:off + (hd + 1) * HEAD_DIM]
                if gi is not None:
                    x = x * lax.rsqrt(jnp.mean(x * x, axis=-1, keepdims=True) + NORM_EPS) * g_ref[gi:gi + 1, :]
                if rope:
                    x = x * cs + pltpu.roll(x, HEAD_DIM // 2, 1) * sn
                if scale != 1.0:
                    x = x * scale
                o_ref[:, lanes] = x.astype(o_ref.dtype)
            if kind == "ones":
                o_ref[:, width:] = jnp.ones((o_ref.shape[0], HEAD_DIM), o_ref.dtype)
        off += width


def prepare_operands(proj, qk_gain, cs, sn):
    S = proj.shape[0]
    T = min(PREP_T, S)
    row = lambda i: (i, 0)
    shapes = _prep_outputs(S)
    outs = pl.pallas_call(
        _prep_kernel,
        grid=(S // T,),
        in_specs=[pl.BlockSpec((T, PREP_WIDTH), row),
                  pl.BlockSpec((N_QK_GAINS, HEAD_DIM), lambda i: (0, 0)),
                  pl.BlockSpec((T, HEAD_DIM), row), pl.BlockSpec((T, HEAD_DIM), row)],
        out_specs=tuple(pl.BlockSpec((T, w), row) for w, _ in shapes),
        out_shape=tuple(jax.ShapeDtypeStruct((S, w), dt) for w, dt in shapes),
        compiler_params=_params(1),
        name="prepare_operands",
    )(proj, qk_gain, cs, sn)
    return {seg[0]: o for seg, o in zip(PREP_SEGMENTS, outs)}


def fused_in_weights(w_in, gate_down):
    names = ("a_q", "a_k", "a_v", "b_q", "b_k", "b_v", "b_iq", "b_ik", "b_iw",
             "c_q", "c_kc", "c_vc", "c_ks", "c_vs", "c_kw", "c_vw", "c_g", "d_q", "d_k", "d_v")
    parts = dict(zip(names, jnp.split(w_in, SPLIT_OFFSETS, axis=1)))
    parts["z"] = gate_down
    parts["misc"] = jnp.concatenate([parts["b_iw"], parts["c_g"]], axis=1)
    cols = []
    for name, width, *_ in PREP_SEGMENTS:
        p = parts[name]
        cols.append(jnp.pad(p, ((0, 0), (0, width - p.shape[1]))))
    return jnp.concatenate(cols, axis=1).astype(BF16)


def token_mixing(h, cs, sn, norm_gain, w_in, qk_gain, nsa_pe, nsa_w1, nsa_w2, gate_down, gate_up, w_branch, w_out):
    proj = norm_mm(h, norm_gain, fused_in_weights(w_in, gate_down), F32)
    t = prepare_operands(proj, qk_gain, cs, sn)
    o_a = sb_attention(t["a_q"], t["a_k"], t["a_v"])
    o_b = dsa_attention(t["b_q"], t["b_k"], t["b_v"], t["b_iq"], t["b_ik"], t["misc"][:, :DSA_IDX_HEADS])
    k_cmp = rmsnorm(nsa_compress(t["c_kc"], nsa_pe[0], nsa_w1[0], nsa_w2[0]), qk_gain[3])
    v_cmp = nsa_compress(t["c_vc"], nsa_pe[1], nsa_w1[1], nsa_w2[1])
    o_c = nsa_attention(t["c_q"], t["misc"][:, DSA_IDX_HEADS:], k_cmp.astype(BF16), v_cmp.astype(BF16),
                        t["c_ks"], t["c_vs"], t["c_kw"], t["c_vw"])
    o_d = dilated_attention(t["d_q"], t["d_k"], t["d_v"])
    merged = gated_merge(t["z"], (o_a, o_b, o_c, o_d), gate_up, w_branch)
    return _mm(merged, w_out, residual=h)


MERGE_TM = 512
MERGE_TN = 512


def _merge_kernel(z_ref, oa_ref, ob_ref, oc_ref, od_ref, gu_ref, wb_ref, o_ref):
    z = z_ref[...]
    acc = None
    off = 0
    for i, br_ref in enumerate((oa_ref, ob_ref, oc_ref, od_ref)):
        width = BRANCH_WIDTHS[i]
        gate = jax.nn.sigmoid(_dot(z, gu_ref[i]))
        term = gate * _dot(br_ref[...], wb_ref[off:off + width, :])
        acc = term if acc is None else acc + term
        off += width
    o_ref[...] = acc.astype(o_ref.dtype)


def gated_merge(z, branches, gate_up, w_branch):
    S = z.shape[0]
    D = w_branch.shape[1]
    tm, tn = min(MERGE_TM, S), min(MERGE_TN, D)
    row = lambda i, j: (i, 0)
    return pl.pallas_call(
        _merge_kernel,
        grid=(S // tm, D // tn),
        in_specs=[pl.BlockSpec((tm, GATE_RANK), row)]
                 + [pl.BlockSpec((tm, w), row) for w in BRANCH_WIDTHS]
                 + [pl.BlockSpec((N_BRANCHES, GATE_RANK, tn), lambda i, j: (0, 0, j)),
                    pl.BlockSpec((BRANCH_WIDTH, tn), lambda i, j: (0, j))],
        out_specs=pl.BlockSpec((tm, tn), lambda i, j: (i, j)),
        out_shape=jax.ShapeDtypeStruct((S, D), BF16),
        compiler_params=_params(2),
        name="gated_merge",
    )(z, *branches, gate_up.astype(BF16), w_branch.astype(BF16))


PEER_ST = 256
PEER_TM = 512
PEER_EC = 512
PEER_HALF = PEER_KEY_DIM // 2
PEER_STAT_ROWS = 32


def _top_rows(scores, n, n_rows):
    st = scores.shape[1]
    rid = lax.broadcasted_iota(jnp.int32, (n_rows, st), 0)
    out = jnp.full((n_rows, st), EXTRACTED, F32)
    for r in range(n):
        m = jnp.max(scores, axis=0, keepdims=True)
        out = jnp.where(rid == r, m, out)
        scores = jnp.where(scores == m, EXTRACTED, scores)
    return out


def _peer_stats_kernel(q_ref, w1_ref, w2_ref, o_ref):
    st = q_ref.shape[0]
    k = PEER_TOPK
    rid = lax.broadcasted_iota(jnp.int32, (PEER_STAT_ROWS, st), 0)
    rid8 = lax.broadcasted_iota(jnp.int32, (8, st), 0)
    stats = jnp.zeros((PEER_STAT_ROWS, st), F32)
    for h in range(PEER_HEADS):
        qh = q_ref[:, h * PEER_KEY_DIM:(h + 1) * PEER_KEY_DIM]
        v1 = _top_rows(_dot_nt(w1_ref[h], qh), k + 1, 24)
        v2 = _top_rows(_dot_nt(w2_ref[h], qh), k + 1, 24)
        v2_top = v2[0:k]
        pieces = [v1[a:a + 1] + v2_top for a in range(k)]
        extra = jnp.where(rid8 == 0, v1[k:k + 1] + v2[0:1],
                          jnp.where(rid8 == 1, v1[0:1] + v2[k:k + 1], EXTRACTED))
        tops = _top_rows(jnp.concatenate(pieces + [extra], axis=0), k + 1, 24)
        c1 = tops[0:1]
        den = jnp.sum(jnp.exp(tops[0:k] - c1), axis=0, keepdims=True)
        thr = 0.5 * (tops[k - 1:k] + tops[k:k + 1])
        stats = jnp.where(rid == h, thr, stats)
        stats = jnp.where(rid == PEER_HEADS + h, c1, stats)
        stats = jnp.where(rid == 2 * PEER_HEADS + h, 1.0 / den, stats)
    o_ref[...] = stats


def _peer_weight_kernel(xn_ref, ut_ref, q_ref, kf_ref, st_ref, o_ref):
    act = jax.nn.gelu(_dot(xn_ref[...], ut_ref[...]))
    st = st_ref[...]
    w = jnp.zeros(act.shape, F32)
    for h in range(PEER_HEADS):
        sf = _dot(q_ref[:, h * PEER_KEY_DIM:(h + 1) * PEER_KEY_DIM], kf_ref[h])
        gate = jnp.exp(sf - st[:, PEER_HEADS + h:PEER_HEADS + h + 1]) * st[:, 2 * PEER_HEADS + h:2 * PEER_HEADS + h + 1]
        w = w + jnp.where(sf >= st[:, h:h + 1], gate, 0.0)
    o_ref[...] = (w * act).astype(o_ref.dtype)


def peer_ffn(h, norm_gain, wq, subkeys, u, v):
    S, D = h.shape
    nk, half = PEER_N_KEYS, PEER_HALF
    q, xn = norm_mm(h, norm_gain, wq.astype(BF16), BF16, return_xn=True)
    zeros = jnp.zeros((PEER_HEADS, nk, half), F32)
    w1 = jnp.concatenate([subkeys[:, 0], zeros], axis=-1).astype(BF16)
    w2 = jnp.concatenate([zeros, subkeys[:, 1]], axis=-1).astype(BF16)
    st_t = pl.pallas_call(
        _peer_stats_kernel,
        grid=(S // PEER_ST,),
        in_specs=[pl.BlockSpec((PEER_ST, PEER_HEADS * PEER_KEY_DIM), lambda i: (i, 0)),
                  pl.BlockSpec((PEER_HEADS, nk, PEER_KEY_DIM), lambda i: (0, 0, 0)),
                  pl.BlockSpec((PEER_HEADS, nk, PEER_KEY_DIM), lambda i: (0, 0, 0))],
        out_specs=pl.BlockSpec((PEER_STAT_ROWS, PEER_ST), lambda i: (0, i)),
        out_shape=jax.ShapeDtypeStruct((PEER_STAT_ROWS, S), F32),
        compiler_params=_params(1),
        name="peer_stats",
    )(q, w1, w2)
    stats = st_t.T
    top = jnp.repeat(jnp.swapaxes(subkeys[:, 0], 1, 2), nk, axis=2)
    bot = jnp.tile(jnp.swapaxes(subkeys[:, 1], 1, 2), (1, 1, nk))
    kfull = jnp.concatenate([top, bot], axis=1).astype(BF16)
    tm, ec = min(PEER_TM, S), PEER_EC
    wmat = pl.pallas_call(
        _peer_weight_kernel,
        grid=(S // tm, PEER_EXPERTS // ec),
        in_specs=[pl.BlockSpec((tm, D), lambda i, j: (i, 0)),
                  pl.BlockSpec((D, ec), lambda i, j: (0, j)),
                  pl.BlockSpec((tm, PEER_HEADS * PEER_KEY_DIM), lambda i, j: (i, 0)),
                  pl.BlockSpec((PEER_HEADS, PEER_KEY_DIM, ec), lambda i, j: (0, 0, j)),
                  pl.BlockSpec((tm, PEER_STAT_ROWS), lambda i, j: (i, 0))],
        out_specs=pl.BlockSpec((tm, ec), lambda i, j: (i, j)),
        out_shape=jax.ShapeDtypeStruct((S, PEER_EXPERTS), BF16),
        compiler_params=_params(2),
        name="peer_weights",
    )(xn, u.T.astype(BF16), q, kfull, stats)
    return _mm(wmat, v, residual=h)


def kernel(x, norm_mix, w_in, qk_gain, nsa_pe, nsa_w1, nsa_w2, gate_down, gate_up, w_branch, w_out,
           norm_ffn, peer_wq, peer_subkeys, peer_u, peer_v):
    S = x.shape[1]
    cos, sin = rope_tables(S, x.dtype)
    cos, sin = cos.reshape(S, HEAD_DIM // 2), sin.reshape(S, HEAD_DIM // 2)
    cs = jnp.concatenate([cos, cos], axis=1)
    sn = jnp.concatenate([-sin, sin], axis=1)
    h = x[0]
    for l in range(DEPTH):
        h = token_mixing(h, cs, sn, norm_mix[l], w_in[l], qk_gain[l], nsa_pe[l], nsa_w1[l], nsa_w2[l],
                         gate_down[l], gate_up[l], w_branch[l], w_out[l])
        h = peer_ffn(h, norm_ffn[l], peer_wq[l], peer_subkeys[l], peer_u[l], peer_v[l])
    return h[None]
```

```python
import math, functools
import jax, jax.numpy as jnp
from jax import lax
import numpy as np
from jax.experimental import pallas as pl
from jax.experimental.pallas import tpu as pltpu

D_MODEL = 4096
BATCH = 1
SEQ = 16384
DEPTH = 4

HEAD_DIM = 128
ROPE_THETA = 10000.0
NORM_EPS = 1e-6
SB_HEADS = 4
DSA_HEADS = 4
DSA_KV_HEADS = 1
DSA_IDX_HEADS = 4
DSA_IDX_DIM = 64
DSA_TOPK = 256
NSA_HEADS = 4
NSA_KV_HEADS = 1
NSA_CMP_LEN = 32
NSA_CMP_STRIDE = 16
NSA_SLC_LEN = 32
NSA_N_SEL = 8
NSA_WINDOW = 512
NSA_FORCED_SCORE = 1e4
DIL_PATTERNS = ((128, 1), (512, 4), (2048, 16))
DIL_GROUPS = len(DIL_PATTERNS)
DIL_HEADS_PER_GROUP = 2
DIL_MAX_WINDOW = max(w for w, _ in DIL_PATTERNS)
N_BRANCHES = 4
GATE_RANK = 256
PEER_HEADS = 8
PEER_N_KEYS = 64
PEER_EXPERTS = PEER_N_KEYS ** 2
PEER_KEY_DIM = 128
PEER_TOPK = 16

SB_W = SB_HEADS * HEAD_DIM
DSA_Q_W = DSA_HEADS * HEAD_DIM
DSA_KV_W = DSA_KV_HEADS * HEAD_DIM
DSA_IQ_W = DSA_IDX_HEADS * DSA_IDX_DIM
NSA_Q_W = NSA_HEADS * HEAD_DIM
NSA_KV_W = NSA_KV_HEADS * HEAD_DIM
DIL_Q_W = DIL_GROUPS * DIL_HEADS_PER_GROUP * HEAD_DIM
DIL_KV_W = DIL_HEADS_PER_GROUP * HEAD_DIM
IN_SPLITS = (SB_W, SB_W, SB_W,
             DSA_Q_W, DSA_KV_W, DSA_KV_W, DSA_IQ_W, DSA_IDX_DIM, DSA_IDX_HEADS,
             NSA_Q_W, NSA_KV_W, NSA_KV_W, NSA_KV_W, NSA_KV_W, NSA_KV_W, NSA_KV_W, NSA_HEADS * 3,
             DIL_Q_W, DIL_KV_W, DIL_KV_W)
IN_WIDTH = sum(IN_SPLITS)
SPLIT_OFFSETS = tuple(int(o) for o in np.cumsum(IN_SPLITS)[:-1])
BRANCH_WIDTHS = (SB_W, DSA_Q_W, NSA_Q_W, DIL_KV_W)
BRANCH_WIDTH = sum(BRANCH_WIDTHS)
BRANCH_OFFSETS = tuple(int(o) for o in np.cumsum(BRANCH_WIDTHS)[:-1])
N_QK_GAINS = 8

ATT_SCALE = HEAD_DIM ** -0.5
LOG2_SCALE = ATT_SCALE * math.log2(math.e)
MASKED = -1e30
EXTRACTED = -3e38
INT32_MIN = -2 ** 31
F32_TINY = float(np.finfo(np.float32).tiny)
VMEM_LIMIT = 56 * 1024 * 1024

BF16 = jnp.bfloat16
F32 = jnp.float32


def _params(n_grid):
    return pltpu.CompilerParams(dimension_semantics=("arbitrary",) * n_grid, vmem_limit_bytes=VMEM_LIMIT)


def _resident(shape):
    nd = len(shape)
    return pl.BlockSpec(shape, lambda *_: (0,) * nd, pipeline_mode=pl.Buffered(1))


def _dot_nt(a, b):
    return lax.dot_general(a, b, (((1,), (1,)), ((), ())), preferred_element_type=F32)


def _dot(a, b):
    return jnp.dot(a, b, preferred_element_type=F32)


def _mm_kernel(a_ref, b_ref, o_ref):
    o_ref[...] = _dot(a_ref[...], b_ref[...]).astype(o_ref.dtype)


def _mm_res_kernel(a_ref, b_ref, r_ref, o_ref):
    o_ref[...] = (r_ref[...] + _dot(a_ref[...], b_ref[...])).astype(o_ref.dtype)


def _pick(n, prefs):
    for p in prefs:
        if n % p == 0:
            return p
    return n


def _mm(a, b, out_dtype=F32, residual=None):
    lead = a.shape[:-1]
    K = a.shape[-1]
    N = b.shape[-1]
    a2 = a.reshape(-1, K).astype(jnp.bfloat16)
    M = a2.shape[0]
    n_pad = (-N) % 128
    b2 = b.astype(jnp.bfloat16)
    if n_pad:
        b2 = jnp.pad(b2, ((0, 0), (0, n_pad)))
    Np = N + n_pad
    tm = _pick(M, (1024, 512, 256, 128))
    tn = _pick(Np, (512, 256, 128))
    in_specs = [pl.BlockSpec((tm, K), lambda i, j: (i, 0)),
                pl.BlockSpec((K, tn), lambda i, j: (0, j))]
    args = [a2, b2]
    body = _mm_kernel
    if residual is not None:
        assert n_pad == 0
        in_specs.append(pl.BlockSpec((tm, tn), lambda i, j: (i, j)))
        args.append(residual.reshape(M, N))
        body = _mm_res_kernel
    out = pl.pallas_call(
        body,
        grid=(M // tm, Np // tn),
        in_specs=in_specs,
        out_specs=pl.BlockSpec((tm, tn), lambda i, j: (i, j)),
        out_shape=jax.ShapeDtypeStruct((M, Np), out_dtype),
        compiler_params=pltpu.CompilerParams(
            dimension_semantics=("arbitrary", "arbitrary"),
            vmem_limit_bytes=VMEM_LIMIT),
        name="dense_mm",
    )(*args)
    if n_pad:
        out = out[:, :N]
    return out.reshape(*lead, N)


def rmsnorm(x, g):
    xf = x.astype(jnp.float32)
    y = xf * lax.rsqrt(jnp.mean(xf * xf, axis=-1, keepdims=True) + NORM_EPS)
    return (y * g.astype(jnp.float32)).astype(x.dtype)


def rope_tables(seq, dtype):
    inv = ROPE_THETA ** (-jnp.arange(0, HEAD_DIM, 2, dtype=jnp.float32) / HEAD_DIM)
    ang = jnp.arange(seq, dtype=jnp.float32)[:, None] * inv[None, :]
    return jnp.cos(ang)[None, :, None, :].astype(dtype), jnp.sin(ang)[None, :, None, :].astype(dtype)


def _stack_heads(q_ref, n_heads):
    return jnp.concatenate([q_ref[:, r * HEAD_DIM:(r + 1) * HEAD_DIM] for r in range(n_heads)], axis=0)


def _tile_rows(x, n):
    return jnp.concatenate([x] * n, axis=0)


def _flash_init(m_scr, acc_scr):
    m_scr[...] = jnp.full(m_scr.shape, MASKED, F32)
    acc_scr[...] = jnp.zeros(acc_scr.shape, F32)


def _flash_step(qs, k_chunk, bias, v_aug, m_scr, acc_scr, n_heads):
    T, C = bias.shape
    s = _dot_nt(qs, k_chunk)
    ps, alphas = [], []
    for r in range(n_heads):
        rows = slice(r * T, (r + 1) * T)
        sm = s[rows] + bias
        m_old = m_scr[rows]
        m_new = jnp.maximum(m_old, jnp.max(sm, axis=1, keepdims=True))
        m_scr[rows] = m_new
        p = [jnp.exp2(sm[:, u * 128:(u + 1) * 128] - m_new) for u in range(C // 128)]
        ps.append(jnp.concatenate(p, axis=1).astype(BF16))
        alpha = jnp.exp2(m_old - m_new)
        alphas.append(jnp.concatenate([alpha, alpha], axis=1))
    pv = _dot(jnp.concatenate(ps, axis=0), v_aug)
    acc_scr[...] = jnp.concatenate(alphas, axis=0) * acc_scr[...] + pv


def _flash_result(acc_scr):
    acc = acc_scr[...]
    return acc[:, :HEAD_DIM] / acc[:, HEAD_DIM:]


def _masked_softmax(s, mask):
    sm = jnp.where(mask, s, MASKED)
    m = jnp.max(sm, axis=1, keepdims=True)
    e = jnp.where(mask, jnp.exp2(sm - m), 0.0)
    den = jnp.sum(e, axis=1, keepdims=True)
    return e / jnp.maximum(den, F32_TINY)


SB_T = 256
SB_DEAD_LOG2 = -200.0


def _sb_kernel(q_ref, k_ref, v_ref, tri_ref, o_ref, later_scr, acc_scr):
    i = pl.program_id(0)
    T = SB_T
    q0 = pl.multiple_of(i * T, T)
    tri = tri_ref[...]
    later_scr[...] = jnp.zeros(later_scr.shape, F32)
    acc_scr[...] = jnp.zeros(acc_scr.shape, F32)

    def chunk(s0, causal):
        heads = [slice(h * HEAD_DIM, (h + 1) * HEAD_DIM) for h in range(SB_HEADS)]
        zs = [_dot_nt(q_ref[:, cols], k_ref[pl.ds(s0, T), cols]) for cols in heads]
        lms = []
        for z in zs:
            sp = jnp.log2(1.0 + jnp.exp2(-jnp.abs(z)))
            lm = -(jnp.maximum(z, 0.0) + sp)
            if causal is not None:
                lm = jnp.where(causal, lm, 0.0)
            lms.append(lm)
        withins = [_dot(lm.astype(BF16), tri) for lm in lms]
        ws = []
        for h in range(SB_HEADS):
            e = zs[h] + lms[h] + withins[h]
            later = later_scr[h]
            w = jnp.concatenate([jnp.exp2(e[:, u * 128:(u + 1) * 128] + later) for u in range(T // 128)], axis=1)
            if causal is not None:
                w = jnp.where(causal, w, 0.0)
            ws.append(w.astype(BF16))
            later_scr[h] = later + jnp.sum(lms[h], axis=1, keepdims=True)
        for h, cols in enumerate(heads):
            acc_scr[h] += _dot(ws[h], v_ref[pl.ds(s0, T), cols])

    row = lax.broadcasted_iota(jnp.int32, (T, T), 0)
    col = lax.broadcasted_iota(jnp.int32, (T, T), 1)
    chunk(q0, col < row)

    def cond(state):
        n, worst = state
        return (n <= i) & (worst > SB_DEAD_LOG2)

    def body(state):
        n, _ = state
        chunk(pl.multiple_of((i - n) * T, T), None)
        return n + 1, jnp.max(later_scr[...])

    lax.while_loop(cond, body, (jnp.int32(1), jnp.max(later_scr[...])))
    for h in range(SB_HEADS):
        o_ref[:, h * HEAD_DIM:(h + 1) * HEAD_DIM] = acc_scr[h].astype(o_ref.dtype)


def sb_attention(q, k, v):
    S = q.shape[0]
    T = SB_T
    tri = jnp.tril(jnp.ones((T, T), F32), -1).astype(BF16)
    return pl.pallas_call(
        _sb_kernel,
        grid=(S // T,),
        in_specs=[pl.BlockSpec((T, SB_W), lambda i: (i, 0)),
                  _resident((S, SB_W)), _resident((S, SB_W)), _resident((T, T))],
        out_specs=pl.BlockSpec((T, SB_W), lambda i: (i, 0)),
        out_shape=jax.ShapeDtypeStruct((S, SB_W), BF16),
        scratch_shapes=[pltpu.VMEM((SB_HEADS, T, 128), F32),
                        pltpu.VMEM((SB_HEADS, T, HEAD_DIM), F32)],
        compiler_params=_params(1),
        name="sb_attention",
    )(q, k, v, tri)


DSA_T = 128
DSA_C = 512
DSA_CH = 256
DSA_BITS_PER_CHECK = 4


def _dsa_kernel(n_top, q_ref, iq_ref, iw_ref, k_ref, v_ref, ik_ref, tri_ref, o_ref,
                key_scr, hi_scr, m_scr, acc_scr):
    i = pl.program_id(0)
    T, C, CH = DSA_T, DSA_C, DSA_CH
    H = DSA_HEADS
    q0 = i * T
    n_chunks = (q0 + T + C - 1) // C
    tpos = q0 + lax.broadcasted_iota(jnp.int32, (T, 1), 0)
    iw = iw_ref[...]
    iq4 = jnp.concatenate([iq_ref[:, h * DSA_IDX_DIM:(h + 1) * DSA_IDX_DIM] for h in range(DSA_IDX_HEADS)], axis=0)

    def index_keys(c, causal):
        s0 = pl.multiple_of(c * C, C)
        d = _dot_nt(iq4, ik_ref[pl.ds(s0, C), :])
        acc = jnp.zeros((T, C), F32)
        for h in range(DSA_IDX_HEADS):
            acc = acc + jnp.maximum(d[h * T:(h + 1) * T], 0.0) * iw[:, h:h + 1]
        bits = lax.bitcast_convert_type(acc, jnp.int32)
        key = jnp.where(bits < 0, bits ^ jnp.int32(0x7FFFFFFF), bits)
        key = jnp.where(acc == 0.0, 0, key)
        if causal:
            spos = s0 + lax.broadcasted_iota(jnp.int32, (T, C), 1)
            key = jnp.where(spos <= tpos, key, INT32_MIN)
        key_scr[c] = key
        hi_scr[c] = (key >> 16).astype(jnp.int16)

    def idx_body(c, carry):
        index_keys(c, False)
        return carry

    lax.fori_loop(0, n_chunks - 1, idx_body, 0)
    index_keys(n_chunks - 1, True)

    def count_ge(cand):
        def body(c, acc):
            kk = key_scr[c]
            for u in range(C // 128):
                acc = acc + jnp.where(kk[:, u * 128:(u + 1) * 128] >= cand, 1.0, 0.0)
            return acc
        acc = lax.fori_loop(0, n_chunks, body, jnp.zeros((T, 128), F32))
        return jnp.sum(acc, axis=1, keepdims=True)

    ge0 = count_ge(jnp.zeros((T, 128), jnp.int32))
    gt0 = count_ge(jnp.ones((T, 128), jnp.int32))
    base0 = jnp.where(ge0 >= n_top, 0, jnp.full((T, 128), INT32_MIN, jnp.int32))
    done0 = ((gt0 < n_top) & (ge0 >= n_top)) | (ge0 == n_top) | (tpos + 1 < n_top)
    done0 = jnp.where(jnp.broadcast_to(done0, (T, 128)), 1.0, 0.0)

    def count_ge_high(cand):
        cand16 = (cand >> 16).astype(jnp.int16)
        one, zero = jnp.ones((), jnp.int16), jnp.zeros((), jnp.int16)

        def body(c, acc):
            kk = hi_scr[c]
            for u in range(C // 128):
                acc = acc + jnp.where(kk[:, u * 128:(u + 1) * 128] >= cand16, one, zero)
            return acc
        acc = lax.fori_loop(0, n_chunks, body, jnp.zeros((T, 128), jnp.int16))
        return jnp.sum(acc.astype(F32), axis=1, keepdims=True)

    def try_bit(count, bit, base, done):
        cand = base + (jnp.int32(1) << bit)
        cnt = count(cand)
        return jnp.where((cnt >= n_top) & (done < 0.5), cand, base), jnp.where(cnt == n_top, 1.0, done)

    def high_body(step, state):
        return try_bit(count_ge_high, jnp.int32(30) - step, *state)

    base, done = lax.fori_loop(0, 15, high_body, (base0, done0))

    def low_cond(state):
        b, _, _, n_open = state
        return (b >= 0) & (n_open > 0.0)

    def low_body(state):
        b, base, done, _ = state
        for step in range(DSA_BITS_PER_CHECK):
            base, done = try_bit(count_ge, b - step, base, done)
        return b - DSA_BITS_PER_CHECK, base, done, jnp.sum(1.0 - done)

    _, thr, _, _ = lax.while_loop(low_cond, low_body, (jnp.int32(15), base, done, jnp.sum(1.0 - done)))
    thr = jnp.maximum(thr, INT32_MIN + 1)
    need = jnp.broadcast_to(n_top - count_ge(thr + 1), (T, 128))
    tri = tri_ref[...]

    _flash_init(m_scr, acc_scr)
    qs = _stack_heads(q_ref, H)

    def att_body(c, eq_seen):
        kk = key_scr[c]
        for half in range(C // CH):
            s0 = pl.multiple_of(c * C + half * CH, CH)
            eqs, gts = [], []
            for u in range(CH // 128):
                ku = kk[:, half * CH + u * 128:half * CH + (u + 1) * 128]
                eqs.append(ku == thr)
                gts.append(ku > thr)
            eq = jnp.concatenate([jnp.where(e, 1.0, 0.0) for e in eqs], axis=1).astype(BF16)
            pref = _dot(eq, tri)
            bias = [jnp.where(gts[u] | (eqs[u] & (eq_seen + pref[:, u * 128:(u + 1) * 128] <= need)),
                              0.0, MASKED) for u in range(CH // 128)]
            eq_seen = eq_seen + pref[:, CH:]
            _flash_step(qs, k_ref[pl.ds(s0, CH), :], jnp.concatenate(bias, axis=1), v_ref[pl.ds(s0, CH), :],
                        m_scr, acc_scr, H)
        return eq_seen

    lax.fori_loop(0, n_chunks, att_body, jnp.zeros((T, 128), F32))
    out = _flash_result(acc_scr)
    for r in range(H):
        o_ref[:, r * HEAD_DIM:(r + 1) * HEAD_DIM] = out[r * T:(r + 1) * T].astype(o_ref.dtype)


def dsa_attention(q, k, v, iq, ik, iw):
    S = q.shape[0]
    T, C, CH = DSA_T, DSA_C, DSA_CH
    n_top = min(DSA_TOPK, S // 4)
    tri = jnp.concatenate([jnp.triu(jnp.ones((CH, CH), F32)), jnp.ones((CH, 128), F32)], axis=1).astype(BF16)
    return pl.pallas_call(
        functools.partial(_dsa_kernel, n_top),
        grid=(S // T,),
        in_specs=[pl.BlockSpec((T, DSA_Q_W), lambda i: (i, 0)),
                  pl.BlockSpec((T, DSA_IQ_W), lambda i: (i, 0)),
                  pl.BlockSpec((T, DSA_IDX_HEADS), lambda i: (i, 0)),
                  _resident((S, HEAD_DIM)), _resident((S, 2 * HEAD_DIM)), _resident((S, DSA_IDX_DIM)),
                  _resident((CH, CH + 128))],
        out_specs=pl.BlockSpec((T, DSA_Q_W), lambda i: (i, 0)),
        out_shape=jax.ShapeDtypeStruct((S, DSA_Q_W), BF16),
        scratch_shapes=[pltpu.VMEM((S // C, T, C), jnp.int32),
                        pltpu.VMEM((S // C, T, C), jnp.int16),
                        pltpu.VMEM((DSA_HEADS * T, HEAD_DIM), F32),
                        pltpu.VMEM((DSA_HEADS * T, 2 * HEAD_DIM), F32)],
        compiler_params=_params(1),
        name="dsa_attention",
    )(q, iq, iw, k, v, ik, tri)


NSA_T = 128
NSA_C = 512
NSA_CH = 256
NSA_GROUP = 128


def _nsa_kernel(q_ref, g_ref, kc_ref, vc_ref, ov_ref, ks_ref, vs_ref, kw_ref, vw_ref, e_ref, o_ref,
                sel_scr, m_scr, acc_scr):
    i = pl.program_id(0)
    T, C, CH = NSA_T, NSA_C, NSA_CH
    H = NSA_HEADS
    q0 = pl.multiple_of(i * T, T)
    ncp = kc_ref.shape[0]
    ns = ov_ref.shape[1]
    tpos = q0 + lax.broadcasted_iota(jnp.int32, (T, 1), 0)
    tpos4 = _tile_rows(tpos, H)
    qs = _stack_heads(q_ref, H)

    s_c = _dot_nt(qs, kc_ref[...])
    cmp_end = lax.broadcasted_iota(jnp.int32, (H * T, ncp), 1) * NSA_CMP_STRIDE + (NSA_CMP_LEN - 1)
    p_c = _masked_softmax(s_c, cmp_end <= tpos4)
    o_c = _dot(p_c.astype(BF16), vc_ref[...])

    psum = p_c[0:T]
    for r in range(1, H):
        psum = psum + p_c[r * T:(r + 1) * T]
    ov = ov_ref[...]
    hi = psum.astype(BF16)
    r1 = psum - hi.astype(F32)
    mid = r1.astype(BF16)
    lo = (r1 - mid.astype(F32)).astype(BF16)
    imp = _dot(hi, ov) + _dot(mid, ov) + _dot(lo, ov)

    blk = lax.broadcasted_iota(jnp.int32, (T, ns), 1)
    cur = tpos >> (NSA_SLC_LEN.bit_length() - 1)
    forced = (blk == 0) | (blk == cur) | (blk == cur - 1)
    imp = jnp.where(forced, NSA_FORCED_SCORE, imp)
    imp = jnp.where(blk * NSA_SLC_LEN <= tpos, imp, MASKED)
    blk_f = blk.astype(F32)
    sel = jnp.zeros((T, ns), F32)
    for _ in range(min(NSA_N_SEL, ns)):
        mx = jnp.max(imp, axis=1, keepdims=True)
        first = jnp.min(jnp.where(imp == mx, blk_f, float(ns)), axis=1, keepdims=True)
        hit = blk_f == first
        sel = jnp.where(hit, 1.0, sel)
        imp = jnp.where(hit, EXTRACTED, imp)
    for g in range(ns // NSA_GROUP):
        sel_scr[g] = sel[:, g * NSA_GROUP:(g + 1) * NSA_GROUP].astype(BF16)

    _flash_init(m_scr, acc_scr)
    chunks_per_group = NSA_GROUP * NSA_SLC_LEN // C

    def sel_body(c, carry):
        s0 = pl.multiple_of(c * C, C)
        member = _dot(sel_scr[c // chunks_per_group], e_ref[c % chunks_per_group])
        spos = s0 + lax.broadcasted_iota(jnp.int32, (T, C), 1)
        bias = jnp.where((member > 0.5) & (spos <= tpos), 0.0, MASKED)
        for half in range(C // CH):
            h0 = pl.multiple_of(s0 + half * CH, CH)
            _flash_step(qs, ks_ref[pl.ds(h0, CH), :], bias[:, half * CH:(half + 1) * CH], vs_ref[pl.ds(h0, CH), :],
                        m_scr, acc_scr, H)
        return carry

    lax.fori_loop(0, (q0 + T + C - 1) // C, sel_body, 0)
    o_s = _flash_result(acc_scr)

    nw = NSA_WINDOW + T
    s_w = _dot_nt(qs, kw_ref[pl.ds(q0, nw), :])
    wpos = q0 - NSA_WINDOW + lax.broadcasted_iota(jnp.int32, (H * T, nw), 1)
    dist = tpos4 - wpos
    p_w = _masked_softmax(s_w, (dist >= 0) & (dist < NSA_WINDOW) & (wpos >= 0))
    o_w = _dot(p_w.astype(BF16), vw_ref[pl.ds(q0, nw), :])

    gate = jax.nn.sigmoid(g_ref[...])
    for r in range(H):
        rows = slice(r * T, (r + 1) * T)
        o = (gate[:, 3 * r:3 * r + 1] * o_c[rows] + gate[:, 3 * r + 1:3 * r + 2] * o_s[rows]
             + gate[:, 3 * r + 2:3 * r + 3] * o_w[rows])
        o_ref[:, r * HEAD_DIM:(r + 1) * HEAD_DIM] = o.astype(o_ref.dtype)


def nsa_attention(q, g, k_cmp, v_cmp, ks, vs, kw, vw):
    S = q.shape[0]
    T, C = NSA_T, NSA_C
    n_cmp = k_cmp.shape[0]
    ncp = -(-n_cmp // 128) * 128
    ns = S // NSA_SLC_LEN
    k_cmp = jnp.pad(k_cmp, ((0, ncp - n_cmp), (0, 0)))
    v_cmp = jnp.pad(v_cmp, ((0, ncp - n_cmp), (0, 0)))
    cmp_start = np.arange(ncp) * NSA_CMP_STRIDE
    slc_start = np.arange(ns) * NSA_SLC_LEN
    overlap = ((cmp_start[:, None] < slc_start[None, :] + NSA_SLC_LEN)
               & (cmp_start[:, None] + NSA_CMP_LEN - 1 >= slc_start[None, :]))
    overlap = jnp.asarray(overlap, BF16)
    cpg = NSA_GROUP * NSA_SLC_LEN // C
    tok_blk = (np.arange(cpg)[:, None] * C + np.arange(C)[None, :]) // NSA_SLC_LEN
    expand = jnp.asarray(np.arange(NSA_GROUP)[None, :, None] == tok_blk[:, None, :], BF16)
    kw = jnp.pad(kw, ((NSA_WINDOW, 0), (0, 0)))
    vw = jnp.pad(vw, ((NSA_WINDOW, 0), (0, 0)))
    return pl.pallas_call(
        _nsa_kernel,
        grid=(S // T,),
        in_specs=[pl.BlockSpec((T, NSA_Q_W), lambda i: (i, 0)),
                  pl.BlockSpec((T, NSA_HEADS * 3), lambda i: (i, 0)),
                  _resident((ncp, HEAD_DIM)), _resident((ncp, HEAD_DIM)), _resident((ncp, ns)),
                  _resident((S, HEAD_DIM)), _resident((S, 2 * HEAD_DIM)),
                  _resident((S + NSA_WINDOW, HEAD_DIM)), _resident((S + NSA_WINDOW, HEAD_DIM)),
                  _resident((cpg, NSA_GROUP, C))],
        out_specs=pl.BlockSpec((T, NSA_Q_W), lambda i: (i, 0)),
        out_shape=jax.ShapeDtypeStruct((S, NSA_Q_W), BF16),
        scratch_shapes=[pltpu.VMEM((ns // NSA_GROUP, T, NSA_GROUP), BF16),
                        pltpu.VMEM((NSA_HEADS * T, HEAD_DIM), F32),
                        pltpu.VMEM((NSA_HEADS * T, 2 * HEAD_DIM), F32)],
        compiler_params=_params(1),
        name="nsa_attention",
    )(q, g, k_cmp, v_cmp, overlap, ks, vs, kw, vw, expand)


def nsa_compress(x, pe, w1, w2):
    S = x.shape[0]
    half = NSA_CMP_STRIDE * HEAD_DIM
    x16 = x.reshape(S // NSA_CMP_STRIDE, half)
    pe_flat = pe.reshape(NSA_CMP_LEN * HEAD_DIM)
    w1_flat = w1.reshape(NSA_CMP_LEN * HEAD_DIM, HEAD_DIM)
    first = _mm(x16 + pe_flat[None, :half], w1_flat[:half])
    second = _mm(x16 + pe_flat[None, half:], w1_flat[half:])
    hid = jax.nn.gelu(first[:-1] + second[1:])
    hid = jnp.pad(hid, ((0, 1), (0, 0)))
    return _mm(hid, w2)[:-1]


DIL_T = 128


def _dil_kernel(q_ref, k_ref, v_ref, o_ref):
    i = pl.program_id(0)
    T = DIL_T
    q0 = pl.multiple_of(i * T, T)
    tl = lax.broadcasted_iota(jnp.int32, (T, 1), 0)
    for s in range(DIL_HEADS_PER_GROUP):
        cols = slice(s * HEAD_DIM, (s + 1) * HEAD_DIM)
        parts = []
        m = jnp.full((T, 1), MASKED, F32)
        for g, (w, r) in enumerate(DIL_PATTERNS):
            n = w + T
            start = pl.multiple_of(q0 + (DIL_MAX_WINDOW - w), T)
            head = g * DIL_HEADS_PER_GROUP + s
            sc = _dot_nt(q_ref[:, head * HEAD_DIM:(head + 1) * HEAD_DIM], k_ref[pl.ds(start, n), cols]) * ATT_SCALE
            b = lax.broadcasted_iota(jnp.int32, (T, n), 1)
            d = tl + w - b
            mask = (d >= 0) & (d <= w) & ((d & (r - 1)) == 0) & (q0 - w + b >= 0)
            sm = jnp.where(mask, sc, MASKED)
            m = jnp.maximum(m, jnp.max(sm, axis=1, keepdims=True))
            parts.append((sm, mask, start, n))
        num = jnp.zeros((T, HEAD_DIM), F32)
        den = jnp.zeros((T, 1), F32)
        for sm, mask, start, n in parts:
            e = jnp.where(mask, jnp.exp(sm - m), 0.0)
            den = den + jnp.sum(e, axis=1, keepdims=True)
            num = num + _dot(e.astype(BF16), v_ref[pl.ds(start, n), cols])
        o_ref[:, cols] = (num / den).astype(o_ref.dtype)


def dilated_attention(q, k, v):
    S = q.shape[0]
    T = DIL_T
    k = jnp.pad(k, ((DIL_MAX_WINDOW, 0), (0, 0)))
    v = jnp.pad(v, ((DIL_MAX_WINDOW, 0), (0, 0)))
    return pl.pallas_call(
        _dil_kernel,
        grid=(S // T,),
        in_specs=[pl.BlockSpec((T, DIL_Q_W), lambda i: (i, 0)),
                  _resident((S + DIL_MAX_WINDOW, DIL_KV_W)), _resident((S + DIL_MAX_WINDOW, DIL_KV_W))],
        out_specs=pl.BlockSpec((T, DIL_KV_W), lambda i: (i, 0)),
        out_shape=jax.ShapeDtypeStruct((S, DIL_KV_W), BF16),
        compiler_params=_params(1),
        name="dilated_attention",
    )(q, k, v)


NORM_TM = 512
NORM_TN = 512


def _norm_mm_kernel(x_ref, g_ref, b_ref, o_ref, xn_ref):
    @pl.when(pl.program_id(1) == 0)
    def _():
        x = x_ref[...]
        y = x * lax.rsqrt(jnp.mean(x * x, axis=-1, keepdims=True) + NORM_EPS)
        xn_ref[...] = (y * g_ref[...]).astype(xn_ref.dtype)
    o_ref[...] = _dot(xn_ref[...], b_ref[...]).astype(o_ref.dtype)


def norm_mm(x, gain, w, out_dtype, return_xn=False):
    M, K = x.shape
    N = w.shape[1]
    tm, tn = min(NORM_TM, M), min(NORM_TN, N)
    main_spec = pl.BlockSpec((tm, tn), lambda i, j: (i, j))
    xn_spec = pl.BlockSpec((tm, K), lambda i, j: (i, 0))
    main_shape = jax.ShapeDtypeStruct((M, N), out_dtype)
    return pl.pallas_call(
        _norm_mm_kernel,
        grid=(M // tm, N // tn),
        in_specs=[xn_spec, pl.BlockSpec((1, K), lambda i, j: (0, 0)), pl.BlockSpec((K, tn), lambda i, j: (0, j))],
        out_specs=(main_spec, xn_spec) if return_xn else main_spec,
        out_shape=(main_shape, jax.ShapeDtypeStruct((M, K), BF16)) if return_xn else main_shape,
        scratch_shapes=[] if return_xn else [pltpu.VMEM((tm, K), BF16)],
        compiler_params=_params(2),
        name="norm_mm",
    )(x, gain.reshape(1, K).astype(F32), w)


PREP_T = 256
PREP_SEGMENTS = (
    ("z", GATE_RANK, None, False, 1.0, "bf16"),
    ("a_q", SB_W, None, False, LOG2_SCALE, "bf16"),
    ("a_k", SB_W, None, False, 1.0, "bf16"),
    ("a_v", SB_W, None, False, 1.0, "bf16"),
    ("b_q", DSA_Q_W, 0, True, LOG2_SCALE, "bf16"),
    ("b_k", DSA_KV_W, 1, True, 1.0, "bf16"),
    ("b_v", DSA_KV_W, None, False, 1.0, "ones"),
    ("b_iq", DSA_IQ_W, None, False, 1.0, "bf16"),
    ("b_ik", HEAD_DIM, None, False, 1.0, "ik"),
    ("c_q", NSA_Q_W, 2, True, LOG2_SCALE, "bf16"),
    ("c_kc", NSA_KV_W, None, True, 1.0, "f32"),
    ("c_vc", NSA_KV_W, None, False, 1.0, "f32"),
    ("c_ks", NSA_KV_W, 4, True, 1.0, "bf16"),
    ("c_vs", NSA_KV_W, None, False, 1.0, "ones"),
    ("c_kw", NSA_KV_W, 5, True, 1.0, "bf16"),
    ("c_vw", NSA_KV_W, None, False, 1.0, "bf16"),
    ("d_q", DIL_Q_W, 6, True, 1.0, "bf16"),
    ("d_k", DIL_KV_W, 7, True, 1.0, "bf16"),
    ("d_v", DIL_KV_W, None, False, 1.0, "bf16"),
    ("misc", HEAD_DIM, None, False, 1.0, "misc"),
)
PREP_WIDTH = sum(seg[1] for seg in PREP_SEGMENTS)
N_MISC = DSA_IDX_HEADS + NSA_HEADS * 3


def _prep_outputs(S):
    shapes = []
    for name, width, _, _, _, kind in PREP_SEGMENTS:
        if kind == "bf16":
            shapes.append((width, BF16))
        elif kind == "ones":
            shapes.append((width + HEAD_DIM, BF16))
        elif kind == "f32":
            shapes.append((width, F32))
        elif kind == "ik":
            shapes.append((DSA_IDX_DIM, BF16))
        else:
            shapes.append((N_MISC, F32))
    return shapes


def _prep_kernel(p_ref, g_ref, cs_ref, sn_ref, *o_refs):
    cs = cs_ref[...]
    sn = sn_ref[...]
    off = 0
    for (name, width, gi, rope, scale, kind), o_ref in zip(PREP_SEGMENTS, o_refs):
        if kind == "ik":
            o_ref[...] = p_ref[:, off:off + DSA_IDX_DIM].astype(o_ref.dtype)
        elif kind == "misc":
            o_ref[...] = p_ref[:, off:off + N_MISC]
        else:
            for hd in range(width // HEAD_DIM):
                lanes = slice(hd * HEAD_DIM, (hd + 1) * HEAD_DIM)
                x = p_ref[:, off + hd * HEAD_DIM:off + (hd + 1) * HEAD_DIM]
                if gi is not None:
                    x = x * lax.rsqrt(jnp.mean(x * x, axis=-1, keepdims=True) + NORM_EPS) * g_ref[gi:gi + 1, :]
                if rope:
                    x = x * cs + pltpu.roll(x, HEAD_DIM // 2, 1) * sn
                if scale != 1.0:
                    x = x * scale
                o_ref[:, lanes] = x.astype(o_ref.dtype)
            if kind == "ones":
                o_ref[:, width:] = jnp.ones((o_ref.shape[0], HEAD_DIM), o_ref.dtype)
        off += width


def prepare_operands(proj, qk_gain, cs, sn):
    S = proj.shape[0]
    T = min(PREP_T, S)
    row = lambda i: (i, 0)
    shapes = _prep_outputs(S)
    outs = pl.pallas_call(
        _prep_kernel,
        grid=(S // T,),
        in_specs=[pl.BlockSpec((T, PREP_WIDTH), row),
                  pl.BlockSpec((N_QK_GAINS, HEAD_DIM), lambda i: (0, 0)),
                  pl.BlockSpec((T, HEAD_DIM), row), pl.BlockSpec((T, HEAD_DIM), row)],
        out_specs=tuple(pl.BlockSpec((T, w), row) for w, _ in shapes),
        out_shape=tuple(jax.ShapeDtypeStruct((S, w), dt) for w, dt in shapes),
        compiler_params=_params(1),
        name="prepare_operands",
    )(proj, qk_gain, cs, sn)
    return {seg[0]: o for seg, o in zip(PREP_SEGMENTS, outs)}


def fused_in_weights(w_in, gate_down):
    names = ("a_q", "a_k", "a_v", "b_q", "b_k", "b_v", "b_iq", "b_ik", "b_iw",
             "c_q", "c_kc", "c_vc", "c_ks", "c_vs", "c_kw", "c_vw", "c_g", "d_q", "d_k", "d_v")
    parts = dict(zip(names, jnp.split(w_in, SPLIT_OFFSETS, axis=1)))
    parts["z"] = gate_down
    parts["misc"] = jnp.concatenate([parts["b_iw"], parts["c_g"]], axis=1)
    cols = []
    for name, width, *_ in PREP_SEGMENTS:
        p = parts[name]
        cols.append(jnp.pad(p, ((0, 0), (0, width - p.shape[1]))))
    return jnp.concatenate(cols, axis=1).astype(BF16)


def token_mixing(h, cs, sn, norm_gain, w_in, qk_gain, nsa_pe, nsa_w1, nsa_w2, gate_down, gate_up, w_branch, w_out):
    proj = norm_mm(h, norm_gain, fused_in_weights(w_in, gate_down), F32)
    t = prepare_operands(proj, qk_gain, cs, sn)
    o_a = sb_attention(t["a_q"], t["a_k"], t["a_v"])
    o_b = dsa_attention(t["b_q"], t["b_k"], t["b_v"], t["b_iq"], t["b_ik"], t["misc"][:, :DSA_IDX_HEADS])
    k_cmp = rmsnorm(nsa_compress(t["c_kc"], nsa_pe[0], nsa_w1[0], nsa_w2[0]), qk_gain[3])
    v_cmp = nsa_compress(t["c_vc"], nsa_pe[1], nsa_w1[1], nsa_w2[1])
    o_c = nsa_attention(t["c_q"], t["misc"][:, DSA_IDX_HEADS:], k_cmp.astype(BF16), v_cmp.astype(BF16),
                        t["c_ks"], t["c_vs"], t["c_kw"], t["c_vw"])
    o_d = dilated_attention(t["d_q"], t["d_k"], t["d_v"])
    merged = gated_merge(t["z"], (o_a, o_b, o_c, o_d), gate_up, w_branch)
    return _mm(merged, w_out, residual=h)


MERGE_TM = 512
MERGE_TN = 512


def _merge_kernel(z_ref, oa_ref, ob_ref, oc_ref, od_ref, gu_ref, wb_ref, o_ref):
    z = z_ref[...]
    acc = None
    off = 0
    for i, br_ref in enumerate((oa_ref, ob_ref, oc_ref, od_ref)):
        width = BRANCH_WIDTHS[i]
        gate = jax.nn.sigmoid(_dot(z, gu_ref[i]))
        term = gate * _dot(br_ref[...], wb_ref[off:off + width, :])
        acc = term if acc is None else acc + term
        off += width
    o_ref[...] = acc.astype(o_ref.dtype)


def gated_merge(z, branches, gate_up, w_branch):
    S = z.shape[0]
    D = w_branch.shape[1]
    tm, tn = min(MERGE_TM, S), min(MERGE_TN, D)
    row = lambda i, j: (i, 0)
    return pl.pallas_call(
        _merge_kernel,
        grid=(S // tm, D // tn),
        in_specs=[pl.BlockSpec((tm, GATE_RANK), row)]
                 + [pl.BlockSpec((tm, w), row) for w in BRANCH_WIDTHS]
                 + [pl.BlockSpec((N_BRANCHES, GATE_RANK, tn), lambda i, j: (0, 0, j)),
                    pl.BlockSpec((BRANCH_WIDTH, tn), lambda i, j: (0, j))],
        out_specs=pl.BlockSpec((tm, tn), lambda i, j: (i, j)),
        out_shape=jax.ShapeDtypeStruct((S, D), BF16),
        compiler_params=_params(2),
        name="gated_merge",
    )(z, *branches, gate_up.astype(BF16), w_branch.astype(BF16))


PEER_ST = 256
PEER_TM = 512
PEER_EC = 512
PEER_HALF = PEER_KEY_DIM // 2
PEER_STAT_ROWS = 32
assert (PEER_TOPK + 1) // 9 == 1 and PEER_TOPK + 1 <= 24


def _top_rows(scores, n, n_rows):
    st = scores.shape[1]
    rid = lax.broadcasted_iota(jnp.int32, (n_rows, st), 0)
    out = jnp.full((n_rows, st), EXTRACTED, F32)
    for r in range(n):
        m = jnp.max(scores, axis=0, keepdims=True)
        out = jnp.where(rid == r, m, out)
        scores = jnp.where(scores == m, EXTRACTED, scores)
    return out


def _peer_stats_kernel(q_ref, w1_ref, w2_ref, o_ref):
    st = q_ref.shape[0]
    k = PEER_TOPK
    rid = lax.broadcasted_iota(jnp.int32, (PEER_STAT_ROWS, st), 0)
    rid8 = lax.broadcasted_iota(jnp.int32, (8, st), 0)
    stats = jnp.zeros((PEER_STAT_ROWS, st), F32)
    for h in range(PEER_HEADS):
        qh = q_ref[:, h * PEER_KEY_DIM:(h + 1) * PEER_KEY_DIM]
        v1 = _top_rows(_dot_nt(w1_ref[h], qh), k + 1, 24)
        v2 = _top_rows(_dot_nt(w2_ref[h], qh), k + 1, 24)
        pieces = [v1[0:1] + v2]
        for a in range(1, 8):
            pieces.append(jnp.where(rid8 < (k + 1) // (a + 1), v1[a:a + 1] + v2[0:8], EXTRACTED))
        pieces.append(v1[8:24] + v2[0:1])
        tops = _top_rows(jnp.concatenate(pieces, axis=0), k + 1, 24)
        c1 = tops[0:1]
        den = jnp.sum(jnp.exp(tops[0:k] - c1), axis=0, keepdims=True)
        thr = 0.5 * (tops[k - 1:k] + tops[k:k + 1])
        stats = jnp.where(rid == h, thr, stats)
        stats = jnp.where(rid == PEER_HEADS + h, c1, stats)
        stats = jnp.where(rid == 2 * PEER_HEADS + h, 1.0 / den, stats)
    o_ref[...] = stats


def _peer_weight_kernel(xn_ref, ut_ref, q_ref, kf_ref, st_ref, o_ref):
    act = jax.nn.gelu(_dot(xn_ref[...], ut_ref[...]))
    st = st_ref[...]
    w = jnp.zeros(act.shape, F32)
    for h in range(PEER_HEADS):
        sf = _dot(q_ref[:, h * PEER_KEY_DIM:(h + 1) * PEER_KEY_DIM], kf_ref[h])
        gate = jnp.exp(sf - st[:, PEER_HEADS + h:PEER_HEADS + h + 1]) * st[:, 2 * PEER_HEADS + h:2 * PEER_HEADS + h + 1]
        w = w + jnp.where(sf >= st[:, h:h + 1], gate, 0.0)
    o_ref[...] = (w * act).astype(o_ref.dtype)


def peer_ffn(h, norm_gain, wq, subkeys, u, v):
    S, D = h.shape
    nk, half = PEER_N_KEYS, PEER_HALF
    q, xn = norm_mm(h, norm_gain, wq.astype(BF16), BF16, return_xn=True)
    zeros = jnp.zeros((PEER_HEADS, nk, half), F32)
    w1 = jnp.concatenate([subkeys[:, 0], zeros], axis=-1).astype(BF16)
    w2 = jnp.concatenate([zeros, subkeys[:, 1]], axis=-1).astype(BF16)
    st_t = pl.pallas_call(
        _peer_stats_kernel,
        grid=(S // PEER_ST,),
        in_specs=[pl.BlockSpec((PEER_ST, PEER_HEADS * PEER_KEY_DIM), lambda i: (i, 0)),
                  pl.BlockSpec((PEER_HEADS, nk, PEER_KEY_DIM), lambda i: (0, 0, 0)),
                  pl.BlockSpec((PEER_HEADS, nk, PEER_KEY_DIM), lambda i: (0, 0, 0))],
        out_specs=pl.BlockSpec((PEER_STAT_ROWS, PEER_ST), lambda i: (0, i)),
        out_shape=jax.ShapeDtypeStruct((PEER_STAT_ROWS, S), F32),
        compiler_params=_params(1),
        name="peer_stats",
    )(q, w1, w2)
    stats = st_t.T
    top = jnp.repeat(jnp.swapaxes(subkeys[:, 0], 1, 2), nk, axis=2)
    bot = jnp.tile(jnp.swapaxes(subkeys[:, 1], 1, 2), (1, 1, nk))
    kfull = jnp.concatenate([top, bot], axis=1).astype(BF16)
    tm, ec = min(PEER_TM, S), PEER_EC
    wmat = pl.pallas_call(
        _peer_weight_kernel,
        grid=(S // tm, PEER_EXPERTS // ec),
        in_specs=[pl.BlockSpec((tm, D), lambda i, j: (i, 0)),
                  pl.BlockSpec((D, ec), lambda i, j: (0, j)),
                  pl.BlockSpec((tm, PEER_HEADS * PEER_KEY_DIM), lambda i, j: (i, 0)),
                  pl.BlockSpec((PEER_HEADS, PEER_KEY_DIM, ec), lambda i, j: (0, 0, j)),
                  pl.BlockSpec((tm, PEER_STAT_ROWS), lambda i, j: (i, 0))],
        out_specs=pl.BlockSpec((tm, ec), lambda i, j: (i, j)),
        out_shape=jax.ShapeDtypeStruct((S, PEER_EXPERTS), BF16),
        compiler_params=_params(2),
        name="peer_weights",
    )(xn, u.T.astype(BF16), q, kfull, stats)
    return _mm(wmat, v, residual=h)


def kernel(x, norm_mix, w_in, qk_gain, nsa_pe, nsa_w1, nsa_w2, gate_down, gate_up, w_branch, w_out,
           norm_ffn, peer_wq, peer_subkeys, peer_u, peer_v):
    S = x.shape[1]
    cos, sin = rope_tables(S, x.dtype)
    cos, sin = cos.reshape(S, HEAD_DIM // 2), sin.reshape(S, HEAD_DIM // 2)
    cs = jnp.concatenate([cos, cos], axis=1)
    sn = jnp.concatenate([-sin, sin], axis=1)
    h = x[0]
    for l in range(DEPTH):
        h = token_mixing(h, cs, sn, norm_mix[l], w_in[l], qk_gain[l], nsa_pe[l], nsa_w1[l], nsa_w2[l],
                         gate_down[l], gate_up[l], w_branch[l], w_out[l])
        h = peer_ffn(h, norm_ffn[l], peer_wq[l], peer_subkeys[l], peer_u[l], peer_v[l])
    return h[None]
```

```python
import math, functools
import jax, jax.numpy as jnp
from jax import lax
import numpy as np
from jax.experimental import pallas as pl
from jax.experimental.pallas import tpu as pltpu

D_MODEL = 4096
BATCH = 1
SEQ = 16384
DEPTH = 4

HEAD_DIM = 128
ROPE_THETA = 10000.0
NORM_EPS = 1e-6
SB_HEADS = 4
DSA_HEADS = 4
DSA_KV_HEADS = 1
DSA_IDX_HEADS = 4
DSA_IDX_DIM = 64
DSA_TOPK = 256
NSA_HEADS = 4
NSA_KV_HEADS = 1
NSA_CMP_LEN = 32
NSA_CMP_STRIDE = 16
NSA_SLC_LEN = 32
NSA_N_SEL = 8
NSA_WINDOW = 512
NSA_FORCED_SCORE = 1e4
DIL_PATTERNS = ((128, 1), (512, 4), (2048, 16))
DIL_GROUPS = len(DIL_PATTERNS)
DIL_HEADS_PER_GROUP = 2
DIL_MAX_WINDOW = max(w for w, _ in DIL_PATTERNS)
N_BRANCHES = 4
GATE_RANK = 256
PEER_HEADS = 8
PEER_N_KEYS = 64
PEER_EXPERTS = PEER_N_KEYS ** 2
PEER_KEY_DIM = 128
PEER_TOPK = 16

SB_W = SB_HEADS * HEAD_DIM
DSA_Q_W = DSA_HEADS * HEAD_DIM
DSA_KV_W = DSA_KV_HEADS * HEAD_DIM
DSA_IQ_W = DSA_IDX_HEADS * DSA_IDX_DIM
NSA_Q_W = NSA_HEADS * HEAD_DIM
NSA_KV_W = NSA_KV_HEADS * HEAD_DIM
DIL_Q_W = DIL_GROUPS * DIL_HEADS_PER_GROUP * HEAD_DIM
DIL_KV_W = DIL_HEADS_PER_GROUP * HEAD_DIM
IN_SPLITS = (SB_W, SB_W, SB_W,
             DSA_Q_W, DSA_KV_W, DSA_KV_W, DSA_IQ_W, DSA_IDX_DIM, DSA_IDX_HEADS,
             NSA_Q_W, NSA_KV_W, NSA_KV_W, NSA_KV_W, NSA_KV_W, NSA_KV_W, NSA_KV_W, NSA_HEADS * 3,
             DIL_Q_W, DIL_KV_W, DIL_KV_W)
IN_WIDTH = sum(IN_SPLITS)
SPLIT_OFFSETS = tuple(int(o) for o in np.cumsum(IN_SPLITS)[:-1])
BRANCH_WIDTHS = (SB_W, DSA_Q_W, NSA_Q_W, DIL_KV_W)
BRANCH_WIDTH = sum(BRANCH_WIDTHS)
BRANCH_OFFSETS = tuple(int(o) for o in np.cumsum(BRANCH_WIDTHS)[:-1])
N_QK_GAINS = 8

ATT_SCALE = HEAD_DIM ** -0.5
LOG2_SCALE = ATT_SCALE * math.log2(math.e)
MASKED = -1e30
EXTRACTED = -3e38
INT32_MIN = -2 ** 31
F32_TINY = float(np.finfo(np.float32).tiny)
VMEM_LIMIT = 56 * 1024 * 1024

BF16 = jnp.bfloat16
F32 = jnp.float32


def _params(n_grid):
    return pltpu.CompilerParams(dimension_semantics=("arbitrary",) * n_grid, vmem_limit_bytes=VMEM_LIMIT)


def _resident(shape):
    nd = len(shape)
    return pl.BlockSpec(shape, lambda *_: (0,) * nd, pipeline_mode=pl.Buffered(1))


def _dot_nt(a, b):
    return lax.dot_general(a, b, (((1,), (1,)), ((), ())), preferred_element_type=F32)


def _dot(a, b):
    return jnp.dot(a, b, preferred_element_type=F32)


def _mm_kernel(a_ref, b_ref, o_ref):
    o_ref[...] = _dot(a_ref[...], b_ref[...]).astype(o_ref.dtype)


def _mm_res_kernel(a_ref, b_ref, r_ref, o_ref):
    o_ref[...] = (r_ref[...] + _dot(a_ref[...], b_ref[...])).astype(o_ref.dtype)


def _pick(n, prefs):
    for p in prefs:
        if n % p == 0:
            return p
    return n


def _mm(a, b, out_dtype=F32, residual=None):
    lead = a.shape[:-1]
    K = a.shape[-1]
    N = b.shape[-1]
    a2 = a.reshape(-1, K).astype(jnp.bfloat16)
    M = a2.shape[0]
    n_pad = (-N) % 128
    b2 = b.astype(jnp.bfloat16)
    if n_pad:
        b2 = jnp.pad(b2, ((0, 0), (0, n_pad)))
    Np = N + n_pad
    tm = _pick(M, (1024, 512, 256, 128))
    tn = _pick(Np, (512, 256, 128))
    in_specs = [pl.BlockSpec((tm, K), lambda i, j: (i, 0)),
                pl.BlockSpec((K, tn), lambda i, j: (0, j))]
    args = [a2, b2]
    body = _mm_kernel
    if residual is not None:
        assert n_pad == 0
        in_specs.append(pl.BlockSpec((tm, tn), lambda i, j: (i, j)))
        args.append(residual.reshape(M, N))
        body = _mm_res_kernel
    out = pl.pallas_call(
        body,
        grid=(M // tm, Np // tn),
        in_specs=in_specs,
        out_specs=pl.BlockSpec((tm, tn), lambda i, j: (i, j)),
        out_shape=jax.ShapeDtypeStruct((M, Np), out_dtype),
        compiler_params=pltpu.CompilerParams(
            dimension_semantics=("arbitrary", "arbitrary"),
            vmem_limit_bytes=VMEM_LIMIT),
        name="dense_mm",
    )(*args)
    if n_pad:
        out = out[:, :N]
    return out.reshape(*lead, N)


def rmsnorm(x, g):
    xf = x.astype(jnp.float32)
    y = xf * lax.rsqrt(jnp.mean(xf * xf, axis=-1, keepdims=True) + NORM_EPS)
    return (y * g.astype(jnp.float32)).astype(x.dtype)


def rope_tables(seq, dtype):
    inv = ROPE_THETA ** (-jnp.arange(0, HEAD_DIM, 2, dtype=jnp.float32) / HEAD_DIM)
    ang = jnp.arange(seq, dtype=jnp.float32)[:, None] * inv[None, :]
    return jnp.cos(ang)[None, :, None, :].astype(dtype), jnp.sin(ang)[None, :, None, :].astype(dtype)


def _stack_heads(q_ref, n_heads):
    return jnp.concatenate([q_ref[:, r * HEAD_DIM:(r + 1) * HEAD_DIM] for r in range(n_heads)], axis=0)


def _tile_rows(x, n):
    return jnp.concatenate([x] * n, axis=0)


def _flash_init(m_scr, acc_scr):
    m_scr[...] = jnp.full(m_scr.shape, MASKED, F32)
    acc_scr[...] = jnp.zeros(acc_scr.shape, F32)


def _flash_step(qs, k_chunk, bias, v_aug, m_scr, acc_scr, n_heads):
    T, C = bias.shape
    s = _dot_nt(qs, k_chunk)
    ps, alphas = [], []
    for r in range(n_heads):
        rows = slice(r * T, (r + 1) * T)
        sm = s[rows] + bias
        m_old = m_scr[rows]
        m_new = jnp.maximum(m_old, jnp.max(sm, axis=1, keepdims=True))
        m_scr[rows] = m_new
        p = [jnp.exp2(sm[:, u * 128:(u + 1) * 128] - m_new) for u in range(C // 128)]
        ps.append(jnp.concatenate(p, axis=1).astype(BF16))
        alpha = jnp.exp2(m_old - m_new)
        alphas.append(jnp.concatenate([alpha, alpha], axis=1))
    pv = _dot(jnp.concatenate(ps, axis=0), v_aug)
    acc_scr[...] = jnp.concatenate(alphas, axis=0) * acc_scr[...] + pv


def _flash_result(acc_scr):
    acc = acc_scr[...]
    return acc[:, :HEAD_DIM] / acc[:, HEAD_DIM:]


def _masked_softmax(s, mask):
    sm = jnp.where(mask, s, MASKED)
    m = jnp.max(sm, axis=1, keepdims=True)
    e = jnp.where(mask, jnp.exp2(sm - m), 0.0)
    den = jnp.sum(e, axis=1, keepdims=True)
    return e * (1.0 / jnp.maximum(den, F32_TINY))


SB_T = 256
SB_DEAD_LOG2 = -200.0


def _sb_kernel(q_ref, k_ref, v_ref, tri_ref, o_ref, later_scr, acc_scr):
    i = pl.program_id(0)
    T = SB_T
    q0 = pl.multiple_of(i * T, T)
    tri = tri_ref[...]
    later_scr[...] = jnp.zeros(later_scr.shape, F32)
    acc_scr[...] = jnp.zeros(acc_scr.shape, F32)

    def chunk(s0, causal):
        heads = [slice(h * HEAD_DIM, (h + 1) * HEAD_DIM) for h in range(SB_HEADS)]
        zs = [_dot_nt(q_ref[:, cols], k_ref[pl.ds(s0, T), cols]) for cols in heads]
        lms = []
        for z in zs:
            sp = jnp.log2(1.0 + jnp.exp2(-jnp.abs(z)))
            lm = -(jnp.maximum(z, 0.0) + sp)
            if causal is not None:
                lm = jnp.where(causal, lm, 0.0)
            lms.append(lm)
        withins = [_dot(lm.astype(BF16), tri) for lm in lms]
        ws = []
        for h in range(SB_HEADS):
            e = zs[h] + lms[h] + withins[h]
            later = later_scr[h]
            w = jnp.concatenate([jnp.exp2(e[:, u * 128:(u + 1) * 128] + later) for u in range(T // 128)], axis=1)
            if causal is not None:
                w = jnp.where(causal, w, 0.0)
            ws.append(w.astype(BF16))
            later_scr[h] = later + jnp.sum(lms[h], axis=1, keepdims=True)
        for h, cols in enumerate(heads):
            acc_scr[h] += _dot(ws[h], v_ref[pl.ds(s0, T), cols])

    row = lax.broadcasted_iota(jnp.int32, (T, T), 0)
    col = lax.broadcasted_iota(jnp.int32, (T, T), 1)
    chunk(q0, col < row)

    def cond(state):
        n, worst = state
        return (n <= i) & (worst > SB_DEAD_LOG2)

    def body(state):
        n, _ = state
        chunk(pl.multiple_of((i - n) * T, T), None)
        return n + 1, jnp.max(later_scr[...])

    lax.while_loop(cond, body, (jnp.int32(1), jnp.max(later_scr[...])))
    for h in range(SB_HEADS):
        o_ref[:, h * HEAD_DIM:(h + 1) * HEAD_DIM] = acc_scr[h].astype(o_ref.dtype)


def sb_attention(q, k, v):
    S = q.shape[0]
    T = SB_T
    tri = jnp.tril(jnp.ones((T, T), F32), -1).astype(BF16)
    return pl.pallas_call(
        _sb_kernel,
        grid=(S // T,),
        in_specs=[pl.BlockSpec((T, SB_W), lambda i: (i, 0)),
                  _resident((S, SB_W)), _resident((S, SB_W)), _resident((T, T))],
        out_specs=pl.BlockSpec((T, SB_W), lambda i: (i, 0)),
        out_shape=jax.ShapeDtypeStruct((S, SB_W), BF16),
        scratch_shapes=[pltpu.VMEM((SB_HEADS, T, 128), F32),
                        pltpu.VMEM((SB_HEADS, T, HEAD_DIM), F32)],
        compiler_params=_params(1),
        name="sb_attention",
    )(q, k, v, tri)


DSA_T = 128
DSA_C = 512
DSA_CH = 256
DSA_BITS_PER_CHECK = 4


def _dsa_kernel(n_top, q_ref, iq_ref, iw_ref, k_ref, v_ref, ik_ref, tri_ref, o_ref,
                key_scr, m_scr, acc_scr):
    i = pl.program_id(0)
    T, C, CH = DSA_T, DSA_C, DSA_CH
    H = DSA_HEADS
    q0 = i * T
    n_chunks = (q0 + T + C - 1) // C
    tpos = q0 + lax.broadcasted_iota(jnp.int32, (T, 1), 0)
    iw = iw_ref[...]
    iq4 = jnp.concatenate([iq_ref[:, h * DSA_IDX_DIM:(h + 1) * DSA_IDX_DIM] for h in range(DSA_IDX_HEADS)], axis=0)

    def index_keys(c, causal):
        s0 = pl.multiple_of(c * C, C)
        d = _dot_nt(iq4, ik_ref[pl.ds(s0, C), :])
        acc = jnp.zeros((T, C), F32)
        for h in range(DSA_IDX_HEADS):
            acc = acc + jnp.maximum(d[h * T:(h + 1) * T], 0.0) * iw[:, h:h + 1]
        bits = lax.bitcast_convert_type(acc, jnp.int32)
        key = jnp.where(bits < 0, bits ^ jnp.int32(0x7FFFFFFF), bits)
        key = jnp.where(acc == 0.0, 0, key)
        if causal:
            spos = s0 + lax.broadcasted_iota(jnp.int32, (T, C), 1)
            key = jnp.where(spos <= tpos, key, INT32_MIN)
        key_scr[c] = key

    def idx_body(c, carry):
        index_keys(c, False)
        return carry

    lax.fori_loop(0, n_chunks - 1, idx_body, 0)
    index_keys(n_chunks - 1, True)

    def count_ge(cand):
        def body(c, acc):
            kk = key_scr[c]
            for u in range(C // 128):
                acc = acc + jnp.where(kk[:, u * 128:(u + 1) * 128] >= cand, 1.0, 0.0)
            return acc
        acc = lax.fori_loop(0, n_chunks, body, jnp.zeros((T, 128), F32))
        return jnp.sum(acc, axis=1, keepdims=True)

    ge0 = count_ge(jnp.zeros((T, 128), jnp.int32))
    gt0 = count_ge(jnp.ones((T, 128), jnp.int32))
    base0 = jnp.where(ge0 >= n_top, 0, jnp.full((T, 128), INT32_MIN, jnp.int32))
    done0 = ((gt0 < n_top) & (ge0 >= n_top)) | (ge0 == n_top) | (tpos + 1 < n_top)
    done0 = jnp.where(jnp.broadcast_to(done0, (T, 128)), 1.0, 0.0)

    def try_bit(bit, base, done):
        cand = base + (jnp.int32(1) << bit)
        cnt = count_ge(cand)
        return jnp.where((cnt >= n_top) & (done < 0.5), cand, base), jnp.where(cnt == n_top, 1.0, done)

    def high_body(step, state):
        return try_bit(jnp.int32(30) - step, *state)

    base, done = lax.fori_loop(0, 15, high_body, (base0, done0))

    def low_cond(state):
        b, _, _, n_open = state
        return (b >= 0) & (n_open > 0.0)

    def low_body(state):
        b, base, done, _ = state
        for step in range(DSA_BITS_PER_CHECK):
            base, done = try_bit(b - step, base, done)
        return b - DSA_BITS_PER_CHECK, base, done, jnp.sum(1.0 - done)

    _, thr, _, _ = lax.while_loop(low_cond, low_body, (jnp.int32(15), base, done, jnp.sum(1.0 - done)))
    thr = jnp.maximum(thr, INT32_MIN + 1)
    need = jnp.broadcast_to(n_top - count_ge(thr + 1), (T, 128))
    tri = tri_ref[...]

    _flash_init(m_scr, acc_scr)
    qs = _stack_heads(q_ref, H)

    def att_body(c, eq_seen):
        kk = key_scr[c]
        for half in range(C // CH):
            s0 = pl.multiple_of(c * C + half * CH, CH)
            eqs, gts = [], []
            for u in range(CH // 128):
                ku = kk[:, half * CH + u * 128:half * CH + (u + 1) * 128]
                eqs.append(ku == thr)
                gts.append(ku > thr)
            eq = jnp.concatenate([jnp.where(e, 1.0, 0.0) for e in eqs], axis=1).astype(BF16)
            pref = _dot(eq, tri)
            bias = [jnp.where(gts[u] | (eqs[u] & (eq_seen + pref[:, u * 128:(u + 1) * 128] <= need)),
                              0.0, MASKED) for u in range(CH // 128)]
            eq_seen = eq_seen + pref[:, CH:]
            _flash_step(qs, k_ref[pl.ds(s0, CH), :], jnp.concatenate(bias, axis=1), v_ref[pl.ds(s0, CH), :],
                        m_scr, acc_scr, H)
        return eq_seen

    lax.fori_loop(0, n_chunks, att_body, jnp.zeros((T, 128), F32))
    out = _flash_result(acc_scr)
    for r in range(H):
        o_ref[:, r * HEAD_DIM:(r + 1) * HEAD_DIM] = out[r * T:(r + 1) * T].astype(o_ref.dtype)


def dsa_attention(q, k, v, iq, ik, iw):
    S = q.shape[0]
    T, C, CH = DSA_T, DSA_C, DSA_CH
    n_top = min(DSA_TOPK, S // 4)
    tri = jnp.concatenate([jnp.triu(jnp.ones((CH, CH), F32)), jnp.ones((CH, 128), F32)], axis=1).astype(BF16)
    return pl.pallas_call(
        functools.partial(_dsa_kernel, n_top),
        grid=(S // T,),
        in_specs=[pl.BlockSpec((T, DSA_Q_W), lambda i: (i, 0)),
                  pl.BlockSpec((T, DSA_IQ_W), lambda i: (i, 0)),
                  pl.BlockSpec((T, DSA_IDX_HEADS), lambda i: (i, 0)),
                  _resident((S, HEAD_DIM)), _resident((S, 2 * HEAD_DIM)), _resident((S, DSA_IDX_DIM)),
                  _resident((CH, CH + 128))],
        out_specs=pl.BlockSpec((T, DSA_Q_W), lambda i: (i, 0)),
        out_shape=jax.ShapeDtypeStruct((S, DSA_Q_W), BF16),
        scratch_shapes=[pltpu.VMEM((S // C, T, C), jnp.int32),
                        pltpu.VMEM((DSA_HEADS * T, HEAD_DIM), F32),
                        pltpu.VMEM((DSA_HEADS * T, 2 * HEAD_DIM), F32)],
        compiler_params=_params(1),
        name="dsa_attention",
    )(q, iq, iw, k, v, ik, tri)


NSA_T = 128
NSA_C = 512
NSA_CH = 256
NSA_GROUP = 128


def _nsa_kernel(q_ref, g_ref, kc_ref, vc_ref, ov_ref, ks_ref, vs_ref, kw_ref, vw_ref, e_ref, o_ref,
                sel_scr, m_scr, acc_scr):
    i = pl.program_id(0)
    T, C, CH = NSA_T, NSA_C, NSA_CH
    H = NSA_HEADS
    q0 = pl.multiple_of(i * T, T)
    ncp = kc_ref.shape[0]
    ns = ov_ref.shape[1]
    tpos = q0 + lax.broadcasted_iota(jnp.int32, (T, 1), 0)
    tpos4 = _tile_rows(tpos, H)
    qs = _stack_heads(q_ref, H)

    s_c = _dot_nt(qs, kc_ref[...])
    cmp_end = lax.broadcasted_iota(jnp.int32, (H * T, ncp), 1) * NSA_CMP_STRIDE + (NSA_CMP_LEN - 1)
    p_c = _masked_softmax(s_c, cmp_end <= tpos4)
    o_c = _dot(p_c.astype(BF16), vc_ref[...])

    psum = p_c[0:T]
    for r in range(1, H):
        psum = psum + p_c[r * T:(r + 1) * T]
    ov = ov_ref[...]
    hi = psum.astype(BF16)
    r1 = psum - hi.astype(F32)
    mid = r1.astype(BF16)
    lo = (r1 - mid.astype(F32)).astype(BF16)
    imp = _dot(hi, ov) + _dot(mid, ov) + _dot(lo, ov)

    blk = lax.broadcasted_iota(jnp.int32, (T, ns), 1)
    cur = tpos >> (NSA_SLC_LEN.bit_length() - 1)
    forced = (blk == 0) | (blk == cur) | (blk == cur - 1)
    imp = jnp.where(forced, NSA_FORCED_SCORE, imp)
    imp = jnp.where(blk * NSA_SLC_LEN <= tpos, imp, MASKED)
    blk_f = blk.astype(F32)
    sel = jnp.zeros((T, ns), F32)
    for _ in range(min(NSA_N_SEL, ns)):
        mx = jnp.max(imp, axis=1, keepdims=True)
        first = jnp.min(jnp.where(imp == mx, blk_f, float(ns)), axis=1, keepdims=True)
        hit = blk_f == first
        sel = jnp.where(hit, 1.0, sel)
        imp = jnp.where(hit, EXTRACTED, imp)
    for g in range(ns // NSA_GROUP):
        sel_scr[g] = sel[:, g * NSA_GROUP:(g + 1) * NSA_GROUP].astype(BF16)

    _flash_init(m_scr, acc_scr)
    chunks_per_group = NSA_GROUP * NSA_SLC_LEN // C

    def sel_body(c, carry):
        s0 = pl.multiple_of(c * C, C)
        member = _dot(sel_scr[c // chunks_per_group], e_ref[c % chunks_per_group])
        spos = s0 + lax.broadcasted_iota(jnp.int32, (T, C), 1)
        bias = jnp.where((member > 0.5) & (spos <= tpos), 0.0, MASKED)
        for half in range(C // CH):
            h0 = pl.multiple_of(s0 + half * CH, CH)
            _flash_step(qs, ks_ref[pl.ds(h0, CH), :], bias[:, half * CH:(half + 1) * CH], vs_ref[pl.ds(h0, CH), :],
                        m_scr, acc_scr, H)
        return carry

    lax.fori_loop(0, (q0 + T + C - 1) // C, sel_body, 0)
    o_s = _flash_result(acc_scr)

    nw = NSA_WINDOW + T
    s_w = _dot_nt(qs, kw_ref[pl.ds(q0, nw), :])
    wpos = q0 - NSA_WINDOW + lax.broadcasted_iota(jnp.int32, (H * T, nw), 1)
    dist = tpos4 - wpos
    p_w = _masked_softmax(s_w, (dist >= 0) & (dist < NSA_WINDOW) & (wpos >= 0))
    o_w = _dot(p_w.astype(BF16), vw_ref[pl.ds(q0, nw), :])

    gate = jax.nn.sigmoid(g_ref[...])
    for r in range(H):
        rows = slice(r * T, (r + 1) * T)
        o = (gate[:, 3 * r:3 * r + 1] * o_c[rows] + gate[:, 3 * r + 1:3 * r + 2] * o_s[rows]
             + gate[:, 3 * r + 2:3 * r + 3] * o_w[rows])
        o_ref[:, r * HEAD_DIM:(r + 1) * HEAD_DIM] = o.astype(o_ref.dtype)


def nsa_attention(q, g, k_cmp, v_cmp, ks, vs, kw, vw):
    S = q.shape[0]
    T, C = NSA_T, NSA_C
    n_cmp = k_cmp.shape[0]
    ncp = -(-n_cmp // 128) * 128
    ns = S // NSA_SLC_LEN
    k_cmp = jnp.pad(k_cmp, ((0, ncp - n_cmp), (0, 0)))
    v_cmp = jnp.pad(v_cmp, ((0, ncp - n_cmp), (0, 0)))
    cmp_start = np.arange(ncp) * NSA_CMP_STRIDE
    slc_start = np.arange(ns) * NSA_SLC_LEN
    overlap = ((cmp_start[:, None] < slc_start[None, :] + NSA_SLC_LEN)
               & (cmp_start[:, None] + NSA_CMP_LEN - 1 >= slc_start[None, :]))
    overlap = jnp.asarray(overlap, BF16)
    cpg = NSA_GROUP * NSA_SLC_LEN // C
    tok_blk = (np.arange(cpg)[:, None] * C + np.arange(C)[None, :]) // NSA_SLC_LEN
    expand = jnp.asarray(np.arange(NSA_GROUP)[None, :, None] == tok_blk[:, None, :], BF16)
    kw = jnp.pad(kw, ((NSA_WINDOW, 0), (0, 0)))
    vw = jnp.pad(vw, ((NSA_WINDOW, 0), (0, 0)))
    return pl.pallas_call(
        _nsa_kernel,
        grid=(S // T,),
        in_specs=[pl.BlockSpec((T, NSA_Q_W), lambda i: (i, 0)),
                  pl.BlockSpec((T, NSA_HEADS * 3), lambda i: (i, 0)),
                  _resident((ncp, HEAD_DIM)), _resident((ncp, HEAD_DIM)), _resident((ncp, ns)),
                  _resident((S, HEAD_DIM)), _resident((S, 2 * HEAD_DIM)),
                  _resident((S + NSA_WINDOW, HEAD_DIM)), _resident((S + NSA_WINDOW, HEAD_DIM)),
                  _resident((cpg, NSA_GROUP, C))],
        out_specs=pl.BlockSpec((T, NSA_Q_W), lambda i: (i, 0)),
        out_shape=jax.ShapeDtypeStruct((S, NSA_Q_W), BF16),
        scratch_shapes=[pltpu.VMEM((ns // NSA_GROUP, T, NSA_GROUP), BF16),
                        pltpu.VMEM((NSA_HEADS * T, HEAD_DIM), F32),
                        pltpu.VMEM((NSA_HEADS * T, 2 * HEAD_DIM), F32)],
        compiler_params=_params(1),
        name="nsa_attention",
    )(q, g, k_cmp, v_cmp, overlap, ks, vs, kw, vw, expand)


def nsa_compress(x, pe, w1, w2):
    S = x.shape[0]
    half = NSA_CMP_STRIDE * HEAD_DIM
    x16 = x.reshape(S // NSA_CMP_STRIDE, half)
    pe_flat = pe.reshape(NSA_CMP_LEN * HEAD_DIM)
    w1_flat = w1.reshape(NSA_CMP_LEN * HEAD_DIM, HEAD_DIM)
    first = _mm(x16 + pe_flat[None, :half], w1_flat[:half])
    second = _mm(x16 + pe_flat[None, half:], w1_flat[half:])
    hid = jax.nn.gelu(first[:-1] + second[1:])
    hid = jnp.pad(hid, ((0, 1), (0, 0)))
    return _mm(hid, w2)[:-1]


DIL_T = 128


def _dil_kernel(q_ref, k_ref, v_ref, o_ref):
    i = pl.program_id(0)
    T = DIL_T
    q0 = pl.multiple_of(i * T, T)
    tl = lax.broadcasted_iota(jnp.int32, (T, 1), 0)
    for s in range(DIL_HEADS_PER_GROUP):
        cols = slice(s * HEAD_DIM, (s + 1) * HEAD_DIM)
        parts = []
        m = jnp.full((T, 1), MASKED, F32)
        for g, (w, r) in enumerate(DIL_PATTERNS):
            n = w + T
            start = pl.multiple_of(q0 + (DIL_MAX_WINDOW - w), T)
            head = g * DIL_HEADS_PER_GROUP + s
            sc = _dot_nt(q_ref[:, head * HEAD_DIM:(head + 1) * HEAD_DIM], k_ref[pl.ds(start, n), cols]) * ATT_SCALE
            b = lax.broadcasted_iota(jnp.int32, (T, n), 1)
            d = tl + w - b
            mask = (d >= 0) & (d <= w) & ((d & (r - 1)) == 0) & (q0 - w + b >= 0)
            sm = jnp.where(mask, sc, MASKED)
            m = jnp.maximum(m, jnp.max(sm, axis=1, keepdims=True))
            parts.append((sm, mask, start, n))
        num = jnp.zeros((T, HEAD_DIM), F32)
        den = jnp.zeros((T, 1), F32)
        for sm, mask, start, n in parts:
            e = jnp.where(mask, jnp.exp(sm - m), 0.0)
            den = den + jnp.sum(e, axis=1, keepdims=True)
            num = num + _dot(e.astype(BF16), v_ref[pl.ds(start, n), cols])
        o_ref[:, cols] = (num / den).astype(o_ref.dtype)


def dilated_attention(q, k, v):
    S = q.shape[0]
    T = DIL_T
    k = jnp.pad(k, ((DIL_MAX_WINDOW, 0), (0, 0)))
    v = jnp.pad(v, ((DIL_MAX_WINDOW, 0), (0, 0)))
    return pl.pallas_call(
        _dil_kernel,
        grid=(S // T,),
        in_specs=[pl.BlockSpec((T, DIL_Q_W), lambda i: (i, 0)),
                  _resident((S + DIL_MAX_WINDOW, DIL_KV_W)), _resident((S + DIL_MAX_WINDOW, DIL_KV_W))],
        out_specs=pl.BlockSpec((T, DIL_KV_W), lambda i: (i, 0)),
        out_shape=jax.ShapeDtypeStruct((S, DIL_KV_W), BF16),
        compiler_params=_params(1),
        name="dilated_attention",
    )(q, k, v)


NORM_TM = 512
NORM_TN = 512


def _norm_mm_kernel(x_ref, g_ref, b_ref, o_ref, xn_ref):
    @pl.when(pl.program_id(1) == 0)
    def _():
        x = x_ref[...]
        y = x * lax.rsqrt(jnp.mean(x * x, axis=-1, keepdims=True) + NORM_EPS)
        xn_ref[...] = (y * g_ref[...]).astype(xn_ref.dtype)
    o_ref[...] = _dot(xn_ref[...], b_ref[...]).astype(o_ref.dtype)


def norm_mm(x, gain, w, out_dtype, return_xn=False):
    M, K = x.shape
    N = w.shape[1]
    tm, tn = min(NORM_TM, M), min(NORM_TN, N)
    main_spec = pl.BlockSpec((tm, tn), lambda i, j: (i, j))
    xn_spec = pl.BlockSpec((tm, K), lambda i, j: (i, 0))
    main_shape = jax.ShapeDtypeStruct((M, N), out_dtype)
    return pl.pallas_call(
        _norm_mm_kernel,
        grid=(M // tm, N // tn),
        in_specs=[xn_spec, pl.BlockSpec((1, K), lambda i, j: (0, 0)), pl.BlockSpec((K, tn), lambda i, j: (0, j))],
        out_specs=(main_spec, xn_spec) if return_xn else main_spec,
        out_shape=(main_shape, jax.ShapeDtypeStruct((M, K), BF16)) if return_xn else main_shape,
        scratch_shapes=[] if return_xn else [pltpu.VMEM((tm, K), BF16)],
        compiler_params=_params(2),
        name="norm_mm",
    )(x, gain.reshape(1, K).astype(F32), w)


PREP_T = 256
PREP_SEGMENTS = (
    ("z", GATE_RANK, None, False, 1.0, "bf16"),
    ("a_q", SB_W, None, False, LOG2_SCALE, "bf16"),
    ("a_k", SB_W, None, False, 1.0, "bf16"),
    ("a_v", SB_W, None, False, 1.0, "bf16"),
    ("b_q", DSA_Q_W, 0, True, LOG2_SCALE, "bf16"),
    ("b_k", DSA_KV_W, 1, True, 1.0, "bf16"),
    ("b_v", DSA_KV_W, None, False, 1.0, "ones"),
    ("b_iq", DSA_IQ_W, None, False, 1.0, "bf16"),
    ("b_ik", HEAD_DIM, None, False, 1.0, "ik"),
    ("c_q", NSA_Q_W, 2, True, LOG2_SCALE, "bf16"),
    ("c_kc", NSA_KV_W, None, True, 1.0, "f32"),
    ("c_vc", NSA_KV_W, None, False, 1.0, "f32"),
    ("c_ks", NSA_KV_W, 4, True, 1.0, "bf16"),
    ("c_vs", NSA_KV_W, None, False, 1.0, "ones"),
    ("c_kw", NSA_KV_W, 5, True, 1.0, "bf16"),
    ("c_vw", NSA_KV_W, None, False, 1.0, "bf16"),
    ("d_q", DIL_Q_W, 6, True, 1.0, "bf16"),
    ("d_k", DIL_KV_W, 7, True, 1.0, "bf16"),
    ("d_v", DIL_KV_W, None, False, 1.0, "bf16"),
    ("misc", HEAD_DIM, None, False, 1.0, "misc"),
)
PREP_WIDTH = sum(seg[1] for seg in PREP_SEGMENTS)
N_MISC = DSA_IDX_HEADS + NSA_HEADS * 3


def _prep_outputs(S):
    shapes = []
    for name, width, _, _, _, kind in PREP_SEGMENTS:
        if kind == "bf16":
            shapes.append((width, BF16))
        elif kind == "ones":
            shapes.append((width + HEAD_DIM, BF16))
        elif kind == "f32":
            shapes.append((width, F32))
        elif kind == "ik":
            shapes.append((DSA_IDX_DIM, BF16))
        else:
            shapes.append((N_MISC, F32))
    return shapes


def _prep_kernel(p_ref, g_ref, cs_ref, sn_ref, *o_refs):
    cs = cs_ref[...]
    sn = sn_ref[...]
    off = 0
    for (name, width, gi, rope, scale, kind), o_ref in zip(PREP_SEGMENTS, o_refs):
        if kind == "ik":
            o_ref[...] = p_ref[:, off:off + DSA_IDX_DIM].astype(o_ref.dtype)
        elif kind == "misc":
            o_ref[...] = p_ref[:, off:off + N_MISC]
        else:
            for hd in range(width // HEAD_DIM):
                lanes = slice(hd * HEAD_DIM, (hd + 1) * HEAD_DIM)
                x = p_ref[:, off + hd * HEAD_DIM:off + (hd + 1) * HEAD_DIM]
                if gi is not None:
                    x = x * lax.rsqrt(jnp.mean(x * x, axis=-1, keepdims=True) + NORM_EPS) * g_ref[gi:gi + 1, :]
                if rope:
                    x = x * cs + pltpu.roll(x, HEAD_DIM // 2, 1) * sn
                if scale != 1.0:
                    x = x * scale
                o_ref[:, lanes] = x.astype(o_ref.dtype)
            if kind == "ones":
                o_ref[:, width:] = jnp.ones((o_ref.shape[0], HEAD_DIM), o_ref.dtype)
        off += width


def prepare_operands(proj, qk_gain, cs, sn):
    S = proj.shape[0]
    T = min(PREP_T, S)
    row = lambda i: (i, 0)
    shapes = _prep_outputs(S)
    outs = pl.pallas_call(
        _prep_kernel,
        grid=(S // T,),
        in_specs=[pl.BlockSpec((T, PREP_WIDTH), row),
                  pl.BlockSpec((N_QK_GAINS, HEAD_DIM), lambda i: (0, 0)),
                  pl.BlockSpec((T, HEAD_DIM), row), pl.BlockSpec((T, HEAD_DIM), row)],
        out_specs=tuple(pl.BlockSpec((T, w), row) for w, _ in shapes),
        out_shape=tuple(jax.ShapeDtypeStruct((S, w), dt) for w, dt in shapes),
        compiler_params=_params(1),
        name="prepare_operands",
    )(proj, qk_gain, cs, sn)
    return {seg[0]: o for seg, o in zip(PREP_SEGMENTS, outs)}


def fused_in_weights(w_in, gate_down):
    names = ("a_q", "a_k", "a_v", "b_q", "b_k", "b_v", "b_iq", "b_ik", "b_iw",
             "c_q", "c_kc", "c_vc", "c_ks", "c_vs", "c_kw", "c_vw", "c_g", "d_q", "d_k", "d_v")
    parts = dict(zip(names, jnp.split(w_in, SPLIT_OFFSETS, axis=1)))
    parts["z"] = gate_down
    parts["misc"] = jnp.concatenate([parts["b_iw"], parts["c_g"]], axis=1)
    cols = []
    for name, width, *_ in PREP_SEGMENTS:
        p = parts[name]
        cols.append(jnp.pad(p, ((0, 0), (0, width - p.shape[1]))))
    return jnp.concatenate(cols, axis=1).astype(BF16)


def token_mixing(h, cs, sn, norm_gain, w_in, qk_gain, nsa_pe, nsa_w1, nsa_w2, gate_down, gate_up, w_branch, w_out):
    proj = norm_mm(h, norm_gain, fused_in_weights(w_in, gate_down), F32)
    t = prepare_operands(proj, qk_gain, cs, sn)
    o_a = sb_attention(t["a_q"], t["a_k"], t["a_v"])
    o_b = dsa_attention(t["b_q"], t["b_k"], t["b_v"], t["b_iq"], t["b_ik"], t["misc"][:, :DSA_IDX_HEADS])
    k_cmp = rmsnorm(nsa_compress(t["c_kc"], nsa_pe[0], nsa_w1[0], nsa_w2[0]), qk_gain[3])
    v_cmp = nsa_compress(t["c_vc"], nsa_pe[1], nsa_w1[1], nsa_w2[1])
    o_c = nsa_attention(t["c_q"], t["misc"][:, DSA_IDX_HEADS:], k_cmp.astype(BF16), v_cmp.astype(BF16),
                        t["c_ks"], t["c_vs"], t["c_kw"], t["c_vw"])
    o_d = dilated_attention(t["d_q"], t["d_k"], t["d_v"])
    merged = gated_merge(t["z"], (o_a, o_b, o_c, o_d), gate_up, w_branch)
    return _mm(merged, w_out, residual=h)


MERGE_TM = 512
MERGE_TN = 512


def _merge_kernel(z_ref, oa_ref, ob_ref, oc_ref, od_ref, gu_ref, wb_ref, o_ref):
    z = z_ref[...]
    acc = None
    off = 0
    for i, br_ref in enumerate((oa_ref, ob_ref, oc_ref, od_ref)):
        width = BRANCH_WIDTHS[i]
        gate = jax.nn.sigmoid(_dot(z, gu_ref[i]))
        term = gate * _dot(br_ref[...], wb_ref[off:off + width, :])
        acc = term if acc is None else acc + term
        off += width
    o_ref[...] = acc.astype(o_ref.dtype)


def gated_merge(z, branches, gate_up, w_branch):
    S = z.shape[0]
    D = w_branch.shape[1]
    tm, tn = min(MERGE_TM, S), min(MERGE_TN, D)
    row = lambda i, j: (i, 0)
    return pl.pallas_call(
        _merge_kernel,
        grid=(S // tm, D // tn),
        in_specs=[pl.BlockSpec((tm, GATE_RANK), row)]
                 + [pl.BlockSpec((tm, w), row) for w in BRANCH_WIDTHS]
                 + [pl.BlockSpec((N_BRANCHES, GATE_RANK, tn), lambda i, j: (0, 0, j)),
                    pl.BlockSpec((BRANCH_WIDTH, tn), lambda i, j: (0, j))],
        out_specs=pl.BlockSpec((tm, tn), lambda i, j: (i, j)),
        out_shape=jax.ShapeDtypeStruct((S, D), BF16),
        compiler_params=_params(2),
        name="gated_merge",
    )(z, *branches, gate_up.astype(BF16), w_branch.astype(BF16))


PEER_ST = 256
PEER_TM = 512
PEER_EC = 512
PEER_HALF = PEER_KEY_DIM // 2
PEER_STAT_ROWS = 32
assert (PEER_TOPK + 1) // 9 == 1 and PEER_TOPK + 1 <= 24


def _top_rows(scores, n, n_rows):
    st = scores.shape[1]
    rid = lax.broadcasted_iota(jnp.int32, (n_rows, st), 0)
    out = jnp.full((n_rows, st), EXTRACTED, F32)
    for r in range(n):
        m = jnp.max(scores, axis=0, keepdims=True)
        out = jnp.where(rid == r, m, out)
        scores = jnp.where(scores == m, EXTRACTED, scores)
    return out


def _peer_stats_kernel(q_ref, w1_ref, w2_ref, o_ref):
    st = q_ref.shape[0]
    k = PEER_TOPK
    rid = lax.broadcasted_iota(jnp.int32, (PEER_STAT_ROWS, st), 0)
    rid8 = lax.broadcasted_iota(jnp.int32, (8, st), 0)
    stats = jnp.zeros((PEER_STAT_ROWS, st), F32)
    for h in range(PEER_HEADS):
        qh = q_ref[:, h * PEER_KEY_DIM:(h + 1) * PEER_KEY_DIM]
        v1 = _top_rows(_dot_nt(w1_ref[h], qh), k + 1, 24)
        v2 = _top_rows(_dot_nt(w2_ref[h], qh), k + 1, 24)
        pieces = [v1[0:1] + v2]
        for a in range(1, 8):
            pieces.append(jnp.where(rid8 < (k + 1) // (a + 1), v1[a:a + 1] + v2[0:8], EXTRACTED))
        pieces.append(v1[8:24] + v2[0:1])
        tops = _top_rows(jnp.concatenate(pieces, axis=0), k + 1, 24)
        c1 = tops[0:1]
        den = jnp.sum(jnp.exp(tops[0:k] - c1), axis=0, keepdims=True)
        thr = 0.5 * (tops[k - 1:k] + tops[k:k + 1])
        stats = jnp.where(rid == h, thr, stats)
        stats = jnp.where(rid == PEER_HEADS + h, c1, stats)
        stats = jnp.where(rid == 2 * PEER_HEADS + h, 1.0 / den, stats)
    o_ref[...] = stats


def _peer_weight_kernel(xn_ref, ut_ref, q_ref, kf_ref, st_ref, o_ref):
    act = jax.nn.gelu(_dot(xn_ref[...], ut_ref[...]))
    st = st_ref[...]
    w = jnp.zeros(act.shape, F32)
    for h in range(PEER_HEADS):
        sf = _dot(q_ref[:, h * PEER_KEY_DIM:(h + 1) * PEER_KEY_DIM], kf_ref[h])
        gate = jnp.exp(sf - st[:, PEER_HEADS + h:PEER_HEADS + h + 1]) * st[:, 2 * PEER_HEADS + h:2 * PEER_HEADS + h + 1]
        w = w + jnp.where(sf >= st[:, h:h + 1], gate, 0.0)
    o_ref[...] = (w * act).astype(o_ref.dtype)


def peer_ffn(h, norm_gain, wq, subkeys, u, v):
    S, D = h.shape
    nk, half = PEER_N_KEYS, PEER_HALF
    q, xn = norm_mm(h, norm_gain, wq.astype(BF16), BF16, return_xn=True)
    zeros = jnp.zeros((PEER_HEADS, nk, half), F32)
    w1 = jnp.concatenate([subkeys[:, 0], zeros], axis=-1).astype(BF16)
    w2 = jnp.concatenate([zeros, subkeys[:, 1]], axis=-1).astype(BF16)
    st_t = pl.pallas_call(
        _peer_stats_kernel,
        grid=(S // PEER_ST,),
        in_specs=[pl.BlockSpec((PEER_ST, PEER_HEADS * PEER_KEY_DIM), lambda i: (i, 0)),
                  pl.BlockSpec((PEER_HEADS, nk, PEER_KEY_DIM), lambda i: (0, 0, 0)),
                  pl.BlockSpec((PEER_HEADS, nk, PEER_KEY_DIM), lambda i: (0, 0, 0))],
        out_specs=pl.BlockSpec((PEER_STAT_ROWS, PEER_ST), lambda i: (0, i)),
        out_shape=jax.ShapeDtypeStruct((PEER_STAT_ROWS, S), F32),
        compiler_params=_params(1),
        name="peer_stats",
    )(q, w1, w2)
    stats = st_t.T
    top = jnp.repeat(jnp.swapaxes(subkeys[:, 0], 1, 2), nk, axis=2)
    bot = jnp.tile(jnp.swapaxes(subkeys[:, 1], 1, 2), (1, 1, nk))
    kfull = jnp.concatenate([top, bot], axis=1).astype(BF16)
    tm, ec = min(PEER_TM, S), PEER_EC
    wmat = pl.pallas_call(
        _peer_weight_kernel,
        grid=(S // tm, PEER_EXPERTS // ec),
        in_specs=[pl.BlockSpec((tm, D), lambda i, j: (i, 0)),
                  pl.BlockSpec((D, ec), lambda i, j: (0, j)),
                  pl.BlockSpec((tm, PEER_HEADS * PEER_KEY_DIM), lambda i, j: (i, 0)),
                  pl.BlockSpec((PEER_HEADS, PEER_KEY_DIM, ec), lambda i, j: (0, 0, j)),
                  pl.BlockSpec((tm, PEER_STAT_ROWS), lambda i, j: (i, 0))],
        out_specs=pl.BlockSpec((tm, ec), lambda i, j: (i, j)),
        out_shape=jax.ShapeDtypeStruct((S, PEER_EXPERTS), BF16),
        compiler_params=_params(2),
        name="peer_weights",
    )(xn, u.T.astype(BF16), q, kfull, stats)
    return _mm(wmat, v, residual=h)


def kernel(x, norm_mix, w_in, qk_gain, nsa_pe, nsa_w1, nsa_w2, gate_down, gate_up, w_branch, w_out,
           norm_ffn, peer_wq, peer_subkeys, peer_u, peer_v):
    S = x.shape[1]
    cos, sin = rope_tables(S, x.dtype)
    cos, sin = cos.reshape(S, HEAD_DIM // 2), sin.reshape(S, HEAD_DIM // 2)
    cs = jnp.concatenate([cos, cos], axis=1)
    sn = jnp.concatenate([-sin, sin], axis=1)
    h = x[0]
    for l in range(DEPTH):
        h = token_mixing(h, cs, sn, norm_mix[l], w_in[l], qk_gain[l], nsa_pe[l], nsa_w1[l], nsa_w2[l],
                         gate_down[l], gate_up[l], w_branch[l], w_out[l])
        h = peer_ffn(h, norm_ffn[l], peer_wq[l], peer_subkeys[l], peer_u[l], peer_v[l])
    return h[None]
```

```python
import math, functools
import jax, jax.numpy as jnp
from jax import lax
import numpy as np
from jax.experimental import pallas as pl
from jax.experimental.pallas import tpu as pltpu

D_MODEL = 4096
BATCH = 1
SEQ = 16384
DEPTH = 4

HEAD_DIM = 128
ROPE_THETA = 10000.0
NORM_EPS = 1e-6
SB_HEADS = 4
DSA_HEADS = 4
DSA_KV_HEADS = 1
DSA_IDX_HEADS = 4
DSA_IDX_DIM = 64
DSA_TOPK = 256
NSA_HEADS = 4
NSA_KV_HEADS = 1
NSA_CMP_LEN = 32
NSA_CMP_STRIDE = 16
NSA_SLC_LEN = 32
NSA_N_SEL = 8
NSA_WINDOW = 512
NSA_FORCED_SCORE = 1e4
DIL_PATTERNS = ((128, 1), (512, 4), (2048, 16))
DIL_GROUPS = len(DIL_PATTERNS)
DIL_HEADS_PER_GROUP = 2
DIL_MAX_WINDOW = max(w for w, _ in DIL_PATTERNS)
N_BRANCHES = 4
GATE_RANK = 256
PEER_HEADS = 8
PEER_N_KEYS = 64
PEER_EXPERTS = PEER_N_KEYS ** 2
PEER_KEY_DIM = 128
PEER_TOPK = 16

SB_W = SB_HEADS * HEAD_DIM
DSA_Q_W = DSA_HEADS * HEAD_DIM
DSA_KV_W = DSA_KV_HEADS * HEAD_DIM
DSA_IQ_W = DSA_IDX_HEADS * DSA_IDX_DIM
NSA_Q_W = NSA_HEADS * HEAD_DIM
NSA_KV_W = NSA_KV_HEADS * HEAD_DIM
DIL_Q_W = DIL_GROUPS * DIL_HEADS_PER_GROUP * HEAD_DIM
DIL_KV_W = DIL_HEADS_PER_GROUP * HEAD_DIM
IN_SPLITS = (SB_W, SB_W, SB_W,
             DSA_Q_W, DSA_KV_W, DSA_KV_W, DSA_IQ_W, DSA_IDX_DIM, DSA_IDX_HEADS,
             NSA_Q_W, NSA_KV_W, NSA_KV_W, NSA_KV_W, NSA_KV_W, NSA_KV_W, NSA_KV_W, NSA_HEADS * 3,
             DIL_Q_W, DIL_KV_W, DIL_KV_W)
IN_WIDTH = sum(IN_SPLITS)
SPLIT_OFFSETS = tuple(int(o) for o in np.cumsum(IN_SPLITS)[:-1])
BRANCH_WIDTHS = (SB_W, DSA_Q_W, NSA_Q_W, DIL_KV_W)
BRANCH_WIDTH = sum(BRANCH_WIDTHS)
BRANCH_OFFSETS = tuple(int(o) for o in np.cumsum(BRANCH_WIDTHS)[:-1])
N_QK_GAINS = 8

ATT_SCALE = HEAD_DIM ** -0.5
LOG2_SCALE = ATT_SCALE * math.log2(math.e)
MASKED = -1e30
EXTRACTED = -3e38
INT32_MIN = -2 ** 31
F32_TINY = float(np.finfo(np.float32).tiny)
VMEM_LIMIT = 56 * 1024 * 1024

BF16 = jnp.bfloat16
F32 = jnp.float32


def _params(n_grid):
    return pltpu.CompilerParams(dimension_semantics=("arbitrary",) * n_grid, vmem_limit_bytes=VMEM_LIMIT)


def _resident(shape):
    nd = len(shape)
    return pl.BlockSpec(shape, lambda *_: (0,) * nd, pipeline_mode=pl.Buffered(1))


def _dot_nt(a, b):
    return lax.dot_general(a, b, (((1,), (1,)), ((), ())), preferred_element_type=F32)


def _dot(a, b):
    return jnp.dot(a, b, preferred_element_type=F32)


def _mm_kernel(a_ref, b_ref, o_ref):
    o_ref[...] = _dot(a_ref[...], b_ref[...]).astype(o_ref.dtype)


def _mm_res_kernel(a_ref, b_ref, r_ref, o_ref):
    o_ref[...] = (r_ref[...] + _dot(a_ref[...], b_ref[...])).astype(o_ref.dtype)


def _pick(n, prefs):
    for p in prefs:
        if n % p == 0:
            return p
    return n


def _mm(a, b, out_dtype=F32, residual=None):
    lead = a.shape[:-1]
    K = a.shape[-1]
    N = b.shape[-1]
    a2 = a.reshape(-1, K).astype(jnp.bfloat16)
    M = a2.shape[0]
    n_pad = (-N) % 128
    b2 = b.astype(jnp.bfloat16)
    if n_pad:
        b2 = jnp.pad(b2, ((0, 0), (0, n_pad)))
    Np = N + n_pad
    tm = _pick(M, (1024, 512, 256, 128))
    tn = _pick(Np, (512, 256, 128))
    in_specs = [pl.BlockSpec((tm, K), lambda i, j: (i, 0)),
                pl.BlockSpec((K, tn), lambda i, j: (0, j))]
    args = [a2, b2]
    body = _mm_kernel
    if residual is not None:
        assert n_pad == 0
        in_specs.append(pl.BlockSpec((tm, tn), lambda i, j: (i, j)))
        args.append(residual.reshape(M, N))
        body = _mm_res_kernel
    out = pl.pallas_call(
        body,
        grid=(M // tm, Np // tn),
        in_specs=in_specs,
        out_specs=pl.BlockSpec((tm, tn), lambda i, j: (i, j)),
        out_shape=jax.ShapeDtypeStruct((M, Np), out_dtype),
        compiler_params=pltpu.CompilerParams(
            dimension_semantics=("arbitrary", "arbitrary"),
            vmem_limit_bytes=VMEM_LIMIT),
        name="dense_mm",
    )(*args)
    if n_pad:
        out = out[:, :N]
    return out.reshape(*lead, N)


def rmsnorm(x, g):
    xf = x.astype(jnp.float32)
    y = xf * lax.rsqrt(jnp.mean(xf * xf, axis=-1, keepdims=True) + NORM_EPS)
    return (y * g.astype(jnp.float32)).astype(x.dtype)


def rope_tables(seq, dtype):
    inv = ROPE_THETA ** (-jnp.arange(0, HEAD_DIM, 2, dtype=jnp.float32) / HEAD_DIM)
    ang = jnp.arange(seq, dtype=jnp.float32)[:, None] * inv[None, :]
    return jnp.cos(ang)[None, :, None, :].astype(dtype), jnp.sin(ang)[None, :, None, :].astype(dtype)


def _stack_heads(q_ref, n_heads):
    return jnp.concatenate([q_ref[:, r * HEAD_DIM:(r + 1) * HEAD_DIM] for r in range(n_heads)], axis=0)


def _tile_rows(x, n):
    return jnp.concatenate([x] * n, axis=0)


def _flash_init(m_scr, acc_scr):
    m_scr[...] = jnp.full(m_scr.shape, MASKED, F32)
    acc_scr[...] = jnp.zeros(acc_scr.shape, F32)


def _flash_step(qs, k_chunk, bias, v_aug, m_scr, acc_scr, n_heads):
    T, C = bias.shape
    s = _dot_nt(qs, k_chunk)
    ps, alphas = [], []
    for r in range(n_heads):
        rows = slice(r * T, (r + 1) * T)
        sm = s[rows] + bias
        m_old = m_scr[rows]
        m_new = jnp.maximum(m_old, jnp.max(sm, axis=1, keepdims=True))
        m_scr[rows] = m_new
        p = [jnp.exp2(sm[:, u * 128:(u + 1) * 128] - m_new) for u in range(C // 128)]
        ps.append(jnp.concatenate(p, axis=1).astype(BF16))
        alpha = jnp.exp2(m_old - m_new)
        alphas.append(jnp.concatenate([alpha, alpha], axis=1))
    pv = _dot(jnp.concatenate(ps, axis=0), v_aug)
    acc_scr[...] = jnp.concatenate(alphas, axis=0) * acc_scr[...] + pv


def _flash_result(acc_scr):
    acc = acc_scr[...]
    return acc[:, :HEAD_DIM] / acc[:, HEAD_DIM:]


def _masked_softmax(s, mask):
    sm = jnp.where(mask, s, MASKED)
    m = jnp.max(sm, axis=1, keepdims=True)
    e = jnp.where(mask, jnp.exp2(sm - m), 0.0)
    den = jnp.sum(e, axis=1, keepdims=True)
    return e * (1.0 / jnp.maximum(den, F32_TINY))


SB_T = 256
SB_DEAD_LOG2 = -200.0


def _sb_kernel(q_ref, k_ref, v_ref, tri_ref, o_ref, later_scr, acc_scr):
    i = pl.program_id(0)
    T = SB_T
    q0 = pl.multiple_of(i * T, T)
    tri = tri_ref[...]
    later_scr[...] = jnp.zeros(later_scr.shape, F32)
    acc_scr[...] = jnp.zeros(acc_scr.shape, F32)

    def chunk(s0, causal):
        heads = [slice(h * HEAD_DIM, (h + 1) * HEAD_DIM) for h in range(SB_HEADS)]
        zs = [_dot_nt(q_ref[:, cols], k_ref[pl.ds(s0, T), cols]) for cols in heads]
        lms = []
        for z in zs:
            sp = jnp.log2(1.0 + jnp.exp2(-jnp.abs(z)))
            lm = -(jnp.maximum(z, 0.0) + sp)
            if causal is not None:
                lm = jnp.where(causal, lm, 0.0)
            lms.append(lm)
        withins = [_dot(lm.astype(BF16), tri) for lm in lms]
        ws = []
        for h in range(SB_HEADS):
            e = zs[h] + lms[h] + withins[h]
            later = later_scr[h]
            w = jnp.concatenate([jnp.exp2(e[:, u * 128:(u + 1) * 128] + later) for u in range(T // 128)], axis=1)
            if causal is not None:
                w = jnp.where(causal, w, 0.0)
            ws.append(w.astype(BF16))
            later_scr[h] = later + jnp.sum(lms[h], axis=1, keepdims=True)
        for h, cols in enumerate(heads):
            acc_scr[h] += _dot(ws[h], v_ref[pl.ds(s0, T), cols])

    row = lax.broadcasted_iota(jnp.int32, (T, T), 0)
    col = lax.broadcasted_iota(jnp.int32, (T, T), 1)
    chunk(q0, col < row)

    def cond(state):
        n, worst = state
        return (n <= i) & (worst > SB_DEAD_LOG2)

    def body(state):
        n, _ = state
        chunk(pl.multiple_of((i - n) * T, T), None)
        return n + 1, jnp.max(later_scr[...])

    lax.while_loop(cond, body, (jnp.int32(1), jnp.max(later_scr[...])))
    for h in range(SB_HEADS):
        o_ref[:, h * HEAD_DIM:(h + 1) * HEAD_DIM] = acc_scr[h].astype(o_ref.dtype)


def sb_attention(q, k, v):
    S = q.shape[0]
    T = SB_T
    tri = jnp.tril(jnp.ones((T, T), F32), -1).astype(BF16)
    return pl.pallas_call(
        _sb_kernel,
        grid=(S // T,),
        in_specs=[pl.BlockSpec((T, SB_W), lambda i: (i, 0)),
                  _resident((S, SB_W)), _resident((S, SB_W)), _resident((T, T))],
        out_specs=pl.BlockSpec((T, SB_W), lambda i: (i, 0)),
        out_shape=jax.ShapeDtypeStruct((S, SB_W), BF16),
        scratch_shapes=[pltpu.VMEM((SB_HEADS, T, 128), F32),
                        pltpu.VMEM((SB_HEADS, T, HEAD_DIM), F32)],
        compiler_params=_params(1),
        name="sb_attention",
    )(q, k, v, tri)


DSA_T = 256
DSA_C = 512
DSA_CH = 256
DSA_BITS_PER_CHECK = 4


def _dsa_kernel(n_top, q_ref, iq_ref, iw_ref, k_ref, v_ref, ik_ref, tri_ref, o_ref,
                key_scr, m_scr, acc_scr):
    i = pl.program_id(0)
    T, C, CH = DSA_T, DSA_C, DSA_CH
    H = DSA_HEADS
    q0 = i * T
    n_chunks = (q0 + T + C - 1) // C
    tpos = q0 + lax.broadcasted_iota(jnp.int32, (T, 1), 0)
    iw = iw_ref[...]
    iq4 = jnp.concatenate([iq_ref[:, h * DSA_IDX_DIM:(h + 1) * DSA_IDX_DIM] for h in range(DSA_IDX_HEADS)], axis=0)

    def index_keys(c, causal):
        s0 = pl.multiple_of(c * C, C)
        d = _dot_nt(iq4, ik_ref[pl.ds(s0, C), :])
        acc = jnp.zeros((T, C), F32)
        for h in range(DSA_IDX_HEADS):
            acc = acc + jnp.maximum(d[h * T:(h + 1) * T], 0.0) * iw[:, h:h + 1]
        bits = lax.bitcast_convert_type(acc, jnp.int32)
        key = jnp.where(bits < 0, bits ^ jnp.int32(0x7FFFFFFF), bits)
        key = jnp.where(acc == 0.0, 0, key)
        if causal:
            spos = s0 + lax.broadcasted_iota(jnp.int32, (T, C), 1)
            key = jnp.where(spos <= tpos, key, INT32_MIN)
        key_scr[c] = key

    def idx_body(c, carry):
        index_keys(c, False)
        return carry

    lax.fori_loop(0, n_chunks - 1, idx_body, 0)
    index_keys(n_chunks - 1, True)

    def count_ge(cand):
        counts = []
        for r0 in range(0, T, 128):
            cand_r = cand[r0:r0 + 128]

            def body(c, acc):
                kk = key_scr[c, r0:r0 + 128, :]
                for u in range(C // 128):
                    acc = acc + jnp.where(kk[:, u * 128:(u + 1) * 128] >= cand_r, 1.0, 0.0)
                return acc
            acc = lax.fori_loop(0, n_chunks, body, jnp.zeros((128, 128), F32))
            counts.append(jnp.sum(acc, axis=1, keepdims=True))
        return jnp.concatenate(counts, axis=0)

    ge0 = count_ge(jnp.zeros((T, 128), jnp.int32))
    gt0 = count_ge(jnp.ones((T, 128), jnp.int32))
    base0 = jnp.where(ge0 >= n_top, 0, jnp.full((T, 128), INT32_MIN, jnp.int32))
    done0 = ((gt0 < n_top) & (ge0 >= n_top)) | (ge0 == n_top) | (tpos + 1 < n_top)
    done0 = jnp.where(jnp.broadcast_to(done0, (T, 128)), 1.0, 0.0)

    def try_bit(bit, base, done):
        cand = base + (jnp.int32(1) << bit)
        cnt = count_ge(cand)
        return jnp.where((cnt >= n_top) & (done < 0.5), cand, base), jnp.where(cnt == n_top, 1.0, done)

    def high_body(step, state):
        return try_bit(jnp.int32(30) - step, *state)

    base, done = lax.fori_loop(0, 15, high_body, (base0, done0))

    def low_cond(state):
        b, _, _, n_open = state
        return (b >= 0) & (n_open > 0.0)

    def low_body(state):
        b, base, done, _ = state
        for step in range(DSA_BITS_PER_CHECK):
            base, done = try_bit(b - step, base, done)
        return b - DSA_BITS_PER_CHECK, base, done, jnp.sum(1.0 - done)

    _, thr, _, _ = lax.while_loop(low_cond, low_body, (jnp.int32(15), base, done, jnp.sum(1.0 - done)))
    thr = jnp.maximum(thr, INT32_MIN + 1)
    need = jnp.broadcast_to(n_top - count_ge(thr + 1), (T, 128))
    tri = tri_ref[...]

    _flash_init(m_scr, acc_scr)
    qs = _stack_heads(q_ref, H)

    def att_body(c, eq_seen):
        kk = key_scr[c]
        for half in range(C // CH):
            s0 = pl.multiple_of(c * C + half * CH, CH)
            eqs, gts = [], []
            for u in range(CH // 128):
                ku = kk[:, half * CH + u * 128:half * CH + (u + 1) * 128]
                eqs.append(ku == thr)
                gts.append(ku > thr)
            eq = jnp.concatenate([jnp.where(e, 1.0, 0.0) for e in eqs], axis=1).astype(BF16)
            pref = _dot(eq, tri)
            bias = [jnp.where(gts[u] | (eqs[u] & (eq_seen + pref[:, u * 128:(u + 1) * 128] <= need)),
                              0.0, MASKED) for u in range(CH // 128)]
            eq_seen = eq_seen + pref[:, CH:]
            _flash_step(qs, k_ref[pl.ds(s0, CH), :], jnp.concatenate(bias, axis=1), v_ref[pl.ds(s0, CH), :],
                        m_scr, acc_scr, H)
        return eq_seen

    lax.fori_loop(0, n_chunks, att_body, jnp.zeros((T, 128), F32))
    out = _flash_result(acc_scr)
    for r in range(H):
        o_ref[:, r * HEAD_DIM:(r + 1) * HEAD_DIM] = out[r * T:(r + 1) * T].astype(o_ref.dtype)


def dsa_attention(q, k, v, iq, ik, iw):
    S = q.shape[0]
    T, C, CH = DSA_T, DSA_C, DSA_CH
    n_top = min(DSA_TOPK, S // 4)
    tri = jnp.concatenate([jnp.triu(jnp.ones((CH, CH), F32)), jnp.ones((CH, 128), F32)], axis=1).astype(BF16)
    return pl.pallas_call(
        functools.partial(_dsa_kernel, n_top),
        grid=(S // T,),
        in_specs=[pl.BlockSpec((T, DSA_Q_W), lambda i: (i, 0)),
                  pl.BlockSpec((T, DSA_IQ_W), lambda i: (i, 0)),
                  pl.BlockSpec((T, DSA_IDX_HEADS), lambda i: (i, 0)),
                  _resident((S, HEAD_DIM)), _resident((S, 2 * HEAD_DIM)), _resident((S, DSA_IDX_DIM)),
                  _resident((CH, CH + 128))],
        out_specs=pl.BlockSpec((T, DSA_Q_W), lambda i: (i, 0)),
        out_shape=jax.ShapeDtypeStruct((S, DSA_Q_W), BF16),
        scratch_shapes=[pltpu.VMEM((S // C, T, C), jnp.int32),
                        pltpu.VMEM((DSA_HEADS * T, HEAD_DIM), F32),
                        pltpu.VMEM((DSA_HEADS * T, 2 * HEAD_DIM), F32)],
        compiler_params=_params(1),
        name="dsa_attention",
    )(q, iq, iw, k, v, ik, tri)


NSA_T = 256
NSA_C = 512
NSA_CH = 256
NSA_GROUP = 128


def _nsa_kernel(q_ref, g_ref, kc_ref, vc_ref, ov_ref, ks_ref, vs_ref, kw_ref, vw_ref, e_ref, o_ref,
                sel_scr, m_scr, acc_scr):
    i = pl.program_id(0)
    T, C, CH = NSA_T, NSA_C, NSA_CH
    H = NSA_HEADS
    q0 = pl.multiple_of(i * T, T)
    ncp = kc_ref.shape[0]
    ns = ov_ref.shape[1]
    tpos = q0 + lax.broadcasted_iota(jnp.int32, (T, 1), 0)
    tpos4 = _tile_rows(tpos, H)
    qs = _stack_heads(q_ref, H)

    s_c = _dot_nt(qs, kc_ref[...])
    cmp_end = lax.broadcasted_iota(jnp.int32, (H * T, ncp), 1) * NSA_CMP_STRIDE + (NSA_CMP_LEN - 1)
    p_c = _masked_softmax(s_c, cmp_end <= tpos4)
    o_c = _dot(p_c.astype(BF16), vc_ref[...])

    psum = p_c[0:T]
    for r in range(1, H):
        psum = psum + p_c[r * T:(r + 1) * T]
    ov = ov_ref[...]
    hi = psum.astype(BF16)
    r1 = psum - hi.astype(F32)
    mid = r1.astype(BF16)
    lo = (r1 - mid.astype(F32)).astype(BF16)
    imp = _dot(hi, ov) + _dot(mid, ov) + _dot(lo, ov)

    blk = lax.broadcasted_iota(jnp.int32, (T, ns), 1)
    cur = tpos >> (NSA_SLC_LEN.bit_length() - 1)
    forced = (blk == 0) | (blk == cur) | (blk == cur - 1)
    imp = jnp.where(forced, NSA_FORCED_SCORE, imp)
    imp = jnp.where(blk * NSA_SLC_LEN <= tpos, imp, MASKED)
    blk_f = blk.astype(F32)
    sel = jnp.zeros((T, ns), F32)
    for _ in range(min(NSA_N_SEL, ns)):
        mx = jnp.max(imp, axis=1, keepdims=True)
        first = jnp.min(jnp.where(imp == mx, blk_f, float(ns)), axis=1, keepdims=True)
        hit = blk_f == first
        sel = jnp.where(hit, 1.0, sel)
        imp = jnp.where(hit, EXTRACTED, imp)
    for g in range(ns // NSA_GROUP):
        sel_scr[g] = sel[:, g * NSA_GROUP:(g + 1) * NSA_GROUP].astype(BF16)

    _flash_init(m_scr, acc_scr)
    chunks_per_group = NSA_GROUP * NSA_SLC_LEN // C

    def sel_body(c, carry):
        s0 = pl.multiple_of(c * C, C)
        member = _dot(sel_scr[c // chunks_per_group], e_ref[c % chunks_per_group])
        spos = s0 + lax.broadcasted_iota(jnp.int32, (T, C), 1)
        bias = jnp.where((member > 0.5) & (spos <= tpos), 0.0, MASKED)
        for half in range(C // CH):
            h0 = pl.multiple_of(s0 + half * CH, CH)
            _flash_step(qs, ks_ref[pl.ds(h0, CH), :], bias[:, half * CH:(half + 1) * CH], vs_ref[pl.ds(h0, CH), :],
                        m_scr, acc_scr, H)
        return carry

    lax.fori_loop(0, (q0 + T + C - 1) // C, sel_body, 0)
    o_s = _flash_result(acc_scr)

    nw = NSA_WINDOW + T
    s_w = _dot_nt(qs, kw_ref[pl.ds(q0, nw), :])
    wpos = q0 - NSA_WINDOW + lax.broadcasted_iota(jnp.int32, (H * T, nw), 1)
    dist = tpos4 - wpos
    p_w = _masked_softmax(s_w, (dist >= 0) & (dist < NSA_WINDOW) & (wpos >= 0))
    o_w = _dot(p_w.astype(BF16), vw_ref[pl.ds(q0, nw), :])

    gate = jax.nn.sigmoid(g_ref[...])
    for r in range(H):
        rows = slice(r * T, (r + 1) * T)
        o = (gate[:, 3 * r:3 * r + 1] * o_c[rows] + gate[:, 3 * r + 1:3 * r + 2] * o_s[rows]
             + gate[:, 3 * r + 2:3 * r + 3] * o_w[rows])
        o_ref[:, r * HEAD_DIM:(r + 1) * HEAD_DIM] = o.astype(o_ref.dtype)


def nsa_attention(q, g, k_cmp, v_cmp, ks, vs, kw, vw):
    S = q.shape[0]
    T, C = NSA_T, NSA_C
    n_cmp = k_cmp.shape[0]
    ncp = -(-n_cmp // 128) * 128
    ns = S // NSA_SLC_LEN
    k_cmp = jnp.pad(k_cmp, ((0, ncp - n_cmp), (0, 0)))
    v_cmp = jnp.pad(v_cmp, ((0, ncp - n_cmp), (0, 0)))
    cmp_start = np.arange(ncp) * NSA_CMP_STRIDE
    slc_start = np.arange(ns) * NSA_SLC_LEN
    overlap = ((cmp_start[:, None] < slc_start[None, :] + NSA_SLC_LEN)
               & (cmp_start[:, None] + NSA_CMP_LEN - 1 >= slc_start[None, :]))
    overlap = jnp.asarray(overlap, BF16)
    cpg = NSA_GROUP * NSA_SLC_LEN // C
    tok_blk = (np.arange(cpg)[:, None] * C + np.arange(C)[None, :]) // NSA_SLC_LEN
    expand = jnp.asarray(np.arange(NSA_GROUP)[None, :, None] == tok_blk[:, None, :], BF16)
    kw = jnp.pad(kw, ((NSA_WINDOW, 0), (0, 0)))
    vw = jnp.pad(vw, ((NSA_WINDOW, 0), (0, 0)))
    return pl.pallas_call(
        _nsa_kernel,
        grid=(S // T,),
        in_specs=[pl.BlockSpec((T, NSA_Q_W), lambda i: (i, 0)),
                  pl.BlockSpec((T, NSA_HEADS * 3), lambda i: (i, 0)),
                  _resident((ncp, HEAD_DIM)), _resident((ncp, HEAD_DIM)), _resident((ncp, ns)),
                  _resident((S, HEAD_DIM)), _resident((S, 2 * HEAD_DIM)),
                  _resident((S + NSA_WINDOW, HEAD_DIM)), _resident((S + NSA_WINDOW, HEAD_DIM)),
                  _resident((cpg, NSA_GROUP, C))],
        out_specs=pl.BlockSpec((T, NSA_Q_W), lambda i: (i, 0)),
        out_shape=jax.ShapeDtypeStruct((S, NSA_Q_W), BF16),
        scratch_shapes=[pltpu.VMEM((ns // NSA_GROUP, T, NSA_GROUP), BF16),
                        pltpu.VMEM((NSA_HEADS * T, HEAD_DIM), F32),
                        pltpu.VMEM((NSA_HEADS * T, 2 * HEAD_DIM), F32)],
        compiler_params=_params(1),
        name="nsa_attention",
    )(q, g, k_cmp, v_cmp, overlap, ks, vs, kw, vw, expand)


def nsa_compress(x, pe, w1, w2):
    S = x.shape[0]
    half = NSA_CMP_STRIDE * HEAD_DIM
    x16 = x.reshape(S // NSA_CMP_STRIDE, half)
    pe_flat = pe.reshape(NSA_CMP_LEN * HEAD_DIM)
    w1_flat = w1.reshape(NSA_CMP_LEN * HEAD_DIM, HEAD_DIM)
    first = _mm(x16 + pe_flat[None, :half], w1_flat[:half])
    second = _mm(x16 + pe_flat[None, half:], w1_flat[half:])
    hid = jax.nn.gelu(first[:-1] + second[1:])
    hid = jnp.pad(hid, ((0, 1), (0, 0)))
    return _mm(hid, w2)[:-1]


DIL_T = 128


def _dil_kernel(q_ref, k_ref, v_ref, o_ref):
    i = pl.program_id(0)
    T = DIL_T
    q0 = pl.multiple_of(i * T, T)
    tl = lax.broadcasted_iota(jnp.int32, (T, 1), 0)
    for s in range(DIL_HEADS_PER_GROUP):
        cols = slice(s * HEAD_DIM, (s + 1) * HEAD_DIM)
        parts = []
        m = jnp.full((T, 1), MASKED, F32)
        for g, (w, r) in enumerate(DIL_PATTERNS):
            n = w + T
            start = pl.multiple_of(q0 + (DIL_MAX_WINDOW - w), T)
            head = g * DIL_HEADS_PER_GROUP + s
            sc = _dot_nt(q_ref[:, head * HEAD_DIM:(head + 1) * HEAD_DIM], k_ref[pl.ds(start, n), cols]) * ATT_SCALE
            b = lax.broadcasted_iota(jnp.int32, (T, n), 1)
            d = tl + w - b
            mask = (d >= 0) & (d <= w) & ((d & (r - 1)) == 0) & (q0 - w + b >= 0)
            sm = jnp.where(mask, sc, MASKED)
            m = jnp.maximum(m, jnp.max(sm, axis=1, keepdims=True))
            parts.append((sm, mask, start, n))
        num = jnp.zeros((T, HEAD_DIM), F32)
        den = jnp.zeros((T, 1), F32)
        for sm, mask, start, n in parts:
            e = jnp.where(mask, jnp.exp(sm - m), 0.0)
            den = den + jnp.sum(e, axis=1, keepdims=True)
            num = num + _dot(e.astype(BF16), v_ref[pl.ds(start, n), cols])
        o_ref[:, cols] = (num / den).astype(o_ref.dtype)


def dilated_attention(q, k, v):
    S = q.shape[0]
    T = DIL_T
    k = jnp.pad(k, ((DIL_MAX_WINDOW, 0), (0, 0)))
    v = jnp.pad(v, ((DIL_MAX_WINDOW, 0), (0, 0)))
    return pl.pallas_call(
        _dil_kernel,
        grid=(S // T,),
        in_specs=[pl.BlockSpec((T, DIL_Q_W), lambda i: (i, 0)),
                  _resident((S + DIL_MAX_WINDOW, DIL_KV_W)), _resident((S + DIL_MAX_WINDOW, DIL_KV_W))],
        out_specs=pl.BlockSpec((T, DIL_KV_W), lambda i: (i, 0)),
        out_shape=jax.ShapeDtypeStruct((S, DIL_KV_W), BF16),
        compiler_params=_params(1),
        name="dilated_attention",
    )(q, k, v)


NORM_TM = 512
NORM_TN = 512


def _norm_mm_kernel(x_ref, g_ref, b_ref, o_ref, xn_ref):
    @pl.when(pl.program_id(1) == 0)
    def _():
        x = x_ref[...]
        y = x * lax.rsqrt(jnp.mean(x * x, axis=-1, keepdims=True) + NORM_EPS)
        xn_ref[...] = (y * g_ref[...]).astype(xn_ref.dtype)
    o_ref[...] = _dot(xn_ref[...], b_ref[...]).astype(o_ref.dtype)


def norm_mm(x, gain, w, out_dtype, return_xn=False):
    M, K = x.shape
    N = w.shape[1]
    tm, tn = min(NORM_TM, M), min(NORM_TN, N)
    main_spec = pl.BlockSpec((tm, tn), lambda i, j: (i, j))
    xn_spec = pl.BlockSpec((tm, K), lambda i, j: (i, 0))
    main_shape = jax.ShapeDtypeStruct((M, N), out_dtype)
    return pl.pallas_call(
        _norm_mm_kernel,
        grid=(M // tm, N // tn),
        in_specs=[xn_spec, pl.BlockSpec((1, K), lambda i, j: (0, 0)), pl.BlockSpec((K, tn), lambda i, j: (0, j))],
        out_specs=(main_spec, xn_spec) if return_xn else main_spec,
        out_shape=(main_shape, jax.ShapeDtypeStruct((M, K), BF16)) if return_xn else main_shape,
        scratch_shapes=[] if return_xn else [pltpu.VMEM((tm, K), BF16)],
        compiler_params=_params(2),
        name="norm_mm",
    )(x, gain.reshape(1, K).astype(F32), w)


PREP_T = 256
PREP_SEGMENTS = (
    ("z", GATE_RANK, None, False, 1.0, "bf16"),
    ("a_q", SB_W, None, False, LOG2_SCALE, "bf16"),
    ("a_k", SB_W, None, False, 1.0, "bf16"),
    ("a_v", SB_W, None, False, 1.0, "bf16"),
    ("b_q", DSA_Q_W, 0, True, LOG2_SCALE, "bf16"),
    ("b_k", DSA_KV_W, 1, True, 1.0, "bf16"),
    ("b_v", DSA_KV_W, None, False, 1.0, "ones"),
    ("b_iq", DSA_IQ_W, None, False, 1.0, "bf16"),
    ("b_ik", HEAD_DIM, None, False, 1.0, "ik"),
    ("c_q", NSA_Q_W, 2, True, LOG2_SCALE, "bf16"),
    ("c_kc", NSA_KV_W, None, True, 1.0, "f32"),
    ("c_vc", NSA_KV_W, None, False, 1.0, "f32"),
    ("c_ks", NSA_KV_W, 4, True, 1.0, "bf16"),
    ("c_vs", NSA_KV_W, None, False, 1.0, "ones"),
    ("c_kw", NSA_KV_W, 5, True, 1.0, "bf16"),
    ("c_vw", NSA_KV_W, None, False, 1.0, "bf16"),
    ("d_q", DIL_Q_W, 6, True, 1.0, "bf16"),
    ("d_k", DIL_KV_W, 7, True, 1.0, "bf16"),
    ("d_v", DIL_KV_W, None, False, 1.0, "bf16"),
    ("misc", HEAD_DIM, None, False, 1.0, "misc"),
)
PREP_WIDTH = sum(seg[1] for seg in PREP_SEGMENTS)
N_MISC = DSA_IDX_HEADS + NSA_HEADS * 3


def _prep_outputs(S):
    shapes = []
    for name, width, _, _, _, kind in PREP_SEGMENTS:
        if kind == "bf16":
            shapes.append((width, BF16))
        elif kind == "ones":
            shapes.append((width + HEAD_DIM, BF16))
        elif kind == "f32":
            shapes.append((width, F32))
        elif kind == "ik":
            shapes.append((DSA_IDX_DIM, BF16))
        else:
            shapes.append((N_MISC, F32))
    return shapes


def _prep_kernel(p_ref, g_ref, cs_ref, sn_ref, *o_refs):
    cs = cs_ref[...]
    sn = sn_ref[...]
    off = 0
    for (name, width, gi, rope, scale, kind), o_ref in zip(PREP_SEGMENTS, o_refs):
        if kind == "ik":
            o_ref[...] = p_ref[:, off:off + DSA_IDX_DIM].astype(o_ref.dtype)
        elif kind == "misc":
            o_ref[...] = p_ref[:, off:off + N_MISC]
        else:
            for hd in range(width // HEAD_DIM):
                lanes = slice(hd * HEAD_DIM, (hd + 1) * HEAD_DIM)
                x = p_ref[:, off + hd * HEAD_DIM:off + (hd + 1) * HEAD_DIM]
                if gi is not None:
                    x = x * lax.rsqrt(jnp.mean(x * x, axis=-1, keepdims=True) + NORM_EPS) * g_ref[gi:gi + 1, :]
                if rope:
                    x = x * cs + pltpu.roll(x, HEAD_DIM // 2, 1) * sn
                if scale != 1.0:
                    x = x * scale
                o_ref[:, lanes] = x.astype(o_ref.dtype)
            if kind == "ones":
                o_ref[:, width:] = jnp.ones((o_ref.shape[0], HEAD_DIM), o_ref.dtype)
        off += width


def prepare_operands(proj, qk_gain, cs, sn):
    S = proj.shape[0]
    T = min(PREP_T, S)
    row = lambda i: (i, 0)
    shapes = _prep_outputs(S)
    outs = pl.pallas_call(
        _prep_kernel,
        grid=(S // T,),
        in_specs=[pl.BlockSpec((T, PREP_WIDTH), row),
                  pl.BlockSpec((N_QK_GAINS, HEAD_DIM), lambda i: (0, 0)),
                  pl.BlockSpec((T, HEAD_DIM), row), pl.BlockSpec((T, HEAD_DIM), row)],
        out_specs=tuple(pl.BlockSpec((T, w), row) for w, _ in shapes),
        out_shape=tuple(jax.ShapeDtypeStruct((S, w), dt) for w, dt in shapes),
        compiler_params=_params(1),
        name="prepare_operands",
    )(proj, qk_gain, cs, sn)
    return {seg[0]: o for seg, o in zip(PREP_SEGMENTS, outs)}


def fused_in_weights(w_in, gate_down):
    names = ("a_q", "a_k", "a_v", "b_q", "b_k", "b_v", "b_iq", "b_ik", "b_iw",
             "c_q", "c_kc", "c_vc", "c_ks", "c_vs", "c_kw", "c_vw", "c_g", "d_q", "d_k", "d_v")
    parts = dict(zip(names, jnp.split(w_in, SPLIT_OFFSETS, axis=1)))
    parts["z"] = gate_down
    parts["misc"] = jnp.concatenate([parts["b_iw"], parts["c_g"]], axis=1)
    cols = []
    for name, width, *_ in PREP_SEGMENTS:
        p = parts[name]
        cols.append(jnp.pad(p, ((0, 0), (0, width - p.shape[1]))))
    return jnp.concatenate(cols, axis=1).astype(BF16)


def token_mixing(h, cs, sn, norm_gain, w_in, qk_gain, nsa_pe, nsa_w1, nsa_w2, gate_down, gate_up, w_branch, w_out):
    proj = norm_mm(h, norm_gain, fused_in_weights(w_in, gate_down), F32)
    t = prepare_operands(proj, qk_gain, cs, sn)
    o_a = sb_attention(t["a_q"], t["a_k"], t["a_v"])
    o_b = dsa_attention(t["b_q"], t["b_k"], t["b_v"], t["b_iq"], t["b_ik"], t["misc"][:, :DSA_IDX_HEADS])
    k_cmp = rmsnorm(nsa_compress(t["c_kc"], nsa_pe[0], nsa_w1[0], nsa_w2[0]), qk_gain[3])
    v_cmp = nsa_compress(t["c_vc"], nsa_pe[1], nsa_w1[1], nsa_w2[1])
    o_c = nsa_attention(t["c_q"], t["misc"][:, DSA_IDX_HEADS:], k_cmp.astype(BF16), v_cmp.astype(BF16),
                        t["c_ks"], t["c_vs"], t["c_kw"], t["c_vw"])
    o_d = dilated_attention(t["d_q"], t["d_k"], t["d_v"])
    merged = gated_merge(t["z"], (o_a, o_b, o_c, o_d), gate_up, w_branch)
    return _mm(merged, w_out, residual=h)


MERGE_TM = 512
MERGE_TN = 512


def _merge_kernel(z_ref, oa_ref, ob_ref, oc_ref, od_ref, gu_ref, wb_ref, o_ref):
    z = z_ref[...]
    acc = None
    off = 0
    for i, br_ref in enumerate((oa_ref, ob_ref, oc_ref, od_ref)):
        width = BRANCH_WIDTHS[i]
        gate = jax.nn.sigmoid(_dot(z, gu_ref[i]))
        term = gate * _dot(br_ref[...], wb_ref[off:off + width, :])
        acc = term if acc is None else acc + term
        off += width
    o_ref[...] = acc.astype(o_ref.dtype)


def gated_merge(z, branches, gate_up, w_branch):
    S = z.shape[0]
    D = w_branch.shape[1]
    tm, tn = min(MERGE_TM, S), min(MERGE_TN, D)
    row = lambda i, j: (i, 0)
    return pl.pallas_call(
        _merge_kernel,
        grid=(S // tm, D // tn),
        in_specs=[pl.BlockSpec((tm, GATE_RANK), row)]
                 + [pl.BlockSpec((tm, w), row) for w in BRANCH_WIDTHS]
                 + [pl.BlockSpec((N_BRANCHES, GATE_RANK, tn), lambda i, j: (0, 0, j)),
                    pl.BlockSpec((BRANCH_WIDTH, tn), lambda i, j: (0, j))],
        out_specs=pl.BlockSpec((tm, tn), lambda i, j: (i, j)),
        out_shape=jax.ShapeDtypeStruct((S, D), BF16),
        compiler_params=_params(2),
        name="gated_merge",
    )(z, *branches, gate_up.astype(BF16), w_branch.astype(BF16))


PEER_ST = 256
PEER_TM = 512
PEER_EC = 512
PEER_HALF = PEER_KEY_DIM // 2
PEER_STAT_ROWS = 32
assert (PEER_TOPK + 1) // 9 == 1 and PEER_TOPK + 1 <= 24


def _top_rows(scores, n, n_rows):
    st = scores.shape[1]
    rid = lax.broadcasted_iota(jnp.int32, (n_rows, st), 0)
    out = jnp.full((n_rows, st), EXTRACTED, F32)
    for r in range(n):
        m = jnp.max(scores, axis=0, keepdims=True)
        out = jnp.where(rid == r, m, out)
        scores = jnp.where(scores == m, EXTRACTED, scores)
    return out


def _peer_stats_kernel(q_ref, w1_ref, w2_ref, o_ref):
    st = q_ref.shape[0]
    k = PEER_TOPK
    rid = lax.broadcasted_iota(jnp.int32, (PEER_STAT_ROWS, st), 0)
    rid8 = lax.broadcasted_iota(jnp.int32, (8, st), 0)
    stats = jnp.zeros((PEER_STAT_ROWS, st), F32)
    for h in range(PEER_HEADS):
        qh = q_ref[:, h * PEER_KEY_DIM:(h + 1) * PEER_KEY_DIM]
        v1 = _top_rows(_dot_nt(w1_ref[h], qh), k + 1, 24)
        v2 = _top_rows(_dot_nt(w2_ref[h], qh), k + 1, 24)
        pieces = [v1[0:1] + v2]
        for a in range(1, 8):
            pieces.append(jnp.where(rid8 < (k + 1) // (a + 1), v1[a:a + 1] + v2[0:8], EXTRACTED))
        pieces.append(v1[8:24] + v2[0:1])
        tops = _top_rows(jnp.concatenate(pieces, axis=0), k + 1, 24)
        c1 = tops[0:1]
        den = jnp.sum(jnp.exp(tops[0:k] - c1), axis=0, keepdims=True)
        thr = 0.5 * (tops[k - 1:k] + tops[k:k + 1])
        stats = jnp.where(rid == h, thr, stats)
        stats = jnp.where(rid == PEER_HEADS + h, c1, stats)
        stats = jnp.where(rid == 2 * PEER_HEADS + h, 1.0 / den, stats)
    o_ref[...] = stats


def _peer_weight_kernel(xn_ref, ut_ref, q_ref, kf_ref, st_ref, o_ref):
    act = jax.nn.gelu(_dot(xn_ref[...], ut_ref[...]))
    st = st_ref[...]
    w = jnp.zeros(act.shape, F32)
    for h in range(PEER_HEADS):
        sf = _dot(q_ref[:, h * PEER_KEY_DIM:(h + 1) * PEER_KEY_DIM], kf_ref[h])
        gate = jnp.exp(sf - st[:, PEER_HEADS + h:PEER_HEADS + h + 1]) * st[:, 2 * PEER_HEADS + h:2 * PEER_HEADS + h + 1]
        w = w + jnp.where(sf >= st[:, h:h + 1], gate, 0.0)
    o_ref[...] = (w * act).astype(o_ref.dtype)


def peer_ffn(h, norm_gain, wq, subkeys, u, v):
    S, D = h.shape
    nk, half = PEER_N_KEYS, PEER_HALF
    q, xn = norm_mm(h, norm_gain, wq.astype(BF16), BF16, return_xn=True)
    zeros = jnp.zeros((PEER_HEADS, nk, half), F32)
    w1 = jnp.concatenate([subkeys[:, 0], zeros], axis=-1).astype(BF16)
    w2 = jnp.concatenate([zeros, subkeys[:, 1]], axis=-1).astype(BF16)
    st_t = pl.pallas_call(
        _peer_stats_kernel,
        grid=(S // PEER_ST,),
        in_specs=[pl.BlockSpec((PEER_ST, PEER_HEADS * PEER_KEY_DIM), lambda i: (i, 0)),
                  pl.BlockSpec((PEER_HEADS, nk, PEER_KEY_DIM), lambda i: (0, 0, 0)),
                  pl.BlockSpec((PEER_HEADS, nk, PEER_KEY_DIM), lambda i: (0, 0, 0))],
        out_specs=pl.BlockSpec((PEER_STAT_ROWS, PEER_ST), lambda i: (0, i)),
        out_shape=jax.ShapeDtypeStruct((PEER_STAT_ROWS, S), F32),
        compiler_params=_params(1),
        name="peer_stats",
    )(q, w1, w2)
    stats = st_t.T
    top = jnp.repeat(jnp.swapaxes(subkeys[:, 0], 1, 2), nk, axis=2)
    bot = jnp.tile(jnp.swapaxes(subkeys[:, 1], 1, 2), (1, 1, nk))
    kfull = jnp.concatenate([top, bot], axis=1).astype(BF16)
    tm, ec = min(PEER_TM, S), PEER_EC
    wmat = pl.pallas_call(
        _peer_weight_kernel,
        grid=(S // tm, PEER_EXPERTS // ec),
        in_specs=[pl.BlockSpec((tm, D), lambda i, j: (i, 0)),
                  pl.BlockSpec((D, ec), lambda i, j: (0, j)),
                  pl.BlockSpec((tm, PEER_HEADS * PEER_KEY_DIM), lambda i, j: (i, 0)),
                  pl.BlockSpec((PEER_HEADS, PEER_KEY_DIM, ec), lambda i, j: (0, 0, j)),
                  pl.BlockSpec((tm, PEER_STAT_ROWS), lambda i, j: (i, 0))],
        out_specs=pl.BlockSpec((tm, ec), lambda i, j: (i, j)),
        out_shape=jax.ShapeDtypeStruct((S, PEER_EXPERTS), BF16),
        compiler_params=_params(2),
        name="peer_weights",
    )(xn, u.T.astype(BF16), q, kfull, stats)
    return _mm(wmat, v, residual=h)


def kernel(x, norm_mix, w_in, qk_gain, nsa_pe, nsa_w1, nsa_w2, gate_down, gate_up, w_branch, w_out,
           norm_ffn, peer_wq, peer_subkeys, peer_u, peer_v):
    S = x.shape[1]
    cos, sin = rope_tables(S, x.dtype)
    cos, sin = cos.reshape(S, HEAD_DIM // 2), sin.reshape(S, HEAD_DIM // 2)
    cs = jnp.concatenate([cos, cos], axis=1)
    sn = jnp.concatenate([-sin, sin], axis=1)
    h = x[0]
    for l in range(DEPTH):
        h = token_mixing(h, cs, sn, norm_mix[l], w_in[l], qk_gain[l], nsa_pe[l], nsa_w1[l], nsa_w2[l],
                         gate_down[l], gate_up[l], w_branch[l], w_out[l])
        h = peer_ffn(h, norm_ffn[l], peer_wq[l], peer_subkeys[l], peer_u[l], peer_v[l])
    return h[None]
```

```python
import math, functools
import jax, jax.numpy as jnp
from jax import lax
import numpy as np
from jax.experimental import pallas as pl
from jax.experimental.pallas import tpu as pltpu

D_MODEL = 4096
BATCH = 1
SEQ = 16384
DEPTH = 4

HEAD_DIM = 128
ROPE_THETA = 10000.0
NORM_EPS = 1e-6
SB_HEADS = 4
DSA_HEADS = 4
DSA_KV_HEADS = 1
DSA_IDX_HEADS = 4
DSA_IDX_DIM = 64
DSA_TOPK = 256
NSA_HEADS = 4
NSA_KV_HEADS = 1
NSA_CMP_LEN = 32
NSA_CMP_STRIDE = 16
NSA_SLC_LEN = 32
NSA_N_SEL = 8
NSA_WINDOW = 512
NSA_FORCED_SCORE = 1e4
DIL_PATTERNS = ((128, 1), (512, 4), (2048, 16))
DIL_GROUPS = len(DIL_PATTERNS)
DIL_HEADS_PER_GROUP = 2
DIL_MAX_WINDOW = max(w for w, _ in DIL_PATTERNS)
N_BRANCHES = 4
GATE_RANK = 256
PEER_HEADS = 8
PEER_N_KEYS = 64
PEER_EXPERTS = PEER_N_KEYS ** 2
PEER_KEY_DIM = 128
PEER_TOPK = 16

SB_W = SB_HEADS * HEAD_DIM
DSA_Q_W = DSA_HEADS * HEAD_DIM
DSA_KV_W = DSA_KV_HEADS * HEAD_DIM
DSA_IQ_W = DSA_IDX_HEADS * DSA_IDX_DIM
NSA_Q_W = NSA_HEADS * HEAD_DIM
NSA_KV_W = NSA_KV_HEADS * HEAD_DIM
DIL_Q_W = DIL_GROUPS * DIL_HEADS_PER_GROUP * HEAD_DIM
DIL_KV_W = DIL_HEADS_PER_GROUP * HEAD_DIM
IN_SPLITS = (SB_W, SB_W, SB_W,
             DSA_Q_W, DSA_KV_W, DSA_KV_W, DSA_IQ_W, DSA_IDX_DIM, DSA_IDX_HEADS,
             NSA_Q_W, NSA_KV_W, NSA_KV_W, NSA_KV_W, NSA_KV_W, NSA_KV_W, NSA_KV_W, NSA_HEADS * 3,
             DIL_Q_W, DIL_KV_W, DIL_KV_W)
IN_WIDTH = sum(IN_SPLITS)
SPLIT_OFFSETS = tuple(int(o) for o in np.cumsum(IN_SPLITS)[:-1])
BRANCH_WIDTHS = (SB_W, DSA_Q_W, NSA_Q_W, DIL_KV_W)
BRANCH_WIDTH = sum(BRANCH_WIDTHS)
BRANCH_OFFSETS = tuple(int(o) for o in np.cumsum(BRANCH_WIDTHS)[:-1])
N_QK_GAINS = 8

ATT_SCALE = HEAD_DIM ** -0.5
LOG2_SCALE = ATT_SCALE * math.log2(math.e)
MASKED = -1e30
EXTRACTED = -3e38
INT32_MIN = -2 ** 31
F32_TINY = float(np.finfo(np.float32).tiny)
VMEM_LIMIT = 56 * 1024 * 1024

BF16 = jnp.bfloat16
F32 = jnp.float32


def _params(n_grid):
    return pltpu.CompilerParams(dimension_semantics=("arbitrary",) * n_grid, vmem_limit_bytes=VMEM_LIMIT)


def _resident(shape):
    nd = len(shape)
    return pl.BlockSpec(shape, lambda *_: (0,) * nd, pipeline_mode=pl.Buffered(1))


def _dot_nt(a, b):
    return lax.dot_general(a, b, (((1,), (1,)), ((), ())), preferred_element_type=F32)


def _dot(a, b):
    return jnp.dot(a, b, preferred_element_type=F32)


def _mm_kernel(a_ref, b_ref, o_ref):
    o_ref[...] = _dot(a_ref[...], b_ref[...]).astype(o_ref.dtype)


def _mm_res_kernel(a_ref, b_ref, r_ref, o_ref):
    o_ref[...] = (r_ref[...] + _dot(a_ref[...], b_ref[...])).astype(o_ref.dtype)


def _pick(n, prefs):
    for p in prefs:
        if n % p == 0:
            return p
    return n


def _mm(a, b, out_dtype=F32, residual=None):
    lead = a.shape[:-1]
    K = a.shape[-1]
    N = b.shape[-1]
    a2 = a.reshape(-1, K).astype(jnp.bfloat16)
    M = a2.shape[0]
    n_pad = (-N) % 128
    b2 = b.astype(jnp.bfloat16)
    if n_pad:
        b2 = jnp.pad(b2, ((0, 0), (0, n_pad)))
    Np = N + n_pad
    tm = _pick(M, (1024, 512, 256, 128))
    tn = _pick(Np, (512, 256, 128))
    in_specs = [pl.BlockSpec((tm, K), lambda i, j: (i, 0)),
                pl.BlockSpec((K, tn), lambda i, j: (0, j))]
    args = [a2, b2]
    body = _mm_kernel
    if residual is not None:
        assert n_pad == 0
        in_specs.append(pl.BlockSpec((tm, tn), lambda i, j: (i, j)))
        args.append(residual.reshape(M, N))
        body = _mm_res_kernel
    out = pl.pallas_call(
        body,
        grid=(M // tm, Np // tn),
        in_specs=in_specs,
        out_specs=pl.BlockSpec((tm, tn), lambda i, j: (i, j)),
        out_shape=jax.ShapeDtypeStruct((M, Np), out_dtype),
        compiler_params=pltpu.CompilerParams(
            dimension_semantics=("arbitrary", "arbitrary"),
            vmem_limit_bytes=VMEM_LIMIT),
        name="dense_mm",
    )(*args)
    if n_pad:
        out = out[:, :N]
    return out.reshape(*lead, N)


def rmsnorm(x, g):
    xf = x.astype(jnp.float32)
    y = xf * lax.rsqrt(jnp.mean(xf * xf, axis=-1, keepdims=True) + NORM_EPS)
    return (y * g.astype(jnp.float32)).astype(x.dtype)


def rope_tables(seq, dtype):
    inv = ROPE_THETA ** (-jnp.arange(0, HEAD_DIM, 2, dtype=jnp.float32) / HEAD_DIM)
    ang = jnp.arange(seq, dtype=jnp.float32)[:, None] * inv[None, :]
    return jnp.cos(ang)[None, :, None, :].astype(dtype), jnp.sin(ang)[None, :, None, :].astype(dtype)


def _stack_heads(q_ref, n_heads):
    return jnp.concatenate([q_ref[:, r * HEAD_DIM:(r + 1) * HEAD_DIM] for r in range(n_heads)], axis=0)


def _tile_rows(x, n):
    return jnp.concatenate([x] * n, axis=0)


def _flash_init(m_scr, acc_scr):
    m_scr[...] = jnp.full(m_scr.shape, MASKED, F32)
    acc_scr[...] = jnp.zeros(acc_scr.shape, F32)


def _flash_step(qs, k_chunk, bias, v_aug, m_scr, acc_scr, n_heads):
    T, C = bias.shape
    s = _dot_nt(qs, k_chunk)
    ps, alphas = [], []
    for r in range(n_heads):
        rows = slice(r * T, (r + 1) * T)
        sm = s[rows] + bias
        m_old = m_scr[rows]
        m_new = jnp.maximum(m_old, jnp.max(sm, axis=1, keepdims=True))
        m_scr[rows] = m_new
        p = [jnp.exp2(sm[:, u * 128:(u + 1) * 128] - m_new) for u in range(C // 128)]
        ps.append(jnp.concatenate(p, axis=1).astype(BF16))
        alpha = jnp.exp2(m_old - m_new)
        alphas.append(jnp.concatenate([alpha, alpha], axis=1))
    pv = _dot(jnp.concatenate(ps, axis=0), v_aug)
    acc_scr[...] = jnp.concatenate(alphas, axis=0) * acc_scr[...] + pv


def _flash_result(acc_scr):
    acc = acc_scr[...]
    return acc[:, :HEAD_DIM] / acc[:, HEAD_DIM:]


def _masked_softmax(s, mask):
    sm = jnp.where(mask, s, MASKED)
    m = jnp.max(sm, axis=1, keepdims=True)
    e = jnp.where(mask, jnp.exp2(sm - m), 0.0)
    den = jnp.sum(e, axis=1, keepdims=True)
    return e * (1.0 / jnp.maximum(den, F32_TINY))


SB_T = 256
SB_DEAD_LOG2 = -200.0


def _sb_kernel(q_ref, k_ref, v_ref, tri_ref, o_ref, later_scr, acc_scr):
    i = pl.program_id(0)
    T = SB_T
    q0 = pl.multiple_of(i * T, T)
    tri = tri_ref[...]
    later_scr[...] = jnp.zeros(later_scr.shape, F32)
    acc_scr[...] = jnp.zeros(acc_scr.shape, F32)

    def chunk(s0, causal):
        heads = [slice(h * HEAD_DIM, (h + 1) * HEAD_DIM) for h in range(SB_HEADS)]
        zs = [_dot_nt(q_ref[:, cols], k_ref[pl.ds(s0, T), cols]) for cols in heads]
        lms = []
        for z in zs:
            sp = jnp.log2(1.0 + jnp.exp2(-jnp.abs(z)))
            lm = -(jnp.maximum(z, 0.0) + sp)
            if causal is not None:
                lm = jnp.where(causal, lm, 0.0)
            lms.append(lm)
        withins = [_dot(lm.astype(BF16), tri) for lm in lms]
        ws = []
        for h in range(SB_HEADS):
            e = zs[h] + lms[h] + withins[h]
            later = later_scr[h]
            w = jnp.concatenate([jnp.exp2(e[:, u * 128:(u + 1) * 128] + later) for u in range(T // 128)], axis=1)
            if causal is not None:
                w = jnp.where(causal, w, 0.0)
            ws.append(w.astype(BF16))
            later_scr[h] = later + jnp.sum(lms[h], axis=1, keepdims=True)
        for h, cols in enumerate(heads):
            acc_scr[h] += _dot(ws[h], v_ref[pl.ds(s0, T), cols])

    row = lax.broadcasted_iota(jnp.int32, (T, T), 0)
    col = lax.broadcasted_iota(jnp.int32, (T, T), 1)
    chunk(q0, col < row)

    def cond(state):
        n, worst = state
        return (n <= i) & (worst > SB_DEAD_LOG2)

    def body(state):
        n, _ = state
        chunk(pl.multiple_of((i - n) * T, T), None)
        return n + 1, jnp.max(later_scr[...])

    lax.while_loop(cond, body, (jnp.int32(1), jnp.max(later_scr[...])))
    for h in range(SB_HEADS):
        o_ref[:, h * HEAD_DIM:(h + 1) * HEAD_DIM] = acc_scr[h].astype(o_ref.dtype)


def sb_attention(q, k, v):
    S = q.shape[0]
    T = SB_T
    tri = jnp.tril(jnp.ones((T, T), F32), -1).astype(BF16)
    return pl.pallas_call(
        _sb_kernel,
        grid=(S // T,),
        in_specs=[pl.BlockSpec((T, SB_W), lambda i: (i, 0)),
                  _resident((S, SB_W)), _resident((S, SB_W)), _resident((T, T))],
        out_specs=pl.BlockSpec((T, SB_W), lambda i: (i, 0)),
        out_shape=jax.ShapeDtypeStruct((S, SB_W), BF16),
        scratch_shapes=[pltpu.VMEM((SB_HEADS, T, 128), F32),
                        pltpu.VMEM((SB_HEADS, T, HEAD_DIM), F32)],
        compiler_params=_params(1),
        name="sb_attention",
    )(q, k, v, tri)


DSA_T = 256
DSA_C = 512
DSA_CH = 256
DSA_BITS_PER_CHECK = 4


def _dsa_kernel(n_top, q_ref, iq_ref, iw_ref, k_ref, v_ref, ik_ref, tri_ref, o_ref,
                key_scr, m_scr, acc_scr):
    i = pl.program_id(0)
    T, C, CH = DSA_T, DSA_C, DSA_CH
    H = DSA_HEADS
    q0 = i * T
    n_chunks = (q0 + T + C - 1) // C
    tpos = q0 + lax.broadcasted_iota(jnp.int32, (T, 1), 0)
    iw = iw_ref[...]
    iq4 = jnp.concatenate([iq_ref[:, h * DSA_IDX_DIM:(h + 1) * DSA_IDX_DIM] for h in range(DSA_IDX_HEADS)], axis=0)

    def index_keys(c, causal):
        s0 = pl.multiple_of(c * C, C)
        d = _dot_nt(iq4, ik_ref[pl.ds(s0, C), :])
        acc = jnp.zeros((T, C), F32)
        for h in range(DSA_IDX_HEADS):
            acc = acc + jnp.maximum(d[h * T:(h + 1) * T], 0.0) * iw[:, h:h + 1]
        bits = lax.bitcast_convert_type(acc, jnp.int32)
        key = jnp.where(bits < 0, bits ^ jnp.int32(0x7FFFFFFF), bits)
        key = jnp.where(acc == 0.0, 0, key)
        if causal:
            spos = s0 + lax.broadcasted_iota(jnp.int32, (T, C), 1)
            key = jnp.where(spos <= tpos, key, INT32_MIN)
        key_scr[c] = key

    def idx_body(c, carry):
        index_keys(c, False)
        return carry

    lax.fori_loop(0, n_chunks - 1, idx_body, 0)
    index_keys(n_chunks - 1, True)

    def count_ge(cand):
        counts = []
        for r0 in range(0, T, 128):
            cand_r = cand[r0:r0 + 128]

            def body(c, acc):
                kk = key_scr[c, r0:r0 + 128, :]
                for u in range(C // 128):
                    acc = acc + jnp.where(kk[:, u * 128:(u + 1) * 128] >= cand_r, 1.0, 0.0)
                return acc
            acc = lax.fori_loop(0, n_chunks, body, jnp.zeros((128, 128), F32))
            counts.append(jnp.sum(acc, axis=1, keepdims=True))
        return jnp.concatenate(counts, axis=0)

    ge0 = count_ge(jnp.zeros((T, 128), jnp.int32))
    gt0 = count_ge(jnp.ones((T, 128), jnp.int32))
    base0 = jnp.where(ge0 >= n_top, 0, jnp.full((T, 128), INT32_MIN, jnp.int32))
    done0 = ((gt0 < n_top) & (ge0 >= n_top)) | (ge0 == n_top) | (tpos + 1 < n_top)
    done0 = jnp.where(jnp.broadcast_to(done0, (T, 128)), 1.0, 0.0)

    def try_bit(bit, base, done):
        cand = base + (jnp.int32(1) << bit)
        cnt = count_ge(cand)
        return jnp.where((cnt >= n_top) & (done < 0.5), cand, base), jnp.where(cnt == n_top, 1.0, done)

    def high_body(step, state):
        return try_bit(jnp.int32(30) - step, *state)

    base, done = lax.fori_loop(0, 15, high_body, (base0, done0))

    def low_cond(state):
        b, _, _, n_open = state
        return (b >= 0) & (n_open > 0.0)

    def low_body(state):
        b, base, done, _ = state
        for step in range(DSA_BITS_PER_CHECK):
            base, done = try_bit(b - step, base, done)
        return b - DSA_BITS_PER_CHECK, base, done, jnp.sum(1.0 - done)

    _, thr, _, _ = lax.while_loop(low_cond, low_body, (jnp.int32(15), base, done, jnp.sum(1.0 - done)))
    thr = jnp.maximum(thr, INT32_MIN + 1)
    need = jnp.broadcast_to(n_top - count_ge(thr + 1), (T, 128))
    tri = tri_ref[...]

    _flash_init(m_scr, acc_scr)
    qs = _stack_heads(q_ref, H)

    def att_body(c, eq_seen):
        kk = key_scr[c]
        for half in range(C // CH):
            s0 = pl.multiple_of(c * C + half * CH, CH)
            eqs, gts = [], []
            for u in range(CH // 128):
                ku = kk[:, half * CH + u * 128:half * CH + (u + 1) * 128]
                eqs.append(ku == thr)
                gts.append(ku > thr)
            eq = jnp.concatenate([jnp.where(e, 1.0, 0.0) for e in eqs], axis=1).astype(BF16)
            pref = _dot(eq, tri)
            bias = [jnp.where(gts[u] | (eqs[u] & (eq_seen + pref[:, u * 128:(u + 1) * 128] <= need)),
                              0.0, MASKED) for u in range(CH // 128)]
            eq_seen = eq_seen + pref[:, CH:]
            _flash_step(qs, k_ref[pl.ds(s0, CH), :], jnp.concatenate(bias, axis=1), v_ref[pl.ds(s0, CH), :],
                        m_scr, acc_scr, H)
        return eq_seen

    lax.fori_loop(0, n_chunks, att_body, jnp.zeros((T, 128), F32))
    out = _flash_result(acc_scr)
    for r in range(H):
        o_ref[:, r * HEAD_DIM:(r + 1) * HEAD_DIM] = out[r * T:(r + 1) * T].astype(o_ref.dtype)


def dsa_attention(q, k, v, iq, ik, iw):
    S = q.shape[0]
    T, C, CH = DSA_T, DSA_C, DSA_CH
    n_top = min(DSA_TOPK, S // 4)
    tri = jnp.concatenate([jnp.triu(jnp.ones((CH, CH), F32)), jnp.ones((CH, 128), F32)], axis=1).astype(BF16)
    return pl.pallas_call(
        functools.partial(_dsa_kernel, n_top),
        grid=(S // T,),
        in_specs=[pl.BlockSpec((T, DSA_Q_W), lambda i: (i, 0)),
                  pl.BlockSpec((T, DSA_IQ_W), lambda i: (i, 0)),
                  pl.BlockSpec((T, DSA_IDX_HEADS), lambda i: (i, 0)),
                  _resident((S, HEAD_DIM)), _resident((S, 2 * HEAD_DIM)), _resident((S, DSA_IDX_DIM)),
                  _resident((CH, CH + 128))],
        out_specs=pl.BlockSpec((T, DSA_Q_W), lambda i: (i, 0)),
        out_shape=jax.ShapeDtypeStruct((S, DSA_Q_W), BF16),
        scratch_shapes=[pltpu.VMEM((S // C, T, C), jnp.int32),
                        pltpu.VMEM((DSA_HEADS * T, HEAD_DIM), F32),
                        pltpu.VMEM((DSA_HEADS * T, 2 * HEAD_DIM), F32)],
        compiler_params=_params(1),
        name="dsa_attention",
    )(q, iq, iw, k, v, ik, tri)


NSA_T = 512
NSA_C = 512
NSA_CH = 256
NSA_GROUP = 128


def _nsa_kernel(q_ref, g_ref, kc_ref, vc_ref, ov_ref, ks_ref, vs_ref, kw_ref, vw_ref, e_ref, o_ref,
                sel_scr, m_scr, acc_scr):
    i = pl.program_id(0)
    T, C, CH = NSA_T, NSA_C, NSA_CH
    H = NSA_HEADS
    q0 = pl.multiple_of(i * T, T)
    ncp = kc_ref.shape[0]
    ns = ov_ref.shape[1]
    tpos = q0 + lax.broadcasted_iota(jnp.int32, (T, 1), 0)
    tpos4 = _tile_rows(tpos, H)
    qs = _stack_heads(q_ref, H)

    s_c = _dot_nt(qs, kc_ref[...])
    cmp_end = lax.broadcasted_iota(jnp.int32, (H * T, ncp), 1) * NSA_CMP_STRIDE + (NSA_CMP_LEN - 1)
    p_c = _masked_softmax(s_c, cmp_end <= tpos4)
    o_c = _dot(p_c.astype(BF16), vc_ref[...])

    psum = p_c[0:T]
    for r in range(1, H):
        psum = psum + p_c[r * T:(r + 1) * T]
    ov = ov_ref[...]
    hi = psum.astype(BF16)
    r1 = psum - hi.astype(F32)
    mid = r1.astype(BF16)
    lo = (r1 - mid.astype(F32)).astype(BF16)
    imp = _dot(hi, ov) + _dot(mid, ov) + _dot(lo, ov)

    blk = lax.broadcasted_iota(jnp.int32, (T, ns), 1)
    cur = tpos >> (NSA_SLC_LEN.bit_length() - 1)
    forced = (blk == 0) | (blk == cur) | (blk == cur - 1)
    imp = jnp.where(forced, NSA_FORCED_SCORE, imp)
    imp = jnp.where(blk * NSA_SLC_LEN <= tpos, imp, MASKED)
    blk_f = blk.astype(F32)
    sel = jnp.zeros((T, ns), F32)
    for _ in range(min(NSA_N_SEL, ns)):
        mx = jnp.max(imp, axis=1, keepdims=True)
        first = jnp.min(jnp.where(imp == mx, blk_f, float(ns)), axis=1, keepdims=True)
        hit = blk_f == first
        sel = jnp.where(hit, 1.0, sel)
        imp = jnp.where(hit, EXTRACTED, imp)
    for g in range(ns // NSA_GROUP):
        sel_scr[g] = sel[:, g * NSA_GROUP:(g + 1) * NSA_GROUP].astype(BF16)

    _flash_init(m_scr, acc_scr)
    chunks_per_group = NSA_GROUP * NSA_SLC_LEN // C

    def sel_body(c, carry):
        s0 = pl.multiple_of(c * C, C)
        member = _dot(sel_scr[c // chunks_per_group], e_ref[c % chunks_per_group])
        spos = s0 + lax.broadcasted_iota(jnp.int32, (T, C), 1)
        bias = jnp.where((member > 0.5) & (spos <= tpos), 0.0, MASKED)
        for half in range(C // CH):
            h0 = pl.multiple_of(s0 + half * CH, CH)
            _flash_step(qs, ks_ref[pl.ds(h0, CH), :], bias[:, half * CH:(half + 1) * CH], vs_ref[pl.ds(h0, CH), :],
                        m_scr, acc_scr, H)
        return carry

    lax.fori_loop(0, (q0 + T + C - 1) // C, sel_body, 0)
    o_s = _flash_result(acc_scr)

    nw = NSA_WINDOW + T
    s_w = _dot_nt(qs, kw_ref[pl.ds(q0, nw), :])
    wpos = q0 - NSA_WINDOW + lax.broadcasted_iota(jnp.int32, (H * T, nw), 1)
    dist = tpos4 - wpos
    p_w = _masked_softmax(s_w, (dist >= 0) & (dist < NSA_WINDOW) & (wpos >= 0))
    o_w = _dot(p_w.astype(BF16), vw_ref[pl.ds(q0, nw), :])

    gate = jax.nn.sigmoid(g_ref[...])
    for r in range(H):
        rows = slice(r * T, (r + 1) * T)
        o = (gate[:, 3 * r:3 * r + 1] * o_c[rows] + gate[:, 3 * r + 1:3 * r + 2] * o_s[rows]
             + gate[:, 3 * r + 2:3 * r + 3] * o_w[rows])
        o_ref[:, r * HEAD_DIM:(r + 1) * HEAD_DIM] = o.astype(o_ref.dtype)


def nsa_attention(q, g, k_cmp, v_cmp, ks, vs, kw, vw):
    S = q.shape[0]
    T, C = NSA_T, NSA_C
    n_cmp = k_cmp.shape[0]
    ncp = -(-n_cmp // 128) * 128
    ns = S // NSA_SLC_LEN
    k_cmp = jnp.pad(k_cmp, ((0, ncp - n_cmp), (0, 0)))
    v_cmp = jnp.pad(v_cmp, ((0, ncp - n_cmp), (0, 0)))
    cmp_start = np.arange(ncp) * NSA_CMP_STRIDE
    slc_start = np.arange(ns) * NSA_SLC_LEN
    overlap = ((cmp_start[:, None] < slc_start[None, :] + NSA_SLC_LEN)
               & (cmp_start[:, None] + NSA_CMP_LEN - 1 >= slc_start[None, :]))
    overlap = jnp.asarray(overlap, BF16)
    cpg = NSA_GROUP * NSA_SLC_LEN // C
    tok_blk = (np.arange(cpg)[:, None] * C + np.arange(C)[None, :]) // NSA_SLC_LEN
    expand = jnp.asarray(np.arange(NSA_GROUP)[None, :, None] == tok_blk[:, None, :], BF16)
    kw = jnp.pad(kw, ((NSA_WINDOW, 0), (0, 0)))
    vw = jnp.pad(vw, ((NSA_WINDOW, 0), (0, 0)))
    return pl.pallas_call(
        _nsa_kernel,
        grid=(S // T,),
        in_specs=[pl.BlockSpec((T, NSA_Q_W), lambda i: (i, 0)),
                  pl.BlockSpec((T, NSA_HEADS * 3), lambda i: (i, 0)),
                  _resident((ncp, HEAD_DIM)), _resident((ncp, HEAD_DIM)), _resident((ncp, ns)),
                  _resident((S, HEAD_DIM)), _resident((S, 2 * HEAD_DIM)),
                  _resident((S + NSA_WINDOW, HEAD_DIM)), _resident((S + NSA_WINDOW, HEAD_DIM)),
                  _resident((cpg, NSA_GROUP, C))],
        out_specs=pl.BlockSpec((T, NSA_Q_W), lambda i: (i, 0)),
        out_shape=jax.ShapeDtypeStruct((S, NSA_Q_W), BF16),
        scratch_shapes=[pltpu.VMEM((ns // NSA_GROUP, T, NSA_GROUP), BF16),
                        pltpu.VMEM((NSA_HEADS * T, HEAD_DIM), F32),
                        pltpu.VMEM((NSA_HEADS * T, 2 * HEAD_DIM), F32)],
        compiler_params=_params(1),
        name="nsa_attention",
    )(q, g, k_cmp, v_cmp, overlap, ks, vs, kw, vw, expand)


def nsa_compress(x, pe, w1, w2):
    S = x.shape[0]
    half = NSA_CMP_STRIDE * HEAD_DIM
    x16 = x.reshape(S // NSA_CMP_STRIDE, half)
    pe_flat = pe.reshape(NSA_CMP_LEN * HEAD_DIM)
    w1_flat = w1.reshape(NSA_CMP_LEN * HEAD_DIM, HEAD_DIM)
    first = _mm(x16 + pe_flat[None, :half], w1_flat[:half])
    second = _mm(x16 + pe_flat[None, half:], w1_flat[half:])
    hid = jax.nn.gelu(first[:-1] + second[1:])
    hid = jnp.pad(hid, ((0, 1), (0, 0)))
    return _mm(hid, w2)[:-1]


DIL_T = 256


def _dil_kernel(q_ref, k_ref, v_ref, o_ref):
    i = pl.program_id(0)
    T = DIL_T
    q0 = pl.multiple_of(i * T, T)
    tl = lax.broadcasted_iota(jnp.int32, (T, 1), 0)
    for s in range(DIL_HEADS_PER_GROUP):
        cols = slice(s * HEAD_DIM, (s + 1) * HEAD_DIM)
        parts = []
        m = jnp.full((T, 1), MASKED, F32)
        for g, (w, r) in enumerate(DIL_PATTERNS):
            n = w + T
            start = pl.multiple_of(q0 + (DIL_MAX_WINDOW - w), 128)
            head = g * DIL_HEADS_PER_GROUP + s
            sc = _dot_nt(q_ref[:, head * HEAD_DIM:(head + 1) * HEAD_DIM], k_ref[pl.ds(start, n), cols]) * ATT_SCALE
            b = lax.broadcasted_iota(jnp.int32, (T, n), 1)
            d = tl + w - b
            mask = (d >= 0) & (d <= w) & ((d & (r - 1)) == 0) & (q0 - w + b >= 0)
            sm = jnp.where(mask, sc, MASKED)
            m = jnp.maximum(m, jnp.max(sm, axis=1, keepdims=True))
            parts.append((sm, mask, start, n))
        num = jnp.zeros((T, HEAD_DIM), F32)
        den = jnp.zeros((T, 1), F32)
        for sm, mask, start, n in parts:
            e = jnp.where(mask, jnp.exp(sm - m), 0.0)
            den = den + jnp.sum(e, axis=1, keepdims=True)
            num = num + _dot(e.astype(BF16), v_ref[pl.ds(start, n), cols])
        o_ref[:, cols] = (num / den).astype(o_ref.dtype)


def dilated_attention(q, k, v):
    S = q.shape[0]
    T = DIL_T
    k = jnp.pad(k, ((DIL_MAX_WINDOW, 0), (0, 0)))
    v = jnp.pad(v, ((DIL_MAX_WINDOW, 0), (0, 0)))
    return pl.pallas_call(
        _dil_kernel,
        grid=(S // T,),
        in_specs=[pl.BlockSpec((T, DIL_Q_W), lambda i: (i, 0)),
                  _resident((S + DIL_MAX_WINDOW, DIL_KV_W)), _resident((S + DIL_MAX_WINDOW, DIL_KV_W))],
        out_specs=pl.BlockSpec((T, DIL_KV_W), lambda i: (i, 0)),
        out_shape=jax.ShapeDtypeStruct((S, DIL_KV_W), BF16),
        compiler_params=_params(1),
        name="dilated_attention",
    )(q, k, v)


NORM_TM = 512
NORM_TN = 512


def _norm_mm_kernel(x_ref, g_ref, b_ref, o_ref, xn_ref):
    @pl.when(pl.program_id(1) == 0)
    def _():
        x = x_ref[...]
        y = x * lax.rsqrt(jnp.mean(x * x, axis=-1, keepdims=True) + NORM_EPS)
        xn_ref[...] = (y * g_ref[...]).astype(xn_ref.dtype)
    o_ref[...] = _dot(xn_ref[...], b_ref[...]).astype(o_ref.dtype)


def norm_mm(x, gain, w, out_dtype, return_xn=False):
    M, K = x.shape
    N = w.shape[1]
    tm, tn = min(NORM_TM, M), min(NORM_TN, N)
    main_spec = pl.BlockSpec((tm, tn), lambda i, j: (i, j))
    xn_spec = pl.BlockSpec((tm, K), lambda i, j: (i, 0))
    main_shape = jax.ShapeDtypeStruct((M, N), out_dtype)
    return pl.pallas_call(
        _norm_mm_kernel,
        grid=(M // tm, N // tn),
        in_specs=[xn_spec, pl.BlockSpec((1, K), lambda i, j: (0, 0)), pl.BlockSpec((K, tn), lambda i, j: (0, j))],
        out_specs=(main_spec, xn_spec) if return_xn else main_spec,
        out_shape=(main_shape, jax.ShapeDtypeStruct((M, K), BF16)) if return_xn else main_shape,
        scratch_shapes=[] if return_xn else [pltpu.VMEM((tm, K), BF16)],
        compiler_params=_params(2),
        name="norm_mm",
    )(x, gain.reshape(1, K).astype(F32), w)


PREP_T = 256
PREP_SEGMENTS = (
    ("z", GATE_RANK, None, False, 1.0, "bf16"),
    ("a_q", SB_W, None, False, LOG2_SCALE, "bf16"),
    ("a_k", SB_W, None, False, 1.0, "bf16"),
    ("a_v", SB_W, None, False, 1.0, "bf16"),
    ("b_q", DSA_Q_W, 0, True, LOG2_SCALE, "bf16"),
    ("b_k", DSA_KV_W, 1, True, 1.0, "bf16"),
    ("b_v", DSA_KV_W, None, False, 1.0, "ones"),
    ("b_iq", DSA_IQ_W, None, False, 1.0, "bf16"),
    ("b_ik", HEAD_DIM, None, False, 1.0, "ik"),
    ("c_q", NSA_Q_W, 2, True, LOG2_SCALE, "bf16"),
    ("c_kc", NSA_KV_W, None, True, 1.0, "f32"),
    ("c_vc", NSA_KV_W, None, False, 1.0, "f32"),
    ("c_ks", NSA_KV_W, 4, True, 1.0, "bf16"),
    ("c_vs", NSA_KV_W, None, False, 1.0, "ones"),
    ("c_kw", NSA_KV_W, 5, True, 1.0, "bf16"),
    ("c_vw", NSA_KV_W, None, False, 1.0, "bf16"),
    ("d_q", DIL_Q_W, 6, True, 1.0, "bf16"),
    ("d_k", DIL_KV_W, 7, True, 1.0, "bf16"),
    ("d_v", DIL_KV_W, None, False, 1.0, "bf16"),
    ("misc", HEAD_DIM, None, False, 1.0, "misc"),
)
PREP_WIDTH = sum(seg[1] for seg in PREP_SEGMENTS)
N_MISC = DSA_IDX_HEADS + NSA_HEADS * 3


def _prep_outputs(S):
    shapes = []
    for name, width, _, _, _, kind in PREP_SEGMENTS:
        if kind == "bf16":
            shapes.append((width, BF16))
        elif kind == "ones":
            shapes.append((width + HEAD_DIM, BF16))
        elif kind == "f32":
            shapes.append((width, F32))
        elif kind == "ik":
            shapes.append((DSA_IDX_DIM, BF16))
        else:
            shapes.append((N_MISC, F32))
    return shapes


def _prep_kernel(p_ref, g_ref, cs_ref, sn_ref, *o_refs):
    cs = cs_ref[...]
    sn = sn_ref[...]
    off = 0
    for (name, width, gi, rope, scale, kind), o_ref in zip(PREP_SEGMENTS, o_refs):
        if kind == "ik":
            o_ref[...] = p_ref[:, off:off + DSA_IDX_DIM].astype(o_ref.dtype)
        elif kind == "misc":
            o_ref[...] = p_ref[:, off:off + N_MISC]
        else:
            for hd in range(width // HEAD_DIM):
                lanes = slice(hd * HEAD_DIM, (hd + 1) * HEAD_DIM)
                x = p_ref[:, off + hd * HEAD_DIM:off + (hd + 1) * HEAD_DIM]
                if gi is not None:
                    x = x * lax.rsqrt(jnp.mean(x * x, axis=-1, keepdims=True) + NORM_EPS) * g_ref[gi:gi + 1, :]
                if rope:
                    x = x * cs + pltpu.roll(x, HEAD_DIM // 2, 1) * sn
                if scale != 1.0:
                    x = x * scale
                o_ref[:, lanes] = x.astype(o_ref.dtype)
            if kind == "ones":
                o_ref[:, width:] = jnp.ones((o_ref.shape[0], HEAD_DIM), o_ref.dtype)
        off += width


def prepare_operands(proj, qk_gain, cs, sn):
    S = proj.shape[0]
    T = min(PREP_T, S)
    row = lambda i: (i, 0)
    shapes = _prep_outputs(S)
    outs = pl.pallas_call(
        _prep_kernel,
        grid=(S // T,),
        in_specs=[pl.BlockSpec((T, PREP_WIDTH), row),
                  pl.BlockSpec((N_QK_GAINS, HEAD_DIM), lambda i: (0, 0)),
                  pl.BlockSpec((T, HEAD_DIM), row), pl.BlockSpec((T, HEAD_DIM), row)],
        out_specs=tuple(pl.BlockSpec((T, w), row) for w, _ in shapes),
        out_shape=tuple(jax.ShapeDtypeStruct((S, w), dt) for w, dt in shapes),
        compiler_params=_params(1),
        name="prepare_operands",
    )(proj, qk_gain, cs, sn)
    return {seg[0]: o for seg, o in zip(PREP_SEGMENTS, outs)}


def fused_in_weights(w_in, gate_down):
    names = ("a_q", "a_k", "a_v", "b_q", "b_k", "b_v", "b_iq", "b_ik", "b_iw",
             "c_q", "c_kc", "c_vc", "c_ks", "c_vs", "c_kw", "c_vw", "c_g", "d_q", "d_k", "d_v")
    parts = dict(zip(names, jnp.split(w_in, SPLIT_OFFSETS, axis=1)))
    parts["z"] = gate_down
    parts["misc"] = jnp.concatenate([parts["b_iw"], parts["c_g"]], axis=1)
    cols = []
    for name, width, *_ in PREP_SEGMENTS:
        p = parts[name]
        cols.append(jnp.pad(p, ((0, 0), (0, width - p.shape[1]))))
    return jnp.concatenate(cols, axis=1).astype(BF16)


def token_mixing(h, cs, sn, norm_gain, w_in, qk_gain, nsa_pe, nsa_w1, nsa_w2, gate_down, gate_up, w_branch, w_out):
    proj = norm_mm(h, norm_gain, fused_in_weights(w_in, gate_down), F32)
    t = prepare_operands(proj, qk_gain, cs, sn)
    o_a = sb_attention(t["a_q"], t["a_k"], t["a_v"])
    o_b = dsa_attention(t["b_q"], t["b_k"], t["b_v"], t["b_iq"], t["b_ik"], t["misc"][:, :DSA_IDX_HEADS])
    k_cmp = rmsnorm(nsa_compress(t["c_kc"], nsa_pe[0], nsa_w1[0], nsa_w2[0]), qk_gain[3])
    v_cmp = nsa_compress(t["c_vc"], nsa_pe[1], nsa_w1[1], nsa_w2[1])
    o_c = nsa_attention(t["c_q"], t["misc"][:, DSA_IDX_HEADS:], k_cmp.astype(BF16), v_cmp.astype(BF16),
                        t["c_ks"], t["c_vs"], t["c_kw"], t["c_vw"])
    o_d = dilated_attention(t["d_q"], t["d_k"], t["d_v"])
    merged = gated_merge(t["z"], (o_a, o_b, o_c, o_d), gate_up, w_branch)
    return _mm(merged, w_out, residual=h)


MERGE_TM = 512
MERGE_TN = 512


def _merge_kernel(z_ref, oa_ref, ob_ref, oc_ref, od_ref, gu_ref, wb_ref, o_ref):
    z = z_ref[...]
    acc = None
    off = 0
    for i, br_ref in enumerate((oa_ref, ob_ref, oc_ref, od_ref)):
        width = BRANCH_WIDTHS[i]
        gate = jax.nn.sigmoid(_dot(z, gu_ref[i]))
        term = gate * _dot(br_ref[...], wb_ref[off:off + width, :])
        acc = term if acc is None else acc + term
        off += width
    o_ref[...] = acc.astype(o_ref.dtype)


def gated_merge(z, branches, gate_up, w_branch):
    S = z.shape[0]
    D = w_branch.shape[1]
    tm, tn = min(MERGE_TM, S), min(MERGE_TN, D)
    row = lambda i, j: (i, 0)
    return pl.pallas_call(
        _merge_kernel,
        grid=(S // tm, D // tn),
        in_specs=[pl.BlockSpec((tm, GATE_RANK), row)]
                 + [pl.BlockSpec((tm, w), row) for w in BRANCH_WIDTHS]
                 + [pl.BlockSpec((N_BRANCHES, GATE_RANK, tn), lambda i, j: (0, 0, j)),
                    pl.BlockSpec((BRANCH_WIDTH, tn), lambda i, j: (0, j))],
        out_specs=pl.BlockSpec((tm, tn), lambda i, j: (i, j)),
        out_shape=jax.ShapeDtypeStruct((S, D), BF16),
        compiler_params=_params(2),
        name="gated_merge",
    )(z, *branches, gate_up.astype(BF16), w_branch.astype(BF16))


PEER_ST = 256
PEER_TM = 512
PEER_EC = 512
PEER_HALF = PEER_KEY_DIM // 2
PEER_STAT_ROWS = 32
assert (PEER_TOPK + 1) // 9 == 1 and PEER_TOPK + 1 <= 24


def _top_rows(scores, n, n_rows):
    st = scores.shape[1]
    rid = lax.broadcasted_iota(jnp.int32, (n_rows, st), 0)
    out = jnp.full((n_rows, st), EXTRACTED, F32)
    for r in range(n):
        m = jnp.max(scores, axis=0, keepdims=True)
        out = jnp.where(rid == r, m, out)
        scores = jnp.where(scores == m, EXTRACTED, scores)
    return out


def _peer_stats_kernel(q_ref, w1_ref, w2_ref, o_ref):
    st = q_ref.shape[0]
    k = PEER_TOPK
    rid = lax.broadcasted_iota(jnp.int32, (PEER_STAT_ROWS, st), 0)
    rid8 = lax.broadcasted_iota(jnp.int32, (8, st), 0)
    stats = jnp.zeros((PEER_STAT_ROWS, st), F32)
    for h in range(PEER_HEADS):
        qh = q_ref[:, h * PEER_KEY_DIM:(h + 1) * PEER_KEY_DIM]
        v1 = _top_rows(_dot_nt(w1_ref[h], qh), k + 1, 24)
        v2 = _top_rows(_dot_nt(w2_ref[h], qh), k + 1, 24)
        pieces = [v1[0:1] + v2]
        for a in range(1, 8):
            pieces.append(jnp.where(rid8 < (k + 1) // (a + 1), v1[a:a + 1] + v2[0:8], EXTRACTED))
        pieces.append(v1[8:24] + v2[0:1])
        tops = _top_rows(jnp.concatenate(pieces, axis=0), k + 1, 24)
        c1 = tops[0:1]
        den = jnp.sum(jnp.exp(tops[0:k] - c1), axis=0, keepdims=True)
        thr = 0.5 * (tops[k - 1:k] + tops[k:k + 1])
        stats = jnp.where(rid == h, thr, stats)
        stats = jnp.where(rid == PEER_HEADS + h, c1, stats)
        stats = jnp.where(rid == 2 * PEER_HEADS + h, 1.0 / den, stats)
    o_ref[...] = stats


def _peer_weight_kernel(xn_ref, ut_ref, q_ref, kf_ref, st_ref, o_ref):
    act = jax.nn.gelu(_dot(xn_ref[...], ut_ref[...]))
    st = st_ref[...]
    w = jnp.zeros(act.shape, F32)
    for h in range(PEER_HEADS):
        sf = _dot(q_ref[:, h * PEER_KEY_DIM:(h + 1) * PEER_KEY_DIM], kf_ref[h])
        gate = jnp.exp(sf - st[:, PEER_HEADS + h:PEER_HEADS + h + 1]) * st[:, 2 * PEER_HEADS + h:2 * PEER_HEADS + h + 1]
        w = w + jnp.where(sf >= st[:, h:h + 1], gate, 0.0)
    o_ref[...] = (w * act).astype(o_ref.dtype)


def peer_ffn(h, norm_gain, wq, subkeys, u, v):
    S, D = h.shape
    nk, half = PEER_N_KEYS, PEER_HALF
    q, xn = norm_mm(h, norm_gain, wq.astype(BF16), BF16, return_xn=True)
    zeros = jnp.zeros((PEER_HEADS, nk, half), F32)
    w1 = jnp.concatenate([subkeys[:, 0], zeros], axis=-1).astype(BF16)
    w2 = jnp.concatenate([zeros, subkeys[:, 1]], axis=-1).astype(BF16)
    st_t = pl.pallas_call(
        _peer_stats_kernel,
        grid=(S // PEER_ST,),
        in_specs=[pl.BlockSpec((PEER_ST, PEER_HEADS * PEER_KEY_DIM), lambda i: (i, 0)),
                  pl.BlockSpec((PEER_HEADS, nk, PEER_KEY_DIM), lambda i: (0, 0, 0)),
                  pl.BlockSpec((PEER_HEADS, nk, PEER_KEY_DIM), lambda i: (0, 0, 0))],
        out_specs=pl.BlockSpec((PEER_STAT_ROWS, PEER_ST), lambda i: (0, i)),
        out_shape=jax.ShapeDtypeStruct((PEER_STAT_ROWS, S), F32),
        compiler_params=_params(1),
        name="peer_stats",
    )(q, w1, w2)
    stats = st_t.T
    top = jnp.repeat(jnp.swapaxes(subkeys[:, 0], 1, 2), nk, axis=2)
    bot = jnp.tile(jnp.swapaxes(subkeys[:, 1], 1, 2), (1, 1, nk))
    kfull = jnp.concatenate([top, bot], axis=1).astype(BF16)
    tm, ec = min(PEER_TM, S), PEER_EC
    wmat = pl.pallas_call(
        _peer_weight_kernel,
        grid=(S // tm, PEER_EXPERTS // ec),
        in_specs=[pl.BlockSpec((tm, D), lambda i, j: (i, 0)),
                  pl.BlockSpec((D, ec), lambda i, j: (0, j)),
                  pl.BlockSpec((tm, PEER_HEADS * PEER_KEY_DIM), lambda i, j: (i, 0)),
                  pl.BlockSpec((PEER_HEADS, PEER_KEY_DIM, ec), lambda i, j: (0, 0, j)),
                  pl.BlockSpec((tm, PEER_STAT_ROWS), lambda i, j: (i, 0))],
        out_specs=pl.BlockSpec((tm, ec), lambda i, j: (i, j)),
        out_shape=jax.ShapeDtypeStruct((S, PEER_EXPERTS), BF16),
        compiler_params=_params(2),
        name="peer_weights",
    )(xn, u.T.astype(BF16), q, kfull, stats)
    return _mm(wmat, v, residual=h)


def kernel(x, norm_mix, w_in, qk_gain, nsa_pe, nsa_w1, nsa_w2, gate_down, gate_up, w_branch, w_out,
           norm_ffn, peer_wq, peer_subkeys, peer_u, peer_v):
    S = x.shape[1]
    cos, sin = rope_tables(S, x.dtype)
    cos, sin = cos.reshape(S, HEAD_DIM // 2), sin.reshape(S, HEAD_DIM // 2)
    cs = jnp.concatenate([cos, cos], axis=1)
    sn = jnp.concatenate([-sin, sin], axis=1)
    h = x[0]
    for l in range(DEPTH):
        h = token_mixing(h, cs, sn, norm_mix[l], w_in[l], qk_gain[l], nsa_pe[l], nsa_w1[l], nsa_w2[l],
                         gate_down[l], gate_up[l], w_branch[l], w_out[l])
        h = peer_ffn(h, norm_ffn[l], peer_wq[l], peer_subkeys[l], peer_u[l], peer_v[l])
    return h[None]
```

```python
import math, functools
import jax, jax.numpy as jnp
from jax import lax
import numpy as np
from jax.experimental import pallas as pl
from jax.experimental.pallas import tpu as pltpu

D_MODEL = 4096
BATCH = 1
SEQ = 16384
DEPTH = 4

HEAD_DIM = 128
ROPE_THETA = 10000.0
NORM_EPS = 1e-6
SB_HEADS = 4
DSA_HEADS = 4
DSA_KV_HEADS = 1
DSA_IDX_HEADS = 4
DSA_IDX_DIM = 64
DSA_TOPK = 256
NSA_HEADS = 4
NSA_KV_HEADS = 1
NSA_CMP_LEN = 32
NSA_CMP_STRIDE = 16
NSA_SLC_LEN = 32
NSA_N_SEL = 8
NSA_WINDOW = 512
NSA_FORCED_SCORE = 1e4
DIL_PATTERNS = ((128, 1), (512, 4), (2048, 16))
DIL_GROUPS = len(DIL_PATTERNS)
DIL_HEADS_PER_GROUP = 2
DIL_MAX_WINDOW = max(w for w, _ in DIL_PATTERNS)
N_BRANCHES = 4
GATE_RANK = 256
PEER_HEADS = 8
PEER_N_KEYS = 64
PEER_EXPERTS = PEER_N_KEYS ** 2
PEER_KEY_DIM = 128
PEER_TOPK = 16

SB_W = SB_HEADS * HEAD_DIM
DSA_Q_W = DSA_HEADS * HEAD_DIM
DSA_KV_W = DSA_KV_HEADS * HEAD_DIM
DSA_IQ_W = DSA_IDX_HEADS * DSA_IDX_DIM
NSA_Q_W = NSA_HEADS * HEAD_DIM
NSA_KV_W = NSA_KV_HEADS * HEAD_DIM
DIL_Q_W = DIL_GROUPS * DIL_HEADS_PER_GROUP * HEAD_DIM
DIL_KV_W = DIL_HEADS_PER_GROUP * HEAD_DIM
IN_SPLITS = (SB_W, SB_W, SB_W,
             DSA_Q_W, DSA_KV_W, DSA_KV_W, DSA_IQ_W, DSA_IDX_DIM, DSA_IDX_HEADS,
             NSA_Q_W, NSA_KV_W, NSA_KV_W, NSA_KV_W, NSA_KV_W, NSA_KV_W, NSA_KV_W, NSA_HEADS * 3,
             DIL_Q_W, DIL_KV_W, DIL_KV_W)
IN_WIDTH = sum(IN_SPLITS)
SPLIT_OFFSETS = tuple(int(o) for o in np.cumsum(IN_SPLITS)[:-1])
BRANCH_WIDTHS = (SB_W, DSA_Q_W, NSA_Q_W, DIL_KV_W)
BRANCH_WIDTH = sum(BRANCH_WIDTHS)
BRANCH_OFFSETS = tuple(int(o) for o in np.cumsum(BRANCH_WIDTHS)[:-1])
N_QK_GAINS = 8

ATT_SCALE = HEAD_DIM ** -0.5
LOG2_SCALE = ATT_SCALE * math.log2(math.e)
MASKED = -1e30
EXTRACTED = -3e38
INT32_MIN = -2 ** 31
F32_TINY = float(np.finfo(np.float32).tiny)
VMEM_LIMIT = 56 * 1024 * 1024

BF16 = jnp.bfloat16
F32 = jnp.float32


def _params(n_grid):
    return pltpu.CompilerParams(dimension_semantics=("arbitrary",) * n_grid, vmem_limit_bytes=VMEM_LIMIT)


def _resident(shape):
    nd = len(shape)
    return pl.BlockSpec(shape, lambda *_: (0,) * nd, pipeline_mode=pl.Buffered(1))


def _dot_nt(a, b):
    return lax.dot_general(a, b, (((1,), (1,)), ((), ())), preferred_element_type=F32)


def _dot(a, b):
    return jnp.dot(a, b, preferred_element_type=F32)


def _mm_kernel(a_ref, b_ref, o_ref):
    o_ref[...] = _dot(a_ref[...], b_ref[...]).astype(o_ref.dtype)


def _mm_res_kernel(a_ref, b_ref, r_ref, o_ref):
    o_ref[...] = (r_ref[...] + _dot(a_ref[...], b_ref[...])).astype(o_ref.dtype)


def _pick(n, prefs):
    for p in prefs:
        if n % p == 0:
            return p
    return n


def _mm(a, b, out_dtype=F32, residual=None):
    lead = a.shape[:-1]
    K = a.shape[-1]
    N = b.shape[-1]
    a2 = a.reshape(-1, K).astype(jnp.bfloat16)
    M = a2.shape[0]
    n_pad = (-N) % 128
    b2 = b.astype(jnp.bfloat16)
    if n_pad:
        b2 = jnp.pad(b2, ((0, 0), (0, n_pad)))
    Np = N + n_pad
    tm = _pick(M, (1024, 512, 256, 128))
    tn = _pick(Np, (512, 256, 128))
    in_specs = [pl.BlockSpec((tm, K), lambda i, j: (i, 0)),
                pl.BlockSpec((K, tn), lambda i, j: (0, j))]
    args = [a2, b2]
    body = _mm_kernel
    if residual is not None:
        assert n_pad == 0
        in_specs.append(pl.BlockSpec((tm, tn), lambda i, j: (i, j)))
        args.append(residual.reshape(M, N))
        body = _mm_res_kernel
    out = pl.pallas_call(
        body,
        grid=(M // tm, Np // tn),
        in_specs=in_specs,
        out_specs=pl.BlockSpec((tm, tn), lambda i, j: (i, j)),
        out_shape=jax.ShapeDtypeStruct((M, Np), out_dtype),
        compiler_params=pltpu.CompilerParams(
            dimension_semantics=("arbitrary", "arbitrary"),
            vmem_limit_bytes=VMEM_LIMIT),
        name="dense_mm",
    )(*args)
    if n_pad:
        out = out[:, :N]
    return out.reshape(*lead, N)


def rmsnorm(x, g):
    xf = x.astype(jnp.float32)
    y = xf * lax.rsqrt(jnp.mean(xf * xf, axis=-1, keepdims=True) + NORM_EPS)
    return (y * g.astype(jnp.float32)).astype(x.dtype)


def rope_tables(seq, dtype):
    inv = ROPE_THETA ** (-jnp.arange(0, HEAD_DIM, 2, dtype=jnp.float32) / HEAD_DIM)
    ang = jnp.arange(seq, dtype=jnp.float32)[:, None] * inv[None, :]
    return jnp.cos(ang)[None, :, None, :].astype(dtype), jnp.sin(ang)[None, :, None, :].astype(dtype)


def _stack_heads(q_ref, n_heads):
    return jnp.concatenate([q_ref[:, r * HEAD_DIM:(r + 1) * HEAD_DIM] for r in range(n_heads)], axis=0)


def _tile_rows(x, n):
    return jnp.concatenate([x] * n, axis=0)


def _flash_init(m_scr, acc_scr):
    m_scr[...] = jnp.full(m_scr.shape, MASKED, F32)
    acc_scr[...] = jnp.zeros(acc_scr.shape, F32)


def _flash_step(qs, k_chunk, bias, v_aug, m_scr, acc_scr, n_heads):
    T, C = bias.shape
    s = _dot_nt(qs, k_chunk)
    ps, alphas = [], []
    for r in range(n_heads):
        rows = slice(r * T, (r + 1) * T)
        sm = s[rows] + bias
        m_old = m_scr[rows]
        m_new = jnp.maximum(m_old, jnp.max(sm, axis=1, keepdims=True))
        m_scr[rows] = m_new
        p = [jnp.exp2(sm[:, u * 128:(u + 1) * 128] - m_new) for u in range(C // 128)]
        ps.append(jnp.concatenate(p, axis=1).astype(BF16))
        alpha = jnp.exp2(m_old - m_new)
        alphas.append(jnp.concatenate([alpha, alpha], axis=1))
    pv = _dot(jnp.concatenate(ps, axis=0), v_aug)
    acc_scr[...] = jnp.concatenate(alphas, axis=0) * acc_scr[...] + pv


def _flash_result(acc_scr):
    acc = acc_scr[...]
    return acc[:, :HEAD_DIM] / acc[:, HEAD_DIM:]


def _masked_softmax(s, mask):
    sm = jnp.where(mask, s, MASKED)
    m = jnp.max(sm, axis=1, keepdims=True)
    e = jnp.where(mask, jnp.exp2(sm - m), 0.0)
    den = jnp.sum(e, axis=1, keepdims=True)
    return e * (1.0 / jnp.maximum(den, F32_TINY))


SB_T = 256
SB_DEAD_LOG2 = -200.0


def _sb_kernel(q_ref, k_ref, v_ref, tri_ref, o_ref, later_scr, acc_scr):
    i = pl.program_id(0)
    T = SB_T
    q0 = pl.multiple_of(i * T, T)
    tri = tri_ref[...]
    later_scr[...] = jnp.zeros(later_scr.shape, F32)
    acc_scr[...] = jnp.zeros(acc_scr.shape, F32)

    def chunk(s0, causal):
        heads = [slice(h * HEAD_DIM, (h + 1) * HEAD_DIM) for h in range(SB_HEADS)]
        zs = [_dot_nt(q_ref[:, cols], k_ref[pl.ds(s0, T), cols]) for cols in heads]
        lms = []
        for z in zs:
            sp = jnp.log2(1.0 + jnp.exp2(-jnp.abs(z)))
            lm = -(jnp.maximum(z, 0.0) + sp)
            if causal is not None:
                lm = jnp.where(causal, lm, 0.0)
            lms.append(lm)
        withins = [_dot(lm.astype(BF16), tri) for lm in lms]
        ws = []
        for h in range(SB_HEADS):
            e = zs[h] + lms[h] + withins[h]
            later = later_scr[h]
            w = jnp.concatenate([jnp.exp2(e[:, u * 128:(u + 1) * 128] + later) for u in range(T // 128)], axis=1)
            if causal is not None:
                w = jnp.where(causal, w, 0.0)
            ws.append(w.astype(BF16))
            later_scr[h] = later + jnp.sum(lms[h], axis=1, keepdims=True)
        for h, cols in enumerate(heads):
            acc_scr[h] += _dot(ws[h], v_ref[pl.ds(s0, T), cols])

    row = lax.broadcasted_iota(jnp.int32, (T, T), 0)
    col = lax.broadcasted_iota(jnp.int32, (T, T), 1)
    chunk(q0, col < row)

    def cond(state):
        n, worst = state
        return (n <= i) & (worst > SB_DEAD_LOG2)

    def body(state):
        n, _ = state
        chunk(pl.multiple_of((i - n) * T, T), None)
        return n + 1, jnp.max(later_scr[...])

    lax.while_loop(cond, body, (jnp.int32(1), jnp.max(later_scr[...])))
    for h in range(SB_HEADS):
        o_ref[:, h * HEAD_DIM:(h + 1) * HEAD_DIM] = acc_scr[h].astype(o_ref.dtype)


def sb_attention(q, k, v):
    S = q.shape[0]
    T = SB_T
    tri = jnp.tril(jnp.ones((T, T), F32), -1).astype(BF16)
    return pl.pallas_call(
        _sb_kernel,
        grid=(S // T,),
        in_specs=[pl.BlockSpec((T, SB_W), lambda i: (i, 0)),
                  _resident((S, SB_W)), _resident((S, SB_W)), _resident((T, T))],
        out_specs=pl.BlockSpec((T, SB_W), lambda i: (i, 0)),
        out_shape=jax.ShapeDtypeStruct((S, SB_W), BF16),
        scratch_shapes=[pltpu.VMEM((SB_HEADS, T, 128), F32),
                        pltpu.VMEM((SB_HEADS, T, HEAD_DIM), F32)],
        compiler_params=_params(1),
        name="sb_attention",
    )(q, k, v, tri)


DSA_T = 256
DSA_C = 512
DSA_CH = 256
DSA_BITS_PER_CHECK = 4


def _dsa_kernel(n_top, q_ref, iq_ref, iw_ref, k_ref, v_ref, ik_ref, tri_ref, o_ref,
                key_scr, m_scr, acc_scr):
    i = pl.program_id(0)
    T, C, CH = DSA_T, DSA_C, DSA_CH
    H = DSA_HEADS
    q0 = i * T
    n_chunks = (q0 + T + C - 1) // C
    tpos = q0 + lax.broadcasted_iota(jnp.int32, (T, 1), 0)
    iw = iw_ref[...]
    iq4 = jnp.concatenate([iq_ref[:, h * DSA_IDX_DIM:(h + 1) * DSA_IDX_DIM] for h in range(DSA_IDX_HEADS)], axis=0)

    def index_keys(c, causal):
        s0 = pl.multiple_of(c * C, C)
        d = _dot_nt(iq4, ik_ref[pl.ds(s0, C), :])
        acc = jnp.zeros((T, C), F32)
        for h in range(DSA_IDX_HEADS):
            acc = acc + jnp.maximum(d[h * T:(h + 1) * T], 0.0) * iw[:, h:h + 1]
        bits = lax.bitcast_convert_type(acc, jnp.int32)
        key = jnp.where(bits < 0, bits ^ jnp.int32(0x7FFFFFFF), bits)
        key = jnp.where(acc == 0.0, 0, key)
        if causal:
            spos = s0 + lax.broadcasted_iota(jnp.int32, (T, C), 1)
            key = jnp.where(spos <= tpos, key, INT32_MIN)
        key_scr[c] = key

    def idx_body(c, carry):
        index_keys(c, False)
        return carry

    lax.fori_loop(0, n_chunks - 1, idx_body, 0)
    index_keys(n_chunks - 1, True)

    def count_ge(cand):
        counts = []
        for r0 in range(0, T, 128):
            cand_r = cand[r0:r0 + 128]

            def body(c, acc):
                kk = key_scr[c, r0:r0 + 128, :]
                for u in range(C // 128):
                    acc = acc + jnp.where(kk[:, u * 128:(u + 1) * 128] >= cand_r, 1.0, 0.0)
                return acc

            def pair_body(p, acc):
                return body(2 * p + 1, body(2 * p, acc))
            acc = lax.fori_loop(0, n_chunks // 2, pair_body, jnp.zeros((128, 128), F32))
            acc = lax.fori_loop(2 * (n_chunks // 2), n_chunks, body, acc)
            counts.append(jnp.sum(acc, axis=1, keepdims=True))
        return jnp.concatenate(counts, axis=0)

    ge0 = count_ge(jnp.zeros((T, 128), jnp.int32))
    gt0 = count_ge(jnp.ones((T, 128), jnp.int32))
    base0 = jnp.where(ge0 >= n_top, 0, jnp.full((T, 128), INT32_MIN, jnp.int32))
    done0 = ((gt0 < n_top) & (ge0 >= n_top)) | (ge0 == n_top) | (tpos + 1 < n_top)
    done0 = jnp.where(jnp.broadcast_to(done0, (T, 128)), 1.0, 0.0)

    def try_bit(bit, base, done):
        cand = base + (jnp.int32(1) << bit)
        cnt = count_ge(cand)
        return jnp.where((cnt >= n_top) & (done < 0.5), cand, base), jnp.where(cnt == n_top, 1.0, done)

    def high_body(step, state):
        return try_bit(jnp.int32(30) - step, *state)

    base, done = lax.fori_loop(0, 15, high_body, (base0, done0))

    def low_cond(state):
        b, _, _, n_open = state
        return (b >= 0) & (n_open > 0.0)

    def low_body(state):
        b, base, done, _ = state
        for step in range(DSA_BITS_PER_CHECK):
            base, done = try_bit(b - step, base, done)
        return b - DSA_BITS_PER_CHECK, base, done, jnp.sum(1.0 - done)

    _, thr, _, _ = lax.while_loop(low_cond, low_body, (jnp.int32(15), base, done, jnp.sum(1.0 - done)))
    thr = jnp.maximum(thr, INT32_MIN + 1)
    need = jnp.broadcast_to(n_top - count_ge(thr + 1), (T, 128))
    tri = tri_ref[...]

    _flash_init(m_scr, acc_scr)
    qs = _stack_heads(q_ref, H)

    def att_body(c, eq_seen):
        kk = key_scr[c]
        for half in range(C // CH):
            s0 = pl.multiple_of(c * C + half * CH, CH)
            eqs, gts = [], []
            for u in range(CH // 128):
                ku = kk[:, half * CH + u * 128:half * CH + (u + 1) * 128]
                eqs.append(ku == thr)
                gts.append(ku > thr)
            eq = jnp.concatenate([jnp.where(e, 1.0, 0.0) for e in eqs], axis=1).astype(BF16)
            pref = _dot(eq, tri)
            bias = [jnp.where(gts[u] | (eqs[u] & (eq_seen + pref[:, u * 128:(u + 1) * 128] <= need)),
                              0.0, MASKED) for u in range(CH // 128)]
            eq_seen = eq_seen + pref[:, CH:]
            _flash_step(qs, k_ref[pl.ds(s0, CH), :], jnp.concatenate(bias, axis=1), v_ref[pl.ds(s0, CH), :],
                        m_scr, acc_scr, H)
        return eq_seen

    lax.fori_loop(0, n_chunks, att_body, jnp.zeros((T, 128), F32))
    out = _flash_result(acc_scr)
    for r in range(H):
        o_ref[:, r * HEAD_DIM:(r + 1) * HEAD_DIM] = out[r * T:(r + 1) * T].astype(o_ref.dtype)


def dsa_attention(q, k, v, iq, ik, iw):
    S = q.shape[0]
    T, C, CH = DSA_T, DSA_C, DSA_CH
    n_top = min(DSA_TOPK, S // 4)
    tri = jnp.concatenate([jnp.triu(jnp.ones((CH, CH), F32)), jnp.ones((CH, 128), F32)], axis=1).astype(BF16)
    return pl.pallas_call(
        functools.partial(_dsa_kernel, n_top),
        grid=(S // T,),
        in_specs=[pl.BlockSpec((T, DSA_Q_W), lambda i: (i, 0)),
                  pl.BlockSpec((T, DSA_IQ_W), lambda i: (i, 0)),
                  pl.BlockSpec((T, DSA_IDX_HEADS), lambda i: (i, 0)),
                  _resident((S, HEAD_DIM)), _resident((S, 2 * HEAD_DIM)), _resident((S, DSA_IDX_DIM)),
                  _resident((CH, CH + 128))],
        out_specs=pl.BlockSpec((T, DSA_Q_W), lambda i: (i, 0)),
        out_shape=jax.ShapeDtypeStruct((S, DSA_Q_W), BF16),
        scratch_shapes=[pltpu.VMEM((S // C, T, C), jnp.int32),
                        pltpu.VMEM((DSA_HEADS * T, HEAD_DIM), F32),
                        pltpu.VMEM((DSA_HEADS * T, 2 * HEAD_DIM), F32)],
        compiler_params=_params(1),
        name="dsa_attention",
    )(q, iq, iw, k, v, ik, tri)


NSA_T = 512
NSA_C = 512
NSA_CH = 256
NSA_GROUP = 128


def _nsa_kernel(q_ref, g_ref, kc_ref, vc_ref, ov_ref, ks_ref, vs_ref, kw_ref, vw_ref, e_ref, o_ref,
                sel_scr, m_scr, acc_scr):
    i = pl.program_id(0)
    T, C, CH = NSA_T, NSA_C, NSA_CH
    H = NSA_HEADS
    q0 = pl.multiple_of(i * T, T)
    ncp = kc_ref.shape[0]
    ns = ov_ref.shape[1]
    tpos = q0 + lax.broadcasted_iota(jnp.int32, (T, 1), 0)
    tpos4 = _tile_rows(tpos, H)
    qs = _stack_heads(q_ref, H)

    s_c = _dot_nt(qs, kc_ref[...])
    cmp_end = lax.broadcasted_iota(jnp.int32, (H * T, ncp), 1) * NSA_CMP_STRIDE + (NSA_CMP_LEN - 1)
    p_c = _masked_softmax(s_c, cmp_end <= tpos4)
    o_c = _dot(p_c.astype(BF16), vc_ref[...])

    psum = p_c[0:T]
    for r in range(1, H):
        psum = psum + p_c[r * T:(r + 1) * T]
    ov = ov_ref[...]
    hi = psum.astype(BF16)
    r1 = psum - hi.astype(F32)
    mid = r1.astype(BF16)
    lo = (r1 - mid.astype(F32)).astype(BF16)
    imp = _dot(hi, ov) + _dot(mid, ov) + _dot(lo, ov)

    blk = lax.broadcasted_iota(jnp.int32, (T, ns), 1)
    cur = tpos >> (NSA_SLC_LEN.bit_length() - 1)
    forced = (blk == 0) | (blk == cur) | (blk == cur - 1)
    imp = jnp.where(forced, NSA_FORCED_SCORE, imp)
    imp = jnp.where(blk * NSA_SLC_LEN <= tpos, imp, MASKED)
    blk_f = blk.astype(F32)
    sel = jnp.zeros((T, ns), F32)
    for _ in range(min(NSA_N_SEL, ns)):
        mx = jnp.max(imp, axis=1, keepdims=True)
        first = jnp.min(jnp.where(imp == mx, blk_f, float(ns)), axis=1, keepdims=True)
        hit = blk_f == first
        sel = jnp.where(hit, 1.0, sel)
        imp = jnp.where(hit, EXTRACTED, imp)
    for g in range(ns // NSA_GROUP):
        sel_scr[g] = sel[:, g * NSA_GROUP:(g + 1) * NSA_GROUP].astype(BF16)

    _flash_init(m_scr, acc_scr)
    chunks_per_group = NSA_GROUP * NSA_SLC_LEN // C

    def sel_body(c, carry):
        s0 = pl.multiple_of(c * C, C)
        member = _dot(sel_scr[c // chunks_per_group], e_ref[c % chunks_per_group])
        spos = s0 + lax.broadcasted_iota(jnp.int32, (T, C), 1)
        bias = jnp.where((member > 0.5) & (spos <= tpos), 0.0, MASKED)
        for half in range(C // CH):
            h0 = pl.multiple_of(s0 + half * CH, CH)
            _flash_step(qs, ks_ref[pl.ds(h0, CH), :], bias[:, half * CH:(half + 1) * CH], vs_ref[pl.ds(h0, CH), :],
                        m_scr, acc_scr, H)
        return carry

    lax.fori_loop(0, (q0 + T + C - 1) // C, sel_body, 0)
    o_s = _flash_result(acc_scr)

    nw = NSA_WINDOW + T
    s_w = _dot_nt(qs, kw_ref[pl.ds(q0, nw), :])
    wpos = q0 - NSA_WINDOW + lax.broadcasted_iota(jnp.int32, (H * T, nw), 1)
    dist = tpos4 - wpos
    p_w = _masked_softmax(s_w, (dist >= 0) & (dist < NSA_WINDOW) & (wpos >= 0))
    o_w = _dot(p_w.astype(BF16), vw_ref[pl.ds(q0, nw), :])

    gate = jax.nn.sigmoid(g_ref[...])
    for r in range(H):
        rows = slice(r * T, (r + 1) * T)
        o = (gate[:, 3 * r:3 * r + 1] * o_c[rows] + gate[:, 3 * r + 1:3 * r + 2] * o_s[rows]
             + gate[:, 3 * r + 2:3 * r + 3] * o_w[rows])
        o_ref[:, r * HEAD_DIM:(r + 1) * HEAD_DIM] = o.astype(o_ref.dtype)


def nsa_attention(q, g, k_cmp, v_cmp, ks, vs, kw, vw):
    S = q.shape[0]
    T, C = NSA_T, NSA_C
    n_cmp = k_cmp.shape[0]
    ncp = -(-n_cmp // 128) * 128
    ns = S // NSA_SLC_LEN
    k_cmp = jnp.pad(k_cmp, ((0, ncp - n_cmp), (0, 0)))
    v_cmp = jnp.pad(v_cmp, ((0, ncp - n_cmp), (0, 0)))
    cmp_start = np.arange(ncp) * NSA_CMP_STRIDE
    slc_start = np.arange(ns) * NSA_SLC_LEN
    overlap = ((cmp_start[:, None] < slc_start[None, :] + NSA_SLC_LEN)
               & (cmp_start[:, None] + NSA_CMP_LEN - 1 >= slc_start[None, :]))
    overlap = jnp.asarray(overlap, BF16)
    cpg = NSA_GROUP * NSA_SLC_LEN // C
    tok_blk = (np.arange(cpg)[:, None] * C + np.arange(C)[None, :]) // NSA_SLC_LEN
    expand = jnp.asarray(np.arange(NSA_GROUP)[None, :, None] == tok_blk[:, None, :], BF16)
    kw = jnp.pad(kw, ((NSA_WINDOW, 0), (0, 0)))
    vw = jnp.pad(vw, ((NSA_WINDOW, 0), (0, 0)))
    return pl.pallas_call(
        _nsa_kernel,
        grid=(S // T,),
        in_specs=[pl.BlockSpec((T, NSA_Q_W), lambda i: (i, 0)),
                  pl.BlockSpec((T, NSA_HEADS * 3), lambda i: (i, 0)),
                  _resident((ncp, HEAD_DIM)), _resident((ncp, HEAD_DIM)), _resident((ncp, ns)),
                  _resident((S, HEAD_DIM)), _resident((S, 2 * HEAD_DIM)),
                  _resident((S + NSA_WINDOW, HEAD_DIM)), _resident((S + NSA_WINDOW, HEAD_DIM)),
                  _resident((cpg, NSA_GROUP, C))],
        out_specs=pl.BlockSpec((T, NSA_Q_W), lambda i: (i, 0)),
        out_shape=jax.ShapeDtypeStruct((S, NSA_Q_W), BF16),
        scratch_shapes=[pltpu.VMEM((ns // NSA_GROUP, T, NSA_GROUP), BF16),
                        pltpu.VMEM((NSA_HEADS * T, HEAD_DIM), F32),
                        pltpu.VMEM((NSA_HEADS * T, 2 * HEAD_DIM), F32)],
        compiler_params=_params(1),
        name="nsa_attention",
    )(q, g, k_cmp, v_cmp, overlap, ks, vs, kw, vw, expand)


def nsa_compress(x, pe, w1, w2):
    S = x.shape[0]
    half = NSA_CMP_STRIDE * HEAD_DIM
    x16 = x.reshape(S // NSA_CMP_STRIDE, half)
    pe_flat = pe.reshape(NSA_CMP_LEN * HEAD_DIM)
    w1_flat = w1.reshape(NSA_CMP_LEN * HEAD_DIM, HEAD_DIM)
    first = _mm(x16 + pe_flat[None, :half], w1_flat[:half])
    second = _mm(x16 + pe_flat[None, half:], w1_flat[half:])
    hid = jax.nn.gelu(first[:-1] + second[1:])
    hid = jnp.pad(hid, ((0, 1), (0, 0)))
    return _mm(hid, w2)[:-1]


DIL_T = 256


def _dil_kernel(q_ref, k_ref, v_ref, o_ref):
    i = pl.program_id(0)
    T = DIL_T
    q0 = pl.multiple_of(i * T, T)
    tl = lax.broadcasted_iota(jnp.int32, (T, 1), 0)
    for s in range(DIL_HEADS_PER_GROUP):
        cols = slice(s * HEAD_DIM, (s + 1) * HEAD_DIM)
        parts = []
        m = jnp.full((T, 1), MASKED, F32)
        for g, (w, r) in enumerate(DIL_PATTERNS):
            n = w + T
            start = pl.multiple_of(q0 + (DIL_MAX_WINDOW - w), 128)
            head = g * DIL_HEADS_PER_GROUP + s
            sc = _dot_nt(q_ref[:, head * HEAD_DIM:(head + 1) * HEAD_DIM], k_ref[pl.ds(start, n), cols]) * ATT_SCALE
            b = lax.broadcasted_iota(jnp.int32, (T, n), 1)
            d = tl + w - b
            mask = (d >= 0) & (d <= w) & ((d & (r - 1)) == 0) & (q0 - w + b >= 0)
            sm = jnp.where(mask, sc, MASKED)
            m = jnp.maximum(m, jnp.max(sm, axis=1, keepdims=True))
            parts.append((sm, mask, start, n))
        num = jnp.zeros((T, HEAD_DIM), F32)
        den = jnp.zeros((T, 1), F32)
        for sm, mask, start, n in parts:
            e = jnp.where(mask, jnp.exp(sm - m), 0.0)
            den = den + jnp.sum(e, axis=1, keepdims=True)
            num = num + _dot(e.astype(BF16), v_ref[pl.ds(start, n), cols])
        o_ref[:, cols] = (num / den).astype(o_ref.dtype)


def dilated_attention(q, k, v):
    S = q.shape[0]
    T = DIL_T
    k = jnp.pad(k, ((DIL_MAX_WINDOW, 0), (0, 0)))
    v = jnp.pad(v, ((DIL_MAX_WINDOW, 0), (0, 0)))
    return pl.pallas_call(
        _dil_kernel,
        grid=(S // T,),
        in_specs=[pl.BlockSpec((T, DIL_Q_W), lambda i: (i, 0)),
                  _resident((S + DIL_MAX_WINDOW, DIL_KV_W)), _resident((S + DIL_MAX_WINDOW, DIL_KV_W))],
        out_specs=pl.BlockSpec((T, DIL_KV_W), lambda i: (i, 0)),
        out_shape=jax.ShapeDtypeStruct((S, DIL_KV_W), BF16),
        compiler_params=_params(1),
        name="dilated_attention",
    )(q, k, v)


NORM_TM = 512
NORM_TN = 512


def _norm_mm_kernel(x_ref, g_ref, b_ref, o_ref, xn_ref):
    @pl.when(pl.program_id(1) == 0)
    def _():
        x = x_ref[...]
        y = x * lax.rsqrt(jnp.mean(x * x, axis=-1, keepdims=True) + NORM_EPS)
        xn_ref[...] = (y * g_ref[...]).astype(xn_ref.dtype)
    o_ref[...] = _dot(xn_ref[...], b_ref[...]).astype(o_ref.dtype)


def norm_mm(x, gain, w, out_dtype, return_xn=False):
    M, K = x.shape
    N = w.shape[1]
    tm, tn = min(NORM_TM, M), min(NORM_TN, N)
    main_spec = pl.BlockSpec((tm, tn), lambda i, j: (i, j))
    xn_spec = pl.BlockSpec((tm, K), lambda i, j: (i, 0))
    main_shape = jax.ShapeDtypeStruct((M, N), out_dtype)
    return pl.pallas_call(
        _norm_mm_kernel,
        grid=(M // tm, N // tn),
        in_specs=[xn_spec, pl.BlockSpec((1, K), lambda i, j: (0, 0)), pl.BlockSpec((K, tn), lambda i, j: (0, j))],
        out_specs=(main_spec, xn_spec) if return_xn else main_spec,
        out_shape=(main_shape, jax.ShapeDtypeStruct((M, K), BF16)) if return_xn else main_shape,
        scratch_shapes=[] if return_xn else [pltpu.VMEM((tm, K), BF16)],
        compiler_params=_params(2),
        name="norm_mm",
    )(x, gain.reshape(1, K).astype(F32), w)


PREP_T = 256
PREP_SEGMENTS = (
    ("z", GATE_RANK, None, False, 1.0, "bf16"),
    ("a_q", SB_W, None, False, LOG2_SCALE, "bf16"),
    ("a_k", SB_W, None, False, 1.0, "bf16"),
    ("a_v", SB_W, None, False, 1.0, "bf16"),
    ("b_q", DSA_Q_W, 0, True, LOG2_SCALE, "bf16"),
    ("b_k", DSA_KV_W, 1, True, 1.0, "bf16"),
    ("b_v", DSA_KV_W, None, False, 1.0, "ones"),
    ("b_iq", DSA_IQ_W, None, False, 1.0, "bf16"),
    ("b_ik", HEAD_DIM, None, False, 1.0, "ik"),
    ("c_q", NSA_Q_W, 2, True, LOG2_SCALE, "bf16"),
    ("c_kc", NSA_KV_W, None, True, 1.0, "f32"),
    ("c_vc", NSA_KV_W, None, False, 1.0, "f32"),
    ("c_ks", NSA_KV_W, 4, True, 1.0, "bf16"),
    ("c_vs", NSA_KV_W, None, False, 1.0, "ones"),
    ("c_kw", NSA_KV_W, 5, True, 1.0, "bf16"),
    ("c_vw", NSA_KV_W, None, False, 1.0, "bf16"),
    ("d_q", DIL_Q_W, 6, True, 1.0, "bf16"),
    ("d_k", DIL_KV_W, 7, True, 1.0, "bf16"),
    ("d_v", DIL_KV_W, None, False, 1.0, "bf16"),
    ("misc", HEAD_DIM, None, False, 1.0, "misc"),
)
PREP_WIDTH = sum(seg[1] for seg in PREP_SEGMENTS)
N_MISC = DSA_IDX_HEADS + NSA_HEADS * 3


def _prep_outputs(S):
    shapes = []
    for name, width, _, _, _, kind in PREP_SEGMENTS:
        if kind == "bf16":
            shapes.append((width, BF16))
        elif kind == "ones":
            shapes.append((width + HEAD_DIM, BF16))
        elif kind == "f32":
            shapes.append((width, F32))
        elif kind == "ik":
            shapes.append((DSA_IDX_DIM, BF16))
        else:
            shapes.append((N_MISC, F32))
    return shapes


def _prep_kernel(p_ref, g_ref, cs_ref, sn_ref, *o_refs):
    cs = cs_ref[...]
    sn = sn_ref[...]
    off = 0
    for (name, width, gi, rope, scale, kind), o_ref in zip(PREP_SEGMENTS, o_refs):
        if kind == "ik":
            o_ref[...] = p_ref[:, off:off + DSA_IDX_DIM].astype(o_ref.dtype)
        elif kind == "misc":
            o_ref[...] = p_ref[:, off:off + N_MISC]
        else:
            for hd in range(width // HEAD_DIM):
                lanes = slice(hd * HEAD_DIM, (hd + 1) * HEAD_DIM)
                x = p_ref[:, off + hd * HEAD_DIM:off + (hd + 1) * HEAD_DIM]
                if gi is not None:
                    x = x * lax.rsqrt(jnp.mean(x * x, axis=-1, keepdims=True) + NORM_EPS) * g_ref[gi:gi + 1, :]
                if rope:
                    x = x * cs + pltpu.roll(x, HEAD_DIM // 2, 1) * sn
                if scale != 1.0:
                    x = x * scale
                o_ref[:, lanes] = x.astype(o_ref.dtype)
            if kind == "ones":
                o_ref[:, width:] = jnp.ones((o_ref.shape[0], HEAD_DIM), o_ref.dtype)
        off += width


def prepare_operands(proj, qk_gain, cs, sn):
    S = proj.shape[0]
    T = min(PREP_T, S)
    row = lambda i: (i, 0)
    shapes = _prep_outputs(S)
    outs = pl.pallas_call(
        _prep_kernel,
        grid=(S // T,),
        in_specs=[pl.BlockSpec((T, PREP_WIDTH), row),
                  pl.BlockSpec((N_QK_GAINS, HEAD_DIM), lambda i: (0, 0)),
                  pl.BlockSpec((T, HEAD_DIM), row), pl.BlockSpec((T, HEAD_DIM), row)],
        out_specs=tuple(pl.BlockSpec((T, w), row) for w, _ in shapes),
        out_shape=tuple(jax.ShapeDtypeStruct((S, w), dt) for w, dt in shapes),
        compiler_params=_params(1),
        name="prepare_operands",
    )(proj, qk_gain, cs, sn)
    return {seg[0]: o for seg, o in zip(PREP_SEGMENTS, outs)}


def fused_in_weights(w_in, gate_down):
    names = ("a_q", "a_k", "a_v", "b_q", "b_k", "b_v", "b_iq", "b_ik", "b_iw",
             "c_q", "c_kc", "c_vc", "c_ks", "c_vs", "c_kw", "c_vw", "c_g", "d_q", "d_k", "d_v")
    parts = dict(zip(names, jnp.split(w_in, SPLIT_OFFSETS, axis=1)))
    parts["z"] = gate_down
    parts["misc"] = jnp.concatenate([parts["b_iw"], parts["c_g"]], axis=1)
    cols = []
    for name, width, *_ in PREP_SEGMENTS:
        p = parts[name]
        cols.append(jnp.pad(p, ((0, 0), (0, width - p.shape[1]))))
    return jnp.concatenate(cols, axis=1).astype(BF16)


def token_mixing(h, cs, sn, norm_gain, w_in, qk_gain, nsa_pe, nsa_w1, nsa_w2, gate_down, gate_up, w_branch, w_out):
    proj = norm_mm(h, norm_gain, fused_in_weights(w_in, gate_down), F32)
    t = prepare_operands(proj, qk_gain, cs, sn)
    o_a = sb_attention(t["a_q"], t["a_k"], t["a_v"])
    o_b = dsa_attention(t["b_q"], t["b_k"], t["b_v"], t["b_iq"], t["b_ik"], t["misc"][:, :DSA_IDX_HEADS])
    k_cmp = rmsnorm(nsa_compress(t["c_kc"], nsa_pe[0], nsa_w1[0], nsa_w2[0]), qk_gain[3])
    v_cmp = nsa_compress(t["c_vc"], nsa_pe[1], nsa_w1[1], nsa_w2[1])
    o_c = nsa_attention(t["c_q"], t["misc"][:, DSA_IDX_HEADS:], k_cmp.astype(BF16), v_cmp.astype(BF16),
                        t["c_ks"], t["c_vs"], t["c_kw"], t["c_vw"])
    o_d = dilated_attention(t["d_q"], t["d_k"], t["d_v"])
    merged = gated_merge(t["z"], (o_a, o_b, o_c, o_d), gate_up, w_branch)
    return _mm(merged, w_out, residual=h)


MERGE_TM = 512
MERGE_TN = 512


def _merge_kernel(z_ref, oa_ref, ob_ref, oc_ref, od_ref, gu_ref, wb_ref, o_ref):
    z = z_ref[...]
    acc = None
    off = 0
    for i, br_ref in enumerate((oa_ref, ob_ref, oc_ref, od_ref)):
        width = BRANCH_WIDTHS[i]
        gate = jax.nn.sigmoid(_dot(z, gu_ref[i]))
        term = gate * _dot(br_ref[...], wb_ref[off:off + width, :])
        acc = term if acc is None else acc + term
        off += width
    o_ref[...] = acc.astype(o_ref.dtype)


def gated_merge(z, branches, gate_up, w_branch):
    S = z.shape[0]
    D = w_branch.shape[1]
    tm, tn = min(MERGE_TM, S), min(MERGE_TN, D)
    row = lambda i, j: (i, 0)
    return pl.pallas_call(
        _merge_kernel,
        grid=(S // tm, D // tn),
        in_specs=[pl.BlockSpec((tm, GATE_RANK), row)]
                 + [pl.BlockSpec((tm, w), row) for w in BRANCH_WIDTHS]
                 + [pl.BlockSpec((N_BRANCHES, GATE_RANK, tn), lambda i, j: (0, 0, j)),
                    pl.BlockSpec((BRANCH_WIDTH, tn), lambda i, j: (0, j))],
        out_specs=pl.BlockSpec((tm, tn), lambda i, j: (i, j)),
        out_shape=jax.ShapeDtypeStruct((S, D), BF16),
        compiler_params=_params(2),
        name="gated_merge",
    )(z, *branches, gate_up.astype(BF16), w_branch.astype(BF16))


PEER_ST = 256
PEER_TM = 512
PEER_EC = 512
PEER_HALF = PEER_KEY_DIM // 2
PEER_STAT_ROWS = 32
assert (PEER_TOPK + 1) // 9 == 1 and PEER_TOPK + 1 <= 24


def _top_rows(scores, n, n_rows):
    st = scores.shape[1]
    rid = lax.broadcasted_iota(jnp.int32, (n_rows, st), 0)
    out = jnp.full((n_rows, st), EXTRACTED, F32)
    for r in range(n):
        m = jnp.max(scores, axis=0, keepdims=True)
        out = jnp.where(rid == r, m, out)
        scores = jnp.where(scores == m, EXTRACTED, scores)
    return out


def _peer_stats_kernel(q_ref, w1_ref, w2_ref, o_ref):
    st = q_ref.shape[0]
    k = PEER_TOPK
    rid = lax.broadcasted_iota(jnp.int32, (PEER_STAT_ROWS, st), 0)
    rid8 = lax.broadcasted_iota(jnp.int32, (8, st), 0)
    stats = jnp.zeros((PEER_STAT_ROWS, st), F32)
    for h in range(PEER_HEADS):
        qh = q_ref[:, h * PEER_KEY_DIM:(h + 1) * PEER_KEY_DIM]
        v1 = _top_rows(_dot_nt(w1_ref[h], qh), k + 1, 24)
        v2 = _top_rows(_dot_nt(w2_ref[h], qh), k + 1, 24)
        pieces = [v1[0:1] + v2]
        for a in range(1, 8):
            pieces.append(jnp.where(rid8 < (k + 1) // (a + 1), v1[a:a + 1] + v2[0:8], EXTRACTED))
        pieces.append(v1[8:24] + v2[0:1])
        tops = _top_rows(jnp.concatenate(pieces, axis=0), k + 1, 24)
        c1 = tops[0:1]
        den = jnp.sum(jnp.exp(tops[0:k] - c1), axis=0, keepdims=True)
        thr = 0.5 * (tops[k - 1:k] + tops[k:k + 1])
        stats = jnp.where(rid == h, thr, stats)
        stats = jnp.where(rid == PEER_HEADS + h, c1, stats)
        stats = jnp.where(rid == 2 * PEER_HEADS + h, 1.0 / den, stats)
    o_ref[...] = stats


def _peer_weight_kernel(xn_ref, ut_ref, q_ref, kf_ref, st_ref, o_ref):
    act = jax.nn.gelu(_dot(xn_ref[...], ut_ref[...]))
    st = st_ref[...]
    w = jnp.zeros(act.shape, F32)
    for h in range(PEER_HEADS):
        sf = _dot(q_ref[:, h * PEER_KEY_DIM:(h + 1) * PEER_KEY_DIM], kf_ref[h])
        gate = jnp.exp(sf - st[:, PEER_HEADS + h:PEER_HEADS + h + 1]) * st[:, 2 * PEER_HEADS + h:2 * PEER_HEADS + h + 1]
        w = w + jnp.where(sf >= st[:, h:h + 1], gate, 0.0)
    o_ref[...] = (w * act).astype(o_ref.dtype)


def peer_ffn(h, norm_gain, wq, subkeys, u, v):
    S, D = h.shape
    nk, half = PEER_N_KEYS, PEER_HALF
    q, xn = norm_mm(h, norm_gain, wq.astype(BF16), BF16, return_xn=True)
    zeros = jnp.zeros((PEER_HEADS, nk, half), F32)
    w1 = jnp.concatenate([subkeys[:, 0], zeros], axis=-1).astype(BF16)
    w2 = jnp.concatenate([zeros, subkeys[:, 1]], axis=-1).astype(BF16)
    st_t = pl.pallas_call(
        _peer_stats_kernel,
        grid=(S // PEER_ST,),
        in_specs=[pl.BlockSpec((PEER_ST, PEER_HEADS * PEER_KEY_DIM), lambda i: (i, 0)),
                  pl.BlockSpec((PEER_HEADS, nk, PEER_KEY_DIM), lambda i: (0, 0, 0)),
                  pl.BlockSpec((PEER_HEADS, nk, PEER_KEY_DIM), lambda i: (0, 0, 0))],
        out_specs=pl.BlockSpec((PEER_STAT_ROWS, PEER_ST), lambda i: (0, i)),
        out_shape=jax.ShapeDtypeStruct((PEER_STAT_ROWS, S), F32),
        compiler_params=_params(1),
        name="peer_stats",
    )(q, w1, w2)
    stats = st_t.T
    top = jnp.repeat(jnp.swapaxes(subkeys[:, 0], 1, 2), nk, axis=2)
    bot = jnp.tile(jnp.swapaxes(subkeys[:, 1], 1, 2), (1, 1, nk))
    kfull = jnp.concatenate([top, bot], axis=1).astype(BF16)
    tm, ec = min(PEER_TM, S), PEER_EC
    wmat = pl.pallas_call(
        _peer_weight_kernel,
        grid=(S // tm, PEER_EXPERTS // ec),
        in_specs=[pl.BlockSpec((tm, D), lambda i, j: (i, 0)),
                  pl.BlockSpec((D, ec), lambda i, j: (0, j)),
                  pl.BlockSpec((tm, PEER_HEADS * PEER_KEY_DIM), lambda i, j: (i, 0)),
                  pl.BlockSpec((PEER_HEADS, PEER_KEY_DIM, ec), lambda i, j: (0, 0, j)),
                  pl.BlockSpec((tm, PEER_STAT_ROWS), lambda i, j: (i, 0))],
        out_specs=pl.BlockSpec((tm, ec), lambda i, j: (i, j)),
        out_shape=jax.ShapeDtypeStruct((S, PEER_EXPERTS), BF16),
        compiler_params=_params(2),
        name="peer_weights",
    )(xn, u.T.astype(BF16), q, kfull, stats)
    return _mm(wmat, v, residual=h)


def kernel(x, norm_mix, w_in, qk_gain, nsa_pe, nsa_w1, nsa_w2, gate_down, gate_up, w_branch, w_out,
           norm_ffn, peer_wq, peer_subkeys, peer_u, peer_v):
    S = x.shape[1]
    cos, sin = rope_tables(S, x.dtype)
    cos, sin = cos.reshape(S, HEAD_DIM // 2), sin.reshape(S, HEAD_DIM // 2)
    cs = jnp.concatenate([cos, cos], axis=1)
    sn = jnp.concatenate([-sin, sin], axis=1)
    h = x[0]
    for l in range(DEPTH):
        h = token_mixing(h, cs, sn, norm_mix[l], w_in[l], qk_gain[l], nsa_pe[l], nsa_w1[l], nsa_w2[l],
                         gate_down[l], gate_up[l], w_branch[l], w_out[l])
        h = peer_ffn(h, norm_ffn[l], peer_wq[l], peer_subkeys[l], peer_u[l], peer_v[l])
    return h[None]
```

```python
import math, functools
import jax, jax.numpy as jnp
from jax import lax
import numpy as np
from jax.experimental import pallas as pl
from jax.experimental.pallas import tpu as pltpu

D_MODEL = 4096
BATCH = 1
SEQ = 16384
DEPTH = 4

HEAD_DIM = 128
ROPE_THETA = 10000.0
NORM_EPS = 1e-6
SB_HEADS = 4
DSA_HEADS = 4
DSA_KV_HEADS = 1
DSA_IDX_HEADS = 4
DSA_IDX_DIM = 64
DSA_TOPK = 256
NSA_HEADS = 4
NSA_KV_HEADS = 1
NSA_CMP_LEN = 32
NSA_CMP_STRIDE = 16
NSA_SLC_LEN = 32
NSA_N_SEL = 8
NSA_WINDOW = 512
NSA_FORCED_SCORE = 1e4
DIL_PATTERNS = ((128, 1), (512, 4), (2048, 16))
DIL_GROUPS = len(DIL_PATTERNS)
DIL_HEADS_PER_GROUP = 2
DIL_MAX_WINDOW = max(w for w, _ in DIL_PATTERNS)
N_BRANCHES = 4
GATE_RANK = 256
PEER_HEADS = 8
PEER_N_KEYS = 64
PEER_EXPERTS = PEER_N_KEYS ** 2
PEER_KEY_DIM = 128
PEER_TOPK = 16

SB_W = SB_HEADS * HEAD_DIM
DSA_Q_W = DSA_HEADS * HEAD_DIM
DSA_KV_W = DSA_KV_HEADS * HEAD_DIM
DSA_IQ_W = DSA_IDX_HEADS * DSA_IDX_DIM
NSA_Q_W = NSA_HEADS * HEAD_DIM
NSA_KV_W = NSA_KV_HEADS * HEAD_DIM
DIL_Q_W = DIL_GROUPS * DIL_HEADS_PER_GROUP * HEAD_DIM
DIL_KV_W = DIL_HEADS_PER_GROUP * HEAD_DIM
IN_SPLITS = (SB_W, SB_W, SB_W,
             DSA_Q_W, DSA_KV_W, DSA_KV_W, DSA_IQ_W, DSA_IDX_DIM, DSA_IDX_HEADS,
             NSA_Q_W, NSA_KV_W, NSA_KV_W, NSA_KV_W, NSA_KV_W, NSA_KV_W, NSA_KV_W, NSA_HEADS * 3,
             DIL_Q_W, DIL_KV_W, DIL_KV_W)
IN_WIDTH = sum(IN_SPLITS)
SPLIT_OFFSETS = tuple(int(o) for o in np.cumsum(IN_SPLITS)[:-1])
BRANCH_WIDTHS = (SB_W, DSA_Q_W, NSA_Q_W, DIL_KV_W)
BRANCH_WIDTH = sum(BRANCH_WIDTHS)
BRANCH_OFFSETS = tuple(int(o) for o in np.cumsum(BRANCH_WIDTHS)[:-1])
N_QK_GAINS = 8

ATT_SCALE = HEAD_DIM ** -0.5
LOG2_SCALE = ATT_SCALE * math.log2(math.e)
MASKED = -1e30
EXTRACTED = -3e38
INT32_MIN = -2 ** 31
F32_TINY = float(np.finfo(np.float32).tiny)
VMEM_LIMIT = 56 * 1024 * 1024

BF16 = jnp.bfloat16
F32 = jnp.float32


def _params(n_grid):
    return pltpu.CompilerParams(dimension_semantics=("arbitrary",) * n_grid, vmem_limit_bytes=VMEM_LIMIT)


def _resident(shape):
    nd = len(shape)
    return pl.BlockSpec(shape, lambda *_: (0,) * nd, pipeline_mode=pl.Buffered(1))


def _dot_nt(a, b):
    return lax.dot_general(a, b, (((1,), (1,)), ((), ())), preferred_element_type=F32)


def _dot(a, b):
    return jnp.dot(a, b, preferred_element_type=F32)


def _mm_kernel(a_ref, b_ref, o_ref):
    o_ref[...] = _dot(a_ref[...], b_ref[...]).astype(o_ref.dtype)


def _mm_res_kernel(a_ref, b_ref, r_ref, o_ref):
    o_ref[...] = (r_ref[...] + _dot(a_ref[...], b_ref[...])).astype(o_ref.dtype)


def _pick(n, prefs):
    for p in prefs:
        if n % p == 0:
            return p
    return n


def _mm(a, b, out_dtype=F32, residual=None):
    lead = a.shape[:-1]
    K = a.shape[-1]
    N = b.shape[-1]
    a2 = a.reshape(-1, K).astype(jnp.bfloat16)
    M = a2.shape[0]
    n_pad = (-N) % 128
    b2 = b.astype(jnp.bfloat16)
    if n_pad:
        b2 = jnp.pad(b2, ((0, 0), (0, n_pad)))
    Np = N + n_pad
    tm = _pick(M, (1024, 512, 256, 128))
    tn = _pick(Np, (512, 256, 128))
    in_specs = [pl.BlockSpec((tm, K), lambda i, j: (i, 0)),
                pl.BlockSpec((K, tn), lambda i, j: (0, j))]
    args = [a2, b2]
    body = _mm_kernel
    if residual is not None:
        assert n_pad == 0
        in_specs.append(pl.BlockSpec((tm, tn), lambda i, j: (i, j)))
        args.append(residual.reshape(M, N))
        body = _mm_res_kernel
    out = pl.pallas_call(
        body,
        grid=(M // tm, Np // tn),
        in_specs=in_specs,
        out_specs=pl.BlockSpec((tm, tn), lambda i, j: (i, j)),
        out_shape=jax.ShapeDtypeStruct((M, Np), out_dtype),
        compiler_params=pltpu.CompilerParams(
            dimension_semantics=("arbitrary", "arbitrary"),
            vmem_limit_bytes=VMEM_LIMIT),
        name="dense_mm",
    )(*args)
    if n_pad:
        out = out[:, :N]
    return out.reshape(*lead, N)


def rmsnorm(x, g):
    xf = x.astype(jnp.float32)
    y = xf * lax.rsqrt(jnp.mean(xf * xf, axis=-1, keepdims=True) + NORM_EPS)
    return (y * g.astype(jnp.float32)).astype(x.dtype)


def rope_tables(seq, dtype):
    inv = ROPE_THETA ** (-jnp.arange(0, HEAD_DIM, 2, dtype=jnp.float32) / HEAD_DIM)
    ang = jnp.arange(seq, dtype=jnp.float32)[:, None] * inv[None, :]
    return jnp.cos(ang)[None, :, None, :].astype(dtype), jnp.sin(ang)[None, :, None, :].astype(dtype)


def _stack_heads(q_ref, n_heads):
    return jnp.concatenate([q_ref[:, r * HEAD_DIM:(r + 1) * HEAD_DIM] for r in range(n_heads)], axis=0)


def _tile_rows(x, n):
    return jnp.concatenate([x] * n, axis=0)


def _flash_init(m_scr, acc_scr):
    m_scr[...] = jnp.full(m_scr.shape, MASKED, F32)
    acc_scr[...] = jnp.zeros(acc_scr.shape, F32)


def _flash_step(qs, k_chunk, bias, v_aug, m_scr, acc_scr, n_heads):
    T, C = bias.shape
    s = _dot_nt(qs, k_chunk)
    ps, alphas = [], []
    for r in range(n_heads):
        rows = slice(r * T, (r + 1) * T)
        sm = s[rows] + bias
        m_old = m_scr[rows]
        m_new = jnp.maximum(m_old, jnp.max(sm, axis=1, keepdims=True))
        m_scr[rows] = m_new
        p = [jnp.exp2(sm[:, u * 128:(u + 1) * 128] - m_new) for u in range(C // 128)]
        ps.append(jnp.concatenate(p, axis=1).astype(BF16))
        alpha = jnp.exp2(m_old - m_new)
        alphas.append(jnp.concatenate([alpha, alpha], axis=1))
    pv = _dot(jnp.concatenate(ps, axis=0), v_aug)
    acc_scr[...] = jnp.concatenate(alphas, axis=0) * acc_scr[...] + pv


def _flash_result(acc_scr):
    acc = acc_scr[...]
    return acc[:, :HEAD_DIM] / acc[:, HEAD_DIM:]


def _masked_softmax(s, mask):
    sm = jnp.where(mask, s, MASKED)
    m = jnp.max(sm, axis=1, keepdims=True)
    e = jnp.where(mask, jnp.exp2(sm - m), 0.0)
    den = jnp.sum(e, axis=1, keepdims=True)
    return e * (1.0 / jnp.maximum(den, F32_TINY))


SB_T = 256
SB_DEAD_LOG2 = -200.0


def _sb_kernel(q_ref, k_ref, v_ref, tri_ref, o_ref, later_scr, acc_scr):
    i = pl.program_id(0)
    T = SB_T
    q0 = pl.multiple_of(i * T, T)
    tri = tri_ref[...]
    later_scr[...] = jnp.zeros(later_scr.shape, F32)
    acc_scr[...] = jnp.zeros(acc_scr.shape, F32)

    def chunk(s0, causal):
        heads = [slice(h * HEAD_DIM, (h + 1) * HEAD_DIM) for h in range(SB_HEADS)]
        zs = [_dot_nt(q_ref[:, cols], k_ref[pl.ds(s0, T), cols]) for cols in heads]
        lms = []
        for z in zs:
            sp = jnp.log2(1.0 + jnp.exp2(-jnp.abs(z)))
            lm = -(jnp.maximum(z, 0.0) + sp)
            if causal is not None:
                lm = jnp.where(causal, lm, 0.0)
            lms.append(lm)
        withins = [_dot(lm.astype(BF16), tri) for lm in lms]
        ws = []
        for h in range(SB_HEADS):
            e = zs[h] + lms[h] + withins[h]
            later = later_scr[h]
            w = jnp.concatenate([jnp.exp2(e[:, u * 128:(u + 1) * 128] + later) for u in range(T // 128)], axis=1)
            if causal is not None:
                w = jnp.where(causal, w, 0.0)
            ws.append(w.astype(BF16))
            later_scr[h] = later + jnp.sum(lms[h], axis=1, keepdims=True)
        for h, cols in enumerate(heads):
            acc_scr[h] += _dot(ws[h], v_ref[pl.ds(s0, T), cols])

    row = lax.broadcasted_iota(jnp.int32, (T, T), 0)
    col = lax.broadcasted_iota(jnp.int32, (T, T), 1)
    chunk(q0, col < row)

    def cond(state):
        n, worst = state
        return (n <= i) & (worst > SB_DEAD_LOG2)

    def body(state):
        n, _ = state
        chunk(pl.multiple_of((i - n) * T, T), None)
        return n + 1, jnp.max(later_scr[...])

    lax.while_loop(cond, body, (jnp.int32(1), jnp.max(later_scr[...])))
    for h in range(SB_HEADS):
        o_ref[:, h * HEAD_DIM:(h + 1) * HEAD_DIM] = acc_scr[h].astype(o_ref.dtype)


def sb_attention(q, k, v):
    S = q.shape[0]
    T = SB_T
    tri = jnp.tril(jnp.ones((T, T), F32), -1).astype(BF16)
    return pl.pallas_call(
        _sb_kernel,
        grid=(S // T,),
        in_specs=[pl.BlockSpec((T, SB_W), lambda i: (i, 0)),
                  _resident((S, SB_W)), _resident((S, SB_W)), _resident((T, T))],
        out_specs=pl.BlockSpec((T, SB_W), lambda i: (i, 0)),
        out_shape=jax.ShapeDtypeStruct((S, SB_W), BF16),
        scratch_shapes=[pltpu.VMEM((SB_HEADS, T, 128), F32),
                        pltpu.VMEM((SB_HEADS, T, HEAD_DIM), F32)],
        compiler_params=_params(1),
        name="sb_attention",
    )(q, k, v, tri)


DSA_T = 256
DSA_C = 512
DSA_CH = 256
DSA_BITS_PER_CHECK = 4
DSA_COUNT_UNROLL = 4


def _dsa_kernel(n_top, q_ref, iq_ref, iw_ref, k_ref, v_ref, ik_ref, tri_ref, o_ref,
                key_scr, m_scr, acc_scr):
    i = pl.program_id(0)
    T, C, CH = DSA_T, DSA_C, DSA_CH
    H = DSA_HEADS
    q0 = i * T
    n_chunks = (q0 + T + C - 1) // C
    tpos = q0 + lax.broadcasted_iota(jnp.int32, (T, 1), 0)
    iw = iw_ref[...]
    iq4 = jnp.concatenate([iq_ref[:, h * DSA_IDX_DIM:(h + 1) * DSA_IDX_DIM] for h in range(DSA_IDX_HEADS)], axis=0)

    def index_keys(c, causal):
        s0 = pl.multiple_of(c * C, C)
        d = _dot_nt(iq4, ik_ref[pl.ds(s0, C), :])
        acc = jnp.zeros((T, C), F32)
        for h in range(DSA_IDX_HEADS):
            acc = acc + jnp.maximum(d[h * T:(h + 1) * T], 0.0) * iw[:, h:h + 1]
        bits = lax.bitcast_convert_type(acc, jnp.int32)
        key = jnp.where(bits < 0, bits ^ jnp.int32(0x7FFFFFFF), bits)
        key = jnp.where(acc == 0.0, 0, key)
        if causal:
            spos = s0 + lax.broadcasted_iota(jnp.int32, (T, C), 1)
            key = jnp.where(spos <= tpos, key, INT32_MIN)
        key_scr[c] = key

    def idx_body(c, carry):
        index_keys(c, False)
        return carry

    lax.fori_loop(0, n_chunks - 1, idx_body, 0)
    index_keys(n_chunks - 1, True)

    def count_ge(cand):
        counts = []
        for r0 in range(0, T, 128):
            cand_r = cand[r0:r0 + 128]

            def body(c, acc):
                kk = key_scr[c, r0:r0 + 128, :]
                for u in range(C // 128):
                    acc = acc + jnp.where(kk[:, u * 128:(u + 1) * 128] >= cand_r, 1.0, 0.0)
                return acc

            def multi_body(p, acc):
                for k in range(DSA_COUNT_UNROLL):
                    acc = body(DSA_COUNT_UNROLL * p + k, acc)
                return acc
            n_multi = n_chunks // DSA_COUNT_UNROLL
            acc = lax.fori_loop(0, n_multi, multi_body, jnp.zeros((128, 128), F32))
            acc = lax.fori_loop(DSA_COUNT_UNROLL * n_multi, n_chunks, body, acc)
            counts.append(jnp.broadcast_to(jnp.sum(acc, axis=1, keepdims=True), (128, 128)))
        return jnp.concatenate(counts, axis=0)

    ge0 = count_ge(jnp.zeros((T, 128), jnp.int32))
    gt0 = count_ge(jnp.ones((T, 128), jnp.int32))
    base0 = jnp.where(ge0 >= n_top, 0, jnp.full((T, 128), INT32_MIN, jnp.int32))
    done0 = ((gt0 < n_top) & (ge0 >= n_top)) | (ge0 == n_top) | (tpos + 1 < n_top)
    done0 = jnp.where(done0, 1.0, 0.0)

    def try_bit(bit, base, done):
        cand = base + (jnp.int32(1) << bit)
        cnt = count_ge(cand)
        return jnp.where((cnt >= n_top) & (done < 0.5), cand, base), jnp.where(cnt == n_top, 1.0, done)

    def high_body(step, state):
        return try_bit(jnp.int32(30) - step, *state)

    base, done = lax.fori_loop(0, 15, high_body, (base0, done0))

    def low_cond(state):
        b, _, _, n_open = state
        return (b >= 0) & (n_open > 0.0)

    def low_body(state):
        b, base, done, _ = state
        for step in range(DSA_BITS_PER_CHECK):
            base, done = try_bit(b - step, base, done)
        return b - DSA_BITS_PER_CHECK, base, done, jnp.sum(1.0 - done)

    _, thr, _, _ = lax.while_loop(low_cond, low_body, (jnp.int32(15), base, done, jnp.sum(1.0 - done)))
    thr = jnp.maximum(thr, INT32_MIN + 1)
    need = n_top - count_ge(thr + 1)
    tri = tri_ref[...]

    _flash_init(m_scr, acc_scr)
    qs = _stack_heads(q_ref, H)

    def att_body(c, eq_seen):
        kk = key_scr[c]
        for half in range(C // CH):
            s0 = pl.multiple_of(c * C + half * CH, CH)
            eqs, gts = [], []
            for u in range(CH // 128):
                ku = kk[:, half * CH + u * 128:half * CH + (u + 1) * 128]
                eqs.append(ku == thr)
                gts.append(ku > thr)
            eq = jnp.concatenate([jnp.where(e, 1.0, 0.0) for e in eqs], axis=1).astype(BF16)
            pref = _dot(eq, tri)
            bias = [jnp.where(gts[u] | (eqs[u] & (eq_seen + pref[:, u * 128:(u + 1) * 128] <= need)),
                              0.0, MASKED) for u in range(CH // 128)]
            eq_seen = eq_seen + pref[:, CH:]
            _flash_step(qs, k_ref[pl.ds(s0, CH), :], jnp.concatenate(bias, axis=1), v_ref[pl.ds(s0, CH), :],
                        m_scr, acc_scr, H)
        return eq_seen

    lax.fori_loop(0, n_chunks, att_body, jnp.zeros((T, 128), F32))
    out = _flash_result(acc_scr)
    for r in range(H):
        o_ref[:, r * HEAD_DIM:(r + 1) * HEAD_DIM] = out[r * T:(r + 1) * T].astype(o_ref.dtype)


def dsa_attention(q, k, v, iq, ik, iw):
    S = q.shape[0]
    T, C, CH = DSA_T, DSA_C, DSA_CH
    n_top = min(DSA_TOPK, S // 4)
    tri = jnp.concatenate([jnp.triu(jnp.ones((CH, CH), F32)), jnp.ones((CH, 128), F32)], axis=1).astype(BF16)
    return pl.pallas_call(
        functools.partial(_dsa_kernel, n_top),
        grid=(S // T,),
        in_specs=[pl.BlockSpec((T, DSA_Q_W), lambda i: (i, 0)),
                  pl.BlockSpec((T, DSA_IQ_W), lambda i: (i, 0)),
                  pl.BlockSpec((T, DSA_IDX_HEADS), lambda i: (i, 0)),
                  _resident((S, HEAD_DIM)), _resident((S, 2 * HEAD_DIM)), _resident((S, DSA_IDX_DIM)),
                  _resident((CH, CH + 128))],
        out_specs=pl.BlockSpec((T, DSA_Q_W), lambda i: (i, 0)),
        out_shape=jax.ShapeDtypeStruct((S, DSA_Q_W), BF16),
        scratch_shapes=[pltpu.VMEM((S // C, T, C), jnp.int32),
                        pltpu.VMEM((DSA_HEADS * T, HEAD_DIM), F32),
                        pltpu.VMEM((DSA_HEADS * T, 2 * HEAD_DIM), F32)],
        compiler_params=_params(1),
        name="dsa_attention",
    )(q, iq, iw, k, v, ik, tri)


NSA_T = 512
NSA_C = 512
NSA_CH = 256
NSA_GROUP = 128


def _nsa_kernel(q_ref, g_ref, kc_ref, vc_ref, ov_ref, ks_ref, vs_ref, kw_ref, vw_ref, e_ref, o_ref,
                sel_scr, m_scr, acc_scr):
    i = pl.program_id(0)
    T, C, CH = NSA_T, NSA_C, NSA_CH
    H = NSA_HEADS
    q0 = pl.multiple_of(i * T, T)
    ncp = kc_ref.shape[0]
    ns = ov_ref.shape[1]
    tpos = q0 + lax.broadcasted_iota(jnp.int32, (T, 1), 0)
    tpos4 = _tile_rows(tpos, H)
    qs = _stack_heads(q_ref, H)

    s_c = _dot_nt(qs, kc_ref[...])
    cmp_end = lax.broadcasted_iota(jnp.int32, (H * T, ncp), 1) * NSA_CMP_STRIDE + (NSA_CMP_LEN - 1)
    p_c = _masked_softmax(s_c, cmp_end <= tpos4)
    o_c = _dot(p_c.astype(BF16), vc_ref[...])

    psum = p_c[0:T]
    for r in range(1, H):
        psum = psum + p_c[r * T:(r + 1) * T]
    ov = ov_ref[...]
    hi = psum.astype(BF16)
    r1 = psum - hi.astype(F32)
    mid = r1.astype(BF16)
    lo = (r1 - mid.astype(F32)).astype(BF16)
    imp = _dot(hi, ov) + _dot(mid, ov) + _dot(lo, ov)

    blk = lax.broadcasted_iota(jnp.int32, (T, ns), 1)
    cur = tpos >> (NSA_SLC_LEN.bit_length() - 1)
    forced = (blk == 0) | (blk == cur) | (blk == cur - 1)
    imp = jnp.where(forced, NSA_FORCED_SCORE, imp)
    imp = jnp.where(blk * NSA_SLC_LEN <= tpos, imp, MASKED)
    blk_f = blk.astype(F32)
    sel = jnp.zeros((T, ns), F32)
    for _ in range(min(NSA_N_SEL, ns)):
        mx = jnp.max(imp, axis=1, keepdims=True)
        first = jnp.min(jnp.where(imp == mx, blk_f, float(ns)), axis=1, keepdims=True)
        hit = blk_f == first
        sel = jnp.where(hit, 1.0, sel)
        imp = jnp.where(hit, EXTRACTED, imp)
    for g in range(ns // NSA_GROUP):
        sel_scr[g] = sel[:, g * NSA_GROUP:(g + 1) * NSA_GROUP].astype(BF16)

    _flash_init(m_scr, acc_scr)
    chunks_per_group = NSA_GROUP * NSA_SLC_LEN // C

    def sel_body(c, carry):
        s0 = pl.multiple_of(c * C, C)
        member = _dot(sel_scr[c // chunks_per_group], e_ref[c % chunks_per_group])
        spos = s0 + lax.broadcasted_iota(jnp.int32, (T, C), 1)
        bias = jnp.where((member > 0.5) & (spos <= tpos), 0.0, MASKED)
        for half in range(C // CH):
            h0 = pl.multiple_of(s0 + half * CH, CH)
            _flash_step(qs, ks_ref[pl.ds(h0, CH), :], bias[:, half * CH:(half + 1) * CH], vs_ref[pl.ds(h0, CH), :],
                        m_scr, acc_scr, H)
        return carry

    lax.fori_loop(0, (q0 + T + C - 1) // C, sel_body, 0)
    o_s = _flash_result(acc_scr)

    nw = NSA_WINDOW + T
    s_w = _dot_nt(qs, kw_ref[pl.ds(q0, nw), :])
    wpos = q0 - NSA_WINDOW + lax.broadcasted_iota(jnp.int32, (H * T, nw), 1)
    dist = tpos4 - wpos
    p_w = _masked_softmax(s_w, (dist >= 0) & (dist < NSA_WINDOW) & (wpos >= 0))
    o_w = _dot(p_w.astype(BF16), vw_ref[pl.ds(q0, nw), :])

    gate = jax.nn.sigmoid(g_ref[...])
    for r in range(H):
        rows = slice(r * T, (r + 1) * T)
        o = (gate[:, 3 * r:3 * r + 1] * o_c[rows] + gate[:, 3 * r + 1:3 * r + 2] * o_s[rows]
             + gate[:, 3 * r + 2:3 * r + 3] * o_w[rows])
        o_ref[:, r * HEAD_DIM:(r + 1) * HEAD_DIM] = o.astype(o_ref.dtype)


def nsa_attention(q, g, k_cmp, v_cmp, ks, vs, kw, vw):
    S = q.shape[0]
    T, C = NSA_T, NSA_C
    n_cmp = k_cmp.shape[0]
    ncp = -(-n_cmp // 128) * 128
    ns = S // NSA_SLC_LEN
    k_cmp = jnp.pad(k_cmp, ((0, ncp - n_cmp), (0, 0)))
    v_cmp = jnp.pad(v_cmp, ((0, ncp - n_cmp), (0, 0)))
    cmp_start = np.arange(ncp) * NSA_CMP_STRIDE
    slc_start = np.arange(ns) * NSA_SLC_LEN
    overlap = ((cmp_start[:, None] < slc_start[None, :] + NSA_SLC_LEN)
               & (cmp_start[:, None] + NSA_CMP_LEN - 1 >= slc_start[None, :]))
    overlap = jnp.asarray(overlap, BF16)
    cpg = NSA_GROUP * NSA_SLC_LEN // C
    tok_blk = (np.arange(cpg)[:, None] * C + np.arange(C)[None, :]) // NSA_SLC_LEN
    expand = jnp.asarray(np.arange(NSA_GROUP)[None, :, None] == tok_blk[:, None, :], BF16)
    kw = jnp.pad(kw, ((NSA_WINDOW, 0), (0, 0)))
    vw = jnp.pad(vw, ((NSA_WINDOW, 0), (0, 0)))
    return pl.pallas_call(
        _nsa_kernel,
        grid=(S // T,),
        in_specs=[pl.BlockSpec((T, NSA_Q_W), lambda i: (i, 0)),
                  pl.BlockSpec((T, NSA_HEADS * 3), lambda i: (i, 0)),
                  _resident((ncp, HEAD_DIM)), _resident((ncp, HEAD_DIM)), _resident((ncp, ns)),
                  _resident((S, HEAD_DIM)), _resident((S, 2 * HEAD_DIM)),
                  _resident((S + NSA_WINDOW, HEAD_DIM)), _resident((S + NSA_WINDOW, HEAD_DIM)),
                  _resident((cpg, NSA_GROUP, C))],
        out_specs=pl.BlockSpec((T, NSA_Q_W), lambda i: (i, 0)),
        out_shape=jax.ShapeDtypeStruct((S, NSA_Q_W), BF16),
        scratch_shapes=[pltpu.VMEM((ns // NSA_GROUP, T, NSA_GROUP), BF16),
                        pltpu.VMEM((NSA_HEADS * T, HEAD_DIM), F32),
                        pltpu.VMEM((NSA_HEADS * T, 2 * HEAD_DIM), F32)],
        compiler_params=_params(1),
        name="nsa_attention",
    )(q, g, k_cmp, v_cmp, overlap, ks, vs, kw, vw, expand)


def nsa_compress(x, pe, w1, w2):
    S = x.shape[0]
    half = NSA_CMP_STRIDE * HEAD_DIM
    x16 = x.reshape(S // NSA_CMP_STRIDE, half)
    pe_flat = pe.reshape(NSA_CMP_LEN * HEAD_DIM)
    w1_flat = w1.reshape(NSA_CMP_LEN * HEAD_DIM, HEAD_DIM)
    first = _mm(x16 + pe_flat[None, :half], w1_flat[:half])
    second = _mm(x16 + pe_flat[None, half:], w1_flat[half:])
    hid = jax.nn.gelu(first[:-1] + second[1:])
    hid = jnp.pad(hid, ((0, 1), (0, 0)))
    return _mm(hid, w2)[:-1]


DIL_T = 256


def _dil_kernel(q_ref, k_ref, v_ref, o_ref):
    i = pl.program_id(0)
    T = DIL_T
    q0 = pl.multiple_of(i * T, T)
    tl = lax.broadcasted_iota(jnp.int32, (T, 1), 0)
    for s in range(DIL_HEADS_PER_GROUP):
        cols = slice(s * HEAD_DIM, (s + 1) * HEAD_DIM)
        parts = []
        m = jnp.full((T, 1), MASKED, F32)
        for g, (w, r) in enumerate(DIL_PATTERNS):
            n = w + T
            start = pl.multiple_of(q0 + (DIL_MAX_WINDOW - w), 128)
            head = g * DIL_HEADS_PER_GROUP + s
            sc = _dot_nt(q_ref[:, head * HEAD_DIM:(head + 1) * HEAD_DIM], k_ref[pl.ds(start, n), cols]) * ATT_SCALE
            b = lax.broadcasted_iota(jnp.int32, (T, n), 1)
            d = tl + w - b
            mask = (d >= 0) & (d <= w) & ((d & (r - 1)) == 0) & (q0 - w + b >= 0)
            sm = jnp.where(mask, sc, MASKED)
            m = jnp.maximum(m, jnp.max(sm, axis=1, keepdims=True))
            parts.append((sm, mask, start, n))
        num = jnp.zeros((T, HEAD_DIM), F32)
        den = jnp.zeros((T, 1), F32)
        for sm, mask, start, n in parts:
            e = jnp.where(mask, jnp.exp(sm - m), 0.0)
            den = den + jnp.sum(e, axis=1, keepdims=True)
            num = num + _dot(e.astype(BF16), v_ref[pl.ds(start, n), cols])
        o_ref[:, cols] = (num / den).astype(o_ref.dtype)


def dilated_attention(q, k, v):
    S = q.shape[0]
    T = DIL_T
    k = jnp.pad(k, ((DIL_MAX_WINDOW, 0), (0, 0)))
    v = jnp.pad(v, ((DIL_MAX_WINDOW, 0), (0, 0)))
    return pl.pallas_call(
        _dil_kernel,
        grid=(S // T,),
        in_specs=[pl.BlockSpec((T, DIL_Q_W), lambda i: (i, 0)),
                  _resident((S + DIL_MAX_WINDOW, DIL_KV_W)), _resident((S + DIL_MAX_WINDOW, DIL_KV_W))],
        out_specs=pl.BlockSpec((T, DIL_KV_W), lambda i: (i, 0)),
        out_shape=jax.ShapeDtypeStruct((S, DIL_KV_W), BF16),
        compiler_params=_params(1),
        name="dilated_attention",
    )(q, k, v)


NORM_TM = 512
NORM_TN = 512


def _norm_mm_kernel(x_ref, g_ref, b_ref, o_ref, xn_ref):
    @pl.when(pl.program_id(1) == 0)
    def _():
        x = x_ref[...]
        y = x * lax.rsqrt(jnp.mean(x * x, axis=-1, keepdims=True) + NORM_EPS)
        xn_ref[...] = (y * g_ref[...]).astype(xn_ref.dtype)
    o_ref[...] = _dot(xn_ref[...], b_ref[...]).astype(o_ref.dtype)


def norm_mm(x, gain, w, out_dtype, return_xn=False):
    M, K = x.shape
    N = w.shape[1]
    tm, tn = min(NORM_TM, M), min(NORM_TN, N)
    main_spec = pl.BlockSpec((tm, tn), lambda i, j: (i, j))
    xn_spec = pl.BlockSpec((tm, K), lambda i, j: (i, 0))
    main_shape = jax.ShapeDtypeStruct((M, N), out_dtype)
    return pl.pallas_call(
        _norm_mm_kernel,
        grid=(M // tm, N // tn),
        in_specs=[xn_spec, pl.BlockSpec((1, K), lambda i, j: (0, 0)), pl.BlockSpec((K, tn), lambda i, j: (0, j))],
        out_specs=(main_spec, xn_spec) if return_xn else main_spec,
        out_shape=(main_shape, jax.ShapeDtypeStruct((M, K), BF16)) if return_xn else main_shape,
        scratch_shapes=[] if return_xn else [pltpu.VMEM((tm, K), BF16)],
        compiler_params=_params(2),
        name="norm_mm",
    )(x, gain.reshape(1, K).astype(F32), w)


PREP_T = 256
PREP_SEGMENTS = (
    ("z", GATE_RANK, None, False, 1.0, "bf16"),
    ("a_q", SB_W, None, False, LOG2_SCALE, "bf16"),
    ("a_k", SB_W, None, False, 1.0, "bf16"),
    ("a_v", SB_W, None, False, 1.0, "bf16"),
    ("b_q", DSA_Q_W, 0, True, LOG2_SCALE, "bf16"),
    ("b_k", DSA_KV_W, 1, True, 1.0, "bf16"),
    ("b_v", DSA_KV_W, None, False, 1.0, "ones"),
    ("b_iq", DSA_IQ_W, None, False, 1.0, "bf16"),
    ("b_ik", HEAD_DIM, None, False, 1.0, "ik"),
    ("c_q", NSA_Q_W, 2, True, LOG2_SCALE, "bf16"),
    ("c_kc", NSA_KV_W, None, True, 1.0, "f32"),
    ("c_vc", NSA_KV_W, None, False, 1.0, "f32"),
    ("c_ks", NSA_KV_W, 4, True, 1.0, "bf16"),
    ("c_vs", NSA_KV_W, None, False, 1.0, "ones"),
    ("c_kw", NSA_KV_W, 5, True, 1.0, "bf16"),
    ("c_vw", NSA_KV_W, None, False, 1.0, "bf16"),
    ("d_q", DIL_Q_W, 6, True, 1.0, "bf16"),
    ("d_k", DIL_KV_W, 7, True, 1.0, "bf16"),
    ("d_v", DIL_KV_W, None, False, 1.0, "bf16"),
    ("misc", HEAD_DIM, None, False, 1.0, "misc"),
)
PREP_WIDTH = sum(seg[1] for seg in PREP_SEGMENTS)
N_MISC = DSA_IDX_HEADS + NSA_HEADS * 3


def _prep_outputs(S):
    shapes = []
    for name, width, _, _, _, kind in PREP_SEGMENTS:
        if kind == "bf16":
            shapes.append((width, BF16))
        elif kind == "ones":
            shapes.append((width + HEAD_DIM, BF16))
        elif kind == "f32":
            shapes.append((width, F32))
        elif kind == "ik":
            shapes.append((DSA_IDX_DIM, BF16))
        else:
            shapes.append((N_MISC, F32))
    return shapes


def _prep_kernel(p_ref, g_ref, cs_ref, sn_ref, *o_refs):
    cs = cs_ref[...]
    sn = sn_ref[...]
    off = 0
    for (name, width, gi, rope, scale, kind), o_ref in zip(PREP_SEGMENTS, o_refs):
        if kind == "ik":
            o_ref[...] = p_ref[:, off:off + DSA_IDX_DIM].astype(o_ref.dtype)
        elif kind == "misc":
            o_ref[...] = p_ref[:, off:off + N_MISC]
        else:
            for hd in range(width // HEAD_DIM):
                lanes = slice(hd * HEAD_DIM, (hd + 1) * HEAD_DIM)
                x = p_ref[:, off + hd * HEAD_DIM:off + (hd + 1) * HEAD_DIM]
                if gi is not None:
                    x = x * lax.rsqrt(jnp.mean(x * x, axis=-1, keepdims=True) + NORM_EPS) * g_ref[gi:gi + 1, :]
                if rope:
                    x = x * cs + pltpu.roll(x, HEAD_DIM // 2, 1) * sn
                if scale != 1.0:
                    x = x * scale
                o_ref[:, lanes] = x.astype(o_ref.dtype)
            if kind == "ones":
                o_ref[:, width:] = jnp.ones((o_ref.shape[0], HEAD_DIM), o_ref.dtype)
        off += width


def prepare_operands(proj, qk_gain, cs, sn):
    S = proj.shape[0]
    T = min(PREP_T, S)
    row = lambda i: (i, 0)
    shapes = _prep_outputs(S)
    outs = pl.pallas_call(
        _prep_kernel,
        grid=(S // T,),
        in_specs=[pl.BlockSpec((T, PREP_WIDTH), row),
                  pl.BlockSpec((N_QK_GAINS, HEAD_DIM), lambda i: (0, 0)),
                  pl.BlockSpec((T, HEAD_DIM), row), pl.BlockSpec((T, HEAD_DIM), row)],
        out_specs=tuple(pl.BlockSpec((T, w), row) for w, _ in shapes),
        out_shape=tuple(jax.ShapeDtypeStruct((S, w), dt) for w, dt in shapes),
        compiler_params=_params(1),
        name="prepare_operands",
    )(proj, qk_gain, cs, sn)
    return {seg[0]: o for seg, o in zip(PREP_SEGMENTS, outs)}


def fused_in_weights(w_in, gate_down):
    names = ("a_q", "a_k", "a_v", "b_q", "b_k", "b_v", "b_iq", "b_ik", "b_iw",
             "c_q", "c_kc", "c_vc", "c_ks", "c_vs", "c_kw", "c_vw", "c_g", "d_q", "d_k", "d_v")
    parts = dict(zip(names, jnp.split(w_in, SPLIT_OFFSETS, axis=1)))
    parts["z"] = gate_down
    parts["misc"] = jnp.concatenate([parts["b_iw"], parts["c_g"]], axis=1)
    cols = []
    for name, width, *_ in PREP_SEGMENTS:
        p = parts[name]
        cols.append(jnp.pad(p, ((0, 0), (0, width - p.shape[1]))))
    return jnp.concatenate(cols, axis=1).astype(BF16)


def token_mixing(h, cs, sn, norm_gain, w_in, qk_gain, nsa_pe, nsa_w1, nsa_w2, gate_down, gate_up, w_branch, w_out):
    proj = norm_mm(h, norm_gain, fused_in_weights(w_in, gate_down), F32)
    t = prepare_operands(proj, qk_gain, cs, sn)
    o_a = sb_attention(t["a_q"], t["a_k"], t["a_v"])
    o_b = dsa_attention(t["b_q"], t["b_k"], t["b_v"], t["b_iq"], t["b_ik"], t["misc"][:, :DSA_IDX_HEADS])
    k_cmp = rmsnorm(nsa_compress(t["c_kc"], nsa_pe[0], nsa_w1[0], nsa_w2[0]), qk_gain[3])
    v_cmp = nsa_compress(t["c_vc"], nsa_pe[1], nsa_w1[1], nsa_w2[1])
    o_c = nsa_attention(t["c_q"], t["misc"][:, DSA_IDX_HEADS:], k_cmp.astype(BF16), v_cmp.astype(BF16),
                        t["c_ks"], t["c_vs"], t["c_kw"], t["c_vw"])
    o_d = dilated_attention(t["d_q"], t["d_k"], t["d_v"])
    merged = gated_merge(t["z"], (o_a, o_b, o_c, o_d), gate_up, w_branch)
    return _mm(merged, w_out, residual=h)


MERGE_TM = 512
MERGE_TN = 512


def _merge_kernel(z_ref, oa_ref, ob_ref, oc_ref, od_ref, gu_ref, wb_ref, o_ref):
    z = z_ref[...]
    acc = None
    off = 0
    for i, br_ref in enumerate((oa_ref, ob_ref, oc_ref, od_ref)):
        width = BRANCH_WIDTHS[i]
        gate = jax.nn.sigmoid(_dot(z, gu_ref[i]))
        term = gate * _dot(br_ref[...], wb_ref[off:off + width, :])
        acc = term if acc is None else acc + term
        off += width
    o_ref[...] = acc.astype(o_ref.dtype)


def gated_merge(z, branches, gate_up, w_branch):
    S = z.shape[0]
    D = w_branch.shape[1]
    tm, tn = min(MERGE_TM, S), min(MERGE_TN, D)
    row = lambda i, j: (i, 0)
    return pl.pallas_call(
        _merge_kernel,
        grid=(S // tm, D // tn),
        in_specs=[pl.BlockSpec((tm, GATE_RANK), row)]
                 + [pl.BlockSpec((tm, w), row) for w in BRANCH_WIDTHS]
                 + [pl.BlockSpec((N_BRANCHES, GATE_RANK, tn), lambda i, j: (0, 0, j)),
                    pl.BlockSpec((BRANCH_WIDTH, tn), lambda i, j: (0, j))],
        out_specs=pl.BlockSpec((tm, tn), lambda i, j: (i, j)),
        out_shape=jax.ShapeDtypeStruct((S, D), BF16),
        compiler_params=_params(2),
        name="gated_merge",
    )(z, *branches, gate_up.astype(BF16), w_branch.astype(BF16))


PEER_ST = 256
PEER_TM = 512
PEER_EC = 512
PEER_HALF = PEER_KEY_DIM // 2
PEER_STAT_ROWS = 32
assert (PEER_TOPK + 1) // 9 == 1 and PEER_TOPK + 1 <= 24


def _top_rows(scores, n, n_rows):
    st = scores.shape[1]
    rid = lax.broadcasted_iota(jnp.int32, (n_rows, st), 0)
    out = jnp.full((n_rows, st), EXTRACTED, F32)
    for r in range(n):
        m = jnp.max(scores, axis=0, keepdims=True)
        out = jnp.where(rid == r, m, out)
        scores = jnp.where(scores == m, EXTRACTED, scores)
    return out


def _peer_stats_kernel(q_ref, w1_ref, w2_ref, o_ref):
    st = q_ref.shape[0]
    k = PEER_TOPK
    rid = lax.broadcasted_iota(jnp.int32, (PEER_STAT_ROWS, st), 0)
    rid8 = lax.broadcasted_iota(jnp.int32, (8, st), 0)
    stats = jnp.zeros((PEER_STAT_ROWS, st), F32)
    for h in range(PEER_HEADS):
        qh = q_ref[:, h * PEER_KEY_DIM:(h + 1) * PEER_KEY_DIM]
        v1 = _top_rows(_dot_nt(w1_ref[h], qh), k + 1, 24)
        v2 = _top_rows(_dot_nt(w2_ref[h], qh), k + 1, 24)
        pieces = [v1[0:1] + v2]
        for a in range(1, 8):
            pieces.append(jnp.where(rid8 < (k + 1) // (a + 1), v1[a:a + 1] + v2[0:8], EXTRACTED))
        pieces.append(v1[8:24] + v2[0:1])
        tops = _top_rows(jnp.concatenate(pieces, axis=0), k + 1, 24)
        c1 = tops[0:1]
        den = jnp.sum(jnp.exp(tops[0:k] - c1), axis=0, keepdims=True)
        thr = 0.5 * (tops[k - 1:k] + tops[k:k + 1])
        stats = jnp.where(rid == h, thr, stats)
        stats = jnp.where(rid == PEER_HEADS + h, c1, stats)
        stats = jnp.where(rid == 2 * PEER_HEADS + h, 1.0 / den, stats)
    o_ref[...] = stats


def _peer_weight_kernel(xn_ref, ut_ref, q_ref, kf_ref, st_ref, o_ref):
    act = jax.nn.gelu(_dot(xn_ref[...], ut_ref[...]))
    st = st_ref[...]
    w = jnp.zeros(act.shape, F32)
    for h in range(PEER_HEADS):
        sf = _dot(q_ref[:, h * PEER_KEY_DIM:(h + 1) * PEER_KEY_DIM], kf_ref[h])
        gate = jnp.exp(sf - st[:, PEER_HEADS + h:PEER_HEADS + h + 1]) * st[:, 2 * PEER_HEADS + h:2 * PEER_HEADS + h + 1]
        w = w + jnp.where(sf >= st[:, h:h + 1], gate, 0.0)
    o_ref[...] = (w * act).astype(o_ref.dtype)


def peer_ffn(h, norm_gain, wq, subkeys, u, v):
    S, D = h.shape
    nk, half = PEER_N_KEYS, PEER_HALF
    q, xn = norm_mm(h, norm_gain, wq.astype(BF16), BF16, return_xn=True)
    zeros = jnp.zeros((PEER_HEADS, nk, half), F32)
    w1 = jnp.concatenate([subkeys[:, 0], zeros], axis=-1).astype(BF16)
    w2 = jnp.concatenate([zeros, subkeys[:, 1]], axis=-1).astype(BF16)
    st_t = pl.pallas_call(
        _peer_stats_kernel,
        grid=(S // PEER_ST,),
        in_specs=[pl.BlockSpec((PEER_ST, PEER_HEADS * PEER_KEY_DIM), lambda i: (i, 0)),
                  pl.BlockSpec((PEER_HEADS, nk, PEER_KEY_DIM), lambda i: (0, 0, 0)),
                  pl.BlockSpec((PEER_HEADS, nk, PEER_KEY_DIM), lambda i: (0, 0, 0))],
        out_specs=pl.BlockSpec((PEER_STAT_ROWS, PEER_ST), lambda i: (0, i)),
        out_shape=jax.ShapeDtypeStruct((PEER_STAT_ROWS, S), F32),
        compiler_params=_params(1),
        name="peer_stats",
    )(q, w1, w2)
    stats = st_t.T
    top = jnp.repeat(jnp.swapaxes(subkeys[:, 0], 1, 2), nk, axis=2)
    bot = jnp.tile(jnp.swapaxes(subkeys[:, 1], 1, 2), (1, 1, nk))
    kfull = jnp.concatenate([top, bot], axis=1).astype(BF16)
    tm, ec = min(PEER_TM, S), PEER_EC
    wmat = pl.pallas_call(
        _peer_weight_kernel,
        grid=(S // tm, PEER_EXPERTS // ec),
        in_specs=[pl.BlockSpec((tm, D), lambda i, j: (i, 0)),
                  pl.BlockSpec((D, ec), lambda i, j: (0, j)),
                  pl.BlockSpec((tm, PEER_HEADS * PEER_KEY_DIM), lambda i, j: (i, 0)),
                  pl.BlockSpec((PEER_HEADS, PEER_KEY_DIM, ec), lambda i, j: (0, 0, j)),
                  pl.BlockSpec((tm, PEER_STAT_ROWS), lambda i, j: (i, 0))],
        out_specs=pl.BlockSpec((tm, ec), lambda i, j: (i, j)),
        out_shape=jax.ShapeDtypeStruct((S, PEER_EXPERTS), BF16),
        compiler_params=_params(2),
        name="peer_weights",
    )(xn, u.T.astype(BF16), q, kfull, stats)
    return _mm(wmat, v, residual=h)


def kernel(x, norm_mix, w_in, qk_gain, nsa_pe, nsa_w1, nsa_w2, gate_down, gate_up, w_branch, w_out,
           norm_ffn, peer_wq, peer_subkeys, peer_u, peer_v):
    S = x.shape[1]
    cos, sin = rope_tables(S, x.dtype)
    cos, sin = cos.reshape(S, HEAD_DIM // 2), sin.reshape(S, HEAD_DIM // 2)
    cs = jnp.concatenate([cos, cos], axis=1)
    sn = jnp.concatenate([-sin, sin], axis=1)
    h = x[0]
    for l in range(DEPTH):
        h = token_mixing(h, cs, sn, norm_mix[l], w_in[l], qk_gain[l], nsa_pe[l], nsa_w1[l], nsa_w2[l],
                         gate_down[l], gate_up[l], w_branch[l], w_out[l])
        h = peer_ffn(h, norm_ffn[l], peer_wq[l], peer_subkeys[l], peer_u[l], peer_v[l])
    return h[None]
```

```python
import math, functools
import jax, jax.numpy as jnp
from jax import lax
import numpy as np
from jax.experimental import pallas as pl
from jax.experimental.pallas import tpu as pltpu

D_MODEL = 4096
BATCH = 1
SEQ = 16384
DEPTH = 4

HEAD_DIM = 128
ROPE_THETA = 10000.0
NORM_EPS = 1e-6
SB_HEADS = 4
DSA_HEADS = 4
DSA_KV_HEADS = 1
DSA_IDX_HEADS = 4
DSA_IDX_DIM = 64
DSA_TOPK = 256
NSA_HEADS = 4
NSA_KV_HEADS = 1
NSA_CMP_LEN = 32
NSA_CMP_STRIDE = 16
NSA_SLC_LEN = 32
NSA_N_SEL = 8
NSA_WINDOW = 512
NSA_FORCED_SCORE = 1e4
DIL_PATTERNS = ((128, 1), (512, 4), (2048, 16))
DIL_GROUPS = len(DIL_PATTERNS)
DIL_HEADS_PER_GROUP = 2
DIL_MAX_WINDOW = max(w for w, _ in DIL_PATTERNS)
N_BRANCHES = 4
GATE_RANK = 256
PEER_HEADS = 8
PEER_N_KEYS = 64
PEER_EXPERTS = PEER_N_KEYS ** 2
PEER_KEY_DIM = 128
PEER_TOPK = 16

SB_W = SB_HEADS * HEAD_DIM
DSA_Q_W = DSA_HEADS * HEAD_DIM
DSA_KV_W = DSA_KV_HEADS * HEAD_DIM
DSA_IQ_W = DSA_IDX_HEADS * DSA_IDX_DIM
NSA_Q_W = NSA_HEADS * HEAD_DIM
NSA_KV_W = NSA_KV_HEADS * HEAD_DIM
DIL_Q_W = DIL_GROUPS * DIL_HEADS_PER_GROUP * HEAD_DIM
DIL_KV_W = DIL_HEADS_PER_GROUP * HEAD_DIM
IN_SPLITS = (SB_W, SB_W, SB_W,
             DSA_Q_W, DSA_KV_W, DSA_KV_W, DSA_IQ_W, DSA_IDX_DIM, DSA_IDX_HEADS,
             NSA_Q_W, NSA_KV_W, NSA_KV_W, NSA_KV_W, NSA_KV_W, NSA_KV_W, NSA_KV_W, NSA_HEADS * 3,
             DIL_Q_W, DIL_KV_W, DIL_KV_W)
IN_WIDTH = sum(IN_SPLITS)
SPLIT_OFFSETS = tuple(int(o) for o in np.cumsum(IN_SPLITS)[:-1])
BRANCH_WIDTHS = (SB_W, DSA_Q_W, NSA_Q_W, DIL_KV_W)
BRANCH_WIDTH = sum(BRANCH_WIDTHS)
BRANCH_OFFSETS = tuple(int(o) for o in np.cumsum(BRANCH_WIDTHS)[:-1])
N_QK_GAINS = 8

ATT_SCALE = HEAD_DIM ** -0.5
LOG2_SCALE = ATT_SCALE * math.log2(math.e)
MASKED = -1e30
EXTRACTED = -3e38
INT32_MIN = -2 ** 31
F32_TINY = float(np.finfo(np.float32).tiny)
VMEM_LIMIT = 56 * 1024 * 1024

BF16 = jnp.bfloat16
F32 = jnp.float32


def _params(n_grid):
    return pltpu.CompilerParams(dimension_semantics=("arbitrary",) * n_grid, vmem_limit_bytes=VMEM_LIMIT)


def _resident(shape):
    nd = len(shape)
    return pl.BlockSpec(shape, lambda *_: (0,) * nd, pipeline_mode=pl.Buffered(1))


def _dot_nt(a, b):
    return lax.dot_general(a, b, (((1,), (1,)), ((), ())), preferred_element_type=F32)


def _dot(a, b):
    return jnp.dot(a, b, preferred_element_type=F32)


def _mm_kernel(a_ref, b_ref, o_ref):
    o_ref[...] = _dot(a_ref[...], b_ref[...]).astype(o_ref.dtype)


def _mm_res_kernel(a_ref, b_ref, r_ref, o_ref):
    o_ref[...] = (r_ref[...] + _dot(a_ref[...], b_ref[...])).astype(o_ref.dtype)


def _pick(n, prefs):
    for p in prefs:
        if n % p == 0:
            return p
    return n


def _mm(a, b, out_dtype=F32, residual=None):
    lead = a.shape[:-1]
    K = a.shape[-1]
    N = b.shape[-1]
    a2 = a.reshape(-1, K).astype(jnp.bfloat16)
    M = a2.shape[0]
    n_pad = (-N) % 128
    b2 = b.astype(jnp.bfloat16)
    if n_pad:
        b2 = jnp.pad(b2, ((0, 0), (0, n_pad)))
    Np = N + n_pad
    tm = _pick(M, (1024, 512, 256, 128))
    tn = _pick(Np, (512, 256, 128))
    in_specs = [pl.BlockSpec((tm, K), lambda i, j: (i, 0)),
                pl.BlockSpec((K, tn), lambda i, j: (0, j))]
    args = [a2, b2]
    body = _mm_kernel
    if residual is not None:
        assert n_pad == 0
        in_specs.append(pl.BlockSpec((tm, tn), lambda i, j: (i, j)))
        args.append(residual.reshape(M, N))
        body = _mm_res_kernel
    out = pl.pallas_call(
        body,
        grid=(M // tm, Np // tn),
        in_specs=in_specs,
        out_specs=pl.BlockSpec((tm, tn), lambda i, j: (i, j)),
        out_shape=jax.ShapeDtypeStruct((M, Np), out_dtype),
        compiler_params=pltpu.CompilerParams(
            dimension_semantics=("arbitrary", "arbitrary"),
            vmem_limit_bytes=VMEM_LIMIT),
        name="dense_mm",
    )(*args)
    if n_pad:
        out = out[:, :N]
    return out.reshape(*lead, N)


def rmsnorm(x, g):
    xf = x.astype(jnp.float32)
    y = xf * lax.rsqrt(jnp.mean(xf * xf, axis=-1, keepdims=True) + NORM_EPS)
    return (y * g.astype(jnp.float32)).astype(x.dtype)


def rope_tables(seq, dtype):
    inv = ROPE_THETA ** (-jnp.arange(0, HEAD_DIM, 2, dtype=jnp.float32) / HEAD_DIM)
    ang = jnp.arange(seq, dtype=jnp.float32)[:, None] * inv[None, :]
    return jnp.cos(ang)[None, :, None, :].astype(dtype), jnp.sin(ang)[None, :, None, :].astype(dtype)


def _stack_heads(q_ref, n_heads):
    return jnp.concatenate([q_ref[:, r * HEAD_DIM:(r + 1) * HEAD_DIM] for r in range(n_heads)], axis=0)


def _tile_rows(x, n):
    return jnp.concatenate([x] * n, axis=0)


def _flash_init(m_scr, acc_scr):
    m_scr[...] = jnp.full(m_scr.shape, MASKED, F32)
    acc_scr[...] = jnp.zeros(acc_scr.shape, F32)


def _flash_step(qs, k_chunk, bias, v_aug, m_scr, acc_scr, n_heads):
    T, C = bias.shape
    s = _dot_nt(qs, k_chunk)
    ps, alphas = [], []
    for r in range(n_heads):
        rows = slice(r * T, (r + 1) * T)
        sm = s[rows] + bias
        m_old = m_scr[rows]
        m_new = jnp.maximum(m_old, jnp.max(sm, axis=1, keepdims=True))
        m_scr[rows] = m_new
        p = [jnp.exp2(sm[:, u * 128:(u + 1) * 128] - m_new) for u in range(C // 128)]
        ps.append(jnp.concatenate(p, axis=1).astype(BF16))
        alpha = jnp.exp2(m_old - m_new)
        alphas.append(jnp.concatenate([alpha, alpha], axis=1))
    pv = _dot(jnp.concatenate(ps, axis=0), v_aug)
    acc_scr[...] = jnp.concatenate(alphas, axis=0) * acc_scr[...] + pv


def _flash_result(acc_scr):
    acc = acc_scr[...]
    return acc[:, :HEAD_DIM] / acc[:, HEAD_DIM:]


def _masked_softmax(s, mask):
    sm = jnp.where(mask, s, MASKED)
    m = jnp.max(sm, axis=1, keepdims=True)
    e = jnp.where(mask, jnp.exp2(sm - m), 0.0)
    den = jnp.sum(e, axis=1, keepdims=True)
    return e * (1.0 / jnp.maximum(den, F32_TINY))


SB_T = 256
SB_DEAD_LOG2 = -200.0


def _sb_kernel(q_ref, k_ref, v_ref, tri_ref, o_ref, later_scr, acc_scr):
    i = pl.program_id(0)
    T = SB_T
    q0 = pl.multiple_of(i * T, T)
    tri = tri_ref[...]
    later_scr[...] = jnp.zeros(later_scr.shape, F32)
    acc_scr[...] = jnp.zeros(acc_scr.shape, F32)

    def chunk(s0, causal):
        heads = [slice(h * HEAD_DIM, (h + 1) * HEAD_DIM) for h in range(SB_HEADS)]
        zs = [_dot_nt(q_ref[:, cols], k_ref[pl.ds(s0, T), cols]) for cols in heads]
        lms = []
        for z in zs:
            sp = jnp.log2(1.0 + jnp.exp2(-jnp.abs(z)))
            lm = -(jnp.maximum(z, 0.0) + sp)
            if causal is not None:
                lm = jnp.where(causal, lm, 0.0)
            lms.append(lm)
        withins = [_dot(lm.astype(BF16), tri) for lm in lms]
        ws = []
        for h in range(SB_HEADS):
            e = zs[h] + lms[h] + withins[h]
            later = later_scr[h]
            w = jnp.concatenate([jnp.exp2(e[:, u * 128:(u + 1) * 128] + later) for u in range(T // 128)], axis=1)
            if causal is not None:
                w = jnp.where(causal, w, 0.0)
            ws.append(w.astype(BF16))
            later_scr[h] = later + jnp.sum(lms[h], axis=1, keepdims=True)
        for h, cols in enumerate(heads):
            acc_scr[h] += _dot(ws[h], v_ref[pl.ds(s0, T), cols])

    row = lax.broadcasted_iota(jnp.int32, (T, T), 0)
    col = lax.broadcasted_iota(jnp.int32, (T, T), 1)
    chunk(q0, col < row)

    def cond(state):
        n, worst = state
        return (n <= i) & (worst > SB_DEAD_LOG2)

    def body(state):
        n, _ = state
        chunk(pl.multiple_of((i - n) * T, T), None)
        return n + 1, jnp.max(later_scr[...])

    lax.while_loop(cond, body, (jnp.int32(1), jnp.max(later_scr[...])))
    for h in range(SB_HEADS):
        o_ref[:, h * HEAD_DIM:(h + 1) * HEAD_DIM] = acc_scr[h].astype(o_ref.dtype)


def sb_attention(q, k, v):
    S = q.shape[0]
    T = SB_T
    tri = jnp.tril(jnp.ones((T, T), F32), -1).astype(BF16)
    return pl.pallas_call(
        _sb_kernel,
        grid=(S // T,),
        in_specs=[pl.BlockSpec((T, SB_W), lambda i: (i, 0)),
                  _resident((S, SB_W)), _resident((S, SB_W)), _resident((T, T))],
        out_specs=pl.BlockSpec((T, SB_W), lambda i: (i, 0)),
        out_shape=jax.ShapeDtypeStruct((S, SB_W), BF16),
        scratch_shapes=[pltpu.VMEM((SB_HEADS, T, 128), F32),
                        pltpu.VMEM((SB_HEADS, T, HEAD_DIM), F32)],
        compiler_params=_params(1),
        name="sb_attention",
    )(q, k, v, tri)


DSA_T = 256
DSA_C = 512
DSA_CH = 256
DSA_BITS_PER_CHECK = 4
DSA_COUNT_UNROLL = 4


def _dsa_kernel(n_top, q_ref, iq_ref, iw_ref, k_ref, v_ref, ik_ref, tri_ref, o_ref,
                key_scr, m_scr, acc_scr):
    i = pl.program_id(0)
    T, C, CH = DSA_T, DSA_C, DSA_CH
    H = DSA_HEADS
    q0 = i * T
    n_chunks = (q0 + T + C - 1) // C
    tpos = q0 + lax.broadcasted_iota(jnp.int32, (T, 1), 0)
    iw = iw_ref[...]
    iq4 = jnp.concatenate([iq_ref[:, h * DSA_IDX_DIM:(h + 1) * DSA_IDX_DIM] for h in range(DSA_IDX_HEADS)], axis=0)

    def index_keys(c, causal):
        s0 = pl.multiple_of(c * C, C)
        d = _dot_nt(iq4, ik_ref[pl.ds(s0, C), :])
        acc = jnp.zeros((T, C), F32)
        for h in range(DSA_IDX_HEADS):
            acc = acc + jnp.maximum(d[h * T:(h + 1) * T], 0.0) * iw[:, h:h + 1]
        bits = lax.bitcast_convert_type(acc, jnp.int32)
        key = jnp.where(bits < 0, bits ^ jnp.int32(0x7FFFFFFF), bits)
        key = jnp.where(acc == 0.0, 0, key)
        if causal:
            spos = s0 + lax.broadcasted_iota(jnp.int32, (T, C), 1)
            key = jnp.where(spos <= tpos, key, INT32_MIN)
        key_scr[c] = key

    def idx_body(c, carry):
        index_keys(c, False)
        return carry

    def idx_pair_body(p, carry):
        index_keys(2 * p, False)
        index_keys(2 * p + 1, False)
        return carry

    n_pairs = (n_chunks - 1) // 2
    lax.fori_loop(0, n_pairs, idx_pair_body, 0)
    lax.fori_loop(2 * n_pairs, n_chunks - 1, idx_body, 0)
    index_keys(n_chunks - 1, True)

    def count_ge(cand):
        counts = []
        for r0 in range(0, T, 128):
            cand_r = cand[r0:r0 + 128]

            def body(c, acc):
                kk = key_scr[c, r0:r0 + 128, :]
                for u in range(C // 128):
                    acc = acc + jnp.where(kk[:, u * 128:(u + 1) * 128] >= cand_r, 1.0, 0.0)
                return acc

            def multi_body(p, acc):
                for k in range(DSA_COUNT_UNROLL):
                    acc = body(DSA_COUNT_UNROLL * p + k, acc)
                return acc
            n_multi = n_chunks // DSA_COUNT_UNROLL
            acc = lax.fori_loop(0, n_multi, multi_body, jnp.zeros((128, 128), F32))
            acc = lax.fori_loop(DSA_COUNT_UNROLL * n_multi, n_chunks, body, acc)
            counts.append(jnp.broadcast_to(jnp.sum(acc, axis=1, keepdims=True), (128, 128)))
        return jnp.concatenate(counts, axis=0)

    ge0 = count_ge(jnp.zeros((T, 128), jnp.int32))
    gt0 = count_ge(jnp.ones((T, 128), jnp.int32))
    base0 = jnp.where(ge0 >= n_top, 0, jnp.full((T, 128), INT32_MIN, jnp.int32))
    done0 = ((gt0 < n_top) & (ge0 >= n_top)) | (ge0 == n_top) | (tpos + 1 < n_top)
    done0 = jnp.where(done0, 1.0, 0.0)

    def try_bit(bit, base, done):
        cand = base + (jnp.int32(1) << bit)
        cnt = count_ge(cand)
        return jnp.where((cnt >= n_top) & (done < 0.5), cand, base), jnp.where(cnt == n_top, 1.0, done)

    def high_body(step, state):
        return try_bit(jnp.int32(30) - step, *state)

    base, done = lax.fori_loop(0, 15, high_body, (base0, done0))

    def low_cond(state):
        b, _, _, n_open = state
        return (b >= 0) & (n_open > 0.0)

    def low_body(state):
        b, base, done, _ = state
        for step in range(DSA_BITS_PER_CHECK):
            base, done = try_bit(b - step, base, done)
        return b - DSA_BITS_PER_CHECK, base, done, jnp.sum(1.0 - done)

    _, thr, _, _ = lax.while_loop(low_cond, low_body, (jnp.int32(15), base, done, jnp.sum(1.0 - done)))
    thr = jnp.maximum(thr, INT32_MIN + 1)
    need = n_top - count_ge(thr + 1)
    tri = tri_ref[...]

    _flash_init(m_scr, acc_scr)
    qs = _stack_heads(q_ref, H)

    def att_body(c, eq_seen):
        kk = key_scr[c]
        for half in range(C // CH):
            s0 = pl.multiple_of(c * C + half * CH, CH)
            eqs, gts = [], []
            for u in range(CH // 128):
                ku = kk[:, half * CH + u * 128:half * CH + (u + 1) * 128]
                eqs.append(ku == thr)
                gts.append(ku > thr)
            eq = jnp.concatenate([jnp.where(e, 1.0, 0.0) for e in eqs], axis=1).astype(BF16)
            pref = _dot(eq, tri)
            bias = [jnp.where(gts[u] | (eqs[u] & (eq_seen + pref[:, u * 128:(u + 1) * 128] <= need)),
                              0.0, MASKED) for u in range(CH // 128)]
            eq_seen = eq_seen + pref[:, CH:]
            _flash_step(qs, k_ref[pl.ds(s0, CH), :], jnp.concatenate(bias, axis=1), v_ref[pl.ds(s0, CH), :],
                        m_scr, acc_scr, H)
        return eq_seen

    lax.fori_loop(0, n_chunks, att_body, jnp.zeros((T, 128), F32))
    out = _flash_result(acc_scr)
    for r in range(H):
        o_ref[:, r * HEAD_DIM:(r + 1) * HEAD_DIM] = out[r * T:(r + 1) * T].astype(o_ref.dtype)


def dsa_attention(q, k, v, iq, ik, iw):
    S = q.shape[0]
    T, C, CH = DSA_T, DSA_C, DSA_CH
    n_top = min(DSA_TOPK, S // 4)
    tri = jnp.concatenate([jnp.triu(jnp.ones((CH, CH), F32)), jnp.ones((CH, 128), F32)], axis=1).astype(BF16)
    return pl.pallas_call(
        functools.partial(_dsa_kernel, n_top),
        grid=(S // T,),
        in_specs=[pl.BlockSpec((T, DSA_Q_W), lambda i: (i, 0)),
                  pl.BlockSpec((T, DSA_IQ_W), lambda i: (i, 0)),
                  pl.BlockSpec((T, DSA_IDX_HEADS), lambda i: (i, 0)),
                  _resident((S, HEAD_DIM)), _resident((S, 2 * HEAD_DIM)), _resident((S, DSA_IDX_DIM)),
                  _resident((CH, CH + 128))],
        out_specs=pl.BlockSpec((T, DSA_Q_W), lambda i: (i, 0)),
        out_shape=jax.ShapeDtypeStruct((S, DSA_Q_W), BF16),
        scratch_shapes=[pltpu.VMEM((S // C, T, C), jnp.int32),
                        pltpu.VMEM((DSA_HEADS * T, HEAD_DIM), F32),
                        pltpu.VMEM((DSA_HEADS * T, 2 * HEAD_DIM), F32)],
        compiler_params=_params(1),
        name="dsa_attention",
    )(q, iq, iw, k, v, ik, tri)


NSA_T = 512
NSA_C = 512
NSA_CH = 256
NSA_GROUP = 128


def _nsa_kernel(q_ref, g_ref, kc_ref, vc_ref, ov_ref, ks_ref, vs_ref, kw_ref, vw_ref, e_ref, o_ref,
                sel_scr, m_scr, acc_scr):
    i = pl.program_id(0)
    T, C, CH = NSA_T, NSA_C, NSA_CH
    H = NSA_HEADS
    q0 = pl.multiple_of(i * T, T)
    ncp = kc_ref.shape[0]
    ns = ov_ref.shape[1]
    tpos = q0 + lax.broadcasted_iota(jnp.int32, (T, 1), 0)
    tpos4 = _tile_rows(tpos, H)
    qs = _stack_heads(q_ref, H)

    s_c = _dot_nt(qs, kc_ref[...])
    cmp_end = lax.broadcasted_iota(jnp.int32, (H * T, ncp), 1) * NSA_CMP_STRIDE + (NSA_CMP_LEN - 1)
    p_c = _masked_softmax(s_c, cmp_end <= tpos4)
    o_c = _dot(p_c.astype(BF16), vc_ref[...])

    psum = p_c[0:T]
    for r in range(1, H):
        psum = psum + p_c[r * T:(r + 1) * T]
    ov = ov_ref[...]
    hi = psum.astype(BF16)
    r1 = psum - hi.astype(F32)
    mid = r1.astype(BF16)
    lo = (r1 - mid.astype(F32)).astype(BF16)
    imp = _dot(hi, ov) + _dot(mid, ov) + _dot(lo, ov)

    blk = lax.broadcasted_iota(jnp.int32, (T, ns), 1)
    cur = tpos >> (NSA_SLC_LEN.bit_length() - 1)
    forced = (blk == 0) | (blk == cur) | (blk == cur - 1)
    imp = jnp.where(forced, NSA_FORCED_SCORE, imp)
    imp = jnp.where(blk * NSA_SLC_LEN <= tpos, imp, MASKED)
    blk_f = blk.astype(F32)
    sel = jnp.zeros((T, ns), F32)
    for _ in range(min(NSA_N_SEL, ns)):
        mx = jnp.max(imp, axis=1, keepdims=True)
        first = jnp.min(jnp.where(imp == mx, blk_f, float(ns)), axis=1, keepdims=True)
        hit = blk_f == first
        sel = jnp.where(hit, 1.0, sel)
        imp = jnp.where(hit, EXTRACTED, imp)
    for g in range(ns // NSA_GROUP):
        sel_scr[g] = sel[:, g * NSA_GROUP:(g + 1) * NSA_GROUP].astype(BF16)

    _flash_init(m_scr, acc_scr)
    chunks_per_group = NSA_GROUP * NSA_SLC_LEN // C

    def sel_body(c, carry):
        s0 = pl.multiple_of(c * C, C)
        member = _dot(sel_scr[c // chunks_per_group], e_ref[c % chunks_per_group])
        spos = s0 + lax.broadcasted_iota(jnp.int32, (T, C), 1)
        bias = jnp.where((member > 0.5) & (spos <= tpos), 0.0, MASKED)
        for half in range(C // CH):
            h0 = pl.multiple_of(s0 + half * CH, CH)
            _flash_step(qs, ks_ref[pl.ds(h0, CH), :], bias[:, half * CH:(half + 1) * CH], vs_ref[pl.ds(h0, CH), :],
                        m_scr, acc_scr, H)
        return carry

    lax.fori_loop(0, (q0 + T + C - 1) // C, sel_body, 0)
    o_s = _flash_result(acc_scr)

    nw = NSA_WINDOW + T
    s_w = _dot_nt(qs, kw_ref[pl.ds(q0, nw), :])
    wpos = q0 - NSA_WINDOW + lax.broadcasted_iota(jnp.int32, (H * T, nw), 1)
    dist = tpos4 - wpos
    p_w = _masked_softmax(s_w, (dist >= 0) & (dist < NSA_WINDOW) & (wpos >= 0))
    o_w = _dot(p_w.astype(BF16), vw_ref[pl.ds(q0, nw), :])

    gate = jax.nn.sigmoid(g_ref[...])
    for r in range(H):
        rows = slice(r * T, (r + 1) * T)
        o = (gate[:, 3 * r:3 * r + 1] * o_c[rows] + gate[:, 3 * r + 1:3 * r + 2] * o_s[rows]
             + gate[:, 3 * r + 2:3 * r + 3] * o_w[rows])
        o_ref[:, r * HEAD_DIM:(r + 1) * HEAD_DIM] = o.astype(o_ref.dtype)


def nsa_attention(q, g, k_cmp, v_cmp, ks, vs, kw, vw):
    S = q.shape[0]
    T, C = NSA_T, NSA_C
    n_cmp = k_cmp.shape[0]
    ncp = -(-n_cmp // 128) * 128
    ns = S // NSA_SLC_LEN
    k_cmp = jnp.pad(k_cmp, ((0, ncp - n_cmp), (0, 0)))
    v_cmp = jnp.pad(v_cmp, ((0, ncp - n_cmp), (0, 0)))
    cmp_start = np.arange(ncp) * NSA_CMP_STRIDE
    slc_start = np.arange(ns) * NSA_SLC_LEN
    overlap = ((cmp_start[:, None] < slc_start[None, :] + NSA_SLC_LEN)
               & (cmp_start[:, None] + NSA_CMP_LEN - 1 >= slc_start[None, :]))
    overlap = jnp.asarray(overlap, BF16)
    cpg = NSA_GROUP * NSA_SLC_LEN // C
    tok_blk = (np.arange(cpg)[:, None] * C + np.arange(C)[None, :]) // NSA_SLC_LEN
    expand = jnp.asarray(np.arange(NSA_GROUP)[None, :, None] == tok_blk[:, None, :], BF16)
    kw = jnp.pad(kw, ((NSA_WINDOW, 0), (0, 0)))
    vw = jnp.pad(vw, ((NSA_WINDOW, 0), (0, 0)))
    return pl.pallas_call(
        _nsa_kernel,
        grid=(S // T,),
        in_specs=[pl.BlockSpec((T, NSA_Q_W), lambda i: (i, 0)),
                  pl.BlockSpec((T, NSA_HEADS * 3), lambda i: (i, 0)),
                  _resident((ncp, HEAD_DIM)), _resident((ncp, HEAD_DIM)), _resident((ncp, ns)),
                  _resident((S, HEAD_DIM)), _resident((S, 2 * HEAD_DIM)),
                  _resident((S + NSA_WINDOW, HEAD_DIM)), _resident((S + NSA_WINDOW, HEAD_DIM)),
                  _resident((cpg, NSA_GROUP, C))],
        out_specs=pl.BlockSpec((T, NSA_Q_W), lambda i: (i, 0)),
        out_shape=jax.ShapeDtypeStruct((S, NSA_Q_W), BF16),
        scratch_shapes=[pltpu.VMEM((ns // NSA_GROUP, T, NSA_GROUP), BF16),
                        pltpu.VMEM((NSA_HEADS * T, HEAD_DIM), F32),
                        pltpu.VMEM((NSA_HEADS * T, 2 * HEAD_DIM), F32)],
        compiler_params=_params(1),
        name="nsa_attention",
    )(q, g, k_cmp, v_cmp, overlap, ks, vs, kw, vw, expand)


def nsa_compress(x, pe, w1, w2):
    S = x.shape[0]
    half = NSA_CMP_STRIDE * HEAD_DIM
    x16 = x.reshape(S // NSA_CMP_STRIDE, half)
    pe_flat = pe.reshape(NSA_CMP_LEN * HEAD_DIM)
    w1_flat = w1.reshape(NSA_CMP_LEN * HEAD_DIM, HEAD_DIM)
    first = _mm(x16 + pe_flat[None, :half], w1_flat[:half])
    second = _mm(x16 + pe_flat[None, half:], w1_flat[half:])
    hid = jax.nn.gelu(first[:-1] + second[1:])
    hid = jnp.pad(hid, ((0, 1), (0, 0)))
    return _mm(hid, w2)[:-1]


DIL_T = 256


def _dil_kernel(q_ref, k_ref, v_ref, o_ref):
    i = pl.program_id(0)
    T = DIL_T
    q0 = pl.multiple_of(i * T, T)
    tl = lax.broadcasted_iota(jnp.int32, (T, 1), 0)
    for s in range(DIL_HEADS_PER_GROUP):
        cols = slice(s * HEAD_DIM, (s + 1) * HEAD_DIM)
        parts = []
        m = jnp.full((T, 1), MASKED, F32)
        for g, (w, r) in enumerate(DIL_PATTERNS):
            n = w + T
            start = pl.multiple_of(q0 + (DIL_MAX_WINDOW - w), 128)
            head = g * DIL_HEADS_PER_GROUP + s
            sc = _dot_nt(q_ref[:, head * HEAD_DIM:(head + 1) * HEAD_DIM], k_ref[pl.ds(start, n), cols]) * ATT_SCALE
            b = lax.broadcasted_iota(jnp.int32, (T, n), 1)
            d = tl + w - b
            mask = (d >= 0) & (d <= w) & ((d & (r - 1)) == 0) & (q0 - w + b >= 0)
            sm = jnp.where(mask, sc, MASKED)
            m = jnp.maximum(m, jnp.max(sm, axis=1, keepdims=True))
            parts.append((sm, mask, start, n))
        num = jnp.zeros((T, HEAD_DIM), F32)
        den = jnp.zeros((T, 1), F32)
        for sm, mask, start, n in parts:
            e = jnp.where(mask, jnp.exp(sm - m), 0.0)
            den = den + jnp.sum(e, axis=1, keepdims=True)
            num = num + _dot(e.astype(BF16), v_ref[pl.ds(start, n), cols])
        o_ref[:, cols] = (num / den).astype(o_ref.dtype)


def dilated_attention(q, k, v):
    S = q.shape[0]
    T = DIL_T
    k = jnp.pad(k, ((DIL_MAX_WINDOW, 0), (0, 0)))
    v = jnp.pad(v, ((DIL_MAX_WINDOW, 0), (0, 0)))
    return pl.pallas_call(
        _dil_kernel,
        grid=(S // T,),
        in_specs=[pl.BlockSpec((T, DIL_Q_W), lambda i: (i, 0)),
                  _resident((S + DIL_MAX_WINDOW, DIL_KV_W)), _resident((S + DIL_MAX_WINDOW, DIL_KV_W))],
        out_specs=pl.BlockSpec((T, DIL_KV_W), lambda i: (i, 0)),
        out_shape=jax.ShapeDtypeStruct((S, DIL_KV_W), BF16),
        compiler_params=_params(1),
        name="dilated_attention",
    )(q, k, v)


NORM_TM = 512
NORM_TN = 512


def _norm_mm_kernel(x_ref, g_ref, b_ref, o_ref, xn_ref):
    @pl.when(pl.program_id(1) == 0)
    def _():
        x = x_ref[...]
        y = x * lax.rsqrt(jnp.mean(x * x, axis=-1, keepdims=True) + NORM_EPS)
        xn_ref[...] = (y * g_ref[...]).astype(xn_ref.dtype)
    o_ref[...] = _dot(xn_ref[...], b_ref[...]).astype(o_ref.dtype)


def norm_mm(x, gain, w, out_dtype, return_xn=False):
    M, K = x.shape
    N = w.shape[1]
    tm, tn = min(NORM_TM, M), min(NORM_TN, N)
    main_spec = pl.BlockSpec((tm, tn), lambda i, j: (i, j))
    xn_spec = pl.BlockSpec((tm, K), lambda i, j: (i, 0))
    main_shape = jax.ShapeDtypeStruct((M, N), out_dtype)
    return pl.pallas_call(
        _norm_mm_kernel,
        grid=(M // tm, N // tn),
        in_specs=[xn_spec, pl.BlockSpec((1, K), lambda i, j: (0, 0)), pl.BlockSpec((K, tn), lambda i, j: (0, j))],
        out_specs=(main_spec, xn_spec) if return_xn else main_spec,
        out_shape=(main_shape, jax.ShapeDtypeStruct((M, K), BF16)) if return_xn else main_shape,
        scratch_shapes=[] if return_xn else [pltpu.VMEM((tm, K), BF16)],
        compiler_params=_params(2),
        name="norm_mm",
    )(x, gain.reshape(1, K).astype(F32), w)


PREP_T = 256
PREP_SEGMENTS = (
    ("z", GATE_RANK, None, False, 1.0, "bf16"),
    ("a_q", SB_W, None, False, LOG2_SCALE, "bf16"),
    ("a_k", SB_W, None, False, 1.0, "bf16"),
    ("a_v", SB_W, None, False, 1.0, "bf16"),
    ("b_q", DSA_Q_W, 0, True, LOG2_SCALE, "bf16"),
    ("b_k", DSA_KV_W, 1, True, 1.0, "bf16"),
    ("b_v", DSA_KV_W, None, False, 1.0, "ones"),
    ("b_iq", DSA_IQ_W, None, False, 1.0, "bf16"),
    ("b_ik", HEAD_DIM, None, False, 1.0, "ik"),
    ("c_q", NSA_Q_W, 2, True, LOG2_SCALE, "bf16"),
    ("c_kc", NSA_KV_W, None, True, 1.0, "f32"),
    ("c_vc", NSA_KV_W, None, False, 1.0, "f32"),
    ("c_ks", NSA_KV_W, 4, True, 1.0, "bf16"),
    ("c_vs", NSA_KV_W, None, False, 1.0, "ones"),
    ("c_kw", NSA_KV_W, 5, True, 1.0, "bf16"),
    ("c_vw", NSA_KV_W, None, False, 1.0, "bf16"),
    ("d_q", DIL_Q_W, 6, True, 1.0, "bf16"),
    ("d_k", DIL_KV_W, 7, True, 1.0, "bf16"),
    ("d_v", DIL_KV_W, None, False, 1.0, "bf16"),
    ("misc", HEAD_DIM, None, False, 1.0, "misc"),
)
PREP_WIDTH = sum(seg[1] for seg in PREP_SEGMENTS)
N_MISC = DSA_IDX_HEADS + NSA_HEADS * 3


def _prep_outputs(S):
    shapes = []
    for name, width, _, _, _, kind in PREP_SEGMENTS:
        if kind == "bf16":
            shapes.append((width, BF16))
        elif kind == "ones":
            shapes.append((width + HEAD_DIM, BF16))
        elif kind == "f32":
            shapes.append((width, F32))
        elif kind == "ik":
            shapes.append((DSA_IDX_DIM, BF16))
        else:
            shapes.append((N_MISC, F32))
    return shapes


def _prep_kernel(p_ref, g_ref, cs_ref, sn_ref, *o_refs):
    cs = cs_ref[...]
    sn = sn_ref[...]
    off = 0
    for (name, width, gi, rope, scale, kind), o_ref in zip(PREP_SEGMENTS, o_refs):
        if kind == "ik":
            o_ref[...] = p_ref[:, off:off + DSA_IDX_DIM].astype(o_ref.dtype)
        elif kind == "misc":
            o_ref[...] = p_ref[:, off:off + N_MISC]
        else:
            for hd in range(width // HEAD_DIM):
                lanes = slice(hd * HEAD_DIM, (hd + 1) * HEAD_DIM)
                x = p_ref[:, off + hd * HEAD_DIM:off + (hd + 1) * HEAD_DIM]
                if gi is not None:
                    x = x * lax.rsqrt(jnp.mean(x * x, axis=-1, keepdims=True) + NORM_EPS) * g_ref[gi:gi + 1, :]
                if rope:
                    x = x * cs + pltpu.roll(x, HEAD_DIM // 2, 1) * sn
                if scale != 1.0:
                    x = x * scale
                o_ref[:, lanes] = x.astype(o_ref.dtype)
            if kind == "ones":
                o_ref[:, width:] = jnp.ones((o_ref.shape[0], HEAD_DIM), o_ref.dtype)
        off += width


def prepare_operands(proj, qk_gain, cs, sn):
    S = proj.shape[0]
    T = min(PREP_T, S)
    row = lambda i: (i, 0)
    shapes = _prep_outputs(S)
    outs = pl.pallas_call(
        _prep_kernel,
        grid=(S // T,),
        in_specs=[pl.BlockSpec((T, PREP_WIDTH), row),
                  pl.BlockSpec((N_QK_GAINS, HEAD_DIM), lambda i: (0, 0)),
                  pl.BlockSpec((T, HEAD_DIM), row), pl.BlockSpec((T, HEAD_DIM), row)],
        out_specs=tuple(pl.BlockSpec((T, w), row) for w, _ in shapes),
        out_shape=tuple(jax.ShapeDtypeStruct((S, w), dt) for w, dt in shapes),
        compiler_params=_params(1),
        name="prepare_operands",
    )(proj, qk_gain, cs, sn)
    return {seg[0]: o for seg, o in zip(PREP_SEGMENTS, outs)}


def fused_in_weights(w_in, gate_down):
    names = ("a_q", "a_k", "a_v", "b_q", "b_k", "b_v", "b_iq", "b_ik", "b_iw",
             "c_q", "c_kc", "c_vc", "c_ks", "c_vs", "c_kw", "c_vw", "c_g", "d_q", "d_k", "d_v")
    parts = dict(zip(names, jnp.split(w_in, SPLIT_OFFSETS, axis=1)))
    parts["z"] = gate_down
    parts["misc"] = jnp.concatenate([parts["b_iw"], parts["c_g"]], axis=1)
    cols = []
    for name, width, *_ in PREP_SEGMENTS:
        p = parts[name]
        cols.append(jnp.pad(p, ((0, 0), (0, width - p.shape[1]))))
    return jnp.concatenate(cols, axis=1).astype(BF16)


def token_mixing(h, cs, sn, norm_gain, w_in, qk_gain, nsa_pe, nsa_w1, nsa_w2, gate_down, gate_up, w_branch, w_out):
    proj = norm_mm(h, norm_gain, fused_in_weights(w_in, gate_down), F32)
    t = prepare_operands(proj, qk_gain, cs, sn)
    o_a = sb_attention(t["a_q"], t["a_k"], t["a_v"])
    o_b = dsa_attention(t["b_q"], t["b_k"], t["b_v"], t["b_iq"], t["b_ik"], t["misc"][:, :DSA_IDX_HEADS])
    k_cmp = rmsnorm(nsa_compress(t["c_kc"], nsa_pe[0], nsa_w1[0], nsa_w2[0]), qk_gain[3])
    v_cmp = nsa_compress(t["c_vc"], nsa_pe[1], nsa_w1[1], nsa_w2[1])
    o_c = nsa_attention(t["c_q"], t["misc"][:, DSA_IDX_HEADS:], k_cmp.astype(BF16), v_cmp.astype(BF16),
                        t["c_ks"], t["c_vs"], t["c_kw"], t["c_vw"])
    o_d = dilated_attention(t["d_q"], t["d_k"], t["d_v"])
    merged = gated_merge(t["z"], (o_a, o_b, o_c, o_d), gate_up, w_branch)
    return _mm(merged, w_out, residual=h)


MERGE_TM = 512
MERGE_TN = 512


def _merge_kernel(z_ref, oa_ref, ob_ref, oc_ref, od_ref, gu_ref, wb_ref, o_ref):
    z = z_ref[...]
    acc = None
    off = 0
    for i, br_ref in enumerate((oa_ref, ob_ref, oc_ref, od_ref)):
        width = BRANCH_WIDTHS[i]
        gate = jax.nn.sigmoid(_dot(z, gu_ref[i]))
        term = gate * _dot(br_ref[...], wb_ref[off:off + width, :])
        acc = term if acc is None else acc + term
        off += width
    o_ref[...] = acc.astype(o_ref.dtype)


def gated_merge(z, branches, gate_up, w_branch):
    S = z.shape[0]
    D = w_branch.shape[1]
    tm, tn = min(MERGE_TM, S), min(MERGE_TN, D)
    row = lambda i, j: (i, 0)
    return pl.pallas_call(
        _merge_kernel,
        grid=(S // tm, D // tn),
        in_specs=[pl.BlockSpec((tm, GATE_RANK), row)]
                 + [pl.BlockSpec((tm, w), row) for w in BRANCH_WIDTHS]
                 + [pl.BlockSpec((N_BRANCHES, GATE_RANK, tn), lambda i, j: (0, 0, j)),
                    pl.BlockSpec((BRANCH_WIDTH, tn), lambda i, j: (0, j))],
        out_specs=pl.BlockSpec((tm, tn), lambda i, j: (i, j)),
        out_shape=jax.ShapeDtypeStruct((S, D), BF16),
        compiler_params=_params(2),
        name="gated_merge",
    )(z, *branches, gate_up.astype(BF16), w_branch.astype(BF16))


PEER_ST = 256
PEER_TM = 512
PEER_EC = 512
PEER_HALF = PEER_KEY_DIM // 2
PEER_STAT_ROWS = 32
assert (PEER_TOPK + 1) // 9 == 1 and PEER_TOPK + 1 <= 24


def _top_rows(scores, n, n_rows):
    st = scores.shape[1]
    rid = lax.broadcasted_iota(jnp.int32, (n_rows, st), 0)
    out = jnp.full((n_rows, st), EXTRACTED, F32)
    for r in range(n):
        m = jnp.max(scores, axis=0, keepdims=True)
        out = jnp.where(rid == r, m, out)
        scores = jnp.where(scores == m, EXTRACTED, scores)
    return out


def _peer_stats_kernel(q_ref, w1_ref, w2_ref, o_ref):
    st = q_ref.shape[0]
    k = PEER_TOPK
    rid = lax.broadcasted_iota(jnp.int32, (PEER_STAT_ROWS, st), 0)
    rid8 = lax.broadcasted_iota(jnp.int32, (8, st), 0)
    stats = jnp.zeros((PEER_STAT_ROWS, st), F32)
    for h in range(PEER_HEADS):
        qh = q_ref[:, h * PEER_KEY_DIM:(h + 1) * PEER_KEY_DIM]
        v1 = _top_rows(_dot_nt(w1_ref[h], qh), k + 1, 24)
        v2 = _top_rows(_dot_nt(w2_ref[h], qh), k + 1, 24)
        pieces = [v1[0:1] + v2]
        for a in range(1, 8):
            pieces.append(jnp.where(rid8 < (k + 1) // (a + 1), v1[a:a + 1] + v2[0:8], EXTRACTED))
        pieces.append(v1[8:24] + v2[0:1])
        tops = _top_rows(jnp.concatenate(pieces, axis=0), k + 1, 24)
        c1 = tops[0:1]
        den = jnp.sum(jnp.exp(tops[0:k] - c1), axis=0, keepdims=True)
        thr = 0.5 * (tops[k - 1:k] + tops[k:k + 1])
        stats = jnp.where(rid == h, thr, stats)
        stats = jnp.where(rid == PEER_HEADS + h, c1, stats)
        stats = jnp.where(rid == 2 * PEER_HEADS + h, 1.0 / den, stats)
    o_ref[...] = stats


def _peer_weight_kernel(xn_ref, ut_ref, q_ref, kf_ref, st_ref, o_ref):
    act = jax.nn.gelu(_dot(xn_ref[...], ut_ref[...]))
    st = st_ref[...]
    w = jnp.zeros(act.shape, F32)
    for h in range(PEER_HEADS):
        sf = _dot(q_ref[:, h * PEER_KEY_DIM:(h + 1) * PEER_KEY_DIM], kf_ref[h])
        gate = jnp.exp(sf - st[:, PEER_HEADS + h:PEER_HEADS + h + 1]) * st[:, 2 * PEER_HEADS + h:2 * PEER_HEADS + h + 1]
        w = w + jnp.where(sf >= st[:, h:h + 1], gate, 0.0)
    o_ref[...] = (w * act).astype(o_ref.dtype)


def peer_ffn(h, norm_gain, wq, subkeys, u, v):
    S, D = h.shape
    nk, half = PEER_N_KEYS, PEER_HALF
    q, xn = norm_mm(h, norm_gain, wq.astype(BF16), BF16, return_xn=True)
    zeros = jnp.zeros((PEER_HEADS, nk, half), F32)
    w1 = jnp.concatenate([subkeys[:, 0], zeros], axis=-1).astype(BF16)
    w2 = jnp.concatenate([zeros, subkeys[:, 1]], axis=-1).astype(BF16)
    st_t = pl.pallas_call(
        _peer_stats_kernel,
        grid=(S // PEER_ST,),
        in_specs=[pl.BlockSpec((PEER_ST, PEER_HEADS * PEER_KEY_DIM), lambda i: (i, 0)),
                  pl.BlockSpec((PEER_HEADS, nk, PEER_KEY_DIM), lambda i: (0, 0, 0)),
                  pl.BlockSpec((PEER_HEADS, nk, PEER_KEY_DIM), lambda i: (0, 0, 0))],
        out_specs=pl.BlockSpec((PEER_STAT_ROWS, PEER_ST), lambda i: (0, i)),
        out_shape=jax.ShapeDtypeStruct((PEER_STAT_ROWS, S), F32),
        compiler_params=_params(1),
        name="peer_stats",
    )(q, w1, w2)
    stats = st_t.T
    top = jnp.repeat(jnp.swapaxes(subkeys[:, 0], 1, 2), nk, axis=2)
    bot = jnp.tile(jnp.swapaxes(subkeys[:, 1], 1, 2), (1, 1, nk))
    kfull = jnp.concatenate([top, bot], axis=1).astype(BF16)
    tm, ec = min(PEER_TM, S), PEER_EC
    wmat = pl.pallas_call(
        _peer_weight_kernel,
        grid=(S // tm, PEER_EXPERTS // ec),
        in_specs=[pl.BlockSpec((tm, D), lambda i, j: (i, 0)),
                  pl.BlockSpec((D, ec), lambda i, j: (0, j)),
                  pl.BlockSpec((tm, PEER_HEADS * PEER_KEY_DIM), lambda i, j: (i, 0)),
                  pl.BlockSpec((PEER_HEADS, PEER_KEY_DIM, ec), lambda i, j: (0, 0, j)),
                  pl.BlockSpec((tm, PEER_STAT_ROWS), lambda i, j: (i, 0))],
        out_specs=pl.BlockSpec((tm, ec), lambda i, j: (i, j)),
        out_shape=jax.ShapeDtypeStruct((S, PEER_EXPERTS), BF16),
        compiler_params=_params(2),
        name="peer_weights",
    )(xn, u.T.astype(BF16), q, kfull, stats)
    return _mm(wmat, v, residual=h)


def kernel(x, norm_mix, w_in, qk_gain, nsa_pe, nsa_w1, nsa_w2, gate_down, gate_up, w_branch, w_out,
           norm_ffn, peer_wq, peer_subkeys, peer_u, peer_v):
    S = x.shape[1]
    cos, sin = rope_tables(S, x.dtype)
    cos, sin = cos.reshape(S, HEAD_DIM // 2), sin.reshape(S, HEAD_DIM // 2)
    cs = jnp.concatenate([cos, cos], axis=1)
    sn = jnp.concatenate([-sin, sin], axis=1)
    h = x[0]
    for l in range(DEPTH):
        h = token_mixing(h, cs, sn, norm_mix[l], w_in[l], qk_gain[l], nsa_pe[l], nsa_w1[l], nsa_w2[l],
                         gate_down[l], gate_up[l], w_branch[l], w_out[l])
        h = peer_ffn(h, norm_ffn[l], peer_wq[l], peer_subkeys[l], peer_u[l], peer_v[l])
    return h[None]
```

```python
import math, functools
import jax, jax.numpy as jnp
from jax import lax
import numpy as np
from jax.experimental import pallas as pl
from jax.experimental.pallas import tpu as pltpu

D_MODEL = 4096
BATCH = 1
SEQ = 16384
DEPTH = 4

HEAD_DIM = 128
ROPE_THETA = 10000.0
NORM_EPS = 1e-6
SB_HEADS = 4
DSA_HEADS = 4
DSA_KV_HEADS = 1
DSA_IDX_HEADS = 4
DSA_IDX_DIM = 64
DSA_TOPK = 256
NSA_HEADS = 4
NSA_KV_HEADS = 1
NSA_CMP_LEN = 32
NSA_CMP_STRIDE = 16
NSA_SLC_LEN = 32
NSA_N_SEL = 8
NSA_WINDOW = 512
NSA_FORCED_SCORE = 1e4
DIL_PATTERNS = ((128, 1), (512, 4), (2048, 16))
DIL_GROUPS = len(DIL_PATTERNS)
DIL_HEADS_PER_GROUP = 2
DIL_MAX_WINDOW = max(w for w, _ in DIL_PATTERNS)
N_BRANCHES = 4
GATE_RANK = 256
PEER_HEADS = 8
PEER_N_KEYS = 64
PEER_EXPERTS = PEER_N_KEYS ** 2
PEER_KEY_DIM = 128
PEER_TOPK = 16

SB_W = SB_HEADS * HEAD_DIM
DSA_Q_W = DSA_HEADS * HEAD_DIM
DSA_KV_W = DSA_KV_HEADS * HEAD_DIM
DSA_IQ_W = DSA_IDX_HEADS * DSA_IDX_DIM
NSA_Q_W = NSA_HEADS * HEAD_DIM
NSA_KV_W = NSA_KV_HEADS * HEAD_DIM
DIL_Q_W = DIL_GROUPS * DIL_HEADS_PER_GROUP * HEAD_DIM
DIL_KV_W = DIL_HEADS_PER_GROUP * HEAD_DIM
IN_SPLITS = (SB_W, SB_W, SB_W,
             DSA_Q_W, DSA_KV_W, DSA_KV_W, DSA_IQ_W, DSA_IDX_DIM, DSA_IDX_HEADS,
             NSA_Q_W, NSA_KV_W, NSA_KV_W, NSA_KV_W, NSA_KV_W, NSA_KV_W, NSA_KV_W, NSA_HEADS * 3,
             DIL_Q_W, DIL_KV_W, DIL_KV_W)
IN_WIDTH = sum(IN_SPLITS)
SPLIT_OFFSETS = tuple(int(o) for o in np.cumsum(IN_SPLITS)[:-1])
BRANCH_WIDTHS = (SB_W, DSA_Q_W, NSA_Q_W, DIL_KV_W)
BRANCH_WIDTH = sum(BRANCH_WIDTHS)
BRANCH_OFFSETS = tuple(int(o) for o in np.cumsum(BRANCH_WIDTHS)[:-1])
N_QK_GAINS = 8

ATT_SCALE = HEAD_DIM ** -0.5
LOG2_SCALE = ATT_SCALE * math.log2(math.e)
MASKED = -1e30
EXTRACTED = -3e38
INT32_MIN = -2 ** 31
F32_TINY = float(np.finfo(np.float32).tiny)
VMEM_LIMIT = 56 * 1024 * 1024

BF16 = jnp.bfloat16
F32 = jnp.float32


def _params(n_grid):
    return pltpu.CompilerParams(dimension_semantics=("arbitrary",) * n_grid, vmem_limit_bytes=VMEM_LIMIT)


def _resident(shape):
    nd = len(shape)
    return pl.BlockSpec(shape, lambda *_: (0,) * nd, pipeline_mode=pl.Buffered(1))


def _dot_nt(a, b):
    return lax.dot_general(a, b, (((1,), (1,)), ((), ())), preferred_element_type=F32)


def _dot(a, b):
    return jnp.dot(a, b, preferred_element_type=F32)


def _mm_kernel(a_ref, b_ref, o_ref):
    o_ref[...] = _dot(a_ref[...], b_ref[...]).astype(o_ref.dtype)


def _mm_res_kernel(a_ref, b_ref, r_ref, o_ref):
    o_ref[...] = (r_ref[...] + _dot(a_ref[...], b_ref[...])).astype(o_ref.dtype)


def _pick(n, prefs):
    for p in prefs:
        if n % p == 0:
            return p
    return n


def _mm(a, b, out_dtype=F32, residual=None):
    lead = a.shape[:-1]
    K = a.shape[-1]
    N = b.shape[-1]
    a2 = a.reshape(-1, K).astype(jnp.bfloat16)
    M = a2.shape[0]
    n_pad = (-N) % 128
    b2 = b.astype(jnp.bfloat16)
    if n_pad:
        b2 = jnp.pad(b2, ((0, 0), (0, n_pad)))
    Np = N + n_pad
    tm = _pick(M, (1024, 512, 256, 128))
    tn = _pick(Np, (512, 256, 128))
    in_specs = [pl.BlockSpec((tm, K), lambda i, j: (i, 0)),
                pl.BlockSpec((K, tn), lambda i, j: (0, j))]
    args = [a2, b2]
    body = _mm_kernel
    if residual is not None:
        assert n_pad == 0
        in_specs.append(pl.BlockSpec((tm, tn), lambda i, j: (i, j)))
        args.append(residual.reshape(M, N))
        body = _mm_res_kernel
    out = pl.pallas_call(
        body,
        grid=(M // tm, Np // tn),
        in_specs=in_specs,
        out_specs=pl.BlockSpec((tm, tn), lambda i, j: (i, j)),
        out_shape=jax.ShapeDtypeStruct((M, Np), out_dtype),
        compiler_params=pltpu.CompilerParams(
            dimension_semantics=("arbitrary", "arbitrary"),
            vmem_limit_bytes=VMEM_LIMIT),
        name="dense_mm",
    )(*args)
    if n_pad:
        out = out[:, :N]
    return out.reshape(*lead, N)


def rmsnorm(x, g):
    xf = x.astype(jnp.float32)
    y = xf * lax.rsqrt(jnp.mean(xf * xf, axis=-1, keepdims=True) + NORM_EPS)
    return (y * g.astype(jnp.float32)).astype(x.dtype)


def rope_tables(seq, dtype):
    inv = ROPE_THETA ** (-jnp.arange(0, HEAD_DIM, 2, dtype=jnp.float32) / HEAD_DIM)
    ang = jnp.arange(seq, dtype=jnp.float32)[:, None] * inv[None, :]
    return jnp.cos(ang)[None, :, None, :].astype(dtype), jnp.sin(ang)[None, :, None, :].astype(dtype)


def _stack_heads(q_ref, n_heads):
    return jnp.concatenate([q_ref[:, r * HEAD_DIM:(r + 1) * HEAD_DIM] for r in range(n_heads)], axis=0)


def _tile_rows(x, n):
    return jnp.concatenate([x] * n, axis=0)


def _flash_init(m_scr, acc_scr):
    m_scr[...] = jnp.full(m_scr.shape, MASKED, F32)
    acc_scr[...] = jnp.zeros(acc_scr.shape, F32)


def _flash_step(qs, k_chunk, bias, v_aug, m_scr, acc_scr, n_heads):
    T, C = bias.shape
    s = _dot_nt(qs, k_chunk)
    ps, alphas = [], []
    for r in range(n_heads):
        rows = slice(r * T, (r + 1) * T)
        sm = s[rows] + bias
        m_old = m_scr[rows]
        m_new = jnp.maximum(m_old, jnp.max(sm, axis=1, keepdims=True))
        m_scr[rows] = m_new
        p = [jnp.exp2(sm[:, u * 128:(u + 1) * 128] - m_new) for u in range(C // 128)]
        ps.append(jnp.concatenate(p, axis=1).astype(BF16))
        alpha = jnp.exp2(m_old - m_new)
        alphas.append(jnp.concatenate([alpha, alpha], axis=1))
    pv = _dot(jnp.concatenate(ps, axis=0), v_aug)
    acc_scr[...] = jnp.concatenate(alphas, axis=0) * acc_scr[...] + pv


def _flash_result(acc_scr):
    acc = acc_scr[...]
    return acc[:, :HEAD_DIM] / acc[:, HEAD_DIM:]


def _masked_softmax(s, mask):
    sm = jnp.where(mask, s, MASKED)
    m = jnp.max(sm, axis=1, keepdims=True)
    e = jnp.where(mask, jnp.exp2(sm - m), 0.0)
    den = jnp.sum(e, axis=1, keepdims=True)
    return e * (1.0 / jnp.maximum(den, F32_TINY))


SB_T = 256
SB_DEAD_LOG2 = -200.0


def _sb_kernel(q_ref, k_ref, v_ref, tri_ref, o_ref, later_scr, acc_scr):
    i = pl.program_id(0)
    T = SB_T
    q0 = pl.multiple_of(i * T, T)
    tri = tri_ref[...]
    later_scr[...] = jnp.zeros(later_scr.shape, F32)
    acc_scr[...] = jnp.zeros(acc_scr.shape, F32)

    def chunk(s0, causal):
        heads = [slice(h * HEAD_DIM, (h + 1) * HEAD_DIM) for h in range(SB_HEADS)]
        zs = [_dot_nt(q_ref[:, cols], k_ref[pl.ds(s0, T), cols]) for cols in heads]
        lms = []
        for z in zs:
            sp = jnp.log2(1.0 + jnp.exp2(-jnp.abs(z)))
            lm = -(jnp.maximum(z, 0.0) + sp)
            if causal is not None:
                lm = jnp.where(causal, lm, 0.0)
            lms.append(lm)
        withins = [_dot(lm.astype(BF16), tri) for lm in lms]
        ws = []
        for h in range(SB_HEADS):
            e = zs[h] + lms[h] + withins[h]
            later = later_scr[h]
            w = jnp.concatenate([jnp.exp2(e[:, u * 128:(u + 1) * 128] + later) for u in range(T // 128)], axis=1)
            if causal is not None:
                w = jnp.where(causal, w, 0.0)
            ws.append(w.astype(BF16))
            later_scr[h] = later + jnp.sum(lms[h], axis=1, keepdims=True)
        for h, cols in enumerate(heads):
            acc_scr[h] += _dot(ws[h], v_ref[pl.ds(s0, T), cols])

    row = lax.broadcasted_iota(jnp.int32, (T, T), 0)
    col = lax.broadcasted_iota(jnp.int32, (T, T), 1)
    chunk(q0, col < row)

    def cond(state):
        n, worst = state
        return (n <= i) & (worst > SB_DEAD_LOG2)

    def body(state):
        n, _ = state
        chunk(pl.multiple_of((i - n) * T, T), None)
        return n + 1, jnp.max(later_scr[...])

    lax.while_loop(cond, body, (jnp.int32(1), jnp.max(later_scr[...])))
    for h in range(SB_HEADS):
        o_ref[:, h * HEAD_DIM:(h + 1) * HEAD_DIM] = acc_scr[h].astype(o_ref.dtype)


def sb_attention(q, k, v):
    S = q.shape[0]
    T = SB_T
    tri = jnp.tril(jnp.ones((T, T), F32), -1).astype(BF16)
    return pl.pallas_call(
        _sb_kernel,
        grid=(S // T,),
        in_specs=[pl.BlockSpec((T, SB_W), lambda i: (i, 0)),
                  _resident((S, SB_W)), _resident((S, SB_W)), _resident((T, T))],
        out_specs=pl.BlockSpec((T, SB_W), lambda i: (i, 0)),
        out_shape=jax.ShapeDtypeStruct((S, SB_W), BF16),
        scratch_shapes=[pltpu.VMEM((SB_HEADS, T, 128), F32),
                        pltpu.VMEM((SB_HEADS, T, HEAD_DIM), F32)],
        compiler_params=_params(1),
        name="sb_attention",
    )(q, k, v, tri)


DSA_T = 256
DSA_C = 512
DSA_CH = 256
DSA_BITS_PER_CHECK = 4
DSA_COUNT_UNROLL = 4


def _dsa_kernel(n_top, q_ref, iq_ref, iw_ref, k_ref, v_ref, ik_ref, tri_ref, o_ref,
                key_scr, m_scr, acc_scr):
    i = pl.program_id(0)
    T, C, CH = DSA_T, DSA_C, DSA_CH
    H = DSA_HEADS
    q0 = i * T
    n_chunks = (q0 + T + C - 1) // C
    tpos = q0 + lax.broadcasted_iota(jnp.int32, (T, 1), 0)
    iw = iw_ref[...]
    iq4 = jnp.concatenate([iq_ref[:, h * DSA_IDX_DIM:(h + 1) * DSA_IDX_DIM] for h in range(DSA_IDX_HEADS)], axis=0)

    def index_keys(c, causal):
        s0 = pl.multiple_of(c * C, C)
        d = _dot_nt(iq4, ik_ref[pl.ds(s0, C), :])
        acc = jnp.zeros((T, C), F32)
        for h in range(DSA_IDX_HEADS):
            acc = acc + jnp.maximum(d[h * T:(h + 1) * T], 0.0) * iw[:, h:h + 1]
        bits = lax.bitcast_convert_type(acc, jnp.int32)
        key = jnp.where(bits < 0, bits ^ jnp.int32(0x7FFFFFFF), bits)
        key = jnp.where(acc == 0.0, 0, key)
        if causal:
            spos = s0 + lax.broadcasted_iota(jnp.int32, (T, C), 1)
            key = jnp.where(spos <= tpos, key, INT32_MIN)
        key_scr[c] = key

    def idx_body(c, carry):
        index_keys(c, False)
        return carry

    def idx_pair_body(p, carry):
        index_keys(2 * p, False)
        index_keys(2 * p + 1, False)
        return carry

    n_pairs = (n_chunks - 1) // 2
    lax.fori_loop(0, n_pairs, idx_pair_body, 0)
    lax.fori_loop(2 * n_pairs, n_chunks - 1, idx_body, 0)
    index_keys(n_chunks - 1, True)

    def count_ge(cand):
        counts = []
        for r0 in range(0, T, 128):
            cand_r = cand[r0:r0 + 128]

            def body(c, acc):
                kk = key_scr[c, r0:r0 + 128, :]
                for u in range(C // 128):
                    acc = acc + jnp.where(kk[:, u * 128:(u + 1) * 128] >= cand_r, 1.0, 0.0)
                return acc

            def multi_body(p, acc):
                for k in range(DSA_COUNT_UNROLL):
                    acc = body(DSA_COUNT_UNROLL * p + k, acc)
                return acc
            n_multi = n_chunks // DSA_COUNT_UNROLL
            acc = lax.fori_loop(0, n_multi, multi_body, jnp.zeros((128, 128), F32))
            acc = lax.fori_loop(DSA_COUNT_UNROLL * n_multi, n_chunks, body, acc)
            counts.append(jnp.broadcast_to(jnp.sum(acc, axis=1, keepdims=True), (128, 128)))
        return jnp.concatenate(counts, axis=0)

    ge0 = count_ge(jnp.zeros((T, 128), jnp.int32))
    gt0 = count_ge(jnp.ones((T, 128), jnp.int32))
    base0 = jnp.where(ge0 >= n_top, 0, jnp.full((T, 128), INT32_MIN, jnp.int32))
    done0 = ((gt0 < n_top) & (ge0 >= n_top)) | (ge0 == n_top) | (tpos + 1 < n_top)
    done0 = jnp.where(done0, 1.0, 0.0)

    def try_bit(bit, base, done):
        cand = base + (jnp.int32(1) << bit)
        cnt = count_ge(cand)
        return jnp.where((cnt >= n_top) & (done < 0.5), cand, base), jnp.where(cnt == n_top, 1.0, done)

    def high_body(step, state):
        return try_bit(jnp.int32(30) - step, *state)

    base, done = lax.fori_loop(0, 15, high_body, (base0, done0))

    def low_cond(state):
        b, _, _, n_open = state
        return (b >= 0) & (n_open > 0.0)

    def low_body(state):
        b, base, done, _ = state
        for step in range(DSA_BITS_PER_CHECK):
            base, done = try_bit(b - step, base, done)
        return b - DSA_BITS_PER_CHECK, base, done, jnp.sum(1.0 - done)

    _, thr, _, n_open = lax.while_loop(low_cond, low_body, (jnp.int32(15), base, done, jnp.sum(1.0 - done)))
    thr = jnp.maximum(thr, INT32_MIN + 1)
    need = lax.cond(n_open > 0.0,
                    lambda: n_top - count_ge(thr + 1),
                    lambda: jnp.where((gt0 < n_top) & (ge0 >= n_top), n_top - gt0, float(key_scr.shape[0] * C)))
    tri = tri_ref[...]

    _flash_init(m_scr, acc_scr)
    qs = _stack_heads(q_ref, H)

    def att_body(c, eq_seen):
        kk = key_scr[c]
        for half in range(C // CH):
            s0 = pl.multiple_of(c * C + half * CH, CH)
            eqs, gts = [], []
            for u in range(CH // 128):
                ku = kk[:, half * CH + u * 128:half * CH + (u + 1) * 128]
                eqs.append(ku == thr)
                gts.append(ku > thr)
            eq = jnp.concatenate([jnp.where(e, 1.0, 0.0) for e in eqs], axis=1).astype(BF16)
            pref = _dot(eq, tri)
            bias = [jnp.where(gts[u] | (eqs[u] & (eq_seen + pref[:, u * 128:(u + 1) * 128] <= need)),
                              0.0, MASKED) for u in range(CH // 128)]
            eq_seen = eq_seen + pref[:, CH:]
            _flash_step(qs, k_ref[pl.ds(s0, CH), :], jnp.concatenate(bias, axis=1), v_ref[pl.ds(s0, CH), :],
                        m_scr, acc_scr, H)
        return eq_seen

    lax.fori_loop(0, n_chunks, att_body, jnp.zeros((T, 128), F32))
    out = _flash_result(acc_scr)
    for r in range(H):
        o_ref[:, r * HEAD_DIM:(r + 1) * HEAD_DIM] = out[r * T:(r + 1) * T].astype(o_ref.dtype)


def dsa_attention(q, k, v, iq, ik, iw):
    S = q.shape[0]
    T, C, CH = DSA_T, DSA_C, DSA_CH
    n_top = min(DSA_TOPK, S // 4)
    tri = jnp.concatenate([jnp.triu(jnp.ones((CH, CH), F32)), jnp.ones((CH, 128), F32)], axis=1).astype(BF16)
    return pl.pallas_call(
        functools.partial(_dsa_kernel, n_top),
        grid=(S // T,),
        in_specs=[pl.BlockSpec((T, DSA_Q_W), lambda i: (i, 0)),
                  pl.BlockSpec((T, DSA_IQ_W), lambda i: (i, 0)),
                  pl.BlockSpec((T, DSA_IDX_HEADS), lambda i: (i, 0)),
                  _resident((S, HEAD_DIM)), _resident((S, 2 * HEAD_DIM)), _resident((S, DSA_IDX_DIM)),
                  _resident((CH, CH + 128))],
        out_specs=pl.BlockSpec((T, DSA_Q_W), lambda i: (i, 0)),
        out_shape=jax.ShapeDtypeStruct((S, DSA_Q_W), BF16),
        scratch_shapes=[pltpu.VMEM((S // C, T, C), jnp.int32),
                        pltpu.VMEM((DSA_HEADS * T, HEAD_DIM), F32),
                        pltpu.VMEM((DSA_HEADS * T, 2 * HEAD_DIM), F32)],
        compiler_params=_params(1),
        name="dsa_attention",
    )(q, iq, iw, k, v, ik, tri)


NSA_T = 512
NSA_C = 512
NSA_CH = 256
NSA_GROUP = 128


def _nsa_kernel(q_ref, g_ref, kc_ref, vc_ref, ov_ref, ks_ref, vs_ref, kw_ref, vw_ref, e_ref, o_ref,
                sel_scr, m_scr, acc_scr):
    i = pl.program_id(0)
    T, C, CH = NSA_T, NSA_C, NSA_CH
    H = NSA_HEADS
    q0 = pl.multiple_of(i * T, T)
    ncp = kc_ref.shape[0]
    ns = ov_ref.shape[1]
    tpos = q0 + lax.broadcasted_iota(jnp.int32, (T, 1), 0)
    tpos4 = _tile_rows(tpos, H)
    qs = _stack_heads(q_ref, H)

    s_c = _dot_nt(qs, kc_ref[...])
    cmp_end = lax.broadcasted_iota(jnp.int32, (H * T, ncp), 1) * NSA_CMP_STRIDE + (NSA_CMP_LEN - 1)
    p_c = _masked_softmax(s_c, cmp_end <= tpos4)
    o_c = _dot(p_c.astype(BF16), vc_ref[...])

    psum = p_c[0:T]
    for r in range(1, H):
        psum = psum + p_c[r * T:(r + 1) * T]
    ov = ov_ref[...]
    hi = psum.astype(BF16)
    r1 = psum - hi.astype(F32)
    mid = r1.astype(BF16)
    lo = (r1 - mid.astype(F32)).astype(BF16)
    imp = _dot(hi, ov) + _dot(mid, ov) + _dot(lo, ov)

    blk = lax.broadcasted_iota(jnp.int32, (T, ns), 1)
    cur = tpos >> (NSA_SLC_LEN.bit_length() - 1)
    forced = (blk == 0) | (blk == cur) | (blk == cur - 1)
    imp = jnp.where(forced, NSA_FORCED_SCORE, imp)
    imp = jnp.where(blk * NSA_SLC_LEN <= tpos, imp, MASKED)
    blk_f = blk.astype(F32)
    sel = jnp.zeros((T, ns), F32)
    for _ in range(min(NSA_N_SEL, ns)):
        mx = jnp.max(imp, axis=1, keepdims=True)
        first = jnp.min(jnp.where(imp == mx, blk_f, float(ns)), axis=1, keepdims=True)
        hit = blk_f == first
        sel = jnp.where(hit, 1.0, sel)
        imp = jnp.where(hit, EXTRACTED, imp)
    for g in range(ns // NSA_GROUP):
        sel_scr[g] = sel[:, g * NSA_GROUP:(g + 1) * NSA_GROUP].astype(BF16)

    _flash_init(m_scr, acc_scr)
    chunks_per_group = NSA_GROUP * NSA_SLC_LEN // C

    def sel_body(c, carry):
        s0 = pl.multiple_of(c * C, C)
        member = _dot(sel_scr[c // chunks_per_group], e_ref[c % chunks_per_group])
        spos = s0 + lax.broadcasted_iota(jnp.int32, (T, C), 1)
        bias = jnp.where((member > 0.5) & (spos <= tpos), 0.0, MASKED)
        for half in range(C // CH):
            h0 = pl.multiple_of(s0 + half * CH, CH)
            _flash_step(qs, ks_ref[pl.ds(h0, CH), :], bias[:, half * CH:(half + 1) * CH], vs_ref[pl.ds(h0, CH), :],
                        m_scr, acc_scr, H)
        return carry

    lax.fori_loop(0, (q0 + T + C - 1) // C, sel_body, 0)
    o_s = _flash_result(acc_scr)

    nw = NSA_WINDOW + T
    s_w = _dot_nt(qs, kw_ref[pl.ds(q0, nw), :])
    wpos = q0 - NSA_WINDOW + lax.broadcasted_iota(jnp.int32, (H * T, nw), 1)
    dist = tpos4 - wpos
    p_w = _masked_softmax(s_w, (dist >= 0) & (dist < NSA_WINDOW) & (wpos >= 0))
    o_w = _dot(p_w.astype(BF16), vw_ref[pl.ds(q0, nw), :])

    gate = jax.nn.sigmoid(g_ref[...])
    for r in range(H):
        rows = slice(r * T, (r + 1) * T)
        o = (gate[:, 3 * r:3 * r + 1] * o_c[rows] + gate[:, 3 * r + 1:3 * r + 2] * o_s[rows]
             + gate[:, 3 * r + 2:3 * r + 3] * o_w[rows])
        o_ref[:, r * HEAD_DIM:(r + 1) * HEAD_DIM] = o.astype(o_ref.dtype)


def nsa_attention(q, g, k_cmp, v_cmp, ks, vs, kw, vw):
    S = q.shape[0]
    T, C = NSA_T, NSA_C
    n_cmp = k_cmp.shape[0]
    ncp = -(-n_cmp // 128) * 128
    ns = S // NSA_SLC_LEN
    k_cmp = jnp.pad(k_cmp, ((0, ncp - n_cmp), (0, 0)))
    v_cmp = jnp.pad(v_cmp, ((0, ncp - n_cmp), (0, 0)))
    cmp_start = np.arange(ncp) * NSA_CMP_STRIDE
    slc_start = np.arange(ns) * NSA_SLC_LEN
    overlap = ((cmp_start[:, None] < slc_start[None, :] + NSA_SLC_LEN)
               & (cmp_start[:, None] + NSA_CMP_LEN - 1 >= slc_start[None, :]))
    overlap = jnp.asarray(overlap, BF16)
    cpg = NSA_GROUP * NSA_SLC_LEN // C
    tok_blk = (np.arange(cpg)[:, None] * C + np.arange(C)[None, :]) // NSA_SLC_LEN
    expand = jnp.asarray(np.arange(NSA_GROUP)[None, :, None] == tok_blk[:, None, :], BF16)
    kw = jnp.pad(kw, ((NSA_WINDOW, 0), (0, 0)))
    vw = jnp.pad(vw, ((NSA_WINDOW, 0), (0, 0)))
    return pl.pallas_call(
        _nsa_kernel,
        grid=(S // T,),
        in_specs=[pl.BlockSpec((T, NSA_Q_W), lambda i: (i, 0)),
                  pl.BlockSpec((T, NSA_HEADS * 3), lambda i: (i, 0)),
                  _resident((ncp, HEAD_DIM)), _resident((ncp, HEAD_DIM)), _resident((ncp, ns)),
                  _resident((S, HEAD_DIM)), _resident((S, 2 * HEAD_DIM)),
                  _resident((S + NSA_WINDOW, HEAD_DIM)), _resident((S + NSA_WINDOW, HEAD_DIM)),
                  _resident((cpg, NSA_GROUP, C))],
        out_specs=pl.BlockSpec((T, NSA_Q_W), lambda i: (i, 0)),
        out_shape=jax.ShapeDtypeStruct((S, NSA_Q_W), BF16),
        scratch_shapes=[pltpu.VMEM((ns // NSA_GROUP, T, NSA_GROUP), BF16),
                        pltpu.VMEM((NSA_HEADS * T, HEAD_DIM), F32),
                        pltpu.VMEM((NSA_HEADS * T, 2 * HEAD_DIM), F32)],
        compiler_params=_params(1),
        name="nsa_attention",
    )(q, g, k_cmp, v_cmp, overlap, ks, vs, kw, vw, expand)


def nsa_compress(x, pe, w1, w2):
    S = x.shape[0]
    half = NSA_CMP_STRIDE * HEAD_DIM
    x16 = x.reshape(S // NSA_CMP_STRIDE, half)
    pe_flat = pe.reshape(NSA_CMP_LEN * HEAD_DIM)
    w1_flat = w1.reshape(NSA_CMP_LEN * HEAD_DIM, HEAD_DIM)
    first = _mm(x16 + pe_flat[None, :half], w1_flat[:half])
    second = _mm(x16 + pe_flat[None, half:], w1_flat[half:])
    hid = jax.nn.gelu(first[:-1] + second[1:])
    hid = jnp.pad(hid, ((0, 1), (0, 0)))
    return _mm(hid, w2)[:-1]


DIL_T = 256


def _dil_kernel(q_ref, k_ref, v_ref, o_ref):
    i = pl.program_id(0)
    T = DIL_T
    q0 = pl.multiple_of(i * T, T)
    tl = lax.broadcasted_iota(jnp.int32, (T, 1), 0)
    for s in range(DIL_HEADS_PER_GROUP):
        cols = slice(s * HEAD_DIM, (s + 1) * HEAD_DIM)
        parts = []
        m = jnp.full((T, 1), MASKED, F32)
        for g, (w, r) in enumerate(DIL_PATTERNS):
            n = w + T
            start = pl.multiple_of(q0 + (DIL_MAX_WINDOW - w), 128)
            head = g * DIL_HEADS_PER_GROUP + s
            sc = _dot_nt(q_ref[:, head * HEAD_DIM:(head + 1) * HEAD_DIM], k_ref[pl.ds(start, n), cols]) * ATT_SCALE
            b = lax.broadcasted_iota(jnp.int32, (T, n), 1)
            d = tl + w - b
            mask = (d >= 0) & (d <= w) & ((d & (r - 1)) == 0) & (q0 - w + b >= 0)
            sm = jnp.where(mask, sc, MASKED)
            m = jnp.maximum(m, jnp.max(sm, axis=1, keepdims=True))
            parts.append((sm, mask, start, n))
        num = jnp.zeros((T, HEAD_DIM), F32)
        den = jnp.zeros((T, 1), F32)
        for sm, mask, start, n in parts:
            e = jnp.where(mask, jnp.exp(sm - m), 0.0)
            den = den + jnp.sum(e, axis=1, keepdims=True)
            num = num + _dot(e.astype(BF16), v_ref[pl.ds(start, n), cols])
        o_ref[:, cols] = (num / den).astype(o_ref.dtype)


def dilated_attention(q, k, v):
    S = q.shape[0]
    T = DIL_T
    k = jnp.pad(k, ((DIL_MAX_WINDOW, 0), (0, 0)))
    v = jnp.pad(v, ((DIL_MAX_WINDOW, 0), (0, 0)))
    return pl.pallas_call(
        _dil_kernel,
        grid=(S // T,),
        in_specs=[pl.BlockSpec((T, DIL_Q_W), lambda i: (i, 0)),
                  _resident((S + DIL_MAX_WINDOW, DIL_KV_W)), _resident((S + DIL_MAX_WINDOW, DIL_KV_W))],
        out_specs=pl.BlockSpec((T, DIL_KV_W), lambda i: (i, 0)),
        out_shape=jax.ShapeDtypeStruct((S, DIL_KV_W), BF16),
        compiler_params=_params(1),
        name="dilated_attention",
    )(q, k, v)


NORM_TM = 512
NORM_TN = 512


def _norm_mm_kernel(x_ref, g_ref, b_ref, o_ref, xn_ref):
    @pl.when(pl.program_id(1) == 0)
    def _():
        x = x_ref[...]
        y = x * lax.rsqrt(jnp.mean(x * x, axis=-1, keepdims=True) + NORM_EPS)
        xn_ref[...] = (y * g_ref[...]).astype(xn_ref.dtype)
    o_ref[...] = _dot(xn_ref[...], b_ref[...]).astype(o_ref.dtype)


def norm_mm(x, gain, w, out_dtype, return_xn=False):
    M, K = x.shape
    N = w.shape[1]
    tm, tn = min(NORM_TM, M), min(NORM_TN, N)
    main_spec = pl.BlockSpec((tm, tn), lambda i, j: (i, j))
    xn_spec = pl.BlockSpec((tm, K), lambda i, j: (i, 0))
    main_shape = jax.ShapeDtypeStruct((M, N), out_dtype)
    return pl.pallas_call(
        _norm_mm_kernel,
        grid=(M // tm, N // tn),
        in_specs=[xn_spec, pl.BlockSpec((1, K), lambda i, j: (0, 0)), pl.BlockSpec((K, tn), lambda i, j: (0, j))],
        out_specs=(main_spec, xn_spec) if return_xn else main_spec,
        out_shape=(main_shape, jax.ShapeDtypeStruct((M, K), BF16)) if return_xn else main_shape,
        scratch_shapes=[] if return_xn else [pltpu.VMEM((tm, K), BF16)],
        compiler_params=_params(2),
        name="norm_mm",
    )(x, gain.reshape(1, K).astype(F32), w)


PREP_T = 256
PREP_SEGMENTS = (
    ("z", GATE_RANK, None, False, 1.0, "bf16"),
    ("a_q", SB_W, None, False, LOG2_SCALE, "bf16"),
    ("a_k", SB_W, None, False, 1.0, "bf16"),
    ("a_v", SB_W, None, False, 1.0, "bf16"),
    ("b_q", DSA_Q_W, 0, True, LOG2_SCALE, "bf16"),
    ("b_k", DSA_KV_W, 1, True, 1.0, "bf16"),
    ("b_v", DSA_KV_W, None, False, 1.0, "ones"),
    ("b_iq", DSA_IQ_W, None, False, 1.0, "bf16"),
    ("b_ik", HEAD_DIM, None, False, 1.0, "ik"),
    ("c_q", NSA_Q_W, 2, True, LOG2_SCALE, "bf16"),
    ("c_kc", NSA_KV_W, None, True, 1.0, "f32"),
    ("c_vc", NSA_KV_W, None, False, 1.0, "f32"),
    ("c_ks", NSA_KV_W, 4, True, 1.0, "bf16"),
    ("c_vs", NSA_KV_W, None, False, 1.0, "ones"),
    ("c_kw", NSA_KV_W, 5, True, 1.0, "bf16"),
    ("c_vw", NSA_KV_W, None, False, 1.0, "bf16"),
    ("d_q", DIL_Q_W, 6, True, 1.0, "bf16"),
    ("d_k", DIL_KV_W, 7, True, 1.0, "bf16"),
    ("d_v", DIL_KV_W, None, False, 1.0, "bf16"),
    ("misc", HEAD_DIM, None, False, 1.0, "misc"),
)
PREP_WIDTH = sum(seg[1] for seg in PREP_SEGMENTS)
N_MISC = DSA_IDX_HEADS + NSA_HEADS * 3


def _prep_outputs(S):
    shapes = []
    for name, width, _, _, _, kind in PREP_SEGMENTS:
        if kind == "bf16":
            shapes.append((width, BF16))
        elif kind == "ones":
            shapes.append((width + HEAD_DIM, BF16))
        elif kind == "f32":
            shapes.append((width, F32))
        elif kind == "ik":
            shapes.append((DSA_IDX_DIM, BF16))
        else:
            shapes.append((N_MISC, F32))
    return shapes


def _prep_kernel(p_ref, g_ref, cs_ref, sn_ref, *o_refs):
    cs = cs_ref[...]
    sn = sn_ref[...]
    off = 0
    for (name, width, gi, rope, scale, kind), o_ref in zip(PREP_SEGMENTS, o_refs):
        if kind == "ik":
            o_ref[...] = p_ref[:, off:off + DSA_IDX_DIM].astype(o_ref.dtype)
        elif kind == "misc":
            o_ref[...] = p_ref[:, off:off + N_MISC]
        else:
            for hd in range(width // HEAD_DIM):
                lanes = slice(hd * HEAD_DIM, (hd + 1) * HEAD_DIM)
                x = p_ref[:, off + hd * HEAD_DIM:off + (hd + 1) * HEAD_DIM]
                if gi is not None:
                    x = x * lax.rsqrt(jnp.mean(x * x, axis=-1, keepdims=True) + NORM_EPS) * g_ref[gi:gi + 1, :]
                if rope:
                    x = x * cs + pltpu.roll(x, HEAD_DIM // 2, 1) * sn
                if scale != 1.0:
                    x = x * scale
                o_ref[:, lanes] = x.astype(o_ref.dtype)
            if kind == "ones":
                o_ref[:, width:] = jnp.ones((o_ref.shape[0], HEAD_DIM), o_ref.dtype)
        off += width


def prepare_operands(proj, qk_gain, cs, sn):
    S = proj.shape[0]
    T = min(PREP_T, S)
    row = lambda i: (i, 0)
    shapes = _prep_outputs(S)
    outs = pl.pallas_call(
        _prep_kernel,
        grid=(S // T,),
        in_specs=[pl.BlockSpec((T, PREP_WIDTH), row),
                  pl.BlockSpec((N_QK_GAINS, HEAD_DIM), lambda i: (0, 0)),
                  pl.BlockSpec((T, HEAD_DIM), row), pl.BlockSpec((T, HEAD_DIM), row)],
        out_specs=tuple(pl.BlockSpec((T, w), row) for w, _ in shapes),
        out_shape=tuple(jax.ShapeDtypeStruct((S, w), dt) for w, dt in shapes),
        compiler_params=_params(1),
        name="prepare_operands",
    )(proj, qk_gain, cs, sn)
    return {seg[0]: o for seg, o in zip(PREP_SEGMENTS, outs)}


def fused_in_weights(w_in, gate_down):
    names = ("a_q", "a_k", "a_v", "b_q", "b_k", "b_v", "b_iq", "b_ik", "b_iw",
             "c_q", "c_kc", "c_vc", "c_ks", "c_vs", "c_kw", "c_vw", "c_g", "d_q", "d_k", "d_v")
    parts = dict(zip(names, jnp.split(w_in, SPLIT_OFFSETS, axis=1)))
    parts["z"] = gate_down
    parts["misc"] = jnp.concatenate([parts["b_iw"], parts["c_g"]], axis=1)
    cols = []
    for name, width, *_ in PREP_SEGMENTS:
        p = parts[name]
        cols.append(jnp.pad(p, ((0, 0), (0, width - p.shape[1]))))
    return jnp.concatenate(cols, axis=1).astype(BF16)


def token_mixing(h, cs, sn, norm_gain, w_in, qk_gain, nsa_pe, nsa_w1, nsa_w2, gate_down, gate_up, w_branch, w_out):
    proj = norm_mm(h, norm_gain, fused_in_weights(w_in, gate_down), F32)
    t = prepare_operands(proj, qk_gain, cs, sn)
    o_a = sb_attention(t["a_q"], t["a_k"], t["a_v"])
    o_b = dsa_attention(t["b_q"], t["b_k"], t["b_v"], t["b_iq"], t["b_ik"], t["misc"][:, :DSA_IDX_HEADS])
    k_cmp = rmsnorm(nsa_compress(t["c_kc"], nsa_pe[0], nsa_w1[0], nsa_w2[0]), qk_gain[3])
    v_cmp = nsa_compress(t["c_vc"], nsa_pe[1], nsa_w1[1], nsa_w2[1])
    o_c = nsa_attention(t["c_q"], t["misc"][:, DSA_IDX_HEADS:], k_cmp.astype(BF16), v_cmp.astype(BF16),
                        t["c_ks"], t["c_vs"], t["c_kw"], t["c_vw"])
    o_d = dilated_attention(t["d_q"], t["d_k"], t["d_v"])
    merged = gated_merge(t["z"], (o_a, o_b, o_c, o_d), gate_up, w_branch)
    return _mm(merged, w_out, residual=h)


MERGE_TM = 512
MERGE_TN = 512


def _merge_kernel(z_ref, oa_ref, ob_ref, oc_ref, od_ref, gu_ref, wb_ref, o_ref):
    z = z_ref[...]
    acc = None
    off = 0
    for i, br_ref in enumerate((oa_ref, ob_ref, oc_ref, od_ref)):
        width = BRANCH_WIDTHS[i]
        gate = jax.nn.sigmoid(_dot(z, gu_ref[i]))
        term = gate * _dot(br_ref[...], wb_ref[off:off + width, :])
        acc = term if acc is None else acc + term
        off += width
    o_ref[...] = acc.astype(o_ref.dtype)


def gated_merge(z, branches, gate_up, w_branch):
    S = z.shape[0]
    D = w_branch.shape[1]
    tm, tn = min(MERGE_TM, S), min(MERGE_TN, D)
    row = lambda i, j: (i, 0)
    return pl.pallas_call(
        _merge_kernel,
        grid=(S // tm, D // tn),
        in_specs=[pl.BlockSpec((tm, GATE_RANK), row)]
                 + [pl.BlockSpec((tm, w), row) for w in BRANCH_WIDTHS]
                 + [pl.BlockSpec((N_BRANCHES, GATE_RANK, tn), lambda i, j: (0, 0, j)),
                    pl.BlockSpec((BRANCH_WIDTH, tn), lambda i, j: (0, j))],
        out_specs=pl.BlockSpec((tm, tn), lambda i, j: (i, j)),
        out_shape=jax.ShapeDtypeStruct((S, D), BF16),
        compiler_params=_params(2),
        name="gated_merge",
    )(z, *branches, gate_up.astype(BF16), w_branch.astype(BF16))


PEER_ST = 256
PEER_TM = 512
PEER_EC = 512
PEER_HALF = PEER_KEY_DIM // 2
PEER_STAT_ROWS = 32
assert (PEER_TOPK + 1) // 9 == 1 and PEER_TOPK + 1 <= 24


def _top_rows(scores, n, n_rows):
    st = scores.shape[1]
    rid = lax.broadcasted_iota(jnp.int32, (n_rows, st), 0)
    out = jnp.full((n_rows, st), EXTRACTED, F32)
    for r in range(n):
        m = jnp.max(scores, axis=0, keepdims=True)
        out = jnp.where(rid == r, m, out)
        scores = jnp.where(scores == m, EXTRACTED, scores)
    return out


def _peer_stats_kernel(q_ref, w1_ref, w2_ref, o_ref):
    st = q_ref.shape[0]
    k = PEER_TOPK
    rid = lax.broadcasted_iota(jnp.int32, (PEER_STAT_ROWS, st), 0)
    rid8 = lax.broadcasted_iota(jnp.int32, (8, st), 0)
    stats = jnp.zeros((PEER_STAT_ROWS, st), F32)
    for h in range(PEER_HEADS):
        qh = q_ref[:, h * PEER_KEY_DIM:(h + 1) * PEER_KEY_DIM]
        v1 = _top_rows(_dot_nt(w1_ref[h], qh), k + 1, 24)
        v2 = _top_rows(_dot_nt(w2_ref[h], qh), k + 1, 24)
        pieces = [v1[0:1] + v2]
        for a in range(1, 8):
            pieces.append(jnp.where(rid8 < (k + 1) // (a + 1), v1[a:a + 1] + v2[0:8], EXTRACTED))
        pieces.append(v1[8:24] + v2[0:1])
        tops = _top_rows(jnp.concatenate(pieces, axis=0), k + 1, 24)
        c1 = tops[0:1]
        den = jnp.sum(jnp.exp(tops[0:k] - c1), axis=0, keepdims=True)
        thr = 0.5 * (tops[k - 1:k] + tops[k:k + 1])
        stats = jnp.where(rid == h, thr, stats)
        stats = jnp.where(rid == PEER_HEADS + h, c1, stats)
        stats = jnp.where(rid == 2 * PEER_HEADS + h, 1.0 / den, stats)
    o_ref[...] = stats


def _peer_weight_kernel(xn_ref, ut_ref, q_ref, kf_ref, st_ref, o_ref):
    act = jax.nn.gelu(_dot(xn_ref[...], ut_ref[...]))
    st = st_ref[...]
    w = jnp.zeros(act.shape, F32)
    for h in range(PEER_HEADS):
        sf = _dot(q_ref[:, h * PEER_KEY_DIM:(h + 1) * PEER_KEY_DIM], kf_ref[h])
        gate = jnp.exp(sf - st[:, PEER_HEADS + h:PEER_HEADS + h + 1]) * st[:, 2 * PEER_HEADS + h:2 * PEER_HEADS + h + 1]
        w = w + jnp.where(sf >= st[:, h:h + 1], gate, 0.0)
    o_ref[...] = (w * act).astype(o_ref.dtype)


def peer_ffn(h, norm_gain, wq, subkeys, u, v):
    S, D = h.shape
    nk, half = PEER_N_KEYS, PEER_HALF
    q, xn = norm_mm(h, norm_gain, wq.astype(BF16), BF16, return_xn=True)
    zeros = jnp.zeros((PEER_HEADS, nk, half), F32)
    w1 = jnp.concatenate([subkeys[:, 0], zeros], axis=-1).astype(BF16)
    w2 = jnp.concatenate([zeros, subkeys[:, 1]], axis=-1).astype(BF16)
    st_t = pl.pallas_call(
        _peer_stats_kernel,
        grid=(S // PEER_ST,),
        in_specs=[pl.BlockSpec((PEER_ST, PEER_HEADS * PEER_KEY_DIM), lambda i: (i, 0)),
                  pl.BlockSpec((PEER_HEADS, nk, PEER_KEY_DIM), lambda i: (0, 0, 0)),
                  pl.BlockSpec((PEER_HEADS, nk, PEER_KEY_DIM), lambda i: (0, 0, 0))],
        out_specs=pl.BlockSpec((PEER_STAT_ROWS, PEER_ST), lambda i: (0, i)),
        out_shape=jax.ShapeDtypeStruct((PEER_STAT_ROWS, S), F32),
        compiler_params=_params(1),
        name="peer_stats",
    )(q, w1, w2)
    stats = st_t.T
    top = jnp.repeat(jnp.swapaxes(subkeys[:, 0], 1, 2), nk, axis=2)
    bot = jnp.tile(jnp.swapaxes(subkeys[:, 1], 1, 2), (1, 1, nk))
    kfull = jnp.concatenate([top, bot], axis=1).astype(BF16)
    tm, ec = min(PEER_TM, S), PEER_EC
    wmat = pl.pallas_call(
        _peer_weight_kernel,
        grid=(S // tm, PEER_EXPERTS // ec),
        in_specs=[pl.BlockSpec((tm, D), lambda i, j: (i, 0)),
                  pl.BlockSpec((D, ec), lambda i, j: (0, j)),
                  pl.BlockSpec((tm, PEER_HEADS * PEER_KEY_DIM), lambda i, j: (i, 0)),
                  pl.BlockSpec((PEER_HEADS, PEER_KEY_DIM, ec), lambda i, j: (0, 0, j)),
                  pl.BlockSpec((tm, PEER_STAT_ROWS), lambda i, j: (i, 0))],
        out_specs=pl.BlockSpec((tm, ec), lambda i, j: (i, j)),
        out_shape=jax.ShapeDtypeStruct((S, PEER_EXPERTS), BF16),
        compiler_params=_params(2),
        name="peer_weights",
    )(xn, u.T.astype(BF16), q, kfull, stats)
    return _mm(wmat, v, residual=h)


def kernel(x, norm_mix, w_in, qk_gain, nsa_pe, nsa_w1, nsa_w2, gate_down, gate_up, w_branch, w_out,
           norm_ffn, peer_wq, peer_subkeys, peer_u, peer_v):
    S = x.shape[1]
    cos, sin = rope_tables(S, x.dtype)
    cos, sin = cos.reshape(S, HEAD_DIM // 2), sin.reshape(S, HEAD_DIM // 2)
    cs = jnp.concatenate([cos, cos], axis=1)
    sn = jnp.concatenate([-sin, sin], axis=1)
    h = x[0]
    for l in range(DEPTH):
        h = token_mixing(h, cs, sn, norm_mix[l], w_in[l], qk_gain[l], nsa_pe[l], nsa_w1[l], nsa_w2[l],
                         gate_down[l], gate_up[l], w_branch[l], w_out[l])
        h = peer_ffn(h, norm_ffn[l], peer_wq[l], peer_subkeys[l], peer_u[l], peer_v[l])
    return h[None]
```

```python
import math, functools
import jax, jax.numpy as jnp
from jax import lax
import numpy as np
from jax.experimental import pallas as pl
from jax.experimental.pallas import tpu as pltpu

D_MODEL = 4096
BATCH = 1
SEQ = 16384
DEPTH = 4

HEAD_DIM = 128
ROPE_THETA = 10000.0
NORM_EPS = 1e-6
SB_HEADS = 4
DSA_HEADS = 4
DSA_KV_HEADS = 1
DSA_IDX_HEADS = 4
DSA_IDX_DIM = 64
DSA_TOPK = 256
NSA_HEADS = 4
NSA_KV_HEADS = 1
NSA_CMP_LEN = 32
NSA_CMP_STRIDE = 16
NSA_SLC_LEN = 32
NSA_N_SEL = 8
NSA_WINDOW = 512
NSA_FORCED_SCORE = 1e4
DIL_PATTERNS = ((128, 1), (512, 4), (2048, 16))
DIL_GROUPS = len(DIL_PATTERNS)
DIL_HEADS_PER_GROUP = 2
DIL_MAX_WINDOW = max(w for w, _ in DIL_PATTERNS)
N_BRANCHES = 4
GATE_RANK = 256
PEER_HEADS = 8
PEER_N_KEYS = 64
PEER_EXPERTS = PEER_N_KEYS ** 2
PEER_KEY_DIM = 128
PEER_TOPK = 16

SB_W = SB_HEADS * HEAD_DIM
DSA_Q_W = DSA_HEADS * HEAD_DIM
DSA_KV_W = DSA_KV_HEADS * HEAD_DIM
DSA_IQ_W = DSA_IDX_HEADS * DSA_IDX_DIM
NSA_Q_W = NSA_HEADS * HEAD_DIM
NSA_KV_W = NSA_KV_HEADS * HEAD_DIM
DIL_Q_W = DIL_GROUPS * DIL_HEADS_PER_GROUP * HEAD_DIM
DIL_KV_W = DIL_HEADS_PER_GROUP * HEAD_DIM
IN_SPLITS = (SB_W, SB_W, SB_W,
             DSA_Q_W, DSA_KV_W, DSA_KV_W, DSA_IQ_W, DSA_IDX_DIM, DSA_IDX_HEADS,
             NSA_Q_W, NSA_KV_W, NSA_KV_W, NSA_KV_W, NSA_KV_W, NSA_KV_W, NSA_KV_W, NSA_HEADS * 3,
             DIL_Q_W, DIL_KV_W, DIL_KV_W)
IN_WIDTH = sum(IN_SPLITS)
SPLIT_OFFSETS = tuple(int(o) for o in np.cumsum(IN_SPLITS)[:-1])
BRANCH_WIDTHS = (SB_W, DSA_Q_W, NSA_Q_W, DIL_KV_W)
BRANCH_WIDTH = sum(BRANCH_WIDTHS)
BRANCH_OFFSETS = tuple(int(o) for o in np.cumsum(BRANCH_WIDTHS)[:-1])
N_QK_GAINS = 8

ATT_SCALE = HEAD_DIM ** -0.5
LOG2_SCALE = ATT_SCALE * math.log2(math.e)
MASKED = -1e30
EXTRACTED = -3e38
INT32_MIN = -2 ** 31
F32_TINY = float(np.finfo(np.float32).tiny)
VMEM_LIMIT = 56 * 1024 * 1024

BF16 = jnp.bfloat16
F32 = jnp.float32


def _params(n_grid):
    return pltpu.CompilerParams(dimension_semantics=("arbitrary",) * n_grid, vmem_limit_bytes=VMEM_LIMIT)


def _resident(shape):
    nd = len(shape)
    return pl.BlockSpec(shape, lambda *_: (0,) * nd, pipeline_mode=pl.Buffered(1))


def _dot_nt(a, b):
    return lax.dot_general(a, b, (((1,), (1,)), ((), ())), preferred_element_type=F32)


def _dot(a, b):
    return jnp.dot(a, b, preferred_element_type=F32)


def _mm_kernel(a_ref, b_ref, o_ref):
    o_ref[...] = _dot(a_ref[...], b_ref[...]).astype(o_ref.dtype)


def _mm_res_kernel(a_ref, b_ref, r_ref, o_ref):
    o_ref[...] = (r_ref[...] + _dot(a_ref[...], b_ref[...])).astype(o_ref.dtype)


def _pick(n, prefs):
    for p in prefs:
        if n % p == 0:
            return p
    return n


def _mm(a, b, out_dtype=F32, residual=None):
    lead = a.shape[:-1]
    K = a.shape[-1]
    N = b.shape[-1]
    a2 = a.reshape(-1, K).astype(jnp.bfloat16)
    M = a2.shape[0]
    n_pad = (-N) % 128
    b2 = b.astype(jnp.bfloat16)
    if n_pad:
        b2 = jnp.pad(b2, ((0, 0), (0, n_pad)))
    Np = N + n_pad
    tm = _pick(M, (1024, 512, 256, 128))
    tn = _pick(Np, (512, 256, 128))
    in_specs = [pl.BlockSpec((tm, K), lambda i, j: (i, 0)),
                pl.BlockSpec((K, tn), lambda i, j: (0, j))]
    args = [a2, b2]
    body = _mm_kernel
    if residual is not None:
        assert n_pad == 0
        in_specs.append(pl.BlockSpec((tm, tn), lambda i, j: (i, j)))
        args.append(residual.reshape(M, N))
        body = _mm_res_kernel
    out = pl.pallas_call(
        body,
        grid=(M // tm, Np // tn),
        in_specs=in_specs,
        out_specs=pl.BlockSpec((tm, tn), lambda i, j: (i, j)),
        out_shape=jax.ShapeDtypeStruct((M, Np), out_dtype),
        compiler_params=pltpu.CompilerParams(
            dimension_semantics=("arbitrary", "arbitrary"),
            vmem_limit_bytes=VMEM_LIMIT),
        name="dense_mm",
    )(*args)
    if n_pad:
        out = out[:, :N]
    return out.reshape(*lead, N)


def rmsnorm(x, g):
    xf = x.astype(jnp.float32)
    y = xf * lax.rsqrt(jnp.mean(xf * xf, axis=-1, keepdims=True) + NORM_EPS)
    return (y * g.astype(jnp.float32)).astype(x.dtype)


def rope_tables(seq, dtype):
    inv = ROPE_THETA ** (-jnp.arange(0, HEAD_DIM, 2, dtype=jnp.float32) / HEAD_DIM)
    ang = jnp.arange(seq, dtype=jnp.float32)[:, None] * inv[None, :]
    return jnp.cos(ang)[None, :, None, :].astype(dtype), jnp.sin(ang)[None, :, None, :].astype(dtype)


def _stack_heads(q_ref, n_heads):
    return jnp.concatenate([q_ref[:, r * HEAD_DIM:(r + 1) * HEAD_DIM] for r in range(n_heads)], axis=0)


def _tile_rows(x, n):
    return jnp.concatenate([x] * n, axis=0)


def _flash_init(m_scr, acc_scr):
    m_scr[...] = jnp.full(m_scr.shape, MASKED, F32)
    acc_scr[...] = jnp.zeros(acc_scr.shape, F32)


def _flash_step(qs, k_chunk, bias, v_aug, m_scr, acc_scr, n_heads):
    T, C = bias.shape
    s = _dot_nt(qs, k_chunk)
    ps, alphas = [], []
    for r in range(n_heads):
        rows = slice(r * T, (r + 1) * T)
        sm = s[rows] + bias
        m_old = m_scr[rows]
        m_new = jnp.maximum(m_old, jnp.max(sm, axis=1, keepdims=True))
        m_scr[rows] = m_new
        p = [jnp.exp2(sm[:, u * 128:(u + 1) * 128] - m_new) for u in range(C // 128)]
        ps.append(jnp.concatenate(p, axis=1).astype(BF16))
        alpha = jnp.exp2(m_old - m_new)
        alphas.append(jnp.concatenate([alpha, alpha], axis=1))
    pv = _dot(jnp.concatenate(ps, axis=0), v_aug)
    acc_scr[...] = jnp.concatenate(alphas, axis=0) * acc_scr[...] + pv


def _flash_result(acc_scr):
    acc = acc_scr[...]
    return acc[:, :HEAD_DIM] / acc[:, HEAD_DIM:]


def _masked_softmax(s, mask):
    sm = jnp.where(mask, s, MASKED)
    m = jnp.max(sm, axis=1, keepdims=True)
    e = jnp.where(mask, jnp.exp2(sm - m), 0.0)
    den = jnp.sum(e, axis=1, keepdims=True)
    return e * (1.0 / jnp.maximum(den, F32_TINY))


SB_T = 256
SB_DEAD_LOG2 = -200.0


def _sb_kernel(q_ref, k_ref, v_ref, tri_ref, o_ref, later_scr, acc_scr):
    i = pl.program_id(0)
    T = SB_T
    q0 = pl.multiple_of(i * T, T)
    tri = tri_ref[...]
    later_scr[...] = jnp.zeros(later_scr.shape, F32)
    acc_scr[...] = jnp.zeros(acc_scr.shape, F32)

    def chunk(s0, causal):
        heads = [slice(h * HEAD_DIM, (h + 1) * HEAD_DIM) for h in range(SB_HEADS)]
        zs = [_dot_nt(q_ref[:, cols], k_ref[pl.ds(s0, T), cols]) for cols in heads]
        lms = []
        for z in zs:
            sp = jnp.log2(1.0 + jnp.exp2(-jnp.abs(z)))
            lm = -(jnp.maximum(z, 0.0) + sp)
            if causal is not None:
                lm = jnp.where(causal, lm, 0.0)
            lms.append(lm)
        withins = [_dot(lm.astype(BF16), tri) for lm in lms]
        ws = []
        for h in range(SB_HEADS):
            e = zs[h] + lms[h] + withins[h]
            later = later_scr[h]
            w = jnp.concatenate([jnp.exp2(e[:, u * 128:(u + 1) * 128] + later) for u in range(T // 128)], axis=1)
            if causal is not None:
                w = jnp.where(causal, w, 0.0)
            ws.append(w.astype(BF16))
            later_scr[h] = later + jnp.sum(lms[h], axis=1, keepdims=True)
        for h, cols in enumerate(heads):
            acc_scr[h] += _dot(ws[h], v_ref[pl.ds(s0, T), cols])

    row = lax.broadcasted_iota(jnp.int32, (T, T), 0)
    col = lax.broadcasted_iota(jnp.int32, (T, T), 1)
    chunk(q0, col < row)

    def cond(state):
        n, worst = state
        return (n <= i) & (worst > SB_DEAD_LOG2)

    def body(state):
        n, _ = state
        chunk(pl.multiple_of((i - n) * T, T), None)
        return n + 1, jnp.max(later_scr[...])

    lax.while_loop(cond, body, (jnp.int32(1), jnp.max(later_scr[...])))
    for h in range(SB_HEADS):
        o_ref[:, h * HEAD_DIM:(h + 1) * HEAD_DIM] = acc_scr[h].astype(o_ref.dtype)


def sb_attention(q, k, v):
    S = q.shape[0]
    T = SB_T
    tri = jnp.tril(jnp.ones((T, T), F32), -1).astype(BF16)
    return pl.pallas_call(
        _sb_kernel,
        grid=(S // T,),
        in_specs=[pl.BlockSpec((T, SB_W), lambda i: (i, 0)),
                  _resident((S, SB_W)), _resident((S, SB_W)), _resident((T, T))],
        out_specs=pl.BlockSpec((T, SB_W), lambda i: (i, 0)),
        out_shape=jax.ShapeDtypeStruct((S, SB_W), BF16),
        scratch_shapes=[pltpu.VMEM((SB_HEADS, T, 128), F32),
                        pltpu.VMEM((SB_HEADS, T, HEAD_DIM), F32)],
        compiler_params=_params(1),
        name="sb_attention",
    )(q, k, v, tri)


DSA_T = 256
DSA_C = 512
DSA_CH = 256
DSA_BITS_PER_CHECK = 4
DSA_COUNT_UNROLL = 4


def _dsa_kernel(n_top, q_ref, iq_ref, iw_ref, k_ref, v_ref, ik_ref, tri_ref, o_ref,
                key_scr, m_scr, acc_scr):
    i = pl.program_id(0)
    T, C, CH = DSA_T, DSA_C, DSA_CH
    H = DSA_HEADS
    q0 = i * T
    n_chunks = (q0 + T + C - 1) // C
    tpos = q0 + lax.broadcasted_iota(jnp.int32, (T, 1), 0)
    iw = iw_ref[...]
    iq4 = jnp.concatenate([iq_ref[:, h * DSA_IDX_DIM:(h + 1) * DSA_IDX_DIM] for h in range(DSA_IDX_HEADS)], axis=0)

    def index_keys(c, causal):
        s0 = pl.multiple_of(c * C, C)
        d = _dot_nt(iq4, ik_ref[pl.ds(s0, C), :])
        acc = jnp.zeros((T, C), F32)
        for h in range(DSA_IDX_HEADS):
            acc = acc + jnp.maximum(d[h * T:(h + 1) * T], 0.0) * iw[:, h:h + 1]
        bits = lax.bitcast_convert_type(acc, jnp.int32)
        key = jnp.where(bits < 0, bits ^ jnp.int32(0x7FFFFFFF), bits)
        key = jnp.where(acc == 0.0, 0, key)
        if causal:
            spos = s0 + lax.broadcasted_iota(jnp.int32, (T, C), 1)
            key = jnp.where(spos <= tpos, key, INT32_MIN)
        key_scr[c] = key

    def idx_body(c, carry):
        index_keys(c, False)
        return carry

    def idx_pair_body(p, carry):
        index_keys(2 * p, False)
        index_keys(2 * p + 1, False)
        return carry

    n_pairs = (n_chunks - 1) // 2
    lax.fori_loop(0, n_pairs, idx_pair_body, 0)
    lax.fori_loop(2 * n_pairs, n_chunks - 1, idx_body, 0)
    index_keys(n_chunks - 1, True)

    def count_ge(cand):
        counts = []
        for r0 in range(0, T, 128):
            cand_r = cand[r0:r0 + 128]

            def body(c, acc):
                kk = key_scr[c, r0:r0 + 128, :]
                for u in range(C // 128):
                    acc = acc + jnp.where(kk[:, u * 128:(u + 1) * 128] >= cand_r, 1.0, 0.0)
                return acc

            def multi_body(p, acc):
                for k in range(DSA_COUNT_UNROLL):
                    acc = body(DSA_COUNT_UNROLL * p + k, acc)
                return acc
            n_multi = n_chunks // DSA_COUNT_UNROLL
            acc = lax.fori_loop(0, n_multi, multi_body, jnp.zeros((128, 128), F32))
            acc = lax.fori_loop(DSA_COUNT_UNROLL * n_multi, n_chunks, body, acc)
            counts.append(jnp.broadcast_to(jnp.sum(acc, axis=1, keepdims=True), (128, 128)))
        return jnp.concatenate(counts, axis=0)

    ge0 = count_ge(jnp.zeros((T, 128), jnp.int32))
    gt0 = count_ge(jnp.ones((T, 128), jnp.int32))
    base0 = jnp.where(ge0 >= n_top, 0, jnp.full((T, 128), INT32_MIN, jnp.int32))
    done0 = ((gt0 < n_top) & (ge0 >= n_top)) | (ge0 == n_top) | (tpos + 1 < n_top)
    done0 = jnp.where(done0, 1.0, 0.0)

    def try_bit(bit, base, done):
        cand = base + (jnp.int32(1) << bit)
        cnt = count_ge(cand)
        return jnp.where((cnt >= n_top) & (done < 0.5), cand, base), jnp.where(cnt == n_top, 1.0, done)

    def high_body(step, state):
        return try_bit(jnp.int32(30) - step, *state)

    base, done = lax.fori_loop(0, 15, high_body, (base0, done0))

    def low_cond(state):
        b, _, _, n_open = state
        return (b >= 0) & (n_open > 0.0)

    def low_body(state):
        b, base, done, _ = state
        for step in range(DSA_BITS_PER_CHECK):
            base, done = try_bit(b - step, base, done)
        return b - DSA_BITS_PER_CHECK, base, done, jnp.sum(1.0 - done)

    _, thr, _, n_open = lax.while_loop(low_cond, low_body, (jnp.int32(15), base, done, jnp.sum(1.0 - done)))
    thr = jnp.maximum(thr, INT32_MIN + 1)
    need = lax.cond(n_open > 0.0,
                    lambda: n_top - count_ge(thr + 1),
                    lambda: jnp.where((gt0 < n_top) & (ge0 >= n_top), n_top - gt0, float(key_scr.shape[0] * C)))
    tri = tri_ref[...]

    _flash_init(m_scr, acc_scr)
    qs = _stack_heads(q_ref, H)

    def att_body(c, eq_seen):
        kk = key_scr[c]
        for half in range(C // CH):
            s0 = pl.multiple_of(c * C + half * CH, CH)
            eqs, gts = [], []
            for u in range(CH // 128):
                ku = kk[:, half * CH + u * 128:half * CH + (u + 1) * 128]
                eqs.append(ku == thr)
                gts.append(ku > thr)
            eq = jnp.concatenate([jnp.where(e, 1.0, 0.0) for e in eqs], axis=1).astype(BF16)
            pref = _dot(eq, tri)
            bias = [jnp.where(gts[u] | (eqs[u] & (eq_seen + pref[:, u * 128:(u + 1) * 128] <= need)),
                              0.0, MASKED) for u in range(CH // 128)]
            eq_seen = eq_seen + pref[:, CH:]
            _flash_step(qs, k_ref[pl.ds(s0, CH), :], jnp.concatenate(bias, axis=1), v_ref[pl.ds(s0, CH), :],
                        m_scr, acc_scr, H)
        return eq_seen

    lax.fori_loop(0, n_chunks, att_body, jnp.zeros((T, 128), F32))
    out = _flash_result(acc_scr)
    for r in range(H):
        o_ref[:, r * HEAD_DIM:(r + 1) * HEAD_DIM] = out[r * T:(r + 1) * T].astype(o_ref.dtype)


def dsa_attention(q, k, v, iq, ik, iw):
    S = q.shape[0]
    T, C, CH = DSA_T, DSA_C, DSA_CH
    n_top = min(DSA_TOPK, S // 4)
    tri = jnp.concatenate([jnp.triu(jnp.ones((CH, CH), F32)), jnp.ones((CH, 128), F32)], axis=1).astype(BF16)
    return pl.pallas_call(
        functools.partial(_dsa_kernel, n_top),
        grid=(S // T,),
        in_specs=[pl.BlockSpec((T, DSA_Q_W), lambda i: (i, 0)),
                  pl.BlockSpec((T, DSA_IQ_W), lambda i: (i, 0)),
                  pl.BlockSpec((T, DSA_IDX_HEADS), lambda i: (i, 0)),
                  _resident((S, HEAD_DIM)), _resident((S, 2 * HEAD_DIM)), _resident((S, DSA_IDX_DIM)),
                  _resident((CH, CH + 128))],
        out_specs=pl.BlockSpec((T, DSA_Q_W), lambda i: (i, 0)),
        out_shape=jax.ShapeDtypeStruct((S, DSA_Q_W), BF16),
        scratch_shapes=[pltpu.VMEM((S // C, T, C), jnp.int32),
                        pltpu.VMEM((DSA_HEADS * T, HEAD_DIM), F32),
                        pltpu.VMEM((DSA_HEADS * T, 2 * HEAD_DIM), F32)],
        compiler_params=_params(1),
        name="dsa_attention",
    )(q, iq, iw, k, v, ik, tri)


NSA_T = 512
NSA_C = 512
NSA_CH = 256
NSA_GROUP = 128


def _nsa_kernel(q_ref, g_ref, kc_ref, vc_ref, ov_ref, ks_ref, vs_ref, kw_ref, vw_ref, e_ref, o_ref,
                sel_scr, m_scr, acc_scr):
    i = pl.program_id(0)
    T, C, CH = NSA_T, NSA_C, NSA_CH
    H = NSA_HEADS
    q0 = pl.multiple_of(i * T, T)
    ncp = kc_ref.shape[0]
    ns = ov_ref.shape[1]
    tpos = q0 + lax.broadcasted_iota(jnp.int32, (T, 1), 0)
    tpos4 = _tile_rows(tpos, H)
    qs = _stack_heads(q_ref, H)

    s_c = _dot_nt(qs, kc_ref[...])
    cmp_end = lax.broadcasted_iota(jnp.int32, (H * T, ncp), 1) * NSA_CMP_STRIDE + (NSA_CMP_LEN - 1)
    p_c = _masked_softmax(s_c, cmp_end <= tpos4)
    o_c = _dot(p_c.astype(BF16), vc_ref[...])

    psum = p_c[0:T]
    for r in range(1, H):
        psum = psum + p_c[r * T:(r + 1) * T]
    ov = ov_ref[...]
    hi = psum.astype(BF16)
    r1 = psum - hi.astype(F32)
    mid = r1.astype(BF16)
    lo = (r1 - mid.astype(F32)).astype(BF16)
    imp = _dot(hi, ov) + _dot(mid, ov) + _dot(lo, ov)

    blk = lax.broadcasted_iota(jnp.int32, (T, ns), 1)
    cur = tpos >> (NSA_SLC_LEN.bit_length() - 1)
    forced = (blk == 0) | (blk == cur) | (blk == cur - 1)
    imp = jnp.where(forced, NSA_FORCED_SCORE, imp)
    imp = jnp.where(blk * NSA_SLC_LEN <= tpos, imp, MASKED)
    blk_f = blk.astype(F32)
    sel = jnp.zeros((T, ns), F32)
    for _ in range(min(NSA_N_SEL, ns)):
        mx = jnp.max(imp, axis=1, keepdims=True)
        first = jnp.min(jnp.where(imp == mx, blk_f, float(ns)), axis=1, keepdims=True)
        hit = blk_f == first
        sel = jnp.where(hit, 1.0, sel)
        imp = jnp.where(hit, EXTRACTED, imp)
    for g in range(ns // NSA_GROUP):
        sel_scr[g] = sel[:, g * NSA_GROUP:(g + 1) * NSA_GROUP].astype(BF16)

    _flash_init(m_scr, acc_scr)
    chunks_per_group = NSA_GROUP * NSA_SLC_LEN // C

    def sel_body(c, carry):
        s0 = pl.multiple_of(c * C, C)
        member = _dot(sel_scr[c // chunks_per_group], e_ref[c % chunks_per_group])
        spos = s0 + lax.broadcasted_iota(jnp.int32, (T, C), 1)
        bias = jnp.where((member > 0.5) & (spos <= tpos), 0.0, MASKED)
        for half in range(C // CH):
            h0 = pl.multiple_of(s0 + half * CH, CH)
            _flash_step(qs, ks_ref[pl.ds(h0, CH), :], bias[:, half * CH:(half + 1) * CH], vs_ref[pl.ds(h0, CH), :],
                        m_scr, acc_scr, H)
        return carry

    lax.fori_loop(0, (q0 + T + C - 1) // C, sel_body, 0)
    o_s = _flash_result(acc_scr)

    nw = NSA_WINDOW + T
    s_w = _dot_nt(qs, kw_ref[pl.ds(q0, nw), :])
    wpos = q0 - NSA_WINDOW + lax.broadcasted_iota(jnp.int32, (H * T, nw), 1)
    dist = tpos4 - wpos
    p_w = _masked_softmax(s_w, (dist >= 0) & (dist < NSA_WINDOW) & (wpos >= 0))
    o_w = _dot(p_w.astype(BF16), vw_ref[pl.ds(q0, nw), :])

    gate = jax.nn.sigmoid(g_ref[...])
    for r in range(H):
        rows = slice(r * T, (r + 1) * T)
        o = (gate[:, 3 * r:3 * r + 1] * o_c[rows] + gate[:, 3 * r + 1:3 * r + 2] * o_s[rows]
             + gate[:, 3 * r + 2:3 * r + 3] * o_w[rows])
        o_ref[:, r * HEAD_DIM:(r + 1) * HEAD_DIM] = o.astype(o_ref.dtype)


def nsa_attention(q, g, k_cmp, v_cmp, ks, vs, kw, vw):
    S = q.shape[0]
    T, C = NSA_T, NSA_C
    n_cmp = k_cmp.shape[0]
    ncp = -(-n_cmp // 128) * 128
    ns = S // NSA_SLC_LEN
    k_cmp = jnp.pad(k_cmp, ((0, ncp - n_cmp), (0, 0)))
    v_cmp = jnp.pad(v_cmp, ((0, ncp - n_cmp), (0, 0)))
    cmp_start = np.arange(ncp) * NSA_CMP_STRIDE
    slc_start = np.arange(ns) * NSA_SLC_LEN
    overlap = ((cmp_start[:, None] < slc_start[None, :] + NSA_SLC_LEN)
               & (cmp_start[:, None] + NSA_CMP_LEN - 1 >= slc_start[None, :]))
    overlap = jnp.asarray(overlap, BF16)
    cpg = NSA_GROUP * NSA_SLC_LEN // C
    tok_blk = (np.arange(cpg)[:, None] * C + np.arange(C)[None, :]) // NSA_SLC_LEN
    expand = jnp.asarray(np.arange(NSA_GROUP)[None, :, None] == tok_blk[:, None, :], BF16)
    kw = jnp.pad(kw, ((NSA_WINDOW, 0), (0, 0)))
    vw = jnp.pad(vw, ((NSA_WINDOW, 0), (0, 0)))
    return pl.pallas_call(
        _nsa_kernel,
        grid=(S // T,),
        in_specs=[pl.BlockSpec((T, NSA_Q_W), lambda i: (i, 0)),
                  pl.BlockSpec((T, NSA_HEADS * 3), lambda i: (i, 0)),
                  _resident((ncp, HEAD_DIM)), _resident((ncp, HEAD_DIM)), _resident((ncp, ns)),
                  _resident((S, HEAD_DIM)), _resident((S, 2 * HEAD_DIM)),
                  _resident((S + NSA_WINDOW, HEAD_DIM)), _resident((S + NSA_WINDOW, HEAD_DIM)),
                  _resident((cpg, NSA_GROUP, C))],
        out_specs=pl.BlockSpec((T, NSA_Q_W), lambda i: (i, 0)),
        out_shape=jax.ShapeDtypeStruct((S, NSA_Q_W), BF16),
        scratch_shapes=[pltpu.VMEM((ns // NSA_GROUP, T, NSA_GROUP), BF16),
                        pltpu.VMEM((NSA_HEADS * T, HEAD_DIM), F32),
                        pltpu.VMEM((NSA_HEADS * T, 2 * HEAD_DIM), F32)],
        compiler_params=_params(1),
        name="nsa_attention",
    )(q, g, k_cmp, v_cmp, overlap, ks, vs, kw, vw, expand)


def nsa_compress(x, pe, w1, w2):
    S = x.shape[0]
    half = NSA_CMP_STRIDE * HEAD_DIM
    x16 = x.reshape(S // NSA_CMP_STRIDE, half)
    pe_flat = pe.reshape(NSA_CMP_LEN * HEAD_DIM)
    w1_flat = w1.reshape(NSA_CMP_LEN * HEAD_DIM, HEAD_DIM)
    first = _mm(x16 + pe_flat[None, :half], w1_flat[:half])
    second = _mm(x16 + pe_flat[None, half:], w1_flat[half:])
    hid = jax.nn.gelu(first[:-1] + second[1:])
    hid = jnp.pad(hid, ((0, 1), (0, 0)))
    return _mm(hid, w2)[:-1]


DIL_T = 256


def _dil_kernel(q_ref, k_ref, v_ref, o_ref):
    i = pl.program_id(0)
    T = DIL_T
    q0 = pl.multiple_of(i * T, T)
    tl = lax.broadcasted_iota(jnp.int32, (T, 1), 0)
    for s in range(DIL_HEADS_PER_GROUP):
        cols = slice(s * HEAD_DIM, (s + 1) * HEAD_DIM)
        parts = []
        m = jnp.full((T, 1), MASKED, F32)
        for g, (w, r) in enumerate(DIL_PATTERNS):
            n = w + T
            start = pl.multiple_of(q0 + (DIL_MAX_WINDOW - w), 128)
            head = g * DIL_HEADS_PER_GROUP + s
            sc = _dot_nt(q_ref[:, head * HEAD_DIM:(head + 1) * HEAD_DIM], k_ref[pl.ds(start, n), cols]) * ATT_SCALE
            b = lax.broadcasted_iota(jnp.int32, (T, n), 1)
            d = tl + w - b
            mask = (d >= 0) & (d <= w) & ((d & (r - 1)) == 0) & (q0 - w + b >= 0)
            sm = jnp.where(mask, sc, MASKED)
            m = jnp.maximum(m, jnp.max(sm, axis=1, keepdims=True))
            parts.append((sm, mask, start, n))
        num = jnp.zeros((T, HEAD_DIM), F32)
        den = jnp.zeros((T, 1), F32)
        for sm, mask, start, n in parts:
            e = jnp.where(mask, jnp.exp(sm - m), 0.0)
            den = den + jnp.sum(e, axis=1, keepdims=True)
            num = num + _dot(e.astype(BF16), v_ref[pl.ds(start, n), cols])
        o_ref[:, cols] = (num / den).astype(o_ref.dtype)


def dilated_attention(q, k, v):
    S = q.shape[0]
    T = DIL_T
    k = jnp.pad(k, ((DIL_MAX_WINDOW, 0), (0, 0)))
    v = jnp.pad(v, ((DIL_MAX_WINDOW, 0), (0, 0)))
    return pl.pallas_call(
        _dil_kernel,
        grid=(S // T,),
        in_specs=[pl.BlockSpec((T, DIL_Q_W), lambda i: (i, 0)),
                  _resident((S + DIL_MAX_WINDOW, DIL_KV_W)), _resident((S + DIL_MAX_WINDOW, DIL_KV_W))],
        out_specs=pl.BlockSpec((T, DIL_KV_W), lambda i: (i, 0)),
        out_shape=jax.ShapeDtypeStruct((S, DIL_KV_W), BF16),
        compiler_params=_params(1),
        name="dilated_attention",
    )(q, k, v)


NORM_TM = 512
NORM_TN = 512


def _norm_mm_kernel(x_ref, g_ref, b_ref, o_ref, xn_ref):
    @pl.when(pl.program_id(1) == 0)
    def _():
        x = x_ref[...]
        y = x * lax.rsqrt(jnp.mean(x * x, axis=-1, keepdims=True) + NORM_EPS)
        xn_ref[...] = (y * g_ref[...]).astype(xn_ref.dtype)
    o_ref[...] = _dot(xn_ref[...], b_ref[...]).astype(o_ref.dtype)


def norm_mm(x, gain, w, out_dtype, return_xn=False):
    M, K = x.shape
    N = w.shape[1]
    tm, tn = min(NORM_TM, M), min(NORM_TN, N)
    main_spec = pl.BlockSpec((tm, tn), lambda i, j: (i, j))
    xn_spec = pl.BlockSpec((tm, K), lambda i, j: (i, 0))
    main_shape = jax.ShapeDtypeStruct((M, N), out_dtype)
    return pl.pallas_call(
        _norm_mm_kernel,
        grid=(M // tm, N // tn),
        in_specs=[xn_spec, pl.BlockSpec((1, K), lambda i, j: (0, 0)), pl.BlockSpec((K, tn), lambda i, j: (0, j))],
        out_specs=(main_spec, xn_spec) if return_xn else main_spec,
        out_shape=(main_shape, jax.ShapeDtypeStruct((M, K), BF16)) if return_xn else main_shape,
        scratch_shapes=[] if return_xn else [pltpu.VMEM((tm, K), BF16)],
        compiler_params=_params(2),
        name="norm_mm",
    )(x, gain.reshape(1, K).astype(F32), w)


PREP_T = 256
PREP_SEGMENTS = (
    ("z", GATE_RANK, None, False, 1.0, "bf16"),
    ("a_q", SB_W, None, False, LOG2_SCALE, "bf16"),
    ("a_k", SB_W, None, False, 1.0, "bf16"),
    ("a_v", SB_W, None, False, 1.0, "bf16"),
    ("b_q", DSA_Q_W, 0, True, LOG2_SCALE, "bf16"),
    ("b_k", DSA_KV_W, 1, True, 1.0, "bf16"),
    ("b_v", DSA_KV_W, None, False, 1.0, "ones"),
    ("b_iq", DSA_IQ_W, None, False, 1.0, "bf16"),
    ("b_ik", HEAD_DIM, None, False, 1.0, "ik"),
    ("c_q", NSA_Q_W, 2, True, LOG2_SCALE, "bf16"),
    ("c_kc", NSA_KV_W, None, True, 1.0, "f32"),
    ("c_vc", NSA_KV_W, None, False, 1.0, "f32"),
    ("c_ks", NSA_KV_W, 4, True, 1.0, "bf16"),
    ("c_vs", NSA_KV_W, None, False, 1.0, "ones"),
    ("c_kw", NSA_KV_W, 5, True, 1.0, "bf16"),
    ("c_vw", NSA_KV_W, None, False, 1.0, "bf16"),
    ("d_q", DIL_Q_W, 6, True, 1.0, "bf16"),
    ("d_k", DIL_KV_W, 7, True, 1.0, "bf16"),
    ("d_v", DIL_KV_W, None, False, 1.0, "bf16"),
    ("misc", HEAD_DIM, None, False, 1.0, "misc"),
)
PREP_WIDTH = sum(seg[1] for seg in PREP_SEGMENTS)
N_MISC = DSA_IDX_HEADS + NSA_HEADS * 3


def _prep_outputs(S):
    shapes = []
    for name, width, _, _, _, kind in PREP_SEGMENTS:
        if kind == "bf16":
            shapes.append((width, BF16))
        elif kind == "ones":
            shapes.append((width + HEAD_DIM, BF16))
        elif kind == "f32":
            shapes.append((width, F32))
        elif kind == "ik":
            shapes.append((DSA_IDX_DIM, BF16))
        else:
            shapes.append((N_MISC, F32))
    return shapes


def _prep_kernel(p_ref, g_ref, cs_ref, sn_ref, *o_refs):
    cs = cs_ref[...]
    sn = sn_ref[...]
    off = 0
    for (name, width, gi, rope, scale, kind), o_ref in zip(PREP_SEGMENTS, o_refs):
        if kind == "ik":
            o_ref[...] = p_ref[:, off:off + DSA_IDX_DIM].astype(o_ref.dtype)
        elif kind == "misc":
            o_ref[...] = p_ref[:, off:off + N_MISC]
        else:
            for hd in range(width // HEAD_DIM):
                lanes = slice(hd * HEAD_DIM, (hd + 1) * HEAD_DIM)
                x = p_ref[:, off + hd * HEAD_DIM:off + (hd + 1) * HEAD_DIM]
                if gi is not None:
                    x = x * lax.rsqrt(jnp.mean(x * x, axis=-1, keepdims=True) + NORM_EPS) * g_ref[gi:gi + 1, :]
                if rope:
                    x = x * cs + pltpu.roll(x, HEAD_DIM // 2, 1) * sn
                if scale != 1.0:
                    x = x * scale
                o_ref[:, lanes] = x.astype(o_ref.dtype)
            if kind == "ones":
                o_ref[:, width:] = jnp.ones((o_ref.shape[0], HEAD_DIM), o_ref.dtype)
        off += width


def prepare_operands(proj, qk_gain, cs, sn):
    S = proj.shape[0]
    T = min(PREP_T, S)
    row = lambda i: (i, 0)
    shapes = _prep_outputs(S)
    outs = pl.pallas_call(
        _prep_kernel,
        grid=(S // T,),
        in_specs=[pl.BlockSpec((T, PREP_WIDTH), row),
                  pl.BlockSpec((N_QK_GAINS, HEAD_DIM), lambda i: (0, 0)),
                  pl.BlockSpec((T, HEAD_DIM), row), pl.BlockSpec((T, HEAD_DIM), row)],
        out_specs=tuple(pl.BlockSpec((T, w), row) for w, _ in shapes),
        out_shape=tuple(jax.ShapeDtypeStruct((S, w), dt) for w, dt in shapes),
        compiler_params=_params(1),
        name="prepare_operands",
    )(proj, qk_gain, cs, sn)
    return {seg[0]: o for seg, o in zip(PREP_SEGMENTS, outs)}


def fused_in_weights(w_in, gate_down):
    names = ("a_q", "a_k", "a_v", "b_q", "b_k", "b_v", "b_iq", "b_ik", "b_iw",
             "c_q", "c_kc", "c_vc", "c_ks", "c_vs", "c_kw", "c_vw", "c_g", "d_q", "d_k", "d_v")
    parts = dict(zip(names, jnp.split(w_in, SPLIT_OFFSETS, axis=1)))
    parts["z"] = gate_down
    parts["misc"] = jnp.concatenate([parts["b_iw"], parts["c_g"]], axis=1)
    cols = []
    for name, width, *_ in PREP_SEGMENTS:
        p = parts[name]
        cols.append(jnp.pad(p, ((0, 0), (0, width - p.shape[1]))))
    return jnp.concatenate(cols, axis=1).astype(BF16)


def token_mixing(h, cs, sn, norm_gain, w_in, qk_gain, nsa_pe, nsa_w1, nsa_w2, gate_down, gate_up, w_branch, w_out):
    proj = norm_mm(h, norm_gain, fused_in_weights(w_in, gate_down), F32)
    t = prepare_operands(proj, qk_gain, cs, sn)
    o_a = sb_attention(t["a_q"], t["a_k"], t["a_v"])
    o_b = dsa_attention(t["b_q"], t["b_k"], t["b_v"], t["b_iq"], t["b_ik"], t["misc"][:, :DSA_IDX_HEADS])
    k_cmp = rmsnorm(nsa_compress(t["c_kc"], nsa_pe[0], nsa_w1[0], nsa_w2[0]), qk_gain[3])
    v_cmp = nsa_compress(t["c_vc"], nsa_pe[1], nsa_w1[1], nsa_w2[1])
    o_c = nsa_attention(t["c_q"], t["misc"][:, DSA_IDX_HEADS:], k_cmp.astype(BF16), v_cmp.astype(BF16),
                        t["c_ks"], t["c_vs"], t["c_kw"], t["c_vw"])
    o_d = dilated_attention(t["d_q"], t["d_k"], t["d_v"])
    merged = gated_merge(t["z"], (o_a, o_b, o_c, o_d), gate_up, w_branch)
    return _mm(merged, w_out, residual=h)


MERGE_TM = 1024
MERGE_TN = 512


def _merge_kernel(z_ref, oa_ref, ob_ref, oc_ref, od_ref, gu_ref, wb_ref, o_ref):
    z = z_ref[...]
    acc = None
    off = 0
    for i, br_ref in enumerate((oa_ref, ob_ref, oc_ref, od_ref)):
        width = BRANCH_WIDTHS[i]
        gate = jax.nn.sigmoid(_dot(z, gu_ref[i]))
        term = gate * _dot(br_ref[...], wb_ref[off:off + width, :])
        acc = term if acc is None else acc + term
        off += width
    o_ref[...] = acc.astype(o_ref.dtype)


def gated_merge(z, branches, gate_up, w_branch):
    S = z.shape[0]
    D = w_branch.shape[1]
    tm, tn = min(MERGE_TM, S), min(MERGE_TN, D)
    row = lambda i, j: (i, 0)
    return pl.pallas_call(
        _merge_kernel,
        grid=(S // tm, D // tn),
        in_specs=[pl.BlockSpec((tm, GATE_RANK), row)]
                 + [pl.BlockSpec((tm, w), row) for w in BRANCH_WIDTHS]
                 + [pl.BlockSpec((N_BRANCHES, GATE_RANK, tn), lambda i, j: (0, 0, j)),
                    pl.BlockSpec((BRANCH_WIDTH, tn), lambda i, j: (0, j))],
        out_specs=pl.BlockSpec((tm, tn), lambda i, j: (i, j)),
        out_shape=jax.ShapeDtypeStruct((S, D), BF16),
        compiler_params=_params(2),
        name="gated_merge",
    )(z, *branches, gate_up.astype(BF16), w_branch.astype(BF16))


PEER_ST = 256
PEER_TM = 512
PEER_EC = 512
PEER_HALF = PEER_KEY_DIM // 2
PEER_STAT_ROWS = 32
assert (PEER_TOPK + 1) // 9 == 1 and PEER_TOPK + 1 <= 24


def _top_rows(scores, n, n_rows):
    st = scores.shape[1]
    rid = lax.broadcasted_iota(jnp.int32, (n_rows, st), 0)
    out = jnp.full((n_rows, st), EXTRACTED, F32)
    for r in range(n):
        m = jnp.max(scores, axis=0, keepdims=True)
        out = jnp.where(rid == r, m, out)
        scores = jnp.where(scores == m, EXTRACTED, scores)
    return out


def _peer_stats_kernel(q_ref, w1_ref, w2_ref, o_ref):
    st = q_ref.shape[0]
    k = PEER_TOPK
    rid = lax.broadcasted_iota(jnp.int32, (PEER_STAT_ROWS, st), 0)
    rid8 = lax.broadcasted_iota(jnp.int32, (8, st), 0)
    stats = jnp.zeros((PEER_STAT_ROWS, st), F32)
    for h in range(PEER_HEADS):
        qh = q_ref[:, h * PEER_KEY_DIM:(h + 1) * PEER_KEY_DIM]
        v1 = _top_rows(_dot_nt(w1_ref[h], qh), k + 1, 24)
        v2 = _top_rows(_dot_nt(w2_ref[h], qh), k + 1, 24)
        pieces = [v1[0:1] + v2]
        for a in range(1, 8):
            pieces.append(jnp.where(rid8 < (k + 1) // (a + 1), v1[a:a + 1] + v2[0:8], EXTRACTED))
        pieces.append(v1[8:24] + v2[0:1])
        tops = _top_rows(jnp.concatenate(pieces, axis=0), k + 1, 24)
        c1 = tops[0:1]
        den = jnp.sum(jnp.exp(tops[0:k] - c1), axis=0, keepdims=True)
        thr = 0.5 * (tops[k - 1:k] + tops[k:k + 1])
        stats = jnp.where(rid == h, thr, stats)
        stats = jnp.where(rid == PEER_HEADS + h, c1, stats)
        stats = jnp.where(rid == 2 * PEER_HEADS + h, 1.0 / den, stats)
    o_ref[...] = stats


def _peer_weight_kernel(xn_ref, ut_ref, q_ref, kf_ref, st_ref, o_ref):
    act = jax.nn.gelu(_dot(xn_ref[...], ut_ref[...]))
    st = st_ref[...]
    w = jnp.zeros(act.shape, F32)
    for h in range(PEER_HEADS):
        sf = _dot(q_ref[:, h * PEER_KEY_DIM:(h + 1) * PEER_KEY_DIM], kf_ref[h])
        gate = jnp.exp(sf - st[:, PEER_HEADS + h:PEER_HEADS + h + 1]) * st[:, 2 * PEER_HEADS + h:2 * PEER_HEADS + h + 1]
        w = w + jnp.where(sf >= st[:, h:h + 1], gate, 0.0)
    o_ref[...] = (w * act).astype(o_ref.dtype)


def peer_ffn(h, norm_gain, wq, subkeys, u, v):
    S, D = h.shape
    nk, half = PEER_N_KEYS, PEER_HALF
    q, xn = norm_mm(h, norm_gain, wq.astype(BF16), BF16, return_xn=True)
    zeros = jnp.zeros((PEER_HEADS, nk, half), F32)
    w1 = jnp.concatenate([subkeys[:, 0], zeros], axis=-1).astype(BF16)
    w2 = jnp.concatenate([zeros, subkeys[:, 1]], axis=-1).astype(BF16)
    st_t = pl.pallas_call(
        _peer_stats_kernel,
        grid=(S // PEER_ST,),
        in_specs=[pl.BlockSpec((PEER_ST, PEER_HEADS * PEER_KEY_DIM), lambda i: (i, 0)),
                  pl.BlockSpec((PEER_HEADS, nk, PEER_KEY_DIM), lambda i: (0, 0, 0)),
                  pl.BlockSpec((PEER_HEADS, nk, PEER_KEY_DIM), lambda i: (0, 0, 0))],
        out_specs=pl.BlockSpec((PEER_STAT_ROWS, PEER_ST), lambda i: (0, i)),
        out_shape=jax.ShapeDtypeStruct((PEER_STAT_ROWS, S), F32),
        compiler_params=_params(1),
        name="peer_stats",
    )(q, w1, w2)
    stats = st_t.T
    top = jnp.repeat(jnp.swapaxes(subkeys[:, 0], 1, 2), nk, axis=2)
    bot = jnp.tile(jnp.swapaxes(subkeys[:, 1], 1, 2), (1, 1, nk))
    kfull = jnp.concatenate([top, bot], axis=1).astype(BF16)
    tm, ec = min(PEER_TM, S), PEER_EC
    wmat = pl.pallas_call(
        _peer_weight_kernel,
        grid=(S // tm, PEER_EXPERTS // ec),
        in_specs=[pl.BlockSpec((tm, D), lambda i, j: (i, 0)),
                  pl.BlockSpec((D, ec), lambda i, j: (0, j)),
                  pl.BlockSpec((tm, PEER_HEADS * PEER_KEY_DIM), lambda i, j: (i, 0)),
                  pl.BlockSpec((PEER_HEADS, PEER_KEY_DIM, ec), lambda i, j: (0, 0, j)),
                  pl.BlockSpec((tm, PEER_STAT_ROWS), lambda i, j: (i, 0))],
        out_specs=pl.BlockSpec((tm, ec), lambda i, j: (i, j)),
        out_shape=jax.ShapeDtypeStruct((S, PEER_EXPERTS), BF16),
        compiler_params=_params(2),
        name="peer_weights",
    )(xn, u.T.astype(BF16), q, kfull, stats)
    return _mm(wmat, v, residual=h)


def kernel(x, norm_mix, w_in, qk_gain, nsa_pe, nsa_w1, nsa_w2, gate_down, gate_up, w_branch, w_out,
           norm_ffn, peer_wq, peer_subkeys, peer_u, peer_v):
    S = x.shape[1]
    cos, sin = rope_tables(S, x.dtype)
    cos, sin = cos.reshape(S, HEAD_DIM // 2), sin.reshape(S, HEAD_DIM // 2)
    cs = jnp.concatenate([cos, cos], axis=1)
    sn = jnp.concatenate([-sin, sin], axis=1)
    h = x[0]
    for l in range(DEPTH):
        h = token_mixing(h, cs, sn, norm_mix[l], w_in[l], qk_gain[l], nsa_pe[l], nsa_w1[l], nsa_w2[l],
                         gate_down[l], gate_up[l], w_branch[l], w_out[l])
        h = peer_ffn(h, norm_ffn[l], peer_wq[l], peer_subkeys[l], peer_u[l], peer_v[l])
    return h[None]
```

```python
import math, functools
import jax, jax.numpy as jnp
from jax import lax
import numpy as np
from jax.experimental import pallas as pl
from jax.experimental.pallas import tpu as pltpu

D_MODEL = 4096
BATCH = 1
SEQ = 16384
DEPTH = 4

HEAD_DIM = 128
ROPE_THETA = 10000.0
NORM_EPS = 1e-6
SB_HEADS = 4
DSA_HEADS = 4
DSA_KV_HEADS = 1
DSA_IDX_HEADS = 4
DSA_IDX_DIM = 64
DSA_TOPK = 256
NSA_HEADS = 4
NSA_KV_HEADS = 1
NSA_CMP_LEN = 32
NSA_CMP_STRIDE = 16
NSA_SLC_LEN = 32
NSA_N_SEL = 8
NSA_WINDOW = 512
NSA_FORCED_SCORE = 1e4
DIL_PATTERNS = ((128, 1), (512, 4), (2048, 16))
DIL_GROUPS = len(DIL_PATTERNS)
DIL_HEADS_PER_GROUP = 2
DIL_MAX_WINDOW = max(w for w, _ in DIL_PATTERNS)
N_BRANCHES = 4
GATE_RANK = 256
PEER_HEADS = 8
PEER_N_KEYS = 64
PEER_EXPERTS = PEER_N_KEYS ** 2
PEER_KEY_DIM = 128
PEER_TOPK = 16

SB_W = SB_HEADS * HEAD_DIM
DSA_Q_W = DSA_HEADS * HEAD_DIM
DSA_KV_W = DSA_KV_HEADS * HEAD_DIM
DSA_IQ_W = DSA_IDX_HEADS * DSA_IDX_DIM
NSA_Q_W = NSA_HEADS * HEAD_DIM
NSA_KV_W = NSA_KV_HEADS * HEAD_DIM
DIL_Q_W = DIL_GROUPS * DIL_HEADS_PER_GROUP * HEAD_DIM
DIL_KV_W = DIL_HEADS_PER_GROUP * HEAD_DIM
IN_SPLITS = (SB_W, SB_W, SB_W,
             DSA_Q_W, DSA_KV_W, DSA_KV_W, DSA_IQ_W, DSA_IDX_DIM, DSA_IDX_HEADS,
             NSA_Q_W, NSA_KV_W, NSA_KV_W, NSA_KV_W, NSA_KV_W, NSA_KV_W, NSA_KV_W, NSA_HEADS * 3,
             DIL_Q_W, DIL_KV_W, DIL_KV_W)
IN_WIDTH = sum(IN_SPLITS)
SPLIT_OFFSETS = tuple(int(o) for o in np.cumsum(IN_SPLITS)[:-1])
BRANCH_WIDTHS = (SB_W, DSA_Q_W, NSA_Q_W, DIL_KV_W)
BRANCH_WIDTH = sum(BRANCH_WIDTHS)
BRANCH_OFFSETS = tuple(int(o) for o in np.cumsum(BRANCH_WIDTHS)[:-1])
N_QK_GAINS = 8

ATT_SCALE = HEAD_DIM ** -0.5
LOG2_SCALE = ATT_SCALE * math.log2(math.e)
MASKED = -1e30
EXTRACTED = -3e38
INT32_MIN = -2 ** 31
F32_TINY = float(np.finfo(np.float32).tiny)
VMEM_LIMIT = 56 * 1024 * 1024

BF16 = jnp.bfloat16
F32 = jnp.float32


def _params(n_grid):
    return pltpu.CompilerParams(dimension_semantics=("arbitrary",) * n_grid, vmem_limit_bytes=VMEM_LIMIT)


def _resident(shape):
    nd = len(shape)
    return pl.BlockSpec(shape, lambda *_: (0,) * nd, pipeline_mode=pl.Buffered(1))


def _dot_nt(a, b):
    return lax.dot_general(a, b, (((1,), (1,)), ((), ())), preferred_element_type=F32)


def _dot(a, b):
    return jnp.dot(a, b, preferred_element_type=F32)


def _mm_kernel(a_ref, b_ref, o_ref):
    o_ref[...] = _dot(a_ref[...], b_ref[...]).astype(o_ref.dtype)


def _mm_res_kernel(a_ref, b_ref, r_ref, o_ref):
    o_ref[...] = (r_ref[...] + _dot(a_ref[...], b_ref[...])).astype(o_ref.dtype)


def _pick(n, prefs):
    for p in prefs:
        if n % p == 0:
            return p
    return n


def _mm(a, b, out_dtype=F32, residual=None):
    lead = a.shape[:-1]
    K = a.shape[-1]
    N = b.shape[-1]
    a2 = a.reshape(-1, K).astype(jnp.bfloat16)
    M = a2.shape[0]
    n_pad = (-N) % 128
    b2 = b.astype(jnp.bfloat16)
    if n_pad:
        b2 = jnp.pad(b2, ((0, 0), (0, n_pad)))
    Np = N + n_pad
    tm = _pick(M, (1024, 512, 256, 128))
    tn = _pick(Np, (512, 256, 128))
    in_specs = [pl.BlockSpec((tm, K), lambda i, j: (i, 0)),
                pl.BlockSpec((K, tn), lambda i, j: (0, j))]
    args = [a2, b2]
    body = _mm_kernel
    if residual is not None:
        assert n_pad == 0
        in_specs.append(pl.BlockSpec((tm, tn), lambda i, j: (i, j)))
        args.append(residual.reshape(M, N))
        body = _mm_res_kernel
    out = pl.pallas_call(
        body,
        grid=(M // tm, Np // tn),
        in_specs=in_specs,
        out_specs=pl.BlockSpec((tm, tn), lambda i, j: (i, j)),
        out_shape=jax.ShapeDtypeStruct((M, Np), out_dtype),
        compiler_params=pltpu.CompilerParams(
            dimension_semantics=("arbitrary", "arbitrary"),
            vmem_limit_bytes=VMEM_LIMIT),
        name="dense_mm",
    )(*args)
    if n_pad:
        out = out[:, :N]
    return out.reshape(*lead, N)


def rmsnorm(x, g):
    xf = x.astype(jnp.float32)
    y = xf * lax.rsqrt(jnp.mean(xf * xf, axis=-1, keepdims=True) + NORM_EPS)
    return (y * g.astype(jnp.float32)).astype(x.dtype)


def rope_tables(seq, dtype):
    inv = ROPE_THETA ** (-jnp.arange(0, HEAD_DIM, 2, dtype=jnp.float32) / HEAD_DIM)
    ang = jnp.arange(seq, dtype=jnp.float32)[:, None] * inv[None, :]
    return jnp.cos(ang)[None, :, None, :].astype(dtype), jnp.sin(ang)[None, :, None, :].astype(dtype)


def _stack_heads(q_ref, n_heads):
    return jnp.concatenate([q_ref[:, r * HEAD_DIM:(r + 1) * HEAD_DIM] for r in range(n_heads)], axis=0)


def _tile_rows(x, n):
    return jnp.concatenate([x] * n, axis=0)


def _flash_init(m_scr, acc_scr):
    m_scr[...] = jnp.full(m_scr.shape, MASKED, F32)
    acc_scr[...] = jnp.zeros(acc_scr.shape, F32)


def _flash_step(qs, k_chunk, bias, v_aug, m_scr, acc_scr, n_heads):
    T, C = bias.shape
    s = _dot_nt(qs, k_chunk)
    ps, alphas = [], []
    for r in range(n_heads):
        rows = slice(r * T, (r + 1) * T)
        sm = s[rows] + bias
        m_old = m_scr[rows]
        m_new = jnp.maximum(m_old, jnp.max(sm, axis=1, keepdims=True))
        m_scr[rows] = m_new
        p = [jnp.exp2(sm[:, u * 128:(u + 1) * 128] - m_new) for u in range(C // 128)]
        ps.append(jnp.concatenate(p, axis=1).astype(BF16))
        alpha = jnp.exp2(m_old - m_new)
        alphas.append(jnp.concatenate([alpha, alpha], axis=1))
    pv = _dot(jnp.concatenate(ps, axis=0), v_aug)
    acc_scr[...] = jnp.concatenate(alphas, axis=0) * acc_scr[...] + pv


def _flash_result(acc_scr):
    acc = acc_scr[...]
    return acc[:, :HEAD_DIM] / acc[:, HEAD_DIM:]


def _masked_softmax(s, mask):
    sm = jnp.where(mask, s, MASKED)
    m = jnp.max(sm, axis=1, keepdims=True)
    e = jnp.where(mask, jnp.exp2(sm - m), 0.0)
    den = jnp.sum(e, axis=1, keepdims=True)
    return e * (1.0 / jnp.maximum(den, F32_TINY))


SB_T = 256
SB_DEAD_LOG2 = -200.0


def _sb_kernel(q_ref, k_ref, v_ref, tri_ref, o_ref, later_scr, acc_scr):
    i = pl.program_id(0)
    T = SB_T
    q0 = pl.multiple_of(i * T, T)
    tri = tri_ref[...]
    later_scr[...] = jnp.zeros(later_scr.shape, F32)
    acc_scr[...] = jnp.zeros(acc_scr.shape, F32)

    def chunk(s0, causal):
        heads = [slice(h * HEAD_DIM, (h + 1) * HEAD_DIM) for h in range(SB_HEADS)]
        zs = [_dot_nt(q_ref[:, cols], k_ref[pl.ds(s0, T), cols]) for cols in heads]
        lms = []
        for z in zs:
            sp = jnp.log2(1.0 + jnp.exp2(-jnp.abs(z)))
            lm = -(jnp.maximum(z, 0.0) + sp)
            if causal is not None:
                lm = jnp.where(causal, lm, 0.0)
            lms.append(lm)
        withins = [_dot(lm.astype(BF16), tri) for lm in lms]
        ws = []
        for h in range(SB_HEADS):
            e = zs[h] + lms[h] + withins[h]
            later = later_scr[h]
            w = jnp.concatenate([jnp.exp2(e[:, u * 128:(u + 1) * 128] + later) for u in range(T // 128)], axis=1)
            if causal is not None:
                w = jnp.where(causal, w, 0.0)
            ws.append(w.astype(BF16))
            later_scr[h] = later + jnp.sum(lms[h], axis=1, keepdims=True)
        for h, cols in enumerate(heads):
            acc_scr[h] += _dot(ws[h], v_ref[pl.ds(s0, T), cols])

    row = lax.broadcasted_iota(jnp.int32, (T, T), 0)
    col = lax.broadcasted_iota(jnp.int32, (T, T), 1)
    chunk(q0, col < row)

    def cond(state):
        n, worst = state
        return (n <= i) & (worst > SB_DEAD_LOG2)

    def body(state):
        n, _ = state
        chunk(pl.multiple_of((i - n) * T, T), None)
        return n + 1, jnp.max(later_scr[...])

    lax.while_loop(cond, body, (jnp.int32(1), jnp.max(later_scr[...])))
    for h in range(SB_HEADS):
        o_ref[:, h * HEAD_DIM:(h + 1) * HEAD_DIM] = acc_scr[h].astype(o_ref.dtype)


def sb_attention(q, k, v):
    S = q.shape[0]
    T = SB_T
    tri = jnp.tril(jnp.ones((T, T), F32), -1).astype(BF16)
    return pl.pallas_call(
        _sb_kernel,
        grid=(S // T,),
        in_specs=[pl.BlockSpec((T, SB_W), lambda i: (i, 0)),
                  _resident((S, SB_W)), _resident((S, SB_W)), _resident((T, T))],
        out_specs=pl.BlockSpec((T, SB_W), lambda i: (i, 0)),
        out_shape=jax.ShapeDtypeStruct((S, SB_W), BF16),
        scratch_shapes=[pltpu.VMEM((SB_HEADS, T, 128), F32),
                        pltpu.VMEM((SB_HEADS, T, HEAD_DIM), F32)],
        compiler_params=_params(1),
        name="sb_attention",
    )(q, k, v, tri)


DSA_T = 256
DSA_C = 512
DSA_CH = 256
DSA_BITS_PER_CHECK = 4
DSA_COUNT_UNROLL = 4


def _dsa_kernel(n_top, q_ref, iq_ref, iw_ref, k_ref, v_ref, ik_ref, tri_ref, o_ref,
                key_scr, m_scr, acc_scr):
    i = pl.program_id(0)
    T, C, CH = DSA_T, DSA_C, DSA_CH
    H = DSA_HEADS
    q0 = i * T
    n_chunks = (q0 + T + C - 1) // C
    tpos = q0 + lax.broadcasted_iota(jnp.int32, (T, 1), 0)
    iw = iw_ref[...]
    iq4 = jnp.concatenate([iq_ref[:, h * DSA_IDX_DIM:(h + 1) * DSA_IDX_DIM] for h in range(DSA_IDX_HEADS)], axis=0)

    def index_keys(c, causal):
        s0 = pl.multiple_of(c * C, C)
        d = _dot_nt(iq4, ik_ref[pl.ds(s0, C), :])
        acc = jnp.zeros((T, C), F32)
        for h in range(DSA_IDX_HEADS):
            acc = acc + jnp.maximum(d[h * T:(h + 1) * T], 0.0) * iw[:, h:h + 1]
        bits = lax.bitcast_convert_type(acc, jnp.int32)
        key = jnp.where(bits < 0, bits ^ jnp.int32(0x7FFFFFFF), bits)
        key = jnp.where(acc == 0.0, 0, key)
        if causal:
            spos = s0 + lax.broadcasted_iota(jnp.int32, (T, C), 1)
            key = jnp.where(spos <= tpos, key, INT32_MIN)
        key_scr[c] = key

    def idx_body(c, carry):
        index_keys(c, False)
        return carry

    def idx_pair_body(p, carry):
        index_keys(2 * p, False)
        index_keys(2 * p + 1, False)
        return carry

    n_pairs = (n_chunks - 1) // 2
    lax.fori_loop(0, n_pairs, idx_pair_body, 0)
    lax.fori_loop(2 * n_pairs, n_chunks - 1, idx_body, 0)
    index_keys(n_chunks - 1, True)

    def count_ge(cand):
        counts = []
        for r0 in range(0, T, 128):
            cand_r = cand[r0:r0 + 128]

            def body(c, acc):
                kk = key_scr[c, r0:r0 + 128, :]
                for u in range(C // 128):
                    acc = acc + jnp.where(kk[:, u * 128:(u + 1) * 128] >= cand_r, 1.0, 0.0)
                return acc

            def multi_body(p, acc):
                for k in range(DSA_COUNT_UNROLL):
                    acc = body(DSA_COUNT_UNROLL * p + k, acc)
                return acc
            n_multi = n_chunks // DSA_COUNT_UNROLL
            acc = lax.fori_loop(0, n_multi, multi_body, jnp.zeros((128, 128), F32))
            acc = lax.fori_loop(DSA_COUNT_UNROLL * n_multi, n_chunks, body, acc)
            counts.append(jnp.broadcast_to(jnp.sum(acc, axis=1, keepdims=True), (128, 128)))
        return jnp.concatenate(counts, axis=0)

    ge0 = count_ge(jnp.zeros((T, 128), jnp.int32))
    gt0 = count_ge(jnp.ones((T, 128), jnp.int32))
    base0 = jnp.where(ge0 >= n_top, 0, jnp.full((T, 128), INT32_MIN, jnp.int32))
    done0 = ((gt0 < n_top) & (ge0 >= n_top)) | (ge0 == n_top) | (tpos + 1 < n_top)
    done0 = jnp.where(done0, 1.0, 0.0)

    def try_bit(bit, base, done):
        cand = base + (jnp.int32(1) << bit)
        cnt = count_ge(cand)
        return jnp.where((cnt >= n_top) & (done < 0.5), cand, base), jnp.where(cnt == n_top, 1.0, done)

    def high_body(step, state):
        return try_bit(jnp.int32(30) - step, *state)

    base, done = lax.fori_loop(0, 15, high_body, (base0, done0))

    def low_cond(state):
        b, _, _, n_open = state
        return (b >= 0) & (n_open > 0.0)

    def low_body(state):
        b, base, done, _ = state
        for step in range(DSA_BITS_PER_CHECK):
            base, done = try_bit(b - step, base, done)
        return b - DSA_BITS_PER_CHECK, base, done, jnp.sum(1.0 - done)

    _, thr, _, n_open = lax.while_loop(low_cond, low_body, (jnp.int32(15), base, done, jnp.sum(1.0 - done)))
    thr = jnp.maximum(thr, INT32_MIN + 1)
    need = lax.cond(n_open > 0.0,
                    lambda: n_top - count_ge(thr + 1),
                    lambda: jnp.where((gt0 < n_top) & (ge0 >= n_top), n_top - gt0, float(key_scr.shape[0] * C)))
    tri = tri_ref[...]

    _flash_init(m_scr, acc_scr)
    qs = _stack_heads(q_ref, H)

    def att_body(c, eq_seen):
        kk = key_scr[c]
        for half in range(C // CH):
            s0 = pl.multiple_of(c * C + half * CH, CH)
            eqs, gts = [], []
            for u in range(CH // 128):
                ku = kk[:, half * CH + u * 128:half * CH + (u + 1) * 128]
                eqs.append(ku == thr)
                gts.append(ku > thr)
            eq = jnp.concatenate([jnp.where(e, 1.0, 0.0) for e in eqs], axis=1).astype(BF16)
            pref = _dot(eq, tri)
            bias = [jnp.where(gts[u] | (eqs[u] & (eq_seen + pref[:, u * 128:(u + 1) * 128] <= need)),
                              0.0, MASKED) for u in range(CH // 128)]
            eq_seen = eq_seen + pref[:, CH:]
            _flash_step(qs, k_ref[pl.ds(s0, CH), :], jnp.concatenate(bias, axis=1), v_ref[pl.ds(s0, CH), :],
                        m_scr, acc_scr, H)
        return eq_seen

    lax.fori_loop(0, n_chunks, att_body, jnp.zeros((T, 128), F32))
    out = _flash_result(acc_scr)
    for r in range(H):
        o_ref[:, r * HEAD_DIM:(r + 1) * HEAD_DIM] = out[r * T:(r + 1) * T].astype(o_ref.dtype)


def dsa_attention(q, k, v, iq, ik, iw):
    S = q.shape[0]
    T, C, CH = DSA_T, DSA_C, DSA_CH
    n_top = min(DSA_TOPK, S // 4)
    tri = jnp.concatenate([jnp.triu(jnp.ones((CH, CH), F32)), jnp.ones((CH, 128), F32)], axis=1).astype(BF16)
    return pl.pallas_call(
        functools.partial(_dsa_kernel, n_top),
        grid=(S // T,),
        in_specs=[pl.BlockSpec((T, DSA_Q_W), lambda i: (i, 0)),
                  pl.BlockSpec((T, DSA_IQ_W), lambda i: (i, 0)),
                  pl.BlockSpec((T, DSA_IDX_HEADS), lambda i: (i, 0)),
                  _resident((S, HEAD_DIM)), _resident((S, 2 * HEAD_DIM)), _resident((S, DSA_IDX_DIM)),
                  _resident((CH, CH + 128))],
        out_specs=pl.BlockSpec((T, DSA_Q_W), lambda i: (i, 0)),
        out_shape=jax.ShapeDtypeStruct((S, DSA_Q_W), BF16),
        scratch_shapes=[pltpu.VMEM((S // C, T, C), jnp.int32),
                        pltpu.VMEM((DSA_HEADS * T, HEAD_DIM), F32),
                        pltpu.VMEM((DSA_HEADS * T, 2 * HEAD_DIM), F32)],
        compiler_params=_params(1),
        name="dsa_attention",
    )(q, iq, iw, k, v, ik, tri)


NSA_T = 512
NSA_C = 512
NSA_CH = 256
NSA_GROUP = 128


def _nsa_kernel(q_ref, g_ref, kc_ref, vc_ref, ov_ref, ks_ref, vs_ref, kw_ref, vw_ref, e_ref, o_ref,
                sel_scr, m_scr, acc_scr):
    i = pl.program_id(0)
    T, C, CH = NSA_T, NSA_C, NSA_CH
    H = NSA_HEADS
    q0 = pl.multiple_of(i * T, T)
    ncp = kc_ref.shape[0]
    ns = ov_ref.shape[1]
    tpos = q0 + lax.broadcasted_iota(jnp.int32, (T, 1), 0)
    tpos4 = _tile_rows(tpos, H)
    qs = _stack_heads(q_ref, H)

    s_c = _dot_nt(qs, kc_ref[...])
    cmp_end = lax.broadcasted_iota(jnp.int32, (H * T, ncp), 1) * NSA_CMP_STRIDE + (NSA_CMP_LEN - 1)
    p_c = _masked_softmax(s_c, cmp_end <= tpos4)
    o_c = _dot(p_c.astype(BF16), vc_ref[...])

    psum = p_c[0:T]
    for r in range(1, H):
        psum = psum + p_c[r * T:(r + 1) * T]
    ov = ov_ref[...]
    hi = psum.astype(BF16)
    r1 = psum - hi.astype(F32)
    mid = r1.astype(BF16)
    lo = (r1 - mid.astype(F32)).astype(BF16)
    imp = _dot(hi, ov) + _dot(mid, ov) + _dot(lo, ov)

    blk = lax.broadcasted_iota(jnp.int32, (T, ns), 1)
    cur = tpos >> (NSA_SLC_LEN.bit_length() - 1)
    forced = (blk == 0) | (blk == cur) | (blk == cur - 1)
    imp = jnp.where(forced, NSA_FORCED_SCORE, imp)
    imp = jnp.where(blk * NSA_SLC_LEN <= tpos, imp, MASKED)
    blk_f = blk.astype(F32)
    sel = jnp.zeros((T, ns), F32)
    for _ in range(min(NSA_N_SEL, ns)):
        mx = jnp.max(imp, axis=1, keepdims=True)
        first = jnp.min(jnp.where(imp == mx, blk_f, float(ns)), axis=1, keepdims=True)
        hit = blk_f == first
        sel = jnp.where(hit, 1.0, sel)
        imp = jnp.where(hit, EXTRACTED, imp)
    for g in range(ns // NSA_GROUP):
        sel_scr[g] = sel[:, g * NSA_GROUP:(g + 1) * NSA_GROUP].astype(BF16)

    _flash_init(m_scr, acc_scr)
    chunks_per_group = NSA_GROUP * NSA_SLC_LEN // C

    def sel_body(c, carry):
        s0 = pl.multiple_of(c * C, C)
        member = _dot(sel_scr[c // chunks_per_group], e_ref[c % chunks_per_group])
        spos = s0 + lax.broadcasted_iota(jnp.int32, (T, C), 1)
        bias = jnp.where((member > 0.5) & (spos <= tpos), 0.0, MASKED)
        for half in range(C // CH):
            h0 = pl.multiple_of(s0 + half * CH, CH)
            _flash_step(qs, ks_ref[pl.ds(h0, CH), :], bias[:, half * CH:(half + 1) * CH], vs_ref[pl.ds(h0, CH), :],
                        m_scr, acc_scr, H)
        return carry

    lax.fori_loop(0, (q0 + T + C - 1) // C, sel_body, 0)
    o_s = _flash_result(acc_scr)

    nw = NSA_WINDOW + T
    s_w = _dot_nt(qs, kw_ref[pl.ds(q0, nw), :])
    wpos = q0 - NSA_WINDOW + lax.broadcasted_iota(jnp.int32, (H * T, nw), 1)
    dist = tpos4 - wpos
    p_w = _masked_softmax(s_w, (dist >= 0) & (dist < NSA_WINDOW) & (wpos >= 0))
    o_w = _dot(p_w.astype(BF16), vw_ref[pl.ds(q0, nw), :])

    gate = jax.nn.sigmoid(g_ref[...])
    for r in range(H):
        rows = slice(r * T, (r + 1) * T)
        o = (gate[:, 3 * r:3 * r + 1] * o_c[rows] + gate[:, 3 * r + 1:3 * r + 2] * o_s[rows]
             + gate[:, 3 * r + 2:3 * r + 3] * o_w[rows])
        o_ref[:, r * HEAD_DIM:(r + 1) * HEAD_DIM] = o.astype(o_ref.dtype)


def nsa_attention(q, g, k_cmp, v_cmp, ks, vs, kw, vw):
    S = q.shape[0]
    T, C = NSA_T, NSA_C
    n_cmp = k_cmp.shape[0]
    ncp = -(-n_cmp // 128) * 128
    ns = S // NSA_SLC_LEN
    k_cmp = jnp.pad(k_cmp, ((0, ncp - n_cmp), (0, 0)))
    v_cmp = jnp.pad(v_cmp, ((0, ncp - n_cmp), (0, 0)))
    cmp_start = np.arange(ncp) * NSA_CMP_STRIDE
    slc_start = np.arange(ns) * NSA_SLC_LEN
    overlap = ((cmp_start[:, None] < slc_start[None, :] + NSA_SLC_LEN)
               & (cmp_start[:, None] + NSA_CMP_LEN - 1 >= slc_start[None, :]))
    overlap = jnp.asarray(overlap, BF16)
    cpg = NSA_GROUP * NSA_SLC_LEN // C
    tok_blk = (np.arange(cpg)[:, None] * C + np.arange(C)[None, :]) // NSA_SLC_LEN
    expand = jnp.asarray(np.arange(NSA_GROUP)[None, :, None] == tok_blk[:, None, :], BF16)
    kw = jnp.pad(kw, ((NSA_WINDOW, 0), (0, 0)))
    vw = jnp.pad(vw, ((NSA_WINDOW, 0), (0, 0)))
    return pl.pallas_call(
        _nsa_kernel,
        grid=(S // T,),
        in_specs=[pl.BlockSpec((T, NSA_Q_W), lambda i: (i, 0)),
                  pl.BlockSpec((T, NSA_HEADS * 3), lambda i: (i, 0)),
                  _resident((ncp, HEAD_DIM)), _resident((ncp, HEAD_DIM)), _resident((ncp, ns)),
                  _resident((S, HEAD_DIM)), _resident((S, 2 * HEAD_DIM)),
                  _resident((S + NSA_WINDOW, HEAD_DIM)), _resident((S + NSA_WINDOW, HEAD_DIM)),
                  _resident((cpg, NSA_GROUP, C))],
        out_specs=pl.BlockSpec((T, NSA_Q_W), lambda i: (i, 0)),
        out_shape=jax.ShapeDtypeStruct((S, NSA_Q_W), BF16),
        scratch_shapes=[pltpu.VMEM((ns // NSA_GROUP, T, NSA_GROUP), BF16),
                        pltpu.VMEM((NSA_HEADS * T, HEAD_DIM), F32),
                        pltpu.VMEM((NSA_HEADS * T, 2 * HEAD_DIM), F32)],
        compiler_params=_params(1),
        name="nsa_attention",
    )(q, g, k_cmp, v_cmp, overlap, ks, vs, kw, vw, expand)


def nsa_compress(x, pe, w1, w2):
    S = x.shape[0]
    half = NSA_CMP_STRIDE * HEAD_DIM
    x16 = x.reshape(S // NSA_CMP_STRIDE, half)
    pe_flat = pe.reshape(NSA_CMP_LEN * HEAD_DIM)
    w1_flat = w1.reshape(NSA_CMP_LEN * HEAD_DIM, HEAD_DIM)
    first = _mm(x16 + pe_flat[None, :half], w1_flat[:half])
    second = _mm(x16 + pe_flat[None, half:], w1_flat[half:])
    hid = jax.nn.gelu(first[:-1] + second[1:])
    hid = jnp.pad(hid, ((0, 1), (0, 0)))
    return _mm(hid, w2)[:-1]


DIL_T = 256


def _dil_kernel(q_ref, k_ref, v_ref, o_ref):
    i = pl.program_id(0)
    T = DIL_T
    q0 = pl.multiple_of(i * T, T)
    tl = lax.broadcasted_iota(jnp.int32, (T, 1), 0)
    for s in range(DIL_HEADS_PER_GROUP):
        cols = slice(s * HEAD_DIM, (s + 1) * HEAD_DIM)
        parts = []
        m = jnp.full((T, 1), MASKED, F32)
        for g, (w, r) in enumerate(DIL_PATTERNS):
            n = w + T
            start = pl.multiple_of(q0 + (DIL_MAX_WINDOW - w), 128)
            head = g * DIL_HEADS_PER_GROUP + s
            sc = _dot_nt(q_ref[:, head * HEAD_DIM:(head + 1) * HEAD_DIM], k_ref[pl.ds(start, n), cols]) * ATT_SCALE
            b = lax.broadcasted_iota(jnp.int32, (T, n), 1)
            d = tl + w - b
            mask = (d >= 0) & (d <= w) & ((d & (r - 1)) == 0) & (q0 - w + b >= 0)
            sm = jnp.where(mask, sc, MASKED)
            m = jnp.maximum(m, jnp.max(sm, axis=1, keepdims=True))
            parts.append((sm, mask, start, n))
        num = jnp.zeros((T, HEAD_DIM), F32)
        den = jnp.zeros((T, 1), F32)
        for sm, mask, start, n in parts:
            e = jnp.where(mask, jnp.exp(sm - m), 0.0)
            den = den + jnp.sum(e, axis=1, keepdims=True)
            num = num + _dot(e.astype(BF16), v_ref[pl.ds(start, n), cols])
        o_ref[:, cols] = (num / den).astype(o_ref.dtype)


def dilated_attention(q, k, v):
    S = q.shape[0]
    T = DIL_T
    k = jnp.pad(k, ((DIL_MAX_WINDOW, 0), (0, 0)))
    v = jnp.pad(v, ((DIL_MAX_WINDOW, 0), (0, 0)))
    return pl.pallas_call(
        _dil_kernel,
        grid=(S // T,),
        in_specs=[pl.BlockSpec((T, DIL_Q_W), lambda i: (i, 0)),
                  _resident((S + DIL_MAX_WINDOW, DIL_KV_W)), _resident((S + DIL_MAX_WINDOW, DIL_KV_W))],
        out_specs=pl.BlockSpec((T, DIL_KV_W), lambda i: (i, 0)),
        out_shape=jax.ShapeDtypeStruct((S, DIL_KV_W), BF16),
        compiler_params=_params(1),
        name="dilated_attention",
    )(q, k, v)


NORM_TM = 512
NORM_TN = 512


def _norm_mm_kernel(x_ref, g_ref, b_ref, o_ref, xn_ref):
    @pl.when(pl.program_id(1) == 0)
    def _():
        x = x_ref[...]
        y = x * lax.rsqrt(jnp.mean(x * x, axis=-1, keepdims=True) + NORM_EPS)
        xn_ref[...] = (y * g_ref[...]).astype(xn_ref.dtype)
    o_ref[...] = _dot(xn_ref[...], b_ref[...]).astype(o_ref.dtype)


def norm_mm(x, gain, w, out_dtype, return_xn=False):
    M, K = x.shape
    N = w.shape[1]
    tm, tn = min(NORM_TM, M), min(NORM_TN, N)
    main_spec = pl.BlockSpec((tm, tn), lambda i, j: (i, j))
    xn_spec = pl.BlockSpec((tm, K), lambda i, j: (i, 0))
    main_shape = jax.ShapeDtypeStruct((M, N), out_dtype)
    return pl.pallas_call(
        _norm_mm_kernel,
        grid=(M // tm, N // tn),
        in_specs=[xn_spec, pl.BlockSpec((1, K), lambda i, j: (0, 0)), pl.BlockSpec((K, tn), lambda i, j: (0, j))],
        out_specs=(main_spec, xn_spec) if return_xn else main_spec,
        out_shape=(main_shape, jax.ShapeDtypeStruct((M, K), BF16)) if return_xn else main_shape,
        scratch_shapes=[] if return_xn else [pltpu.VMEM((tm, K), BF16)],
        compiler_params=_params(2),
        name="norm_mm",
    )(x, gain.reshape(1, K).astype(F32), w)


PREP_T = 256
PREP_SEGMENTS = (
    ("z", GATE_RANK, None, False, 1.0, "bf16"),
    ("a_q", SB_W, None, False, LOG2_SCALE, "bf16"),
    ("a_k", SB_W, None, False, 1.0, "bf16"),
    ("a_v", SB_W, None, False, 1.0, "bf16"),
    ("b_q", DSA_Q_W, 0, True, LOG2_SCALE, "bf16"),
    ("b_k", DSA_KV_W, 1, True, 1.0, "bf16"),
    ("b_v", DSA_KV_W, None, False, 1.0, "ones"),
    ("b_iq", DSA_IQ_W, None, False, 1.0, "bf16"),
    ("b_ik", HEAD_DIM, None, False, 1.0, "ik"),
    ("c_q", NSA_Q_W, 2, True, LOG2_SCALE, "bf16"),
    ("c_kc", NSA_KV_W, None, True, 1.0, "f32"),
    ("c_vc", NSA_KV_W, None, False, 1.0, "f32"),
    ("c_ks", NSA_KV_W, 4, True, 1.0, "bf16"),
    ("c_vs", NSA_KV_W, None, False, 1.0, "ones"),
    ("c_kw", NSA_KV_W, 5, True, 1.0, "bf16"),
    ("c_vw", NSA_KV_W, None, False, 1.0, "bf16"),
    ("d_q", DIL_Q_W, 6, True, 1.0, "bf16"),
    ("d_k", DIL_KV_W, 7, True, 1.0, "bf16"),
    ("d_v", DIL_KV_W, None, False, 1.0, "bf16"),
    ("misc", HEAD_DIM, None, False, 1.0, "misc"),
)
PREP_WIDTH = sum(seg[1] for seg in PREP_SEGMENTS)
N_MISC = DSA_IDX_HEADS + NSA_HEADS * 3


def _prep_outputs(S):
    shapes = []
    for name, width, _, _, _, kind in PREP_SEGMENTS:
        if kind == "bf16":
            shapes.append((width, BF16))
        elif kind == "ones":
            shapes.append((width + HEAD_DIM, BF16))
        elif kind == "f32":
            shapes.append((width, F32))
        elif kind == "ik":
            shapes.append((DSA_IDX_DIM, BF16))
        else:
            shapes.append((N_MISC, F32))
    return shapes


def _prep_kernel(p_ref, g_ref, cs_ref, sn_ref, *o_refs):
    cs = cs_ref[...]
    sn = sn_ref[...]
    off = 0
    for (name, width, gi, rope, scale, kind), o_ref in zip(PREP_SEGMENTS, o_refs):
        if kind == "ik":
            o_ref[...] = p_ref[:, off:off + DSA_IDX_DIM].astype(o_ref.dtype)
        elif kind == "misc":
            o_ref[...] = p_ref[:, off:off + N_MISC]
        else:
            for hd in range(width // HEAD_DIM):
                lanes = slice(hd * HEAD_DIM, (hd + 1) * HEAD_DIM)
                x = p_ref[:, off + hd * HEAD_DIM:off + (hd + 1) * HEAD_DIM]
                if gi is not None:
                    x = x * lax.rsqrt(jnp.mean(x * x, axis=-1, keepdims=True) + NORM_EPS) * g_ref[gi:gi + 1, :]
                if rope:
                    x = x * cs + pltpu.roll(x, HEAD_DIM // 2, 1) * sn
                if scale != 1.0:
                    x = x * scale
                o_ref[:, lanes] = x.astype(o_ref.dtype)
            if kind == "ones":
                o_ref[:, width:] = jnp.ones((o_ref.shape[0], HEAD_DIM), o_ref.dtype)
        off += width


def prepare_operands(proj, qk_gain, cs, sn):
    S = proj.shape[0]
    T = min(PREP_T, S)
    row = lambda i: (i, 0)
    shapes = _prep_outputs(S)
    outs = pl.pallas_call(
        _prep_kernel,
        grid=(S // T,),
        in_specs=[pl.BlockSpec((T, PREP_WIDTH), row),
                  pl.BlockSpec((N_QK_GAINS, HEAD_DIM), lambda i: (0, 0)),
                  pl.BlockSpec((T, HEAD_DIM), row), pl.BlockSpec((T, HEAD_DIM), row)],
        out_specs=tuple(pl.BlockSpec((T, w), row) for w, _ in shapes),
        out_shape=tuple(jax.ShapeDtypeStruct((S, w), dt) for w, dt in shapes),
        compiler_params=_params(1),
        name="prepare_operands",
    )(proj, qk_gain, cs, sn)
    return {seg[0]: o for seg, o in zip(PREP_SEGMENTS, outs)}


def fused_in_weights(w_in, gate_down):
    names = ("a_q", "a_k", "a_v", "b_q", "b_k", "b_v", "b_iq", "b_ik", "b_iw",
             "c_q", "c_kc", "c_vc", "c_ks", "c_vs", "c_kw", "c_vw", "c_g", "d_q", "d_k", "d_v")
    parts = dict(zip(names, jnp.split(w_in, SPLIT_OFFSETS, axis=1)))
    parts["z"] = gate_down
    parts["misc"] = jnp.concatenate([parts["b_iw"], parts["c_g"]], axis=1)
    cols = []
    for name, width, *_ in PREP_SEGMENTS:
        p = parts[name]
        cols.append(jnp.pad(p, ((0, 0), (0, width - p.shape[1]))))
    return jnp.concatenate(cols, axis=1).astype(BF16)


def token_mixing(h, cs, sn, norm_gain, w_in, qk_gain, nsa_pe, nsa_w1, nsa_w2, gate_down, gate_up, w_branch, w_out):
    proj = norm_mm(h, norm_gain, fused_in_weights(w_in, gate_down), F32)
    t = prepare_operands(proj, qk_gain, cs, sn)
    o_a = sb_attention(t["a_q"], t["a_k"], t["a_v"])
    o_b = dsa_attention(t["b_q"], t["b_k"], t["b_v"], t["b_iq"], t["b_ik"], t["misc"][:, :DSA_IDX_HEADS])
    k_cmp = rmsnorm(nsa_compress(t["c_kc"], nsa_pe[0], nsa_w1[0], nsa_w2[0]), qk_gain[3])
    v_cmp = nsa_compress(t["c_vc"], nsa_pe[1], nsa_w1[1], nsa_w2[1])
    o_c = nsa_attention(t["c_q"], t["misc"][:, DSA_IDX_HEADS:], k_cmp.astype(BF16), v_cmp.astype(BF16),
                        t["c_ks"], t["c_vs"], t["c_kw"], t["c_vw"])
    o_d = dilated_attention(t["d_q"], t["d_k"], t["d_v"])
    merged = gated_merge(t["z"], (o_a, o_b, o_c, o_d), gate_up, w_branch)
    return _mm(merged, w_out, residual=h)


MERGE_TM = 1024
MERGE_TN = 512


def _merge_kernel(z_ref, oa_ref, ob_ref, oc_ref, od_ref, gu_ref, wb_ref, o_ref):
    z = z_ref[...]
    acc = None
    off = 0
    for i, br_ref in enumerate((oa_ref, ob_ref, oc_ref, od_ref)):
        width = BRANCH_WIDTHS[i]
        gate = jax.nn.sigmoid(_dot(z, gu_ref[i]))
        term = gate * _dot(br_ref[...], wb_ref[off:off + width, :])
        acc = term if acc is None else acc + term
        off += width
    o_ref[...] = acc.astype(o_ref.dtype)


def gated_merge(z, branches, gate_up, w_branch):
    S = z.shape[0]
    D = w_branch.shape[1]
    tm, tn = min(MERGE_TM, S), min(MERGE_TN, D)
    row = lambda i, j: (i, 0)
    return pl.pallas_call(
        _merge_kernel,
        grid=(S // tm, D // tn),
        in_specs=[pl.BlockSpec((tm, GATE_RANK), row)]
                 + [pl.BlockSpec((tm, w), row) for w in BRANCH_WIDTHS]
                 + [pl.BlockSpec((N_BRANCHES, GATE_RANK, tn), lambda i, j: (0, 0, j)),
                    pl.BlockSpec((BRANCH_WIDTH, tn), lambda i, j: (0, j))],
        out_specs=pl.BlockSpec((tm, tn), lambda i, j: (i, j)),
        out_shape=jax.ShapeDtypeStruct((S, D), BF16),
        compiler_params=_params(2),
        name="gated_merge",
    )(z, *branches, gate_up.astype(BF16), w_branch.astype(BF16))


PEER_ST = 256
PEER_TM = 512
PEER_EC = 512
PEER_HALF = PEER_KEY_DIM // 2
PEER_STAT_ROWS = 32
assert (PEER_TOPK + 1) // 9 == 1 and PEER_TOPK + 1 <= 24


def _top_rows(scores, n, n_rows):
    st = scores.shape[1]
    rid = lax.broadcasted_iota(jnp.int32, (n_rows, st), 0)
    out = jnp.full((n_rows, st), EXTRACTED, F32)
    for r in range(n):
        m = jnp.max(scores, axis=0, keepdims=True)
        out = jnp.where(rid == r, m, out)
        scores = jnp.where(scores == m, EXTRACTED, scores)
    return out


def _peer_stats_kernel(q_ref, w1_ref, w2_ref, o_ref):
    st = q_ref.shape[0]
    k = PEER_TOPK
    rid = lax.broadcasted_iota(jnp.int32, (PEER_STAT_ROWS, st), 0)
    rid8 = lax.broadcasted_iota(jnp.int32, (8, st), 0)
    stats = jnp.zeros((PEER_STAT_ROWS, st), F32)
    for h in range(PEER_HEADS):
        qh = q_ref[:, h * PEER_KEY_DIM:(h + 1) * PEER_KEY_DIM]
        v1 = _top_rows(_dot_nt(w1_ref[h], qh), k + 1, 24)
        v2 = _top_rows(_dot_nt(w2_ref[h], qh), k + 1, 24)
        pieces = [v1[0:1] + v2]
        for a in range(1, 8):
            pieces.append(jnp.where(rid8 < (k + 1) // (a + 1), v1[a:a + 1] + v2[0:8], EXTRACTED))
        pieces.append(v1[8:24] + v2[0:1])
        tops = _top_rows(jnp.concatenate(pieces, axis=0), k + 1, 24)
        c1 = tops[0:1]
        den = jnp.sum(jnp.exp(tops[0:k] - c1), axis=0, keepdims=True)
        thr = 0.5 * (tops[k - 1:k] + tops[k:k + 1])
        stats = jnp.where(rid == h, thr, stats)
        stats = jnp.where(rid == PEER_HEADS + h, c1, stats)
        stats = jnp.where(rid == 2 * PEER_HEADS + h, 1.0 / den, stats)
    o_ref[...] = stats


def _peer_weight_kernel(xn_ref, ut_ref, q_ref, kf_ref, st_ref, o_ref):
    act = jax.nn.gelu(_dot_nt(xn_ref[...], ut_ref[...]))
    st = st_ref[...]
    w = jnp.zeros(act.shape, F32)
    for h in range(PEER_HEADS):
        sf = _dot(q_ref[:, h * PEER_KEY_DIM:(h + 1) * PEER_KEY_DIM], kf_ref[h])
        gate = jnp.exp(sf - st[:, PEER_HEADS + h:PEER_HEADS + h + 1]) * st[:, 2 * PEER_HEADS + h:2 * PEER_HEADS + h + 1]
        w = w + jnp.where(sf >= st[:, h:h + 1], gate, 0.0)
    o_ref[...] = (w * act).astype(o_ref.dtype)


def peer_ffn(h, norm_gain, wq, subkeys, u, v):
    S, D = h.shape
    nk, half = PEER_N_KEYS, PEER_HALF
    q, xn = norm_mm(h, norm_gain, wq.astype(BF16), BF16, return_xn=True)
    zeros = jnp.zeros((PEER_HEADS, nk, half), F32)
    w1 = jnp.concatenate([subkeys[:, 0], zeros], axis=-1).astype(BF16)
    w2 = jnp.concatenate([zeros, subkeys[:, 1]], axis=-1).astype(BF16)
    st_t = pl.pallas_call(
        _peer_stats_kernel,
        grid=(S // PEER_ST,),
        in_specs=[pl.BlockSpec((PEER_ST, PEER_HEADS * PEER_KEY_DIM), lambda i: (i, 0)),
                  pl.BlockSpec((PEER_HEADS, nk, PEER_KEY_DIM), lambda i: (0, 0, 0)),
                  pl.BlockSpec((PEER_HEADS, nk, PEER_KEY_DIM), lambda i: (0, 0, 0))],
        out_specs=pl.BlockSpec((PEER_STAT_ROWS, PEER_ST), lambda i: (0, i)),
        out_shape=jax.ShapeDtypeStruct((PEER_STAT_ROWS, S), F32),
        compiler_params=_params(1),
        name="peer_stats",
    )(q, w1, w2)
    stats = st_t.T
    top = jnp.repeat(jnp.swapaxes(subkeys[:, 0], 1, 2), nk, axis=2)
    bot = jnp.tile(jnp.swapaxes(subkeys[:, 1], 1, 2), (1, 1, nk))
    kfull = jnp.concatenate([top, bot], axis=1).astype(BF16)
    tm, ec = min(PEER_TM, S), PEER_EC
    wmat = pl.pallas_call(
        _peer_weight_kernel,
        grid=(S // tm, PEER_EXPERTS // ec),
        in_specs=[pl.BlockSpec((tm, D), lambda i, j: (i, 0)),
                  pl.BlockSpec((ec, D), lambda i, j: (j, 0)),
                  pl.BlockSpec((tm, PEER_HEADS * PEER_KEY_DIM), lambda i, j: (i, 0)),
                  pl.BlockSpec((PEER_HEADS, PEER_KEY_DIM, ec), lambda i, j: (0, 0, j)),
                  pl.BlockSpec((tm, PEER_STAT_ROWS), lambda i, j: (i, 0))],
        out_specs=pl.BlockSpec((tm, ec), lambda i, j: (i, j)),
        out_shape=jax.ShapeDtypeStruct((S, PEER_EXPERTS), BF16),
        compiler_params=_params(2),
        name="peer_weights",
    )(xn, u.astype(BF16), q, kfull, stats)
    return _mm(wmat, v, residual=h)


def kernel(x, norm_mix, w_in, qk_gain, nsa_pe, nsa_w1, nsa_w2, gate_down, gate_up, w_branch, w_out,
           norm_ffn, peer_wq, peer_subkeys, peer_u, peer_v):
    S = x.shape[1]
    cos, sin = rope_tables(S, x.dtype)
    cos, sin = cos.reshape(S, HEAD_DIM // 2), sin.reshape(S, HEAD_DIM // 2)
    cs = jnp.concatenate([cos, cos], axis=1)
    sn = jnp.concatenate([-sin, sin], axis=1)
    h = x[0]
    for l in range(DEPTH):
        h = token_mixing(h, cs, sn, norm_mix[l], w_in[l], qk_gain[l], nsa_pe[l], nsa_w1[l], nsa_w2[l],
                         gate_down[l], gate_up[l], w_branch[l], w_out[l])
        h = peer_ffn(h, norm_ffn[l], peer_wq[l], peer_subkeys[l], peer_u[l], peer_v[l])
    return h[None]
```
